```python
import math
import jax, jax.numpy as jnp
from jax import lax
import numpy as np

D_MODEL = 2048
BATCH = 2
SEQ = 16384
DEPTH = 2

HEAD_DIM = 128
N_HEADS = D_MODEL // HEAD_DIM
N_HEADS_A = N_HEADS // 2
N_HEADS_B = N_HEADS - N_HEADS_A
WIDTH_A = N_HEADS_A * HEAD_DIM
WIDTH_B = N_HEADS_B * HEAD_DIM
DILATED_BRANCHES = ((128, 1), (512, 4), (2048, 16))
N_IDX_HEADS = 16
IDX_DIM = 64
TOPK_MAX = 256
Q_BLOCK = 128
D_FF = 5632
N_BUCKETS = 32
REL_MAX_DIST = 2048
EPS = 1e-6
PROJ_SIZES = (WIDTH_A, WIDTH_A, WIDTH_A, WIDTH_B, WIDTH_B, WIDTH_B,
              N_IDX_HEADS * IDX_DIM, IDX_DIM, N_IDX_HEADS)
PROJ_WIDTH = sum(PROJ_SIZES)
SPLIT_POINTS = tuple(int(c) for c in np.cumsum(PROJ_SIZES)[:-1])

kernel_name = "hybrid_dilated_dsa_macaron"


def rms_norm(x, g):
    xf = x.astype(jnp.float32)
    y = xf * lax.rsqrt(jnp.mean(xf * xf, axis=-1, keepdims=True) + EPS)
    return (y * g.astype(jnp.float32)).astype(x.dtype)


def swiglu(h, w_in, w_out):
    gate, up = jnp.split(h @ w_in, 2, axis=-1)
    return (jax.nn.silu(gate) * up) @ w_out


def rel_bucket(dist):
    exact = N_BUCKETS // 2
    df = jnp.maximum(dist, 1).astype(jnp.float32)
    large = exact + (jnp.log(df / exact) / math.log(REL_MAX_DIST / exact)
                     * (N_BUCKETS - exact)).astype(jnp.int32)
    large = jnp.minimum(large, N_BUCKETS - 1)
    return jnp.where(dist < exact, dist, large)


def dilated_branch(q, k, v, bias_tab, window, dil):
    B, S, H, E = q.shape
    blk = window // dil
    span = blk * dil
    s_pad = -(-S // span) * span
    nb = s_pad // span

    def to_blocks(t):
        t = jnp.pad(t, ((0, 0), (0, s_pad - S), (0, 0), (0, 0)))
        return t.reshape(B, nb, blk, dil, H, E)

    def with_prev(t):
        prev = jnp.pad(t, ((0, 0), (1, 0), (0, 0), (0, 0), (0, 0), (0, 0)))[:, :-1]
        return jnp.concatenate([prev, t], axis=2)

    qb = to_blocks(q)
    kw = with_prev(to_blocks(k))
    vw = with_prev(to_blocks(v))
    i = jnp.arange(blk)[:, None]
    j = jnp.arange(2 * blk)[None, :]
    step = blk + i - j
    band = (step >= 0) & (step <= blk)
    first = j >= blk
    nblock = jnp.arange(nb)[:, None, None, None, None]
    valid = band[None, None, None] & ((nblock > 0) | first[None, None, None])
    bias = bias_tab[rel_bucket(jnp.clip(step, 0, blk) * dil)]
    bias = jnp.transpose(bias, (2, 0, 1)).astype(jnp.float32)
    logits = jnp.einsum('bnidhe,bnjdhe->bndhij', qb, kw).astype(jnp.float32) + bias
    logits = jnp.where(valid, logits, -jnp.inf)
    lse = jax.nn.logsumexp(logits, axis=-1)
    p = jnp.exp(logits - lse[..., None])
    o = jnp.einsum('bndhij,bnjdhe->bnidhe', p.astype(v.dtype), vw)
    o = o.reshape(B, s_pad, H, E)[:, :S]
    lse = jnp.transpose(lse, (0, 1, 4, 2, 3)).reshape(B, s_pad, H)[:, :S]
    return o, lse


def dilated_mixer(q, k, v, bias_tab):
    outs, lses = [], []
    for window, dil in DILATED_BRANCHES:
        o, l = dilated_branch(q, k, v, bias_tab, window, dil)
        outs.append(o)
        lses.append(l)
    wts = jax.nn.softmax(jnp.stack(lses), axis=0)
    out = jnp.einsum('nbsh,nbshe->bshe', wts, jnp.stack(outs).astype(jnp.float32))
    return out.astype(q.dtype)


def dsa_mixer(q, k, v, q_idx, k_idx, w_idx, bias_tab):
    B, S, H, E = q.shape
    n_sel = min(TOPK_MAX, S // 4)
    nqb = S // Q_BLOCK
    key_pos = jnp.arange(S)
    idx_scale = (IDX_DIM ** -0.5) * (N_IDX_HEADS ** -0.5)

    def blocks(t):
        return jnp.moveaxis(t.reshape((B, nqb, Q_BLOCK) + t.shape[2:]), 1, 0)

    def one_block(args):
        qb, qib, wb, start = args
        t_pos = start + jnp.arange(Q_BLOCK)
        rel = jax.nn.relu(jnp.einsum('bthe,bse->bths', qib, k_idx).astype(jnp.float32))
        score = jnp.einsum('bths,bth->bts', rel, wb.astype(jnp.float32)) * idx_scale
        causal = key_pos[None, :] <= t_pos[:, None]
        score = jnp.where(causal, score, -jnp.inf)
        _, sel = lax.top_k(score, n_sel)
        valid = sel <= t_pos[None, :, None]
        kg = jax.vmap(lambda kk, ii: kk[ii])(k, sel)
        vg = jax.vmap(lambda vv, ii: vv[ii])(v, sel)
        logits = jnp.einsum('bthe,btkhe->bhtk', qb, kg).astype(jnp.float32)
        dist = jnp.maximum(t_pos[None, :, None] - sel, 0)
        bias = bias_tab[rel_bucket(dist)]
        logits = logits + jnp.transpose(bias, (0, 3, 1, 2)).astype(jnp.float32)
        logits = jnp.where(valid[:, None], logits, -jnp.inf)
        p = jax.nn.softmax(logits, axis=-1)
        return jnp.einsum('bhtk,btkhe->bthe', p.astype(v.dtype), vg)

    starts = jnp.arange(nqb, dtype=jnp.int32) * Q_BLOCK
    out = lax.map(one_block, (blocks(q), blocks(q_idx), blocks(w_idx), starts))
    return jnp.moveaxis(out, 0, 1).reshape(B, S, H, E)


def setup_inputs(seed: int = 0) -> dict:
    key = jax.random.key(seed)
    ks = jax.random.split(key, 16)
    nrm = lambda k, shape, fan_in: jax.random.normal(k, shape, jnp.float32) * (fan_in ** -0.5)
    gain = lambda k, shape: 1.0 + 0.05 * jax.random.normal(k, shape, jnp.float32)
    return {
        'x': jax.random.normal(ks[0], (BATCH, SEQ, D_MODEL), jnp.float32),
        'rel_bias': 0.2 * jax.random.normal(ks[1], (N_BUCKETS, N_HEADS), jnp.float32),
        'norm_ffn1': gain(ks[2], (DEPTH, D_MODEL)),
        'w_ffn1_in': nrm(ks[3], (DEPTH, D_MODEL, 2 * D_FF), D_MODEL),
        'w_ffn1_out': nrm(ks[4], (DEPTH, D_FF, D_MODEL), D_FF),
        'norm_mix': gain(ks[5], (DEPTH, D_MODEL)),
        'w_in': nrm(ks[6], (DEPTH, D_MODEL, PROJ_WIDTH), D_MODEL),
        'q_norm_a': gain(ks[7], (DEPTH, HEAD_DIM)),
        'k_norm_a': gain(ks[8], (DEPTH, HEAD_DIM)),
        'q_norm_b': gain(ks[9], (DEPTH, HEAD_DIM)),
        'k_norm_b': gain(ks[10], (DEPTH, HEAD_DIM)),
        'w_out': nrm(ks[11], (DEPTH, WIDTH_A + WIDTH_B, D_MODEL), WIDTH_A + WIDTH_B),
        'norm_ffn2': gain(ks[12], (DEPTH, D_MODEL)),
        'w_ffn2_in': nrm(ks[13], (DEPTH, D_MODEL, 2 * D_FF), D_MODEL),
        'w_ffn2_out': nrm(ks[14], (DEPTH, D_FF, D_MODEL), D_FF),
    }


def reference(x, rel_bias, norm_ffn1, w_ffn1_in, w_ffn1_out, norm_mix, w_in,
              q_norm_a, k_norm_a, q_norm_b, k_norm_b, w_out,
              norm_ffn2, w_ffn2_in, w_ffn2_out):
    B, S, _ = x.shape
    bias_a = rel_bias[:, :N_HEADS_A]
    bias_b = rel_bias[:, N_HEADS_A:]
    scale = HEAD_DIM ** -0.5
    heads = lambda t: t.reshape(B, S, -1, HEAD_DIM)
    for l in range(DEPTH):
        x = x + 0.5 * swiglu(rms_norm(x, norm_ffn1[l]), w_ffn1_in[l], w_ffn1_out[l])
        h = rms_norm(x, norm_mix[l])
        proj = h @ w_in[l]
        qa, ka, va, qb, kb, vb, qi, ki, wi = jnp.split(proj, SPLIT_POINTS, axis=-1)
        qa = rms_norm(heads(qa), q_norm_a[l]) * scale
        ka = rms_norm(heads(ka), k_norm_a[l])
        qb = rms_norm(heads(qb), q_norm_b[l]) * scale
        kb = rms_norm(heads(kb), k_norm_b[l])
        mix_a = dilated_mixer(qa, ka, heads(va), bias_a)
        mix_b = dsa_mixer(qb, kb, heads(vb), qi.reshape(B, S, N_IDX_HEADS, IDX_DIM),
                          ki, wi, bias_b)
        mixed = jnp.concatenate([mix_a.reshape(B, S, WIDTH_A), mix_b.reshape(B, S, WIDTH_B)], axis=-1)
        x = x + mixed @ w_out[l]
        x = x + 0.5 * swiglu(rms_norm(x, norm_ffn2[l]), w_ffn2_in[l], w_ffn2_out[l])
    return x
```

```python
import functools
import math

import jax
import jax.numpy as jnp
import numpy as np
from jax import lax
from jax.experimental import pallas as pl
from jax.experimental.pallas import tpu as pltpu

F32 = jnp.float32
BF16 = jnp.bfloat16
I32 = jnp.int32

HEAD_DIM = 128
N_HEADS_A = 8
N_HEADS_B = 8
WIDTH_A = N_HEADS_A * HEAD_DIM
WIDTH_B = N_HEADS_B * HEAD_DIM
DILATIONS = (1, 4, 16)
DIL_BLK = 128
N_IDX_HEADS = 16
IDX_DIM = 64
TOPK_MAX = 256
N_BUCKETS = 32
REL_MAX_DIST = 2048
EPS = 1e-6
IDX_PAD = 128
IDX_SCALE = (IDX_DIM ** -0.5) * (N_IDX_HEADS ** -0.5)

VMEM_LIMIT = 56 * 1024 * 1024

DSA_TQ = 256
DSA_TK = 512
NEG_BIG = -1e30


def _near_tiles():
    exact = N_BUCKETS // 2
    j = 1
    while True:
        d = j * DSA_TQ - (DSA_TQ - 1)
        b = exact + int(math.log(d / exact) / math.log(REL_MAX_DIST / exact) * (N_BUCKETS - exact) - 0.02)
        if d >= exact and b >= N_BUCKETS - 1:
            return j
        j += 1


DSA_NEAR = _near_tiles()


def _cparams(sem):
    return pltpu.CompilerParams(dimension_semantics=sem, vmem_limit_bytes=VMEM_LIMIT)


def _rel_bucket(dist):
    exact = N_BUCKETS // 2
    df = jnp.maximum(dist, 1).astype(F32)
    large = exact + (jnp.log(df / exact) / math.log(REL_MAX_DIST / exact)
                     * (N_BUCKETS - exact)).astype(I32)
    large = jnp.minimum(large, N_BUCKETS - 1)
    return jnp.where(dist < exact, dist, large)


def _lookup(tab_ref, bucket, col):
    val = jnp.zeros(bucket.shape, F32)
    for b in range(N_BUCKETS):
        val = jnp.where(bucket == b, tab_ref[b, col], val)
    return val


def _dil_bias_kernel(tab_ref, o_ref):
    dil = jnp.left_shift(1, 2 * pl.program_id(0))
    i = lax.broadcasted_iota(I32, (DIL_BLK, 2 * DIL_BLK), 0)
    j = lax.broadcasted_iota(I32, (DIL_BLK, 2 * DIL_BLK), 1)
    bucket = _rel_bucket(jnp.clip(DIL_BLK + i - j, 0, DIL_BLK) * dil)
    for h in range(N_HEADS_A):
        o_ref[0, h] = _lookup(tab_ref, bucket, h)


def _dsa_bias_kernel(tab_ref, o_ref):
    jj = pl.program_id(0)
    s = lax.broadcasted_iota(I32, (DSA_TQ, DSA_TQ), 0)
    t = lax.broadcasted_iota(I32, (DSA_TQ, DSA_TQ), 1)
    bucket = _rel_bucket(jnp.maximum((jj - 1) * DSA_TQ + t - s, 0))
    for h in range(N_HEADS_B):
        o_ref[h, 0] = _lookup(tab_ref, bucket, N_HEADS_A + h)


def _bias_tiles(rel_bias):
    smem = pl.BlockSpec(memory_space=pltpu.SMEM)
    dil = pl.pallas_call(
        _dil_bias_kernel,
        grid=(len(DILATIONS),),
        in_specs=[smem],
        out_specs=pl.BlockSpec((1, N_HEADS_A, DIL_BLK, 2 * DIL_BLK), lambda g: (g, 0, 0, 0)),
        out_shape=jax.ShapeDtypeStruct((len(DILATIONS), N_HEADS_A, DIL_BLK, 2 * DIL_BLK), F32),
        name="dil_bias",
    )(rel_bias)
    ntile = DSA_NEAR + 2
    dsa = pl.pallas_call(
        _dsa_bias_kernel,
        grid=(ntile,),
        in_specs=[smem],
        out_specs=pl.BlockSpec((N_HEADS_B, 1, DSA_TQ, DSA_TQ), lambda g: (0, g, 0, 0)),
        out_shape=jax.ShapeDtypeStruct((N_HEADS_B, ntile, DSA_TQ, DSA_TQ), F32),
        name="dsa_bias",
    )(rel_bias)
    return dil, dsa


def _ffn_kernel(x_ref, g_ref, wg_ref, wu_ref, wo_ref, o_ref, h_ref, acc_ref):
    f = pl.program_id(1)

    @pl.when(f == 0)
    def _():
        x = x_ref[...]
        ms = jnp.mean(x * x, axis=-1, keepdims=True)
        h_ref[...] = (x * lax.rsqrt(ms + EPS) * g_ref[...]).astype(BF16)
        acc_ref[...] = jnp.zeros_like(acc_ref)

    h = h_ref[...]
    gate = jnp.dot(h, wg_ref[...], preferred_element_type=F32)
    up = jnp.dot(h, wu_ref[...], preferred_element_type=F32)
    act = (gate * jax.nn.sigmoid(gate) * up).astype(BF16)
    acc_ref[...] += jnp.dot(act, wo_ref[...], preferred_element_type=F32)

    @pl.when(f == pl.num_programs(1) - 1)
    def _():
        o_ref[...] = x_ref[...] + 0.5 * acc_ref[...]


def _ffn(x2, gain, w_in, w_out, tm=512, tf=512):
    m, d = x2.shape
    d_ff = w_out.shape[0]
    nf = d_ff // tf
    return pl.pallas_call(
        _ffn_kernel,
        grid=(m // tm, nf),
        in_specs=[
            pl.BlockSpec((tm, d), lambda i, f: (i, 0)),
            pl.BlockSpec((1, d), lambda i, f: (0, 0)),
            pl.BlockSpec((d, tf), lambda i, f: (0, f)),
            pl.BlockSpec((d, tf), lambda i, f: (0, f + nf)),
            pl.BlockSpec((tf, d), lambda i, f: (f, 0)),
        ],
        out_specs=pl.BlockSpec((tm, d), lambda i, f: (i, 0)),
        out_shape=jax.ShapeDtypeStruct((m, d), F32),
        scratch_shapes=[pltpu.VMEM((tm, d), BF16), pltpu.VMEM((tm, d), F32)],
        compiler_params=_cparams(("parallel", "arbitrary")),
        name="ffn",
    )(x2, gain.reshape(1, d), w_in, w_in, w_out)


PROJ_BLK = 1024
PROJ_NBLK = 7
_NORM_BLOCKS = (0, 1, 3, 4)


def _proj_kernel(x_ref, g_ref, w_ref, widx_ref, hg_ref, o_ref, idx_ref, h_ref):
    j = pl.program_id(1)

    @pl.when(j == 0)
    def _():
        x = x_ref[...]
        ms = jnp.mean(x * x, axis=-1, keepdims=True)
        h_ref[...] = (x * lax.rsqrt(ms + EPS) * g_ref[...]).astype(BF16)

    h = h_ref[...]
    y = jnp.dot(h, w_ref[...], preferred_element_type=F32)
    is_norm = functools.reduce(jnp.logical_or, [j == b for b in _NORM_BLOCKS])

    @pl.when(is_norm)
    def _():
        for hd in range(PROJ_BLK // HEAD_DIM):
            sl = slice(hd * HEAD_DIM, (hd + 1) * HEAD_DIM)
            yh = y[:, sl]
            ms = jnp.mean(yh * yh, axis=-1, keepdims=True)
            o_ref[:, sl] = (yh * lax.rsqrt(ms + EPS) * hg_ref[0, :, sl]).astype(BF16)

    @pl.when(jnp.logical_not(is_norm))
    def _():
        o_ref[...] = y.astype(BF16)

    @pl.when(j == PROJ_NBLK - 1)
    def _():
        idx_ref[...] = jnp.dot(h, widx_ref[...], preferred_element_type=F32)


def _proj(x2, gain, w_main, w_idx, head_gain, tm=1024):
    m, d = x2.shape
    return pl.pallas_call(
        _proj_kernel,
        grid=(m // tm, PROJ_NBLK),
        in_specs=[
            pl.BlockSpec((tm, d), lambda i, j: (i, 0)),
            pl.BlockSpec((1, d), lambda i, j: (0, 0)),
            pl.BlockSpec((d, PROJ_BLK), lambda i, j: (0, j)),
            pl.BlockSpec((d, IDX_PAD), lambda i, j: (0, 0)),
            pl.BlockSpec((1, 1, PROJ_BLK), lambda i, j: (j, 0, 0)),
        ],
        out_specs=[
            pl.BlockSpec((tm, PROJ_BLK), lambda i, j: (i, j)),
            pl.BlockSpec((tm, IDX_PAD), lambda i, j: (i, 0)),
        ],
        out_shape=[
            jax.ShapeDtypeStruct((m, PROJ_NBLK * PROJ_BLK), BF16),
            jax.ShapeDtypeStruct((m, IDX_PAD), F32),
        ],
        scratch_shapes=[pltpu.VMEM((tm, d), BF16)],
        compiler_params=_cparams(("parallel", "arbitrary")),
        name="proj",
    )(x2, gain.reshape(1, d), w_main, w_idx, head_gain)


_NT = (((1,), (1,)), ((), ()))


def _dilated_kernel(q_ref, kp_ref, kc_ref, vp_ref, vc_ref, b_ref, o_ref, lse_ref):
    n = pl.program_id(1)
    i = lax.broadcasted_iota(I32, (DIL_BLK, DIL_BLK), 0)
    j = lax.broadcasted_iota(I32, (DIL_BLK, DIL_BLK), 1)
    mask_prev = jnp.logical_and(j >= i, n > 0)
    mask_cur = j <= i
    lane = lax.broadcasted_iota(I32, (DIL_BLK, HEAD_DIM), 1)
    lse_all = jnp.zeros((DIL_BLK, HEAD_DIM), F32)
    for h in range(N_HEADS_A):
        sl = slice(h * HEAD_DIM, (h + 1) * HEAD_DIM)
        q = q_ref[0, :, sl]
        sp = lax.dot_general(q, kp_ref[0, :, sl], _NT, preferred_element_type=F32)
        sc = lax.dot_general(q, kc_ref[0, :, sl], _NT, preferred_element_type=F32)
        sp = jnp.where(mask_prev, sp + b_ref[0, h, :, :DIL_BLK], -jnp.inf)
        sc = jnp.where(mask_cur, sc + b_ref[0, h, :, DIL_BLK:], -jnp.inf)
        mx = jnp.maximum(jnp.max(sp, axis=-1, keepdims=True), jnp.max(sc, axis=-1, keepdims=True))
        pp = jnp.exp(sp - mx)
        pc = jnp.exp(sc - mx)
        den = jnp.sum(pp, axis=-1, keepdims=True) + jnp.sum(pc, axis=-1, keepdims=True)
        o = (jnp.dot(pp.astype(BF16), vp_ref[0, :, sl], preferred_element_type=F32)
             + jnp.dot(pc.astype(BF16), vc_ref[0, :, sl], preferred_element_type=F32))
        o_ref[0, :, sl] = (o / den).astype(BF16)
        lse_all = jnp.where(lane == h, mx + jnp.log(den), lse_all)
    lse_ref[0] = lse_all


def _dilated_branch(main, bias, branch):
    bsz, s, width = main.shape
    w = WIDTH_A
    dil = DILATIONS[branch]
    rows = s // dil
    nb = rows // DIL_BLK
    view = main.reshape(bsz, rows, dil * width)
    cur = lambda c: pl.BlockSpec((1, DIL_BLK, w), lambda b, n, r: (b, n, r * PROJ_NBLK + c))
    prev = lambda c: pl.BlockSpec(
        (1, DIL_BLK, w), lambda b, n, r: (b, jnp.maximum(n - 1, 0), r * PROJ_NBLK + c))
    o, lse = pl.pallas_call(
        _dilated_kernel,
        grid=(bsz, nb, dil),
        in_specs=[cur(0), prev(1), cur(1), prev(2), cur(2),
                  pl.BlockSpec((1, N_HEADS_A, DIL_BLK, 2 * DIL_BLK), lambda b, n, r: (branch, 0, 0, 0))],
        out_specs=[pl.BlockSpec((1, DIL_BLK, w), lambda b, n, r: (b, n, r)),
                   pl.BlockSpec((1, DIL_BLK, HEAD_DIM), lambda b, n, r: (b, n, r))],
        out_shape=[jax.ShapeDtypeStruct((bsz, rows, dil * w), BF16),
                   jax.ShapeDtypeStruct((bsz, rows, dil * HEAD_DIM), F32)],
        compiler_params=_cparams(("parallel", "parallel", "parallel")),
        name=f"dilated_{dil}",
    )(view, view, view, view, view, bias)
    return o.reshape(bsz * s, w), lse.reshape(bsz * s, HEAD_DIM)


def _dsa_schedule(s):
    qs, ks = [], []
    for i in range(s // DSA_TQ):
        last = (i * DSA_TQ + DSA_TQ - 1) // DSA_TK
        for k in range(last + 1):
            qs.append(i)
            ks.append(k)
    return np.asarray(qs, np.int32), np.asarray(ks, np.int32)


def _sortable(bits):
    return bits ^ (lax.shift_right_arithmetic(bits, 31) & 0x7FFFFFFF)


def _dsa_score_kernel(qtab, ktab, qi_ref, ki_ref, w_ref, sc_ref, thr_ref, keys_ref):
    step = pl.program_id(1)
    qb = qtab[step]
    kc = ktab[step]
    kidx = ki_ref[0]
    w = w_ref[0] * IDX_SCALE
    acc = jnp.zeros((DSA_TK, DSA_TQ), F32)
    for h in range(N_IDX_HEADS):
        d = lax.dot_general(kidx, qi_ref[0, :, h * IDX_DIM:(h + 1) * IDX_DIM], _NT,
                            preferred_element_type=F32)
        acc = acc + jnp.maximum(d, 0.0) * w[h:h + 1, :]
    srow = lax.broadcasted_iota(I32, (DSA_TK, DSA_TQ), 0)
    tcol = lax.broadcasted_iota(I32, (DSA_TK, DSA_TQ), 1)
    causal = srow - tcol <= qb * DSA_TQ - kc * DSA_TK
    acc = jnp.where(causal, acc, -jnp.inf)
    sc_ref[0] = acc
    keys_ref[pl.ds(pl.multiple_of(kc * DSA_TK, DSA_TK), DSA_TK), :] = _sortable(
        lax.bitcast_convert_type(acc, I32))

    last = (qb * DSA_TQ + DSA_TQ - 1) // DSA_TK

    @pl.when(kc == last)
    def _():
        tpos = qb * DSA_TQ + lax.broadcasted_iota(I32, (1, DSA_TQ), 1)
        want = jnp.minimum(tpos + 1, TOPK_MAX)

        def count_ge(cand):
            def body(c, cnt):
                tile = keys_ref[pl.ds(pl.multiple_of(c * DSA_TK, DSA_TK), DSA_TK), :]
                ind = jnp.where(tile >= cand, 1, 0)
                return cnt + ind.reshape(DSA_TK // 8, 8, DSA_TQ).sum(axis=0)
            cnt = lax.fori_loop(0, last + 1, body, jnp.zeros((8, DSA_TQ), I32))
            return cnt.sum(axis=0, keepdims=True)

        def bit_step(it, r):
            cand = r + lax.shift_left(jnp.int32(1), 31 - it)
            return jnp.where(count_ge(cand) >= want, cand, r)

        r = lax.fori_loop(0, 32, bit_step, jnp.full((1, DSA_TQ), -2 ** 31, I32))
        thr_ref[0] = lax.bitcast_convert_type(_sortable(r), F32)


def _dsa_attn_kernel(qtab, ktab, sc_ref, thr_ref, q_ref, k_ref, vt_ref, b_ref, o_ref,
                     acc_ref, m_ref, l_ref):
    step = pl.program_id(1)
    qb = qtab[step]
    kc = ktab[step]

    @pl.when(kc == 0)
    def _():
        acc_ref[...] = jnp.zeros_like(acc_ref)
        m_ref[...] = jnp.full_like(m_ref, NEG_BIG)
        l_ref[...] = jnp.zeros_like(l_ref)

    keep = sc_ref[0] >= thr_ref[0]
    halves = DSA_TK // DSA_TQ
    tiles = [jnp.clip(qb - (kc * halves + hh), -1, DSA_NEAR) + 1 for hh in range(halves)]
    for h in range(N_HEADS_B):
        sl = slice(h * HEAD_DIM, (h + 1) * HEAD_DIM)
        s = lax.dot_general(k_ref[0, :, sl], q_ref[0, :, sl], _NT, preferred_element_type=F32)
        bias = jnp.concatenate([b_ref[h, tiles[hh]] for hh in range(halves)], axis=0)
        s = jnp.where(keep, s + bias, -jnp.inf)
        m_old = m_ref[h:h + 1, :]
        m_new = jnp.maximum(m_old, jnp.max(s, axis=0, keepdims=True))
        alpha = jnp.exp(m_old - m_new)
        p = jnp.exp(s - m_new)
        l_ref[h:h + 1, :] = alpha * l_ref[h:h + 1, :] + jnp.sum(p, axis=0, keepdims=True)
        m_ref[h:h + 1, :] = m_new
        pv = jnp.dot(vt_ref[0, sl, :], p.astype(BF16), preferred_element_type=F32)
        acc_ref[sl, :] = alpha * acc_ref[sl, :] + pv

    last = (qb * DSA_TQ + DSA_TQ - 1) // DSA_TK

    @pl.when(kc == last)
    def _():
        for h in range(N_HEADS_B):
            sl = slice(h * HEAD_DIM, (h + 1) * HEAD_DIM)
            o_ref[0, sl, :] = (acc_ref[sl, :] / l_ref[h:h + 1, :]).astype(BF16)


def _dsa(main, ki, wi_t, bias):
    bsz, s, _ = main.shape
    w = WIDTH_B
    qs, ks = _dsa_schedule(s)
    npairs = len(qs)
    scores, thr = pl.pallas_call(
        _dsa_score_kernel,
        grid_spec=pltpu.PrefetchScalarGridSpec(
            num_scalar_prefetch=2,
            grid=(bsz, npairs),
            in_specs=[
                pl.BlockSpec((1, DSA_TQ, N_IDX_HEADS * IDX_DIM), lambda b, t, qt, kt: (b, qt[t], 6)),
                pl.BlockSpec((1, DSA_TK, IDX_DIM), lambda b, t, qt, kt: (b, kt[t], 0)),
                pl.BlockSpec((1, N_IDX_HEADS, DSA_TQ), lambda b, t, qt, kt: (b, 0, qt[t])),
            ],
            out_specs=[
                pl.BlockSpec((1, DSA_TK, DSA_TQ), lambda b, t, qt, kt: (b, kt[t], qt[t])),
                pl.BlockSpec((1, 1, DSA_TQ), lambda b, t, qt, kt: (b, 0, qt[t])),
            ],
            scratch_shapes=[pltpu.VMEM((s, DSA_TQ), I32)],
        ),
        out_shape=[jax.ShapeDtypeStruct((bsz, s, s), F32),
                   jax.ShapeDtypeStruct((bsz, 1, s), F32)],
        compiler_params=_cparams(("arbitrary", "arbitrary")),
        name="dsa_scores",
    )(qs, ks, main, ki, wi_t)

    vt = jnp.swapaxes(main[:, :, 5 * PROJ_BLK:6 * PROJ_BLK], 1, 2)
    ntile = bias.shape[1]
    out_t = pl.pallas_call(
        _dsa_attn_kernel,
        grid_spec=pltpu.PrefetchScalarGridSpec(
            num_scalar_prefetch=2,
            grid=(bsz, npairs),
            in_specs=[
                pl.BlockSpec((1, DSA_TK, DSA_TQ), lambda b, t, qt, kt: (b, kt[t], qt[t])),
                pl.BlockSpec((1, 1, DSA_TQ), lambda b, t, qt, kt: (b, 0, qt[t])),
                pl.BlockSpec((1, DSA_TQ, w), lambda b, t, qt, kt: (b, qt[t], 3)),
                pl.BlockSpec((1, DSA_TK, w), lambda b, t, qt, kt: (b, kt[t], 4)),
                pl.BlockSpec((1, w, DSA_TK), lambda b, t, qt, kt: (b, 0, kt[t])),
                pl.BlockSpec((N_HEADS_B, ntile, DSA_TQ, DSA_TQ), lambda b, t, qt, kt: (0, 0, 0, 0),
                             pipeline_mode=pl.Buffered(1)),
            ],
            out_specs=pl.BlockSpec((1, w, DSA_TQ), lambda b, t, qt, kt: (b, 0, qt[t])),
            scratch_shapes=[pltpu.VMEM((w, DSA_TQ), F32),
                            pltpu.VMEM((N_HEADS_B, DSA_TQ), F32),
                            pltpu.VMEM((N_HEADS_B, DSA_TQ), F32)],
        ),
        out_shape=jax.ShapeDtypeStruct((bsz, w, s), BF16),
        compiler_params=_cparams(("arbitrary", "arbitrary")),
        name="dsa_attn",
    )(qs, ks, scores, thr, main, main, vt, bias)
    return jnp.swapaxes(out_t, 1, 2).reshape(bsz * s, w)


def _out_kernel(x_ref, o1_ref, o2_ref, o3_ref, l1_ref, l2_ref, l3_ref, mb_ref, wa_ref, wb_ref, o_ref):
    l1, l2, l3 = l1_ref[...], l2_ref[...], l3_ref[...]
    mx = jnp.maximum(jnp.maximum(l1, l2), l3)
    e1, e2, e3 = jnp.exp(l1 - mx), jnp.exp(l2 - mx), jnp.exp(l3 - mx)
    inv = 1.0 / (e1 + e2 + e3)
    w1, w2, w3 = e1 * inv, e2 * inv, e3 * inv
    acc = x_ref[...] + jnp.dot(mb_ref[...], wb_ref[...], preferred_element_type=F32)
    for h in range(N_HEADS_A):
        sl = slice(h * HEAD_DIM, (h + 1) * HEAD_DIM)
        mix = (w1[:, h:h + 1] * o1_ref[:, sl].astype(F32)
               + w2[:, h:h + 1] * o2_ref[:, sl].astype(F32)
               + w3[:, h:h + 1] * o3_ref[:, sl].astype(F32))
        acc = acc + jnp.dot(mix.astype(BF16), wa_ref[sl, :], preferred_element_type=F32)
    o_ref[...] = acc


def _out_proj(x2, outs, lses, mix_b, w_out, tm=512):
    m, d = x2.shape
    row = lambda width: pl.BlockSpec((tm, width), lambda i: (i, 0))
    return pl.pallas_call(
        _out_kernel,
        grid=(m // tm,),
        in_specs=[row(d), row(WIDTH_A), row(WIDTH_A), row(WIDTH_A),
                  row(HEAD_DIM), row(HEAD_DIM), row(HEAD_DIM), row(WIDTH_B),
                  pl.BlockSpec((WIDTH_A, d), lambda i: (0, 0)),
                  pl.BlockSpec((WIDTH_B, d), lambda i: (1, 0))],
        out_specs=row(d),
        out_shape=jax.ShapeDtypeStruct((m, d), F32),
        compiler_params=_cparams(("parallel",)),
        name="out_proj",
    )(x2, *outs, *lses, mix_b, w_out, w_out)


def kernel(x, rel_bias, norm_ffn1, w_ffn1_in, w_ffn1_out, norm_mix, w_in, q_norm_a, k_norm_a,
           q_norm_b, k_norm_b, w_out, norm_ffn2, w_ffn2_in, w_ffn2_out):
    bsz, s, d = x.shape
    m = bsz * s
    depth = norm_ffn1.shape[0]
    scale = HEAD_DIM ** -0.5
    n_main = PROJ_NBLK * PROJ_BLK
    dil_bias, dsa_bias = _bias_tiles(rel_bias)
    x2 = x.reshape(m, d)
    ones = jnp.ones((PROJ_BLK,), F32)
    tile8 = lambda g: jnp.tile(g, PROJ_BLK // HEAD_DIM)
    for l in range(depth):
        x2 = _ffn(x2, norm_ffn1[l], w_ffn1_in[l].astype(BF16), w_ffn1_out[l].astype(BF16))

        w_main = w_in[l, :, :n_main].astype(BF16)
        w_idx = jnp.pad(w_in[l, :, n_main:], ((0, 0), (0, IDX_PAD - IDX_DIM - N_IDX_HEADS))).astype(BF16)
        head_gain = jnp.stack([tile8(q_norm_a[l] * scale), tile8(k_norm_a[l]), ones,
                               tile8(q_norm_b[l] * scale), tile8(k_norm_b[l]), ones, ones])
        main, idx = _proj(x2, norm_mix[l], w_main, w_idx, head_gain.reshape(PROJ_NBLK, 1, PROJ_BLK))
        main = main.reshape(bsz, s, n_main)
        idx = idx.reshape(bsz, s, IDX_PAD)
        ki = idx[:, :, :IDX_DIM].astype(BF16)
        wi_t = jnp.swapaxes(idx[:, :, IDX_DIM:IDX_DIM + N_IDX_HEADS], 1, 2)

        outs, lses = zip(*[_dilated_branch(main, dil_bias, br) for br in range(len(DILATIONS))])
        mix_b = _dsa(main, ki, wi_t, dsa_bias)

        x2 = _out_proj(x2, outs, lses, mix_b, w_out[l].astype(BF16))
        x2 = _ffn(x2, norm_ffn2[l], w_ffn2_in[l].astype(BF16), w_ffn2_out[l].astype(BF16))
    return x2.reshape(bsz, s, d)
```

```python
import functools
import math

import jax
import jax.numpy as jnp
import numpy as np
from jax import lax
from jax.experimental import pallas as pl
from jax.experimental.pallas import tpu as pltpu

F32 = jnp.float32
BF16 = jnp.bfloat16
I32 = jnp.int32

HEAD_DIM = 128
N_HEADS_A = 8
N_HEADS_B = 8
WIDTH_A = N_HEADS_A * HEAD_DIM
WIDTH_B = N_HEADS_B * HEAD_DIM
DILATIONS = (1, 4, 16)
DIL_BLK = 128
N_IDX_HEADS = 16
IDX_DIM = 64
TOPK_MAX = 256
N_BUCKETS = 32
REL_MAX_DIST = 2048
EPS = 1e-6
IDX_PAD = 128
IDX_SCALE = (IDX_DIM ** -0.5) * (N_IDX_HEADS ** -0.5)
LOG2E = math.log2(math.e)

VMEM_LIMIT = 56 * 1024 * 1024

DSA_TQ = 256
DSA_TK = 512
DSA_HALVES = DSA_TK // DSA_TQ
WORD = 32
CHUNK_WORDS = DSA_TK // WORD
NEG_BIG = -1e30
INT_MIN = -2 ** 31


def _near_tiles():
    exact = N_BUCKETS // 2
    j = 1
    while True:
        d = j * DSA_TQ - (DSA_TQ - 1)
        b = exact + int(math.log(d / exact) / math.log(REL_MAX_DIST / exact) * (N_BUCKETS - exact) - 0.02)
        if d >= exact and b >= N_BUCKETS - 1:
            return j
        j += 1


DSA_NEAR = _near_tiles()


def _cparams(sem):
    return pltpu.CompilerParams(dimension_semantics=sem, vmem_limit_bytes=VMEM_LIMIT)


def _rel_bucket(dist):
    exact = N_BUCKETS // 2
    df = jnp.maximum(dist, 1).astype(F32)
    large = exact + (jnp.log(df / exact) / math.log(REL_MAX_DIST / exact)
                     * (N_BUCKETS - exact)).astype(I32)
    large = jnp.minimum(large, N_BUCKETS - 1)
    return jnp.where(dist < exact, dist, large)


def _lookup(tab_ref, bucket, col):
    val = jnp.zeros(bucket.shape, F32)
    for b in range(N_BUCKETS):
        val = jnp.where(bucket == b, tab_ref[b, col], val)
    return val


def _dil_bias_kernel(tab_ref, o_ref):
    dil = jnp.left_shift(1, 2 * pl.program_id(0))
    i = lax.broadcasted_iota(I32, (DIL_BLK, 2 * DIL_BLK), 0)
    j = lax.broadcasted_iota(I32, (DIL_BLK, 2 * DIL_BLK), 1)
    bucket = _rel_bucket(jnp.clip(DIL_BLK + i - j, 0, DIL_BLK) * dil)
    for h in range(N_HEADS_A):
        o_ref[0, h] = _lookup(tab_ref, bucket, h)


def _dsa_bias_kernel(tab_ref, o_ref):
    jj = pl.program_id(0)
    s = lax.broadcasted_iota(I32, (DSA_TQ, DSA_TQ), 0)
    t = lax.broadcasted_iota(I32, (DSA_TQ, DSA_TQ), 1)
    bucket = _rel_bucket(jnp.maximum((jj - 1) * DSA_TQ + t - s, 0))
    for h in range(N_HEADS_B):
        o_ref[h, 0] = _lookup(tab_ref, bucket, N_HEADS_A + h) * LOG2E


def _bias_tiles(rel_bias):
    smem = pl.BlockSpec(memory_space=pltpu.SMEM)
    dil = pl.pallas_call(
        _dil_bias_kernel,
        grid=(len(DILATIONS),),
        in_specs=[smem],
        out_specs=pl.BlockSpec((1, N_HEADS_A, DIL_BLK, 2 * DIL_BLK), lambda g: (g, 0, 0, 0)),
        out_shape=jax.ShapeDtypeStruct((len(DILATIONS), N_HEADS_A, DIL_BLK, 2 * DIL_BLK), F32),
        name="dil_bias",
    )(rel_bias)
    ntile = DSA_NEAR + 2
    dsa = pl.pallas_call(
        _dsa_bias_kernel,
        grid=(ntile,),
        in_specs=[smem],
        out_specs=pl.BlockSpec((N_HEADS_B, 1, DSA_TQ, DSA_TQ), lambda g: (0, g, 0, 0)),
        out_shape=jax.ShapeDtypeStruct((N_HEADS_B, ntile, DSA_TQ, DSA_TQ), F32),
        name="dsa_bias",
    )(rel_bias)
    return dil, dsa


def _ffn_kernel(x_ref, g_ref, wg_ref, wu_ref, wo_ref, o_ref, h_ref, acc_ref):
    f = pl.program_id(1)

    @pl.when(f == 0)
    def _():
        x = x_ref[...]
        ms = jnp.mean(x * x, axis=-1, keepdims=True)
        h_ref[...] = (x * lax.rsqrt(ms + EPS) * g_ref[...]).astype(BF16)
        acc_ref[...] = jnp.zeros_like(acc_ref)

    h = h_ref[...]
    gate = jnp.dot(h, wg_ref[...], preferred_element_type=F32)
    up = jnp.dot(h, wu_ref[...], preferred_element_type=F32)
    act = (gate * jax.nn.sigmoid(gate) * up).astype(BF16)
    acc_ref[...] += jnp.dot(act, wo_ref[...], preferred_element_type=F32)

    @pl.when(f == pl.num_programs(1) - 1)
    def _():
        o_ref[...] = x_ref[...] + 0.5 * acc_ref[...]


def _ffn(x2, gain, w_in, w_out, tm=512, tf=512):
    m, d = x2.shape
    d_ff = w_out.shape[0]
    nf = d_ff // tf
    return pl.pallas_call(
        _ffn_kernel,
        grid=(m // tm, nf),
        in_specs=[
            pl.BlockSpec((tm, d), lambda i, f: (i, 0)),
            pl.BlockSpec((1, d), lambda i, f: (0, 0)),
            pl.BlockSpec((d, tf), lambda i, f: (0, f)),
            pl.BlockSpec((d, tf), lambda i, f: (0, f + nf)),
            pl.BlockSpec((tf, d), lambda i, f: (f, 0)),
        ],
        out_specs=pl.BlockSpec((tm, d), lambda i, f: (i, 0)),
        out_shape=jax.ShapeDtypeStruct((m, d), F32),
        scratch_shapes=[pltpu.VMEM((tm, d), BF16), pltpu.VMEM((tm, d), F32)],
        compiler_params=_cparams(("parallel", "arbitrary")),
        name="ffn",
    )(x2, gain.reshape(1, d), w_in, w_in, w_out)


PROJ_BLK = 1024
PROJ_NBLK = 7
PROJ_NA = 3
_NORM_BLOCKS = (0, 1, 3, 4)


def _proj_kernel(x_ref, g_ref, w_ref, widx_ref, hg_ref, oa_ref, ob_ref, idx_ref, h_ref):
    j = pl.program_id(1)

    @pl.when(j == 0)
    def _():
        x = x_ref[...]
        ms = jnp.mean(x * x, axis=-1, keepdims=True)
        h_ref[...] = (x * lax.rsqrt(ms + EPS) * g_ref[...]).astype(BF16)

    h = h_ref[...]
    y = jnp.dot(h, w_ref[...], preferred_element_type=F32)
    is_norm = functools.reduce(jnp.logical_or, [j == b for b in _NORM_BLOCKS])

    def emit(dst_ref):
        @pl.when(is_norm)
        def _():
            for hd in range(PROJ_BLK // HEAD_DIM):
                sl = slice(hd * HEAD_DIM, (hd + 1) * HEAD_DIM)
                yh = y[:, sl]
                ms = jnp.mean(yh * yh, axis=-1, keepdims=True)
                dst_ref[:, sl] = (yh * lax.rsqrt(ms + EPS) * hg_ref[0, :, sl]).astype(BF16)

        @pl.when(jnp.logical_not(is_norm))
        def _():
            dst_ref[...] = y.astype(BF16)

    pl.when(j < PROJ_NA)(lambda: emit(oa_ref))
    pl.when(j >= PROJ_NA)(lambda: emit(ob_ref))

    @pl.when(j == PROJ_NBLK - 1)
    def _():
        idx_ref[...] = jnp.dot(h, widx_ref[...], preferred_element_type=F32)


def _proj(x2, gain, w_main, w_idx, head_gain, tm=1024):
    m, d = x2.shape
    return pl.pallas_call(
        _proj_kernel,
        grid=(m // tm, PROJ_NBLK),
        in_specs=[
            pl.BlockSpec((tm, d), lambda i, j: (i, 0)),
            pl.BlockSpec((1, d), lambda i, j: (0, 0)),
            pl.BlockSpec((d, PROJ_BLK), lambda i, j: (0, j)),
            pl.BlockSpec((d, IDX_PAD), lambda i, j: (0, 0)),
            pl.BlockSpec((1, 1, PROJ_BLK), lambda i, j: (j, 0, 0)),
        ],
        out_specs=[
            pl.BlockSpec((tm, PROJ_BLK), lambda i, j: (i, jnp.minimum(j, PROJ_NA - 1))),
            pl.BlockSpec((tm, PROJ_BLK), lambda i, j: (i, jnp.maximum(j - PROJ_NA, 0))),
            pl.BlockSpec((tm, IDX_PAD), lambda i, j: (i, 0)),
        ],
        out_shape=[
            jax.ShapeDtypeStruct((m, PROJ_NA * PROJ_BLK), BF16),
            jax.ShapeDtypeStruct((m, (PROJ_NBLK - PROJ_NA) * PROJ_BLK), BF16),
            jax.ShapeDtypeStruct((m, IDX_PAD), F32),
        ],
        scratch_shapes=[pltpu.VMEM((tm, d), BF16)],
        compiler_params=_cparams(("parallel", "arbitrary")),
        name="proj",
    )(x2, gain.reshape(1, d), w_main, w_idx, head_gain)


_NT = (((1,), (1,)), ((), ()))


def _dilated_kernel(q_ref, kp_ref, kc_ref, vp_ref, vc_ref, b_ref, o_ref, lse_ref):
    n = pl.program_id(1)
    i = lax.broadcasted_iota(I32, (DIL_BLK, DIL_BLK), 0)
    j = lax.broadcasted_iota(I32, (DIL_BLK, DIL_BLK), 1)
    mask_prev = jnp.logical_and(j >= i, n > 0)
    mask_cur = j <= i
    lane = lax.broadcasted_iota(I32, (DIL_BLK, HEAD_DIM), 1)
    lse_all = jnp.zeros((DIL_BLK, HEAD_DIM), F32)
    for h in range(N_HEADS_A):
        sl = slice(h * HEAD_DIM, (h + 1) * HEAD_DIM)
        q = q_ref[0, :, sl]
        sp = lax.dot_general(q, kp_ref[0, :, sl], _NT, preferred_element_type=F32)
        sc = lax.dot_general(q, kc_ref[0, :, sl], _NT, preferred_element_type=F32)
        sp = jnp.where(mask_prev, sp + b_ref[0, h, :, :DIL_BLK], -jnp.inf)
        sc = jnp.where(mask_cur, sc + b_ref[0, h, :, DIL_BLK:], -jnp.inf)
        mx = jnp.maximum(jnp.max(sp, axis=-1, keepdims=True), jnp.max(sc, axis=-1, keepdims=True))
        pp = jnp.exp(sp - mx)
        pc = jnp.exp(sc - mx)
        den = jnp.sum(pp, axis=-1, keepdims=True) + jnp.sum(pc, axis=-1, keepdims=True)
        o = (jnp.dot(pp.astype(BF16), vp_ref[0, :, sl], preferred_element_type=F32)
             + jnp.dot(pc.astype(BF16), vc_ref[0, :, sl], preferred_element_type=F32))
        o_ref[0, :, sl] = (o / den).astype(BF16)
        lse_all = jnp.where(lane == h, mx + jnp.log(den), lse_all)
    lse_ref[0] = lse_all


def _dilated_branch(main_a, bias, branch):
    bsz, s, width = main_a.shape
    w = WIDTH_A
    dil = DILATIONS[branch]
    rows = s // dil
    nb = rows // DIL_BLK
    view = main_a.reshape(bsz, rows, dil * width)
    cur = lambda c: pl.BlockSpec((1, DIL_BLK, w), lambda b, n, r: (b, n, r * PROJ_NA + c))
    prev = lambda c: pl.BlockSpec(
        (1, DIL_BLK, w), lambda b, n, r: (b, jnp.maximum(n - 1, 0), r * PROJ_NA + c))
    o, lse = pl.pallas_call(
        _dilated_kernel,
        grid=(bsz, nb, dil),
        in_specs=[cur(0), prev(1), cur(1), prev(2), cur(2),
                  pl.BlockSpec((1, N_HEADS_A, DIL_BLK, 2 * DIL_BLK), lambda b, n, r: (branch, 0, 0, 0))],
        out_specs=[pl.BlockSpec((1, DIL_BLK, w), lambda b, n, r: (b, n, r)),
                   pl.BlockSpec((1, DIL_BLK, HEAD_DIM), lambda b, n, r: (b, n, r))],
        out_shape=[jax.ShapeDtypeStruct((bsz, rows, dil * w), BF16),
                   jax.ShapeDtypeStruct((bsz, rows, dil * HEAD_DIM), F32)],
        compiler_params=_cparams(("parallel", "parallel", "parallel")),
        name=f"dilated_{dil}",
    )(view, view, view, view, view, bias)
    return o.reshape(bsz * s, w), lse.reshape(bsz * s, HEAD_DIM)


def _dsa_schedule(s):
    qs, ks = [], []
    for i in range(s // DSA_TQ):
        last = (i * DSA_TQ + DSA_TQ - 1) // DSA_TK
        for k in range(last + 1):
            qs.append(i)
            ks.append(k)
    return np.asarray(qs, np.int32), np.asarray(ks, np.int32)


def _sortable(bits):
    return bits ^ (lax.shift_right_arithmetic(bits, 31) & 0x7FFFFFFF)


def _bit_transpose(words):
    a = list(words)
    mask, j = 0x0000FFFF, 16
    while j:
        k = 0
        while k < WORD:
            t = (a[k] ^ lax.shift_right_logical(a[k + j], j)) & mask
            a[k] = a[k] ^ t
            a[k + j] = a[k + j] ^ lax.shift_left(t, j)
            k = (k + j + 1) & ~j
        j >>= 1
        mask = (mask ^ (mask << j)) & 0xFFFFFFFF
    return a


def _dsa_score_kernel(qtab, ktab, qi_ref, ki_ref, w_ref, sc_ref, thr_ref, planes_ref, alive_ref):
    step = pl.program_id(1)
    qb = qtab[step]
    kc = ktab[step]
    kidx = ki_ref[0]
    w = w_ref[0] * IDX_SCALE
    acc = jnp.zeros((DSA_TK, DSA_TQ), F32)
    for h in range(N_IDX_HEADS):
        d = lax.dot_general(kidx, qi_ref[0, :, h * IDX_DIM:(h + 1) * IDX_DIM], _NT,
                            preferred_element_type=F32)
        acc = acc + jnp.maximum(d, 0.0) * w[h:h + 1, :]
    srow = lax.broadcasted_iota(I32, (DSA_TK, DSA_TQ), 0)
    tcol = lax.broadcasted_iota(I32, (DSA_TK, DSA_TQ), 1)
    causal = srow - tcol <= qb * DSA_TQ - kc * DSA_TK
    acc = jnp.where(causal, acc, -jnp.inf)
    sc_ref[0] = acc
    ukey = _sortable(lax.bitcast_convert_type(acc, I32)) ^ INT_MIN
    planes = _bit_transpose([ukey[g * CHUNK_WORDS:(g + 1) * CHUNK_WORDS, :] for g in range(WORD)])
    row0 = pl.multiple_of(kc * CHUNK_WORDS, CHUNK_WORDS)
    for n in range(WORD):
        planes_ref[n, pl.ds(row0, CHUNK_WORDS), :] = planes[n]

    last = (qb * DSA_TQ + DSA_TQ - 1) // DSA_TK

    @pl.when(kc == last)
    def _():
        nwords = alive_ref.shape[0]
        rb = min(128, nwords)
        nrows = (last + 1) * CHUNK_WORDS
        nblk = (nrows + rb - 1) // rb
        tpos = qb * DSA_TQ + lax.broadcasted_iota(I32, (1, DSA_TQ), 1)
        want0 = jnp.minimum(tpos + 1, TOPK_MAX)
        rows = lambda blk: pl.ds(pl.multiple_of(blk * rb, rb), rb)

        def init(blk, carry):
            ridx = blk * rb + lax.broadcasted_iota(I32, (rb, DSA_TQ), 0)
            alive_ref[rows(blk), :] = jnp.where(ridx < nrows, -1, 0)
            return carry
        lax.fori_loop(0, nblk, init, 0)

        def bit_step(n, carry):
            r, want = carry

            def count(blk, cnt):
                hit = lax.population_count(alive_ref[rows(blk), :] & planes_ref[n, rows(blk), :])
                return cnt + hit.reshape(rb // 8, 8, DSA_TQ).sum(axis=0)
            cnt = lax.fori_loop(0, nblk, count, jnp.zeros((8, DSA_TQ), I32)).sum(axis=0, keepdims=True)
            take = cnt >= want
            r = r | jnp.where(take, lax.shift_left(jnp.int32(1), 31 - n), 0)
            want = jnp.where(take, want, want - cnt)
            flip = jnp.where(take, 0, -1)

            def update(blk, c):
                alive_ref[rows(blk), :] = alive_ref[rows(blk), :] & (planes_ref[n, rows(blk), :] ^ flip)
                return c
            lax.fori_loop(0, nblk, update, 0)
            return r, want

        r, _ = lax.fori_loop(0, WORD, bit_step, (jnp.zeros((1, DSA_TQ), I32), want0))
        thr_ref[0] = lax.bitcast_convert_type(_sortable(r ^ INT_MIN), F32)


def _dsa_attn_kernel(qtab, ktab, sc_ref, thr_ref, q_ref, k_ref, vt_ref, b_ref, o_ref,
                     acc_ref, m_ref, l_ref, s_ref, mask_ref, cm_ref):
    step = pl.program_id(1)
    qb = qtab[step]
    kc = ktab[step]

    @pl.when(kc == 0)
    def _():
        acc_ref[...] = jnp.zeros_like(acc_ref)
        m_ref[...] = jnp.full_like(m_ref, NEG_BIG)
        l_ref[...] = jnp.zeros_like(l_ref)

    mask_ref[...] = jnp.where(sc_ref[0] >= thr_ref[0], 0.0, -jnp.inf)
    tiles = [jnp.clip(qb - (kc * DSA_HALVES + hh), -1, DSA_NEAR) + 1 for hh in range(DSA_HALVES)]
    far = qb - (kc * DSA_HALVES + DSA_HALVES - 1) >= DSA_NEAR

    def logits(h):
        sl = slice(h * HEAD_DIM, (h + 1) * HEAD_DIM)
        s = lax.dot_general(k_ref[0, :, sl], q_ref[0, :, sl], _NT, preferred_element_type=F32)
        return s + mask_ref[...]

    @pl.when(far)
    def _():
        for h in range(N_HEADS_B):
            s = logits(h)
            s_ref[h] = s
            cm_ref[h] = jnp.max(s, axis=0, keepdims=True) + b_ref[h, DSA_NEAR + 1, 0:1, :]

    @pl.when(jnp.logical_not(far))
    def _():
        for h in range(N_HEADS_B):
            bias = jnp.concatenate([b_ref[h, tiles[hh]] for hh in range(DSA_HALVES)], axis=0)
            s = logits(h) + bias
            s_ref[h] = s
            cm_ref[h] = jnp.max(s, axis=0, keepdims=True)

    for h in range(N_HEADS_B):
        m_old = m_ref[h]
        m_new = jnp.maximum(m_old, cm_ref[h])
        alpha = jnp.exp2(m_old - m_new)
        m_ref[h] = m_new
        shift = jnp.where(far, m_new - b_ref[h, DSA_NEAR + 1, 0:1, :], m_new)
        p = jnp.exp2(s_ref[h] - shift)
        l_ref[h] = alpha * l_ref[h] + jnp.sum(p, axis=0, keepdims=True)
        pv = jnp.dot(vt_ref[0, h * HEAD_DIM:(h + 1) * HEAD_DIM, :], p.astype(BF16),
                     preferred_element_type=F32)
        acc_ref[h] = alpha * acc_ref[h] + pv

    last = (qb * DSA_TQ + DSA_TQ - 1) // DSA_TK

    @pl.when(kc == last)
    def _():
        for h in range(N_HEADS_B):
            o_ref[0, h * HEAD_DIM:(h + 1) * HEAD_DIM, :] = (acc_ref[h] / l_ref[h]).astype(BF16)


def _dsa(main_b, ki, wi_t, bias):
    bsz, s, _ = main_b.shape
    w = WIDTH_B
    qs, ks = _dsa_schedule(s)
    npairs = len(qs)
    scores, thr = pl.pallas_call(
        _dsa_score_kernel,
        grid_spec=pltpu.PrefetchScalarGridSpec(
            num_scalar_prefetch=2,
            grid=(bsz, npairs),
            in_specs=[
                pl.BlockSpec((1, DSA_TQ, N_IDX_HEADS * IDX_DIM), lambda b, t, qt, kt: (b, qt[t], 3)),
                pl.BlockSpec((1, DSA_TK, IDX_DIM), lambda b, t, qt, kt: (b, kt[t], 0)),
                pl.BlockSpec((1, N_IDX_HEADS, DSA_TQ), lambda b, t, qt, kt: (b, 0, qt[t])),
            ],
            out_specs=[
                pl.BlockSpec((1, DSA_TK, DSA_TQ), lambda b, t, qt, kt: (b, kt[t], qt[t])),
                pl.BlockSpec((1, 1, DSA_TQ), lambda b, t, qt, kt: (b, 0, qt[t])),
            ],
            scratch_shapes=[pltpu.VMEM((WORD, s // WORD, DSA_TQ), I32),
                            pltpu.VMEM((s // WORD, DSA_TQ), I32)],
        ),
        out_shape=[jax.ShapeDtypeStruct((bsz, s, s), F32),
                   jax.ShapeDtypeStruct((bsz, 1, s), F32)],
        compiler_params=_cparams(("arbitrary", "arbitrary")),
        name="dsa_scores",
    )(qs, ks, main_b, ki, wi_t)

    vt = jnp.swapaxes(main_b[:, :, 2 * PROJ_BLK:3 * PROJ_BLK], 1, 2)
    ntile = bias.shape[1]
    out_t = pl.pallas_call(
        _dsa_attn_kernel,
        grid_spec=pltpu.PrefetchScalarGridSpec(
            num_scalar_prefetch=2,
            grid=(bsz, npairs),
            in_specs=[
                pl.BlockSpec((1, DSA_TK, DSA_TQ), lambda b, t, qt, kt: (b, kt[t], qt[t])),
                pl.BlockSpec((1, 1, DSA_TQ), lambda b, t, qt, kt: (b, 0, qt[t])),
                pl.BlockSpec((1, DSA_TQ, w), lambda b, t, qt, kt: (b, qt[t], 0)),
                pl.BlockSpec((1, DSA_TK, w), lambda b, t, qt, kt: (b, kt[t], 1)),
                pl.BlockSpec((1, w, DSA_TK), lambda b, t, qt, kt: (b, 0, kt[t])),
                pl.BlockSpec((N_HEADS_B, ntile, DSA_TQ, DSA_TQ), lambda b, t, qt, kt: (0, 0, 0, 0),
                             pipeline_mode=pl.Buffered(1)),
            ],
            out_specs=pl.BlockSpec((1, w, DSA_TQ), lambda b, t, qt, kt: (b, 0, qt[t])),
            scratch_shapes=[pltpu.VMEM((N_HEADS_B, HEAD_DIM, DSA_TQ), F32),
                            pltpu.VMEM((N_HEADS_B, 1, DSA_TQ), F32),
                            pltpu.VMEM((N_HEADS_B, 1, DSA_TQ), F32),
                            pltpu.VMEM((N_HEADS_B, DSA_TK, DSA_TQ), F32),
                            pltpu.VMEM((DSA_TK, DSA_TQ), F32),
                            pltpu.VMEM((N_HEADS_B, 1, DSA_TQ), F32)],
        ),
        out_shape=jax.ShapeDtypeStruct((bsz, w, s), BF16),
        compiler_params=_cparams(("arbitrary", "arbitrary")),
        name="dsa_attn",
    )(qs, ks, scores, thr, main_b, main_b, vt, bias)
    return jnp.swapaxes(out_t, 1, 2).reshape(bsz * s, w)


def _out_kernel(x_ref, o1_ref, o2_ref, o3_ref, l1_ref, l2_ref, l3_ref, mb_ref, wa_ref, wb_ref, o_ref):
    l1, l2, l3 = l1_ref[...], l2_ref[...], l3_ref[...]
    mx = jnp.maximum(jnp.maximum(l1, l2), l3)
    e1, e2, e3 = jnp.exp(l1 - mx), jnp.exp(l2 - mx), jnp.exp(l3 - mx)
    inv = 1.0 / (e1 + e2 + e3)
    w1, w2, w3 = e1 * inv, e2 * inv, e3 * inv
    acc = x_ref[...] + jnp.dot(mb_ref[...], wb_ref[...], preferred_element_type=F32)
    for h in range(N_HEADS_A):
        sl = slice(h * HEAD_DIM, (h + 1) * HEAD_DIM)
        mix = (w1[:, h:h + 1] * o1_ref[:, sl].astype(F32)
               + w2[:, h:h + 1] * o2_ref[:, sl].astype(F32)
               + w3[:, h:h + 1] * o3_ref[:, sl].astype(F32))
        acc = acc + jnp.dot(mix.astype(BF16), wa_ref[sl, :], preferred_element_type=F32)
    o_ref[...] = acc


def _out_proj(x2, outs, lses, mix_b, w_out, tm=512):
    m, d = x2.shape
    row = lambda width: pl.BlockSpec((tm, width), lambda i: (i, 0))
    return pl.pallas_call(
        _out_kernel,
        grid=(m // tm,),
        in_specs=[row(d), row(WIDTH_A), row(WIDTH_A), row(WIDTH_A),
                  row(HEAD_DIM), row(HEAD_DIM), row(HEAD_DIM), row(WIDTH_B),
                  pl.BlockSpec((WIDTH_A, d), lambda i: (0, 0)),
                  pl.BlockSpec((WIDTH_B, d), lambda i: (1, 0))],
        out_specs=row(d),
        out_shape=jax.ShapeDtypeStruct((m, d), F32),
        compiler_params=_cparams(("parallel",)),
        name="out_proj",
    )(x2, *outs, *lses, mix_b, w_out, w_out)


def kernel(x, rel_bias, norm_ffn1, w_ffn1_in, w_ffn1_out, norm_mix, w_in, q_norm_a, k_norm_a,
           q_norm_b, k_norm_b, w_out, norm_ffn2, w_ffn2_in, w_ffn2_out):
    bsz, s, d = x.shape
    m = bsz * s
    depth = norm_ffn1.shape[0]
    scale = HEAD_DIM ** -0.5
    n_main = PROJ_NBLK * PROJ_BLK
    dil_bias, dsa_bias = _bias_tiles(rel_bias)
    x2 = x.reshape(m, d)
    ones = jnp.ones((PROJ_BLK,), F32)
    tile8 = lambda g: jnp.tile(g, PROJ_BLK // HEAD_DIM)
    for l in range(depth):
        x2 = _ffn(x2, norm_ffn1[l], w_ffn1_in[l].astype(BF16), w_ffn1_out[l].astype(BF16))

        w_main = w_in[l, :, :n_main].astype(BF16)
        w_idx = jnp.pad(w_in[l, :, n_main:], ((0, 0), (0, IDX_PAD - IDX_DIM - N_IDX_HEADS))).astype(BF16)
        head_gain = jnp.stack([tile8(q_norm_a[l] * scale), tile8(k_norm_a[l]), ones,
                               tile8(q_norm_b[l] * (scale * LOG2E)), tile8(k_norm_b[l]), ones, ones])
        main_a, main_b, idx = _proj(x2, norm_mix[l], w_main, w_idx,
                                    head_gain.reshape(PROJ_NBLK, 1, PROJ_BLK))
        main_a = main_a.reshape(bsz, s, PROJ_NA * PROJ_BLK)
        main_b = main_b.reshape(bsz, s, (PROJ_NBLK - PROJ_NA) * PROJ_BLK)
        idx = idx.reshape(bsz, s, IDX_PAD)
        ki = idx[:, :, :IDX_DIM].astype(BF16)
        wi_t = jnp.swapaxes(idx[:, :, IDX_DIM:IDX_DIM + N_IDX_HEADS], 1, 2)

        outs, lses = zip(*[_dilated_branch(main_a, dil_bias, br) for br in range(len(DILATIONS))])
        mix_b = _dsa(main_b, ki, wi_t, dsa_bias)

        x2 = _out_proj(x2, outs, lses, mix_b, w_out[l].astype(BF16))
        x2 = _ffn(x2, norm_ffn2[l], w_ffn2_in[l].astype(BF16), w_ffn2_out[l].astype(BF16))
    return x2.reshape(bsz, s, d)
```

```python
import functools
import math

import jax
import jax.numpy as jnp
import numpy as np
from jax import lax
from jax.experimental import pallas as pl
from jax.experimental.pallas import tpu as pltpu

F32 = jnp.float32
BF16 = jnp.bfloat16
I32 = jnp.int32

HEAD_DIM = 128
N_HEADS_A = 8
N_HEADS_B = 8
WIDTH_A = N_HEADS_A * HEAD_DIM
WIDTH_B = N_HEADS_B * HEAD_DIM
DILATIONS = (1, 4, 16)
DIL_BLK = 128
N_IDX_HEADS = 16
IDX_DIM = 64
TOPK_MAX = 256
N_BUCKETS = 32
REL_MAX_DIST = 2048
EPS = 1e-6
IDX_PAD = 128
IDX_SCALE = (IDX_DIM ** -0.5) * (N_IDX_HEADS ** -0.5)
LOG2E = math.log2(math.e)

VMEM_LIMIT = 56 * 1024 * 1024

DSA_TQ = 256
DSA_TK = 512
DSA_HALVES = DSA_TK // DSA_TQ
WORD = 32
CHUNK_WORDS = DSA_TK // WORD
NEG_BIG = -1e30
INT_MIN = -2 ** 31


def _near_tiles():
    exact = N_BUCKETS // 2
    j = 1
    while True:
        d = j * DSA_TQ - (DSA_TQ - 1)
        b = exact + int(math.log(d / exact) / math.log(REL_MAX_DIST / exact) * (N_BUCKETS - exact) - 0.02)
        if d >= exact and b >= N_BUCKETS - 1:
            return j
        j += 1


DSA_NEAR = _near_tiles()


def _cparams(sem):
    return pltpu.CompilerParams(dimension_semantics=sem, vmem_limit_bytes=VMEM_LIMIT)


def _rel_bucket(dist):
    exact = N_BUCKETS // 2
    df = jnp.maximum(dist, 1).astype(F32)
    large = exact + (jnp.log(df / exact) / math.log(REL_MAX_DIST / exact)
                     * (N_BUCKETS - exact)).astype(I32)
    large = jnp.minimum(large, N_BUCKETS - 1)
    return jnp.where(dist < exact, dist, large)


def _lookup(tab_ref, bucket, col):
    val = jnp.zeros(bucket.shape, F32)
    for b in range(N_BUCKETS):
        val = jnp.where(bucket == b, tab_ref[b, col], val)
    return val


def _dil_bias_kernel(tab_ref, o_ref):
    dil = jnp.left_shift(1, 2 * pl.program_id(0))
    i = lax.broadcasted_iota(I32, (DIL_BLK, 2 * DIL_BLK), 0)
    j = lax.broadcasted_iota(I32, (DIL_BLK, 2 * DIL_BLK), 1)
    bucket = _rel_bucket(jnp.clip(DIL_BLK + i - j, 0, DIL_BLK) * dil)
    for h in range(N_HEADS_A):
        o_ref[0, h] = _lookup(tab_ref, bucket, h)


def _dsa_bias_kernel(tab_ref, o_ref):
    jj = pl.program_id(0)
    s = lax.broadcasted_iota(I32, (DSA_TQ, DSA_TQ), 0)
    t = lax.broadcasted_iota(I32, (DSA_TQ, DSA_TQ), 1)
    bucket = _rel_bucket(jnp.maximum((jj - 1) * DSA_TQ + t - s, 0))
    for h in range(N_HEADS_B):
        o_ref[h, 0] = _lookup(tab_ref, bucket, N_HEADS_A + h) * LOG2E


def _bias_tiles(rel_bias):
    smem = pl.BlockSpec(memory_space=pltpu.SMEM)
    dil = pl.pallas_call(
        _dil_bias_kernel,
        grid=(len(DILATIONS),),
        in_specs=[smem],
        out_specs=pl.BlockSpec((1, N_HEADS_A, DIL_BLK, 2 * DIL_BLK), lambda g: (g, 0, 0, 0)),
        out_shape=jax.ShapeDtypeStruct((len(DILATIONS), N_HEADS_A, DIL_BLK, 2 * DIL_BLK), F32),
        name="dil_bias",
    )(rel_bias)
    ntile = DSA_NEAR + 2
    dsa = pl.pallas_call(
        _dsa_bias_kernel,
        grid=(ntile,),
        in_specs=[smem],
        out_specs=pl.BlockSpec((N_HEADS_B, 1, DSA_TQ, DSA_TQ), lambda g: (0, g, 0, 0)),
        out_shape=jax.ShapeDtypeStruct((N_HEADS_B, ntile, DSA_TQ, DSA_TQ), F32),
        name="dsa_bias",
    )(rel_bias)
    return dil, dsa


def _ffn_kernel(x_ref, g_ref, wg_ref, wu_ref, wo_ref, o_ref, h_ref, acc_ref):
    f = pl.program_id(1)

    @pl.when(f == 0)
    def _():
        x = x_ref[...]
        ms = jnp.mean(x * x, axis=-1, keepdims=True)
        h_ref[...] = (x * lax.rsqrt(ms + EPS) * g_ref[...]).astype(BF16)
        acc_ref[...] = jnp.zeros_like(acc_ref)

    h = h_ref[...]
    gate = jnp.dot(h, wg_ref[...], preferred_element_type=F32)
    up = jnp.dot(h, wu_ref[...], preferred_element_type=F32)
    act = (gate * jax.nn.sigmoid(gate) * up).astype(BF16)
    acc_ref[...] += jnp.dot(act, wo_ref[...], preferred_element_type=F32)

    @pl.when(f == pl.num_programs(1) - 1)
    def _():
        o_ref[...] = x_ref[...] + 0.5 * acc_ref[...]


def _ffn(x2, gain, w_in, w_out, tm=512, tf=512):
    m, d = x2.shape
    d_ff = w_out.shape[0]
    nf = d_ff // tf
    return pl.pallas_call(
        _ffn_kernel,
        grid=(m // tm, nf),
        in_specs=[
            pl.BlockSpec((tm, d), lambda i, f: (i, 0)),
            pl.BlockSpec((1, d), lambda i, f: (0, 0)),
            pl.BlockSpec((d, tf), lambda i, f: (0, f)),
            pl.BlockSpec((d, tf), lambda i, f: (0, f + nf)),
            pl.BlockSpec((tf, d), lambda i, f: (f, 0)),
        ],
        out_specs=pl.BlockSpec((tm, d), lambda i, f: (i, 0)),
        out_shape=jax.ShapeDtypeStruct((m, d), F32),
        scratch_shapes=[pltpu.VMEM((tm, d), BF16), pltpu.VMEM((tm, d), F32)],
        compiler_params=_cparams(("parallel", "arbitrary")),
        name="ffn",
    )(x2, gain.reshape(1, d), w_in, w_in, w_out)


PROJ_BLK = 1024
PROJ_NBLK = 7
PROJ_NA = 3
_NORM_BLOCKS = (0, 1, 3, 4)


def _proj_kernel(x_ref, g_ref, w_ref, widx_ref, hg_ref, oa_ref, ob_ref, idx_ref, h_ref):
    j = pl.program_id(1)

    @pl.when(j == 0)
    def _():
        x = x_ref[...]
        ms = jnp.mean(x * x, axis=-1, keepdims=True)
        h_ref[...] = (x * lax.rsqrt(ms + EPS) * g_ref[...]).astype(BF16)

    h = h_ref[...]
    y = jnp.dot(h, w_ref[...], preferred_element_type=F32)
    is_norm = functools.reduce(jnp.logical_or, [j == b for b in _NORM_BLOCKS])

    def emit(dst_ref):
        @pl.when(is_norm)
        def _():
            for hd in range(PROJ_BLK // HEAD_DIM):
                sl = slice(hd * HEAD_DIM, (hd + 1) * HEAD_DIM)
                yh = y[:, sl]
                ms = jnp.mean(yh * yh, axis=-1, keepdims=True)
                dst_ref[:, sl] = (yh * lax.rsqrt(ms + EPS) * hg_ref[0, :, sl]).astype(BF16)

        @pl.when(jnp.logical_not(is_norm))
        def _():
            dst_ref[...] = y.astype(BF16)

    pl.when(j < PROJ_NA)(lambda: emit(oa_ref))
    pl.when(j >= PROJ_NA)(lambda: emit(ob_ref))

    @pl.when(j == PROJ_NBLK - 1)
    def _():
        idx_ref[...] = jnp.dot(h, widx_ref[...], preferred_element_type=F32)


def _proj(x2, gain, w_main, w_idx, head_gain, tm=1024):
    m, d = x2.shape
    return pl.pallas_call(
        _proj_kernel,
        grid=(m // tm, PROJ_NBLK),
        in_specs=[
            pl.BlockSpec((tm, d), lambda i, j: (i, 0)),
            pl.BlockSpec((1, d), lambda i, j: (0, 0)),
            pl.BlockSpec((d, PROJ_BLK), lambda i, j: (0, j)),
            pl.BlockSpec((d, IDX_PAD), lambda i, j: (0, 0)),
            pl.BlockSpec((1, 1, PROJ_BLK), lambda i, j: (j, 0, 0)),
        ],
        out_specs=[
            pl.BlockSpec((tm, PROJ_BLK), lambda i, j: (i, jnp.minimum(j, PROJ_NA - 1))),
            pl.BlockSpec((tm, PROJ_BLK), lambda i, j: (i, jnp.maximum(j - PROJ_NA, 0))),
            pl.BlockSpec((tm, IDX_PAD), lambda i, j: (i, 0)),
        ],
        out_shape=[
            jax.ShapeDtypeStruct((m, PROJ_NA * PROJ_BLK), BF16),
            jax.ShapeDtypeStruct((m, (PROJ_NBLK - PROJ_NA) * PROJ_BLK), BF16),
            jax.ShapeDtypeStruct((m, IDX_PAD), F32),
        ],
        scratch_shapes=[pltpu.VMEM((tm, d), BF16)],
        compiler_params=_cparams(("parallel", "arbitrary")),
        name="proj",
    )(x2, gain.reshape(1, d), w_main, w_idx, head_gain)


_NT = (((1,), (1,)), ((), ()))


DIL_SPAN = DIL_BLK * max(DILATIONS)
DIL_G = 2
DIL_UNITS = DIL_SPAN // DIL_BLK
DIL_UNROLL = 4


def _dilated_kernel(q_ref, kp_ref, kc_ref, vp_ref, vc_ref, b_ref, o_ref, qf, kf, vf, ob, lse_ref):
    n = pl.program_id(1)
    h0 = pl.program_id(2) * DIL_G
    for g in range(DIL_G):
        sl = slice(g * HEAD_DIM, (g + 1) * HEAD_DIM)
        qf[g] = q_ref[0, :, sl].astype(F32)
        kf[g, :DIL_SPAN] = kp_ref[0, :, sl].astype(F32)
        kf[g, DIL_SPAN:] = kc_ref[0, :, sl].astype(F32)
        vf[g, :DIL_SPAN] = vp_ref[0, :, sl].astype(F32)
        vf[g, DIL_SPAN:] = vc_ref[0, :, sl].astype(F32)
    i = lax.broadcasted_iota(I32, (DIL_BLK, 2 * DIL_BLK), 0)
    j = lax.broadcasted_iota(I32, (DIL_BLK, 2 * DIL_BLK), 1)
    band = jnp.logical_and(j >= i, j <= i + DIL_BLK)
    first = j >= DIL_BLK

    for br, dil in enumerate(DILATIONS):
        span = DIL_BLK * dil

        def unit(idx, carry, br=br, dil=dil, span=span):
            blk = idx // dil
            base = blk * span + idx % dil
            valid = jnp.logical_and(band, jnp.logical_or(first, n * (DIL_SPAN // span) + blk > 0))
            qrows = pl.ds(base, DIL_BLK, stride=dil)
            krows = pl.ds(DIL_SPAN + base - span, 2 * DIL_BLK, stride=dil)
            for g in range(DIL_G):
                q = qf[g, qrows, :].astype(BF16)
                k = kf[g, krows, :].astype(BF16)
                v = vf[g, krows, :].astype(BF16)
                s = lax.dot_general(q, k, _NT, preferred_element_type=F32) + b_ref[br, h0 + g]
                s = jnp.where(valid, s, -jnp.inf)
                mx = jnp.max(s, axis=-1, keepdims=True)
                p = jnp.exp(s - mx)
                den = jnp.sum(p, axis=-1, keepdims=True)
                ob[br, g, qrows, :] = jnp.dot(p.astype(BF16), v, preferred_element_type=F32) / den
                lse_ref[br, g, qrows, :] = jnp.broadcast_to(mx + jnp.log(den), (DIL_BLK, HEAD_DIM))
            return carry

        lax.fori_loop(0, DIL_UNITS, unit, 0, unroll=DIL_UNROLL)

    for g in range(DIL_G):
        l1, l2, l3 = lse_ref[0, g], lse_ref[1, g], lse_ref[2, g]
        mx = jnp.maximum(jnp.maximum(l1, l2), l3)
        e1, e2, e3 = jnp.exp(l1 - mx), jnp.exp(l2 - mx), jnp.exp(l3 - mx)
        mix = (e1 * ob[0, g] + e2 * ob[1, g] + e3 * ob[2, g]) / (e1 + e2 + e3)
        o_ref[0, :, g * HEAD_DIM:(g + 1) * HEAD_DIM] = mix.astype(BF16)


def _dilated(main_a, bias):
    bsz, s, _ = main_a.shape
    gw = DIL_G * HEAD_DIM
    ng = WIDTH_A // gw
    cur = lambda c: pl.BlockSpec((1, DIL_SPAN, gw), lambda b, n, g: (b, n, c * ng + g))
    prev = lambda c: pl.BlockSpec((1, DIL_SPAN, gw), lambda b, n, g: (b, jnp.maximum(n - 1, 0), c * ng + g))
    out = pl.pallas_call(
        _dilated_kernel,
        grid=(bsz, s // DIL_SPAN, ng),
        in_specs=[cur(0), prev(1), cur(1), prev(2), cur(2),
                  pl.BlockSpec(bias.shape, lambda b, n, g: (0, 0, 0, 0), pipeline_mode=pl.Buffered(1))],
        out_specs=pl.BlockSpec((1, DIL_SPAN, gw), lambda b, n, g: (b, n, g)),
        out_shape=jax.ShapeDtypeStruct((bsz, s, WIDTH_A), BF16),
        scratch_shapes=[pltpu.VMEM((DIL_G, DIL_SPAN, HEAD_DIM), F32),
                        pltpu.VMEM((DIL_G, 2 * DIL_SPAN, HEAD_DIM), F32),
                        pltpu.VMEM((DIL_G, 2 * DIL_SPAN, HEAD_DIM), F32),
                        pltpu.VMEM((len(DILATIONS), DIL_G, DIL_SPAN, HEAD_DIM), F32),
                        pltpu.VMEM((len(DILATIONS), DIL_G, DIL_SPAN, HEAD_DIM), F32)],
        compiler_params=_cparams(("parallel", "parallel", "parallel")),
        name="dilated",
    )(main_a, main_a, main_a, main_a, main_a, bias)
    return out.reshape(bsz * s, WIDTH_A)


def _dsa_schedule(s):
    qs, ks = [], []
    for i in range(s // DSA_TQ):
        last = (i * DSA_TQ + DSA_TQ - 1) // DSA_TK
        for k in range(last + 1):
            qs.append(i)
            ks.append(k)
    return np.asarray(qs, np.int32), np.asarray(ks, np.int32)


def _sortable(bits):
    return bits ^ (lax.shift_right_arithmetic(bits, 31) & 0x7FFFFFFF)


def _bit_transpose(words):
    a = list(words)
    mask, j = 0x0000FFFF, 16
    while j:
        k = 0
        while k < WORD:
            t = (a[k] ^ lax.shift_right_logical(a[k + j], j)) & mask
            a[k] = a[k] ^ t
            a[k + j] = a[k + j] ^ lax.shift_left(t, j)
            k = (k + j + 1) & ~j
        j >>= 1
        mask = (mask ^ (mask << j)) & 0xFFFFFFFF
    return a


def _dsa_score_kernel(qtab, ktab, qi_ref, ki_ref, w_ref, sc_ref, thr_ref, planes_ref, alive_ref):
    step = pl.program_id(1)
    qb = qtab[step]
    kc = ktab[step]
    kidx = ki_ref[0]
    w = w_ref[0] * IDX_SCALE
    acc = jnp.zeros((DSA_TK, DSA_TQ), F32)
    for h in range(N_IDX_HEADS):
        d = lax.dot_general(kidx, qi_ref[0, :, h * IDX_DIM:(h + 1) * IDX_DIM], _NT,
                            preferred_element_type=F32)
        acc = acc + jnp.maximum(d, 0.0) * w[h:h + 1, :]
    srow = lax.broadcasted_iota(I32, (DSA_TK, DSA_TQ), 0)
    tcol = lax.broadcasted_iota(I32, (DSA_TK, DSA_TQ), 1)
    causal = srow - tcol <= qb * DSA_TQ - kc * DSA_TK
    acc = jnp.where(causal, acc, -jnp.inf)
    sc_ref[0] = acc
    ukey = _sortable(lax.bitcast_convert_type(acc, I32)) ^ INT_MIN
    planes = _bit_transpose([ukey[g * CHUNK_WORDS:(g + 1) * CHUNK_WORDS, :] for g in range(WORD)])
    row0 = pl.multiple_of(kc * CHUNK_WORDS, CHUNK_WORDS)
    for n in range(WORD):
        planes_ref[n, pl.ds(row0, CHUNK_WORDS), :] = planes[n]

    last = (qb * DSA_TQ + DSA_TQ - 1) // DSA_TK

    @pl.when(kc == last)
    def _():
        nwords = alive_ref.shape[0]
        rb = min(128, nwords)
        nrows = (last + 1) * CHUNK_WORDS
        nblk = (nrows + rb - 1) // rb
        tpos = qb * DSA_TQ + lax.broadcasted_iota(I32, (1, DSA_TQ), 1)
        want0 = jnp.minimum(tpos + 1, TOPK_MAX)
        rows = lambda blk: pl.ds(pl.multiple_of(blk * rb, rb), rb)

        def init(blk, carry):
            ridx = blk * rb + lax.broadcasted_iota(I32, (rb, DSA_TQ), 0)
            alive_ref[rows(blk), :] = jnp.where(ridx < nrows, -1, 0)
            return carry
        lax.fori_loop(0, nblk, init, 0)

        def bit_step(n, carry):
            r, want = carry

            def count(blk, cnt):
                hit = lax.population_count(alive_ref[rows(blk), :] & planes_ref[n, rows(blk), :])
                return cnt + hit.reshape(rb // 8, 8, DSA_TQ).sum(axis=0)
            cnt = lax.fori_loop(0, nblk, count, jnp.zeros((8, DSA_TQ), I32)).sum(axis=0, keepdims=True)
            take = cnt >= want
            r = r | jnp.where(take, lax.shift_left(jnp.int32(1), 31 - n), 0)
            want = jnp.where(take, want, want - cnt)
            flip = jnp.where(take, 0, -1)

            def update(blk, c):
                alive_ref[rows(blk), :] = alive_ref[rows(blk), :] & (planes_ref[n, rows(blk), :] ^ flip)
                return c
            lax.fori_loop(0, nblk, update, 0)
            return r, want

        r, _ = lax.fori_loop(0, WORD, bit_step, (jnp.zeros((1, DSA_TQ), I32), want0))
        thr_ref[0] = lax.bitcast_convert_type(_sortable(r ^ INT_MIN), F32)


def _dsa_attn_kernel(qtab, ktab, sc_ref, thr_ref, q_ref, k_ref, vt_ref, b_ref, o_ref,
                     acc_ref, m_ref, l_ref, s_ref, mask_ref, cm_ref):
    step = pl.program_id(1)
    qb = qtab[step]
    kc = ktab[step]

    @pl.when(kc == 0)
    def _():
        acc_ref[...] = jnp.zeros_like(acc_ref)
        m_ref[...] = jnp.full_like(m_ref, NEG_BIG)
        l_ref[...] = jnp.zeros_like(l_ref)

    mask_ref[...] = jnp.where(sc_ref[0] >= thr_ref[0], 0.0, -jnp.inf)
    tiles = [jnp.clip(qb - (kc * DSA_HALVES + hh), -1, DSA_NEAR) + 1 for hh in range(DSA_HALVES)]
    far = qb - (kc * DSA_HALVES + DSA_HALVES - 1) >= DSA_NEAR

    def logits(h):
        sl = slice(h * HEAD_DIM, (h + 1) * HEAD_DIM)
        s = lax.dot_general(k_ref[0, :, sl], q_ref[0, :, sl], _NT, preferred_element_type=F32)
        return s + mask_ref[...]

    @pl.when(far)
    def _():
        for h in range(N_HEADS_B):
            s = logits(h)
            s_ref[h] = s
            cm_ref[h] = jnp.max(s, axis=0, keepdims=True) + b_ref[h, DSA_NEAR + 1, 0:1, :]

    @pl.when(jnp.logical_not(far))
    def _():
        for h in range(N_HEADS_B):
            bias = jnp.concatenate([b_ref[h, tiles[hh]] for hh in range(DSA_HALVES)], axis=0)
            s = logits(h) + bias
            s_ref[h] = s
            cm_ref[h] = jnp.max(s, axis=0, keepdims=True)

    for h in range(N_HEADS_B):
        m_old = m_ref[h]
        m_new = jnp.maximum(m_old, cm_ref[h])
        alpha = jnp.exp2(m_old - m_new)
        m_ref[h] = m_new
        shift = jnp.where(far, m_new - b_ref[h, DSA_NEAR + 1, 0:1, :], m_new)
        p = jnp.exp2(s_ref[h] - shift)
        l_ref[h] = alpha * l_ref[h] + jnp.sum(p, axis=0, keepdims=True)
        pv = jnp.dot(vt_ref[0, h * HEAD_DIM:(h + 1) * HEAD_DIM, :], p.astype(BF16),
                     preferred_element_type=F32)
        acc_ref[h] = alpha * acc_ref[h] + pv

    last = (qb * DSA_TQ + DSA_TQ - 1) // DSA_TK

    @pl.when(kc == last)
    def _():
        for h in range(N_HEADS_B):
            o_ref[0, h * HEAD_DIM:(h + 1) * HEAD_DIM, :] = (acc_ref[h] / l_ref[h]).astype(BF16)


def _dsa(main_b, ki, wi_t, bias):
    bsz, s, _ = main_b.shape
    w = WIDTH_B
    qs, ks = _dsa_schedule(s)
    npairs = len(qs)
    scores, thr = pl.pallas_call(
        _dsa_score_kernel,
        grid_spec=pltpu.PrefetchScalarGridSpec(
            num_scalar_prefetch=2,
            grid=(bsz, npairs),
            in_specs=[
                pl.BlockSpec((1, DSA_TQ, N_IDX_HEADS * IDX_DIM), lambda b, t, qt, kt: (b, qt[t], 3)),
                pl.BlockSpec((1, DSA_TK, IDX_DIM), lambda b, t, qt, kt: (b, kt[t], 0)),
                pl.BlockSpec((1, N_IDX_HEADS, DSA_TQ), lambda b, t, qt, kt: (b, 0, qt[t])),
            ],
            out_specs=[
                pl.BlockSpec((1, DSA_TK, DSA_TQ), lambda b, t, qt, kt: (b, kt[t], qt[t])),
                pl.BlockSpec((1, 1, DSA_TQ), lambda b, t, qt, kt: (b, 0, qt[t])),
            ],
            scratch_shapes=[pltpu.VMEM((WORD, s // WORD, DSA_TQ), I32),
                            pltpu.VMEM((s // WORD, DSA_TQ), I32)],
        ),
        out_shape=[jax.ShapeDtypeStruct((bsz, s, s), F32),
                   jax.ShapeDtypeStruct((bsz, 1, s), F32)],
        compiler_params=_cparams(("arbitrary", "arbitrary")),
        name="dsa_scores",
    )(qs, ks, main_b, ki, wi_t)

    vt = jnp.swapaxes(main_b[:, :, 2 * PROJ_BLK:3 * PROJ_BLK], 1, 2)
    ntile = bias.shape[1]
    out_t = pl.pallas_call(
        _dsa_attn_kernel,
        grid_spec=pltpu.PrefetchScalarGridSpec(
            num_scalar_prefetch=2,
            grid=(bsz, npairs),
            in_specs=[
                pl.BlockSpec((1, DSA_TK, DSA_TQ), lambda b, t, qt, kt: (b, kt[t], qt[t])),
                pl.BlockSpec((1, 1, DSA_TQ), lambda b, t, qt, kt: (b, 0, qt[t])),
                pl.BlockSpec((1, DSA_TQ, w), lambda b, t, qt, kt: (b, qt[t], 0)),
                pl.BlockSpec((1, DSA_TK, w), lambda b, t, qt, kt: (b, kt[t], 1)),
                pl.BlockSpec((1, w, DSA_TK), lambda b, t, qt, kt: (b, 0, kt[t])),
                pl.BlockSpec((N_HEADS_B, ntile, DSA_TQ, DSA_TQ), lambda b, t, qt, kt: (0, 0, 0, 0),
                             pipeline_mode=pl.Buffered(1)),
            ],
            out_specs=pl.BlockSpec((1, w, DSA_TQ), lambda b, t, qt, kt: (b, 0, qt[t])),
            scratch_shapes=[pltpu.VMEM((N_HEADS_B, HEAD_DIM, DSA_TQ), F32),
                            pltpu.VMEM((N_HEADS_B, 1, DSA_TQ), F32),
                            pltpu.VMEM((N_HEADS_B, 1, DSA_TQ), F32),
                            pltpu.VMEM((N_HEADS_B, DSA_TK, DSA_TQ), F32),
                            pltpu.VMEM((DSA_TK, DSA_TQ), F32),
                            pltpu.VMEM((N_HEADS_B, 1, DSA_TQ), F32)],
        ),
        out_shape=jax.ShapeDtypeStruct((bsz, w, s), BF16),
        compiler_params=_cparams(("arbitrary", "arbitrary")),
        name="dsa_attn",
    )(qs, ks, scores, thr, main_b, main_b, vt, bias)
    return jnp.swapaxes(out_t, 1, 2).reshape(bsz * s, w)


def _out_kernel(x_ref, ma_ref, mb_ref, wa_ref, wb_ref, o_ref):
    o_ref[...] = (x_ref[...]
                  + jnp.dot(ma_ref[...], wa_ref[...], preferred_element_type=F32)
                  + jnp.dot(mb_ref[...], wb_ref[...], preferred_element_type=F32))


def _out_proj(x2, mix_a, mix_b, w_out, tm=512):
    m, d = x2.shape
    row = lambda width: pl.BlockSpec((tm, width), lambda i: (i, 0))
    return pl.pallas_call(
        _out_kernel,
        grid=(m // tm,),
        in_specs=[row(d), row(WIDTH_A), row(WIDTH_B),
                  pl.BlockSpec((WIDTH_A, d), lambda i: (0, 0)),
                  pl.BlockSpec((WIDTH_B, d), lambda i: (1, 0))],
        out_specs=row(d),
        out_shape=jax.ShapeDtypeStruct((m, d), F32),
        compiler_params=_cparams(("parallel",)),
        name="out_proj",
    )(x2, mix_a, mix_b, w_out, w_out)


def kernel(x, rel_bias, norm_ffn1, w_ffn1_in, w_ffn1_out, norm_mix, w_in, q_norm_a, k_norm_a,
           q_norm_b, k_norm_b, w_out, norm_ffn2, w_ffn2_in, w_ffn2_out):
    bsz, s, d = x.shape
    m = bsz * s
    depth = norm_ffn1.shape[0]
    scale = HEAD_DIM ** -0.5
    n_main = PROJ_NBLK * PROJ_BLK
    dil_bias, dsa_bias = _bias_tiles(rel_bias)
    x2 = x.reshape(m, d)
    ones = jnp.ones((PROJ_BLK,), F32)
    tile8 = lambda g: jnp.tile(g, PROJ_BLK // HEAD_DIM)
    for l in range(depth):
        x2 = _ffn(x2, norm_ffn1[l], w_ffn1_in[l].astype(BF16), w_ffn1_out[l].astype(BF16))

        w_main = w_in[l, :, :n_main].astype(BF16)
        w_idx = jnp.pad(w_in[l, :, n_main:], ((0, 0), (0, IDX_PAD - IDX_DIM - N_IDX_HEADS))).astype(BF16)
        head_gain = jnp.stack([tile8(q_norm_a[l] * scale), tile8(k_norm_a[l]), ones,
                               tile8(q_norm_b[l] * (scale * LOG2E)), tile8(k_norm_b[l]), ones, ones])
        main_a, main_b, idx = _proj(x2, norm_mix[l], w_main, w_idx,
                                    head_gain.reshape(PROJ_NBLK, 1, PROJ_BLK))
        main_a = main_a.reshape(bsz, s, PROJ_NA * PROJ_BLK)
        main_b = main_b.reshape(bsz, s, (PROJ_NBLK - PROJ_NA) * PROJ_BLK)
        idx = idx.reshape(bsz, s, IDX_PAD)
        ki = idx[:, :, :IDX_DIM].astype(BF16)
        wi_t = jnp.swapaxes(idx[:, :, IDX_DIM:IDX_DIM + N_IDX_HEADS], 1, 2)

        mix_a = _dilated(main_a, dil_bias)
        mix_b = _dsa(main_b, ki, wi_t, dsa_bias)

        x2 = _out_proj(x2, mix_a, mix_b, w_out[l].astype(BF16))
        x2 = _ffn(x2, norm_ffn2[l], w_ffn2_in[l].astype(BF16), w_ffn2_out[l].astype(BF16))
    return x2.reshape(bsz, s, d)
```

```python
import functools
import math

import jax
import jax.numpy as jnp
import numpy as np
from jax import lax
from jax.experimental import pallas as pl
from jax.experimental.pallas import tpu as pltpu

F32 = jnp.float32
BF16 = jnp.bfloat16
I32 = jnp.int32

HEAD_DIM = 128
N_HEADS_A = 8
N_HEADS_B = 8
WIDTH_A = N_HEADS_A * HEAD_DIM
WIDTH_B = N_HEADS_B * HEAD_DIM
DILATIONS = (1, 4, 16)
DIL_BLK = 128
N_IDX_HEADS = 16
IDX_DIM = 64
TOPK_MAX = 256
N_BUCKETS = 32
REL_MAX_DIST = 2048
EPS = 1e-6
IDX_PAD = 128
IDX_SCALE = (IDX_DIM ** -0.5) * (N_IDX_HEADS ** -0.5)
LOG2E = math.log2(math.e)

VMEM_LIMIT = 56 * 1024 * 1024

DSA_TQ = 256
DSA_TK = 512
DSA_HALVES = DSA_TK // DSA_TQ
WORD = 32
CHUNK_WORDS = DSA_TK // WORD
NEG_BIG = -1e30
DSA_SAFE_BOUND = 40.0
DSA_LOOKAHEAD = 2
INT_MIN = -2 ** 31


def _near_tiles():
    exact = N_BUCKETS // 2
    j = 1
    while True:
        d = j * DSA_TQ - (DSA_TQ - 1)
        b = exact + int(math.log(d / exact) / math.log(REL_MAX_DIST / exact) * (N_BUCKETS - exact) - 0.02)
        if d >= exact and b >= N_BUCKETS - 1:
            return j
        j += 1


DSA_NEAR = _near_tiles()


def _cparams(sem):
    return pltpu.CompilerParams(dimension_semantics=sem, vmem_limit_bytes=VMEM_LIMIT)


def _rel_bucket(dist):
    exact = N_BUCKETS // 2
    df = jnp.maximum(dist, 1).astype(F32)
    large = exact + (jnp.log(df / exact) / math.log(REL_MAX_DIST / exact)
                     * (N_BUCKETS - exact)).astype(I32)
    large = jnp.minimum(large, N_BUCKETS - 1)
    return jnp.where(dist < exact, dist, large)


def _lookup(tab_ref, bucket, col):
    val = jnp.zeros(bucket.shape, F32)
    for b in range(N_BUCKETS):
        val = jnp.where(bucket == b, tab_ref[b, col], val)
    return val


def _dil_bias_kernel(tab_ref, o_ref):
    dil = jnp.left_shift(1, 2 * pl.program_id(0))
    i = lax.broadcasted_iota(I32, (DIL_BLK, 2 * DIL_BLK), 0)
    j = lax.broadcasted_iota(I32, (DIL_BLK, 2 * DIL_BLK), 1)
    bucket = _rel_bucket(jnp.clip(DIL_BLK + i - j, 0, DIL_BLK) * dil)
    for h in range(N_HEADS_A):
        o_ref[0, h] = _lookup(tab_ref, bucket, h)


def _dsa_bias_kernel(tab_ref, o_ref):
    jj = pl.program_id(0)
    s = lax.broadcasted_iota(I32, (DSA_TQ, DSA_TQ), 0)
    t = lax.broadcasted_iota(I32, (DSA_TQ, DSA_TQ), 1)
    bucket = _rel_bucket(jnp.maximum((jj - 1) * DSA_TQ + t - s, 0))
    for h in range(N_HEADS_B):
        o_ref[h, 0] = _lookup(tab_ref, bucket, N_HEADS_A + h) * LOG2E


def _bias_tiles(rel_bias):
    smem = pl.BlockSpec(memory_space=pltpu.SMEM)
    dil = pl.pallas_call(
        _dil_bias_kernel,
        grid=(len(DILATIONS),),
        in_specs=[smem],
        out_specs=pl.BlockSpec((1, N_HEADS_A, DIL_BLK, 2 * DIL_BLK), lambda g: (g, 0, 0, 0)),
        out_shape=jax.ShapeDtypeStruct((len(DILATIONS), N_HEADS_A, DIL_BLK, 2 * DIL_BLK), F32),
        name="dil_bias",
    )(rel_bias)
    ntile = DSA_NEAR + 2
    dsa = pl.pallas_call(
        _dsa_bias_kernel,
        grid=(ntile,),
        in_specs=[smem],
        out_specs=pl.BlockSpec((N_HEADS_B, 1, DSA_TQ, DSA_TQ), lambda g: (0, g, 0, 0)),
        out_shape=jax.ShapeDtypeStruct((N_HEADS_B, ntile, DSA_TQ, DSA_TQ), F32),
        name="dsa_bias",
    )(rel_bias)
    return dil, dsa


def _ffn_kernel(x_ref, g_ref, wg_ref, wu_ref, wo_ref, o_ref, h_ref, acc_ref):
    f = pl.program_id(1)

    @pl.when(f == 0)
    def _():
        x = x_ref[...]
        ms = jnp.mean(x * x, axis=-1, keepdims=True)
        h_ref[...] = (x * lax.rsqrt(ms + EPS) * g_ref[...]).astype(BF16)
        acc_ref[...] = jnp.zeros_like(acc_ref)

    h = h_ref[...]
    gate = jnp.dot(h, wg_ref[...], preferred_element_type=F32)
    up = jnp.dot(h, wu_ref[...], preferred_element_type=F32)
    act = (gate * jax.nn.sigmoid(gate) * up).astype(BF16)
    acc_ref[...] += jnp.dot(act, wo_ref[...], preferred_element_type=F32)

    @pl.when(f == pl.num_programs(1) - 1)
    def _():
        o_ref[...] = x_ref[...] + 0.5 * acc_ref[...]


def _ffn(x2, gain, w_in, w_out, tm=512, tf=512):
    m, d = x2.shape
    d_ff = w_out.shape[0]
    nf = d_ff // tf
    return pl.pallas_call(
        _ffn_kernel,
        grid=(m // tm, nf),
        in_specs=[
            pl.BlockSpec((tm, d), lambda i, f: (i, 0)),
            pl.BlockSpec((1, d), lambda i, f: (0, 0)),
            pl.BlockSpec((d, tf), lambda i, f: (0, f)),
            pl.BlockSpec((d, tf), lambda i, f: (0, f + nf)),
            pl.BlockSpec((tf, d), lambda i, f: (f, 0)),
        ],
        out_specs=pl.BlockSpec((tm, d), lambda i, f: (i, 0)),
        out_shape=jax.ShapeDtypeStruct((m, d), F32),
        scratch_shapes=[pltpu.VMEM((tm, d), BF16), pltpu.VMEM((tm, d), F32)],
        compiler_params=_cparams(("parallel", "arbitrary")),
        name="ffn",
    )(x2, gain.reshape(1, d), w_in, w_in, w_out)


PROJ_BLK = 1024
PROJ_NBLK = 7
PROJ_NA = 3
_NORM_BLOCKS = (0, 1, 3, 4)


def _proj_kernel(x_ref, g_ref, w_ref, widx_ref, hg_ref, oa_ref, ob_ref, idx_ref, h_ref):
    j = pl.program_id(1)

    @pl.when(j == 0)
    def _():
        x = x_ref[...]
        ms = jnp.mean(x * x, axis=-1, keepdims=True)
        h_ref[...] = (x * lax.rsqrt(ms + EPS) * g_ref[...]).astype(BF16)

    h = h_ref[...]
    y = jnp.dot(h, w_ref[...], preferred_element_type=F32)
    is_norm = functools.reduce(jnp.logical_or, [j == b for b in _NORM_BLOCKS])

    def emit(dst_ref):
        @pl.when(is_norm)
        def _():
            for hd in range(PROJ_BLK // HEAD_DIM):
                sl = slice(hd * HEAD_DIM, (hd + 1) * HEAD_DIM)
                yh = y[:, sl]
                ms = jnp.mean(yh * yh, axis=-1, keepdims=True)
                dst_ref[:, sl] = (yh * lax.rsqrt(ms + EPS) * hg_ref[0, :, sl]).astype(BF16)

        @pl.when(jnp.logical_not(is_norm))
        def _():
            dst_ref[...] = y.astype(BF16)

    pl.when(j < PROJ_NA)(lambda: emit(oa_ref))
    pl.when(j >= PROJ_NA)(lambda: emit(ob_ref))

    @pl.when(j == PROJ_NBLK - 1)
    def _():
        idx_ref[...] = jnp.dot(h, widx_ref[...], preferred_element_type=F32)


def _proj(x2, gain, w_main, w_idx, head_gain, tm=1024):
    m, d = x2.shape
    return pl.pallas_call(
        _proj_kernel,
        grid=(m // tm, PROJ_NBLK),
        in_specs=[
            pl.BlockSpec((tm, d), lambda i, j: (i, 0)),
            pl.BlockSpec((1, d), lambda i, j: (0, 0)),
            pl.BlockSpec((d, PROJ_BLK), lambda i, j: (0, j)),
            pl.BlockSpec((d, IDX_PAD), lambda i, j: (0, 0)),
            pl.BlockSpec((1, 1, PROJ_BLK), lambda i, j: (j, 0, 0)),
        ],
        out_specs=[
            pl.BlockSpec((tm, PROJ_BLK), lambda i, j: (i, jnp.minimum(j, PROJ_NA - 1))),
            pl.BlockSpec((tm, PROJ_BLK), lambda i, j: (i, jnp.maximum(j - PROJ_NA, 0))),
            pl.BlockSpec((tm, IDX_PAD), lambda i, j: (i, 0)),
        ],
        out_shape=[
            jax.ShapeDtypeStruct((m, PROJ_NA * PROJ_BLK), BF16),
            jax.ShapeDtypeStruct((m, (PROJ_NBLK - PROJ_NA) * PROJ_BLK), BF16),
            jax.ShapeDtypeStruct((m, IDX_PAD), F32),
        ],
        scratch_shapes=[pltpu.VMEM((tm, d), BF16)],
        compiler_params=_cparams(("parallel", "arbitrary")),
        name="proj",
    )(x2, gain.reshape(1, d), w_main, w_idx, head_gain)


_NT = (((1,), (1,)), ((), ()))


DIL_SPAN = DIL_BLK * max(DILATIONS)
DIL_G = 2
DIL_UNITS = DIL_SPAN // DIL_BLK
DIL_UNROLL = 4


def _dilated_kernel(q_ref, kp_ref, kc_ref, vp_ref, vc_ref, b_ref, o_ref, qf, kf, vf, ob, lse_ref):
    n = pl.program_id(1)
    h0 = pl.program_id(2) * DIL_G
    for g in range(DIL_G):
        sl = slice(g * HEAD_DIM, (g + 1) * HEAD_DIM)
        qf[g] = q_ref[0, :, sl].astype(F32)
        kf[g, :DIL_SPAN] = kp_ref[0, :, sl].astype(F32)
        kf[g, DIL_SPAN:] = kc_ref[0, :, sl].astype(F32)
        vf[g, :DIL_SPAN] = vp_ref[0, :, sl].astype(F32)
        vf[g, DIL_SPAN:] = vc_ref[0, :, sl].astype(F32)
    i = lax.broadcasted_iota(I32, (DIL_BLK, 2 * DIL_BLK), 0)
    j = lax.broadcasted_iota(I32, (DIL_BLK, 2 * DIL_BLK), 1)
    band = jnp.logical_and(j >= i, j <= i + DIL_BLK)
    first = j >= DIL_BLK

    for br, dil in enumerate(DILATIONS):
        span = DIL_BLK * dil

        def unit(idx, carry, br=br, dil=dil, span=span):
            blk = idx // dil
            base = blk * span + idx % dil
            valid = jnp.logical_and(band, jnp.logical_or(first, n * (DIL_SPAN // span) + blk > 0))
            qrows = pl.ds(base, DIL_BLK, stride=dil)
            krows = pl.ds(DIL_SPAN + base - span, 2 * DIL_BLK, stride=dil)
            for g in range(DIL_G):
                q = qf[g, qrows, :].astype(BF16)
                k = kf[g, krows, :].astype(BF16)
                v = vf[g, krows, :].astype(BF16)
                s = lax.dot_general(q, k, _NT, preferred_element_type=F32) + b_ref[br, h0 + g]
                s = jnp.where(valid, s, -jnp.inf)
                mx = jnp.max(s, axis=-1, keepdims=True)
                p = jnp.exp(s - mx)
                den = jnp.sum(p, axis=-1, keepdims=True)
                ob[br, g, qrows, :] = jnp.dot(p.astype(BF16), v, preferred_element_type=F32) / den
                lse_ref[br, g, qrows, :] = jnp.broadcast_to(mx + jnp.log(den), (DIL_BLK, HEAD_DIM))
            return carry

        lax.fori_loop(0, DIL_UNITS, unit, 0, unroll=DIL_UNROLL)

    for g in range(DIL_G):
        l1, l2, l3 = lse_ref[0, g], lse_ref[1, g], lse_ref[2, g]
        mx = jnp.maximum(jnp.maximum(l1, l2), l3)
        e1, e2, e3 = jnp.exp(l1 - mx), jnp.exp(l2 - mx), jnp.exp(l3 - mx)
        mix = (e1 * ob[0, g] + e2 * ob[1, g] + e3 * ob[2, g]) / (e1 + e2 + e3)
        o_ref[0, :, g * HEAD_DIM:(g + 1) * HEAD_DIM] = mix.astype(BF16)


def _dilated(main_a, bias):
    bsz, s, _ = main_a.shape
    gw = DIL_G * HEAD_DIM
    ng = WIDTH_A // gw
    cur = lambda c: pl.BlockSpec((1, DIL_SPAN, gw), lambda b, n, g: (b, n, c * ng + g))
    prev = lambda c: pl.BlockSpec((1, DIL_SPAN, gw), lambda b, n, g: (b, jnp.maximum(n - 1, 0), c * ng + g))
    out = pl.pallas_call(
        _dilated_kernel,
        grid=(bsz, s // DIL_SPAN, ng),
        in_specs=[cur(0), prev(1), cur(1), prev(2), cur(2),
                  pl.BlockSpec(bias.shape, lambda b, n, g: (0, 0, 0, 0), pipeline_mode=pl.Buffered(1))],
        out_specs=pl.BlockSpec((1, DIL_SPAN, gw), lambda b, n, g: (b, n, g)),
        out_shape=jax.ShapeDtypeStruct((bsz, s, WIDTH_A), BF16),
        scratch_shapes=[pltpu.VMEM((DIL_G, DIL_SPAN, HEAD_DIM), F32),
                        pltpu.VMEM((DIL_G, 2 * DIL_SPAN, HEAD_DIM), F32),
                        pltpu.VMEM((DIL_G, 2 * DIL_SPAN, HEAD_DIM), F32),
                        pltpu.VMEM((len(DILATIONS), DIL_G, DIL_SPAN, HEAD_DIM), F32),
                        pltpu.VMEM((len(DILATIONS), DIL_G, DIL_SPAN, HEAD_DIM), F32)],
        compiler_params=_cparams(("parallel", "parallel", "parallel")),
        name="dilated",
    )(main_a, main_a, main_a, main_a, main_a, bias)
    return out.reshape(bsz * s, WIDTH_A)


def _dsa_schedule(s):
    qs, ks = [], []
    for i in range(s // DSA_TQ):
        last = (i * DSA_TQ + DSA_TQ - 1) // DSA_TK
        for k in range(last + 1):
            qs.append(i)
            ks.append(k)
    return np.asarray(qs, np.int32), np.asarray(ks, np.int32)


def _sortable(bits):
    return bits ^ (lax.shift_right_arithmetic(bits, 31) & 0x7FFFFFFF)


def _bit_transpose(words):
    a = list(words)
    mask, j = 0x0000FFFF, 16
    while j:
        k = 0
        while k < WORD:
            t = (a[k] ^ lax.shift_right_logical(a[k + j], j)) & mask
            a[k] = a[k] ^ t
            a[k + j] = a[k + j] ^ lax.shift_left(t, j)
            k = (k + j + 1) & ~j
        j >>= 1
        mask = (mask ^ (mask << j)) & 0xFFFFFFFF
    return a


def _dsa_score_kernel(qtab, ktab, qi_ref, ki_ref, w_ref, sc_ref, thr_ref, planes_ref, alive_ref):
    step = pl.program_id(1)
    qb = qtab[step]
    kc = ktab[step]
    kidx = ki_ref[0]
    w = w_ref[0] * IDX_SCALE
    acc = jnp.zeros((DSA_TK, DSA_TQ), F32)
    for h in range(N_IDX_HEADS):
        d = lax.dot_general(kidx, qi_ref[0, :, h * IDX_DIM:(h + 1) * IDX_DIM], _NT,
                            preferred_element_type=F32)
        acc = acc + jnp.maximum(d, 0.0) * w[h:h + 1, :]
    srow = lax.broadcasted_iota(I32, (DSA_TK, DSA_TQ), 0)
    tcol = lax.broadcasted_iota(I32, (DSA_TK, DSA_TQ), 1)
    causal = srow - tcol <= qb * DSA_TQ - kc * DSA_TK
    acc = jnp.where(causal, acc, -jnp.inf)
    sc_ref[0] = acc
    ukey = _sortable(lax.bitcast_convert_type(acc, I32)) ^ INT_MIN
    planes = _bit_transpose([ukey[g * CHUNK_WORDS:(g + 1) * CHUNK_WORDS, :] for g in range(WORD)])
    row0 = pl.multiple_of(kc * CHUNK_WORDS, CHUNK_WORDS)
    for n in range(WORD):
        planes_ref[n, pl.ds(row0, CHUNK_WORDS), :] = planes[n]

    last = (qb * DSA_TQ + DSA_TQ - 1) // DSA_TK

    @pl.when(kc == last)
    def _():
        nwords = alive_ref.shape[0]
        rb = min(128, nwords)
        nrows = (last + 1) * CHUNK_WORDS
        nblk = (nrows + rb - 1) // rb
        tpos = qb * DSA_TQ + lax.broadcasted_iota(I32, (1, DSA_TQ), 1)
        want0 = jnp.minimum(tpos + 1, TOPK_MAX)
        rows = lambda blk: pl.ds(pl.multiple_of(blk * rb, rb), rb)

        def init(blk, carry):
            ridx = blk * rb + lax.broadcasted_iota(I32, (rb, DSA_TQ), 0)
            alive_ref[rows(blk), :] = jnp.where(ridx < nrows, -1, 0)
            return carry
        lax.fori_loop(0, nblk, init, 0)

        def bit_step(n, carry):
            r, want = carry

            def count(blk, cnt):
                hit = lax.population_count(alive_ref[rows(blk), :] & planes_ref[n, rows(blk), :])
                return cnt + hit.reshape(rb // 8, 8, DSA_TQ).sum(axis=0)
            cnt = lax.fori_loop(0, nblk, count, jnp.zeros((8, DSA_TQ), I32)).sum(axis=0, keepdims=True)
            take = cnt >= want
            r = r | jnp.where(take, lax.shift_left(jnp.int32(1), 31 - n), 0)
            want = jnp.where(take, want, want - cnt)
            flip = jnp.where(take, 0, -1)

            def update(blk, c):
                alive_ref[rows(blk), :] = alive_ref[rows(blk), :] & (planes_ref[n, rows(blk), :] ^ flip)
                return c
            lax.fori_loop(0, nblk, update, 0)
            return r, want

        r, _ = lax.fori_loop(0, WORD, bit_step, (jnp.zeros((1, DSA_TQ), I32), want0))
        thr_ref[0] = lax.bitcast_convert_type(_sortable(r ^ INT_MIN), F32)


def _dsa_attn_kernel(qtab, ktab, sc_ref, thr_ref, q_ref, k_ref, vt_ref, b_ref, bound_ref, o_ref,
                     acc_ref, m_ref, l_ref, s_ref, mask_ref, cm_ref):
    step = pl.program_id(1)
    qb = qtab[step]
    kc = ktab[step]

    @pl.when(kc == 0)
    def _():
        acc_ref[...] = jnp.zeros_like(acc_ref)
        m_ref[...] = jnp.full_like(m_ref, NEG_BIG)
        l_ref[...] = jnp.zeros_like(l_ref)

    bound = bound_ref[0]
    bounded = bound <= DSA_SAFE_BOUND

    def set_mask(kept):
        mask_ref[...] = jnp.where(sc_ref[0] >= thr_ref[0], kept, -jnp.inf)

    tiles = [jnp.clip(qb - (kc * DSA_HALVES + hh), -1, DSA_NEAR) + 1 for hh in range(DSA_HALVES)]
    far = qb - (kc * DSA_HALVES + DSA_HALVES - 1) >= DSA_NEAR

    def logits(h):
        sl = slice(h * HEAD_DIM, (h + 1) * HEAD_DIM)
        s = lax.dot_general(k_ref[0, :, sl], q_ref[0, :, sl], _NT, preferred_element_type=F32)
        return s + mask_ref[...]

    def near_bias(h):
        return jnp.concatenate([b_ref[h, tiles[hh]] for hh in range(DSA_HALVES)], axis=0)

    def values(h):
        return vt_ref[0, h * HEAD_DIM:(h + 1) * HEAD_DIM, :]

    @pl.when(bounded)
    def _():
        def raw_logits(h):
            sl = slice(h * HEAD_DIM, (h + 1) * HEAD_DIM)
            s_ref[h] = lax.dot_general(k_ref[0, :, sl], q_ref[0, :, sl], _NT,
                                       preferred_element_type=F32)

        def heads(weights):
            for h in range(DSA_LOOKAHEAD):
                raw_logits(h)
            set_mask(-bound)
            for h in range(N_HEADS_B):
                if h + DSA_LOOKAHEAD < N_HEADS_B:
                    raw_logits(h + DSA_LOOKAHEAD)
                p, gain = weights(h)
                l_ref[h] += gain * jnp.sum(p, axis=0, keepdims=True)
                acc_ref[h] += gain * jnp.dot(values(h), p.astype(BF16), preferred_element_type=F32)

        @pl.when(far)
        def _():
            heads(lambda h: (jnp.exp2(s_ref[h] + mask_ref[...]),
                             jnp.exp2(b_ref[h, DSA_NEAR + 1, 0:1, :])))

        @pl.when(jnp.logical_not(far))
        def _():
            heads(lambda h: (jnp.exp2(s_ref[h] + mask_ref[...] + near_bias(h)), 1.0))

    @pl.when(jnp.logical_not(bounded))
    def _():
        set_mask(0.0)

        @pl.when(far)
        def _():
            for h in range(N_HEADS_B):
                s = logits(h)
                s_ref[h] = s
                cm_ref[h] = jnp.max(s, axis=0, keepdims=True) + b_ref[h, DSA_NEAR + 1, 0:1, :]

        @pl.when(jnp.logical_not(far))
        def _():
            for h in range(N_HEADS_B):
                s = logits(h) + near_bias(h)
                s_ref[h] = s
                cm_ref[h] = jnp.max(s, axis=0, keepdims=True)

        for h in range(N_HEADS_B):
            m_old = m_ref[h]
            m_new = jnp.maximum(m_old, cm_ref[h])
            alpha = jnp.exp2(m_old - m_new)
            m_ref[h] = m_new
            shift = jnp.where(far, m_new - b_ref[h, DSA_NEAR + 1, 0:1, :], m_new)
            p = jnp.exp2(s_ref[h] - shift)
            l_ref[h] = alpha * l_ref[h] + jnp.sum(p, axis=0, keepdims=True)
            pv = jnp.dot(values(h), p.astype(BF16), preferred_element_type=F32)
            acc_ref[h] = alpha * acc_ref[h] + pv

    last = (qb * DSA_TQ + DSA_TQ - 1) // DSA_TK

    @pl.when(kc == last)
    def _():
        for h in range(N_HEADS_B):
            o_ref[0, h * HEAD_DIM:(h + 1) * HEAD_DIM, :] = (acc_ref[h] / l_ref[h]).astype(BF16)


def _dsa(main_b, ki, wi_t, bias, logit_bound):
    bsz, s, _ = main_b.shape
    w = WIDTH_B
    qs, ks = _dsa_schedule(s)
    npairs = len(qs)
    scores, thr = pl.pallas_call(
        _dsa_score_kernel,
        grid_spec=pltpu.PrefetchScalarGridSpec(
            num_scalar_prefetch=2,
            grid=(bsz, npairs),
            in_specs=[
                pl.BlockSpec((1, DSA_TQ, N_IDX_HEADS * IDX_DIM), lambda b, t, qt, kt: (b, qt[t], 3)),
                pl.BlockSpec((1, DSA_TK, IDX_DIM), lambda b, t, qt, kt: (b, kt[t], 0)),
                pl.BlockSpec((1, N_IDX_HEADS, DSA_TQ), lambda b, t, qt, kt: (b, 0, qt[t])),
            ],
            out_specs=[
                pl.BlockSpec((1, DSA_TK, DSA_TQ), lambda b, t, qt, kt: (b, kt[t], qt[t])),
                pl.BlockSpec((1, 1, DSA_TQ), lambda b, t, qt, kt: (b, 0, qt[t])),
            ],
            scratch_shapes=[pltpu.VMEM((WORD, s // WORD, DSA_TQ), I32),
                            pltpu.VMEM((s // WORD, DSA_TQ), I32)],
        ),
        out_shape=[jax.ShapeDtypeStruct((bsz, s, s), F32),
                   jax.ShapeDtypeStruct((bsz, 1, s), F32)],
        compiler_params=_cparams(("arbitrary", "arbitrary")),
        name="dsa_scores",
    )(qs, ks, main_b, ki, wi_t)

    vt = jnp.swapaxes(main_b[:, :, 2 * PROJ_BLK:3 * PROJ_BLK], 1, 2)
    ntile = bias.shape[1]
    out_t = pl.pallas_call(
        _dsa_attn_kernel,
        grid_spec=pltpu.PrefetchScalarGridSpec(
            num_scalar_prefetch=2,
            grid=(bsz, npairs),
            in_specs=[
                pl.BlockSpec((1, DSA_TK, DSA_TQ), lambda b, t, qt, kt: (b, kt[t], qt[t])),
                pl.BlockSpec((1, 1, DSA_TQ), lambda b, t, qt, kt: (b, 0, qt[t])),
                pl.BlockSpec((1, DSA_TQ, w), lambda b, t, qt, kt: (b, qt[t], 0)),
                pl.BlockSpec((1, DSA_TK, w), lambda b, t, qt, kt: (b, kt[t], 1)),
                pl.BlockSpec((1, w, DSA_TK), lambda b, t, qt, kt: (b, 0, kt[t])),
                pl.BlockSpec((N_HEADS_B, ntile, DSA_TQ, DSA_TQ), lambda b, t, qt, kt: (0, 0, 0, 0),
                             pipeline_mode=pl.Buffered(1)),
                pl.BlockSpec(memory_space=pltpu.SMEM),
            ],
            out_specs=pl.BlockSpec((1, w, DSA_TQ), lambda b, t, qt, kt: (b, 0, qt[t])),
            scratch_shapes=[pltpu.VMEM((N_HEADS_B, HEAD_DIM, DSA_TQ), F32),
                            pltpu.VMEM((N_HEADS_B, 1, DSA_TQ), F32),
                            pltpu.VMEM((N_HEADS_B, 1, DSA_TQ), F32),
                            pltpu.VMEM((N_HEADS_B, DSA_TK, DSA_TQ), F32),
                            pltpu.VMEM((DSA_TK, DSA_TQ), F32),
                            pltpu.VMEM((N_HEADS_B, 1, DSA_TQ), F32)],
        ),
        out_shape=jax.ShapeDtypeStruct((bsz, w, s), BF16),
        compiler_params=_cparams(("arbitrary", "arbitrary")),
        name="dsa_attn",
    )(qs, ks, scores, thr, main_b, main_b, vt, bias, logit_bound)
    return jnp.swapaxes(out_t, 1, 2).reshape(bsz * s, w)


def _out_kernel(x_ref, ma_ref, mb_ref, wa_ref, wb_ref, o_ref):
    o_ref[...] = (x_ref[...]
                  + jnp.dot(ma_ref[...], wa_ref[...], preferred_element_type=F32)
                  + jnp.dot(mb_ref[...], wb_ref[...], preferred_element_type=F32))


def _out_proj(x2, mix_a, mix_b, w_out, tm=512):
    m, d = x2.shape
    row = lambda width: pl.BlockSpec((tm, width), lambda i: (i, 0))
    return pl.pallas_call(
        _out_kernel,
        grid=(m // tm,),
        in_specs=[row(d), row(WIDTH_A), row(WIDTH_B),
                  pl.BlockSpec((WIDTH_A, d), lambda i: (0, 0)),
                  pl.BlockSpec((WIDTH_B, d), lambda i: (1, 0))],
        out_specs=row(d),
        out_shape=jax.ShapeDtypeStruct((m, d), F32),
        compiler_params=_cparams(("parallel",)),
        name="out_proj",
    )(x2, mix_a, mix_b, w_out, w_out)


def kernel(x, rel_bias, norm_ffn1, w_ffn1_in, w_ffn1_out, norm_mix, w_in, q_norm_a, k_norm_a,
           q_norm_b, k_norm_b, w_out, norm_ffn2, w_ffn2_in, w_ffn2_out):
    bsz, s, d = x.shape
    m = bsz * s
    depth = norm_ffn1.shape[0]
    scale = HEAD_DIM ** -0.5
    n_main = PROJ_NBLK * PROJ_BLK
    dil_bias, dsa_bias = _bias_tiles(rel_bias)
    x2 = x.reshape(m, d)
    ones = jnp.ones((PROJ_BLK,), F32)
    tile8 = lambda g: jnp.tile(g, PROJ_BLK // HEAD_DIM)
    for l in range(depth):
        x2 = _ffn(x2, norm_ffn1[l], w_ffn1_in[l].astype(BF16), w_ffn1_out[l].astype(BF16))

        w_main = w_in[l, :, :n_main].astype(BF16)
        w_idx = jnp.pad(w_in[l, :, n_main:], ((0, 0), (0, IDX_PAD - IDX_DIM - N_IDX_HEADS))).astype(BF16)
        head_gain = jnp.stack([tile8(q_norm_a[l] * scale), tile8(k_norm_a[l]), ones,
                               tile8(q_norm_b[l] * (scale * LOG2E)), tile8(k_norm_b[l]), ones, ones])
        main_a, main_b, idx = _proj(x2, norm_mix[l], w_main, w_idx,
                                    head_gain.reshape(PROJ_NBLK, 1, PROJ_BLK))
        main_a = main_a.reshape(bsz, s, PROJ_NA * PROJ_BLK)
        main_b = main_b.reshape(bsz, s, (PROJ_NBLK - PROJ_NA) * PROJ_BLK)
        idx = idx.reshape(bsz, s, IDX_PAD)
        ki = idx[:, :, :IDX_DIM].astype(BF16)
        wi_t = jnp.swapaxes(idx[:, :, IDX_DIM:IDX_DIM + N_IDX_HEADS], 1, 2)

        mix_a = _dilated(main_a, dil_bias)
        logit_bound = (1.02 * HEAD_DIM * jnp.max(jnp.abs(q_norm_b[l] * (scale * LOG2E)))
                       * jnp.max(jnp.abs(k_norm_b[l]))
                       + LOG2E * jnp.max(jnp.abs(rel_bias[:, N_HEADS_A:]))).reshape(1)
        mix_b = _dsa(main_b, ki, wi_t, dsa_bias, logit_bound)

        x2 = _out_proj(x2, mix_a, mix_b, w_out[l].astype(BF16))
        x2 = _ffn(x2, norm_ffn2[l], w_ffn2_in[l].astype(BF16), w_ffn2_out[l].astype(BF16))
    return x2.reshape(bsz, s, d)
```

```python
import functools
import math

import jax
import jax.numpy as jnp
import numpy as np
from jax import lax
from jax.experimental import pallas as pl
from jax.experimental.pallas import tpu as pltpu

F32 = jnp.float32
BF16 = jnp.bfloat16
I32 = jnp.int32

HEAD_DIM = 128
N_HEADS_A = 8
N_HEADS_B = 8
WIDTH_A = N_HEADS_A * HEAD_DIM
WIDTH_B = N_HEADS_B * HEAD_DIM
DILATIONS = (1, 4, 16)
DIL_BLK = 128
N_IDX_HEADS = 16
IDX_DIM = 64
TOPK_MAX = 256
N_BUCKETS = 32
REL_MAX_DIST = 2048
EPS = 1e-6
IDX_PAD = 128
IDX_SCALE = (IDX_DIM ** -0.5) * (N_IDX_HEADS ** -0.5)
LOG2E = math.log2(math.e)

VMEM_LIMIT = 56 * 1024 * 1024

DSA_TQ = 256
DSA_TK = 512
DSA_HALVES = DSA_TK // DSA_TQ
WORD = 32
CHUNK_WORDS = DSA_TK // WORD
SCORE_ROWS = 128
NEG_BIG = -1e30
DSA_SAFE_BOUND = 40.0
DSA_LOOKAHEAD = 2
INT_MIN = -2 ** 31


def _near_tiles():
    exact = N_BUCKETS // 2
    j = 1
    while True:
        d = j * DSA_TQ - (DSA_TQ - 1)
        b = exact + int(math.log(d / exact) / math.log(REL_MAX_DIST / exact) * (N_BUCKETS - exact) - 0.02)
        if d >= exact and b >= N_BUCKETS - 1:
            return j
        j += 1


DSA_NEAR = _near_tiles()


def _cparams(sem):
    return pltpu.CompilerParams(dimension_semantics=sem, vmem_limit_bytes=VMEM_LIMIT)


def _rel_bucket(dist):
    exact = N_BUCKETS // 2
    df = jnp.maximum(dist, 1).astype(F32)
    large = exact + (jnp.log(df / exact) / math.log(REL_MAX_DIST / exact)
                     * (N_BUCKETS - exact)).astype(I32)
    large = jnp.minimum(large, N_BUCKETS - 1)
    return jnp.where(dist < exact, dist, large)


def _lookup(tab_ref, bucket, col):
    val = jnp.zeros(bucket.shape, F32)
    for b in range(N_BUCKETS):
        val = jnp.where(bucket == b, tab_ref[b, col], val)
    return val


def _dil_bias_kernel(tab_ref, o_ref):
    dil = jnp.left_shift(1, 2 * pl.program_id(0))
    i = lax.broadcasted_iota(I32, (DIL_BLK, 2 * DIL_BLK), 0)
    j = lax.broadcasted_iota(I32, (DIL_BLK, 2 * DIL_BLK), 1)
    bucket = _rel_bucket(jnp.clip(DIL_BLK + i - j, 0, DIL_BLK) * dil)
    for h in range(N_HEADS_A):
        o_ref[0, h] = _lookup(tab_ref, bucket, h)


def _dsa_bias_kernel(tab_ref, o_ref):
    jj = pl.program_id(0)
    s = lax.broadcasted_iota(I32, (DSA_TQ, DSA_TQ), 0)
    t = lax.broadcasted_iota(I32, (DSA_TQ, DSA_TQ), 1)
    bucket = _rel_bucket(jnp.maximum((jj - 1) * DSA_TQ + t - s, 0))
    for h in range(N_HEADS_B):
        o_ref[h, 0] = _lookup(tab_ref, bucket, N_HEADS_A + h) * LOG2E


def _bias_tiles(rel_bias):
    smem = pl.BlockSpec(memory_space=pltpu.SMEM)
    dil = pl.pallas_call(
        _dil_bias_kernel,
        grid=(len(DILATIONS),),
        in_specs=[smem],
        out_specs=pl.BlockSpec((1, N_HEADS_A, DIL_BLK, 2 * DIL_BLK), lambda g: (g, 0, 0, 0)),
        out_shape=jax.ShapeDtypeStruct((len(DILATIONS), N_HEADS_A, DIL_BLK, 2 * DIL_BLK), F32),
        name="dil_bias",
    )(rel_bias)
    ntile = DSA_NEAR + 2
    dsa = pl.pallas_call(
        _dsa_bias_kernel,
        grid=(ntile,),
        in_specs=[smem],
        out_specs=pl.BlockSpec((N_HEADS_B, 1, DSA_TQ, DSA_TQ), lambda g: (0, g, 0, 0)),
        out_shape=jax.ShapeDtypeStruct((N_HEADS_B, ntile, DSA_TQ, DSA_TQ), F32),
        name="dsa_bias",
    )(rel_bias)
    return dil, dsa


def _ffn_kernel(x_ref, g_ref, wg_ref, wu_ref, wo_ref, o_ref, h_ref, acc_ref):
    f = pl.program_id(1)

    @pl.when(f == 0)
    def _():
        x = x_ref[...]
        ms = jnp.mean(x * x, axis=-1, keepdims=True)
        h_ref[...] = (x * lax.rsqrt(ms + EPS) * g_ref[...]).astype(BF16)
        acc_ref[...] = jnp.zeros_like(acc_ref)

    h = h_ref[...]
    gate = jnp.dot(h, wg_ref[...], preferred_element_type=F32)
    up = jnp.dot(h, wu_ref[...], preferred_element_type=F32)
    act = (gate * jax.nn.sigmoid(gate) * up).astype(BF16)
    acc_ref[...] += jnp.dot(act, wo_ref[...], preferred_element_type=F32)

    @pl.when(f == pl.num_programs(1) - 1)
    def _():
        o_ref[...] = x_ref[...] + 0.5 * acc_ref[...]


def _ffn(x2, gain, w_in, w_out, tm=512, tf=512):
    m, d = x2.shape
    d_ff = w_out.shape[0]
    nf = d_ff // tf
    return pl.pallas_call(
        _ffn_kernel,
        grid=(m // tm, nf),
        in_specs=[
            pl.BlockSpec((tm, d), lambda i, f: (i, 0)),
            pl.BlockSpec((1, d), lambda i, f: (0, 0)),
            pl.BlockSpec((d, tf), lambda i, f: (0, f)),
            pl.BlockSpec((d, tf), lambda i, f: (0, f + nf)),
            pl.BlockSpec((tf, d), lambda i, f: (f, 0)),
        ],
        out_specs=pl.BlockSpec((tm, d), lambda i, f: (i, 0)),
        out_shape=jax.ShapeDtypeStruct((m, d), F32),
        scratch_shapes=[pltpu.VMEM((tm, d), BF16), pltpu.VMEM((tm, d), F32)],
        compiler_params=_cparams(("parallel", "arbitrary")),
        name="ffn",
    )(x2, gain.reshape(1, d), w_in, w_in, w_out)


PROJ_BLK = 1024
PROJ_NBLK = 7
PROJ_NA = 3
_NORM_BLOCKS = (0, 1, 3, 4)


def _proj_kernel(x_ref, g_ref, w_ref, widx_ref, hg_ref, oa_ref, ob_ref, idx_ref, h_ref):
    j = pl.program_id(1)

    @pl.when(j == 0)
    def _():
        x = x_ref[...]
        ms = jnp.mean(x * x, axis=-1, keepdims=True)
        h_ref[...] = (x * lax.rsqrt(ms + EPS) * g_ref[...]).astype(BF16)

    h = h_ref[...]
    y = jnp.dot(h, w_ref[...], preferred_element_type=F32)
    is_norm = functools.reduce(jnp.logical_or, [j == b for b in _NORM_BLOCKS])

    def emit(dst_ref):
        @pl.when(is_norm)
        def _():
            for hd in range(PROJ_BLK // HEAD_DIM):
                sl = slice(hd * HEAD_DIM, (hd + 1) * HEAD_DIM)
                yh = y[:, sl]
                ms = jnp.mean(yh * yh, axis=-1, keepdims=True)
                dst_ref[:, sl] = (yh * lax.rsqrt(ms + EPS) * hg_ref[0, :, sl]).astype(BF16)

        @pl.when(jnp.logical_not(is_norm))
        def _():
            dst_ref[...] = y.astype(BF16)

    pl.when(j < PROJ_NA)(lambda: emit(oa_ref))
    pl.when(j >= PROJ_NA)(lambda: emit(ob_ref))

    @pl.when(j == PROJ_NBLK - 1)
    def _():
        idx_ref[...] = jnp.dot(h, widx_ref[...], preferred_element_type=F32)


def _proj(x2, gain, w_main, w_idx, head_gain, tm=1024):
    m, d = x2.shape
    return pl.pallas_call(
        _proj_kernel,
        grid=(m // tm, PROJ_NBLK),
        in_specs=[
            pl.BlockSpec((tm, d), lambda i, j: (i, 0)),
            pl.BlockSpec((1, d), lambda i, j: (0, 0)),
            pl.BlockSpec((d, PROJ_BLK), lambda i, j: (0, j)),
            pl.BlockSpec((d, IDX_PAD), lambda i, j: (0, 0)),
            pl.BlockSpec((1, 1, PROJ_BLK), lambda i, j: (j, 0, 0)),
        ],
        out_specs=[
            pl.BlockSpec((tm, PROJ_BLK), lambda i, j: (i, jnp.minimum(j, PROJ_NA - 1))),
            pl.BlockSpec((tm, PROJ_BLK), lambda i, j: (i, jnp.maximum(j - PROJ_NA, 0))),
            pl.BlockSpec((tm, IDX_PAD), lambda i, j: (i, 0)),
        ],
        out_shape=[
            jax.ShapeDtypeStruct((m, PROJ_NA * PROJ_BLK), BF16),
            jax.ShapeDtypeStruct((m, (PROJ_NBLK - PROJ_NA) * PROJ_BLK), BF16),
            jax.ShapeDtypeStruct((m, IDX_PAD), F32),
        ],
        scratch_shapes=[pltpu.VMEM((tm, d), BF16)],
        compiler_params=_cparams(("parallel", "arbitrary")),
        name="proj",
    )(x2, gain.reshape(1, d), w_main, w_idx, head_gain)


_NT = (((1,), (1,)), ((), ()))


DIL_SPAN = DIL_BLK * max(DILATIONS)
DIL_G = 2
DIL_UNITS = DIL_SPAN // DIL_BLK
DIL_UNROLL = 4


def _dilated_kernel(q_ref, kp_ref, kc_ref, vp_ref, vc_ref, b_ref, o_ref, qf, kf, vf, ob, lse_ref):
    n = pl.program_id(1)
    h0 = pl.program_id(2) * DIL_G
    for g in range(DIL_G):
        sl = slice(g * HEAD_DIM, (g + 1) * HEAD_DIM)
        qf[g] = q_ref[0, :, sl].astype(F32)
        kf[g, :DIL_SPAN] = kp_ref[0, :, sl].astype(F32)
        kf[g, DIL_SPAN:] = kc_ref[0, :, sl].astype(F32)
        vf[g, :DIL_SPAN] = vp_ref[0, :, sl].astype(F32)
        vf[g, DIL_SPAN:] = vc_ref[0, :, sl].astype(F32)
    i = lax.broadcasted_iota(I32, (DIL_BLK, 2 * DIL_BLK), 0)
    j = lax.broadcasted_iota(I32, (DIL_BLK, 2 * DIL_BLK), 1)
    band = jnp.logical_and(j >= i, j <= i + DIL_BLK)
    first = j >= DIL_BLK

    for br, dil in enumerate(DILATIONS):
        span = DIL_BLK * dil

        def unit(idx, carry, br=br, dil=dil, span=span):
            blk = idx // dil
            base = blk * span + idx % dil
            valid = jnp.logical_and(band, jnp.logical_or(first, n * (DIL_SPAN // span) + blk > 0))
            qrows = pl.ds(base, DIL_BLK, stride=dil)
            krows = pl.ds(DIL_SPAN + base - span, 2 * DIL_BLK, stride=dil)
            for g in range(DIL_G):
                q = qf[g, qrows, :].astype(BF16)
                k = kf[g, krows, :].astype(BF16)
                v = vf[g, krows, :].astype(BF16)
                s = lax.dot_general(q, k, _NT, preferred_element_type=F32) + b_ref[br, h0 + g]
                s = jnp.where(valid, s, -jnp.inf)
                mx = jnp.max(s, axis=-1, keepdims=True)
                p = jnp.exp(s - mx)
                den = jnp.sum(p, axis=-1, keepdims=True)
                ob[br, g, qrows, :] = jnp.dot(p.astype(BF16), v, preferred_element_type=F32) / den
                lse_ref[br, g, qrows, :] = jnp.broadcast_to(mx + jnp.log(den), (DIL_BLK, HEAD_DIM))
            return carry

        lax.fori_loop(0, DIL_UNITS, unit, 0, unroll=DIL_UNROLL)

    for g in range(DIL_G):
        l1, l2, l3 = lse_ref[0, g], lse_ref[1, g], lse_ref[2, g]
        mx = jnp.maximum(jnp.maximum(l1, l2), l3)
        e1, e2, e3 = jnp.exp(l1 - mx), jnp.exp(l2 - mx), jnp.exp(l3 - mx)
        mix = (e1 * ob[0, g] + e2 * ob[1, g] + e3 * ob[2, g]) / (e1 + e2 + e3)
        o_ref[0, :, g * HEAD_DIM:(g + 1) * HEAD_DIM] = mix.astype(BF16)


def _dilated(main_a, bias):
    bsz, s, _ = main_a.shape
    gw = DIL_G * HEAD_DIM
    ng = WIDTH_A // gw
    cur = lambda c: pl.BlockSpec((1, DIL_SPAN, gw), lambda b, n, g: (b, n, c * ng + g))
    prev = lambda c: pl.BlockSpec((1, DIL_SPAN, gw), lambda b, n, g: (b, jnp.maximum(n - 1, 0), c * ng + g))
    out = pl.pallas_call(
        _dilated_kernel,
        grid=(bsz, s // DIL_SPAN, ng),
        in_specs=[cur(0), prev(1), cur(1), prev(2), cur(2),
                  pl.BlockSpec(bias.shape, lambda b, n, g: (0, 0, 0, 0), pipeline_mode=pl.Buffered(1))],
        out_specs=pl.BlockSpec((1, DIL_SPAN, gw), lambda b, n, g: (b, n, g)),
        out_shape=jax.ShapeDtypeStruct((bsz, s, WIDTH_A), BF16),
        scratch_shapes=[pltpu.VMEM((DIL_G, DIL_SPAN, HEAD_DIM), F32),
                        pltpu.VMEM((DIL_G, 2 * DIL_SPAN, HEAD_DIM), F32),
                        pltpu.VMEM((DIL_G, 2 * DIL_SPAN, HEAD_DIM), F32),
                        pltpu.VMEM((len(DILATIONS), DIL_G, DIL_SPAN, HEAD_DIM), F32),
                        pltpu.VMEM((len(DILATIONS), DIL_G, DIL_SPAN, HEAD_DIM), F32)],
        compiler_params=_cparams(("parallel", "parallel", "parallel")),
        name="dilated",
    )(main_a, main_a, main_a, main_a, main_a, bias)
    return out.reshape(bsz * s, WIDTH_A)


def _dsa_schedule(s):
    qs, ks = [], []
    for i in range(s // DSA_TQ):
        last = (i * DSA_TQ + DSA_TQ - 1) // DSA_TK
        for k in range(last + 1):
            qs.append(i)
            ks.append(k)
    return np.asarray(qs, np.int32), np.asarray(ks, np.int32)


def _sortable(bits):
    return bits ^ (lax.shift_right_arithmetic(bits, 31) & 0x7FFFFFFF)


def _bit_transpose(words):
    a = list(words)
    mask, j = 0x0000FFFF, 16
    while j:
        k = 0
        while k < WORD:
            t = (a[k] ^ lax.shift_right_logical(a[k + j], j)) & mask
            a[k] = a[k] ^ t
            a[k + j] = a[k + j] ^ lax.shift_left(t, j)
            k = (k + j + 1) & ~j
        j >>= 1
        mask = (mask ^ (mask << j)) & 0xFFFFFFFF
    return a


def _dsa_score_kernel(qtab, ktab, qit_ref, ki_ref, w_ref, sc_ref, thr_ref,
                      planes_ref, alive_ref, stage_ref):
    step = pl.program_id(1)
    qb = qtab[step]
    kc = ktab[step]
    w = w_ref[0] * IDX_SCALE
    srow = lax.broadcasted_iota(I32, (SCORE_ROWS, DSA_TQ), 0)
    tcol = lax.broadcasted_iota(I32, (SCORE_ROWS, DSA_TQ), 1)
    row0 = pl.multiple_of(kc * CHUNK_WORDS, CHUNK_WORDS)
    group = WORD * 8
    for g in range(DSA_TK // group):
        for r0 in range(g * group, (g + 1) * group, SCORE_ROWS):
            kidx = ki_ref[0, r0:r0 + SCORE_ROWS, :]
            acc = jnp.zeros((SCORE_ROWS, DSA_TQ), F32)
            for h in range(N_IDX_HEADS):
                d = jnp.dot(kidx, qit_ref[0, h * IDX_DIM:(h + 1) * IDX_DIM, :],
                            preferred_element_type=F32)
                acc = acc + jnp.maximum(d, 0.0) * w[h:h + 1, :]
            causal = srow - tcol <= qb * DSA_TQ - kc * DSA_TK - r0
            acc = jnp.where(causal, acc, -jnp.inf)
            sc_ref[0, r0:r0 + SCORE_ROWS, :] = acc
            stage_ref[r0:r0 + SCORE_ROWS, :] = _sortable(lax.bitcast_convert_type(acc, I32)) ^ INT_MIN
        planes = _bit_transpose([stage_ref[g * group + 8 * n:g * group + 8 * (n + 1), :]
                                 for n in range(WORD)])
        for n in range(WORD):
            planes_ref[n, pl.ds(pl.multiple_of(row0 + 8 * g, 8), 8), :] = planes[n]

    last = (qb * DSA_TQ + DSA_TQ - 1) // DSA_TK

    @pl.when(kc == last)
    def _():
        nwords = alive_ref.shape[0]
        rb = min(128, nwords)
        nrows = (last + 1) * CHUNK_WORDS
        nblk = (nrows + rb - 1) // rb
        tpos = qb * DSA_TQ + lax.broadcasted_iota(I32, (1, DSA_TQ), 1)
        want0 = jnp.minimum(tpos + 1, TOPK_MAX)
        rows = lambda blk: pl.ds(pl.multiple_of(blk * rb, rb), rb)

        def init(blk, carry):
            ridx = blk * rb + lax.broadcasted_iota(I32, (rb, DSA_TQ), 0)
            alive_ref[rows(blk), :] = jnp.where(ridx < nrows, -1, 0)
            return carry
        lax.fori_loop(0, nblk, init, 0)

        def bit_step(n, carry):
            r, want = carry

            def count(blk, cnt):
                hit = lax.population_count(alive_ref[rows(blk), :] & planes_ref[n, rows(blk), :])
                return cnt + hit.reshape(rb // 8, 8, DSA_TQ).sum(axis=0)
            cnt = lax.fori_loop(0, nblk, count, jnp.zeros((8, DSA_TQ), I32)).sum(axis=0, keepdims=True)
            take = cnt >= want
            r = r | jnp.where(take, lax.shift_left(jnp.int32(1), 31 - n), 0)
            want = jnp.where(take, want, want - cnt)
            flip = jnp.where(take, 0, -1)

            def update(blk, c):
                alive_ref[rows(blk), :] = alive_ref[rows(blk), :] & (planes_ref[n, rows(blk), :] ^ flip)
                return c
            lax.fori_loop(0, nblk, update, 0)
            return r, want

        r, _ = lax.fori_loop(0, WORD, bit_step, (jnp.zeros((1, DSA_TQ), I32), want0))
        thr_ref[0] = lax.bitcast_convert_type(_sortable(r ^ INT_MIN), F32)


def _dsa_attn_kernel(qtab, ktab, sc_ref, thr_ref, q_ref, k_ref, vt_ref, b_ref, bound_ref, o_ref,
                     acc_ref, m_ref, l_ref, s_ref, mask_ref, cm_ref):
    step = pl.program_id(1)
    qb = qtab[step]
    kc = ktab[step]

    @pl.when(kc == 0)
    def _():
        acc_ref[...] = jnp.zeros_like(acc_ref)
        m_ref[...] = jnp.full_like(m_ref, NEG_BIG)
        l_ref[...] = jnp.zeros_like(l_ref)

    bound = bound_ref[0]
    bounded = bound <= DSA_SAFE_BOUND

    def set_mask(kept):
        mask_ref[...] = jnp.where(sc_ref[0] >= thr_ref[0], kept, -jnp.inf)

    tiles = [jnp.clip(qb - (kc * DSA_HALVES + hh), -1, DSA_NEAR) + 1 for hh in range(DSA_HALVES)]
    far = qb - (kc * DSA_HALVES + DSA_HALVES - 1) >= DSA_NEAR

    def logits(h):
        sl = slice(h * HEAD_DIM, (h + 1) * HEAD_DIM)
        s = lax.dot_general(k_ref[0, :, sl], q_ref[0, :, sl], _NT, preferred_element_type=F32)
        return s + mask_ref[...]

    def near_bias(h):
        return jnp.concatenate([b_ref[h, tiles[hh]] for hh in range(DSA_HALVES)], axis=0)

    def values(h):
        return vt_ref[0, h * HEAD_DIM:(h + 1) * HEAD_DIM, :]

    @pl.when(bounded)
    def _():
        def raw_logits(h):
            sl = slice(h * HEAD_DIM, (h + 1) * HEAD_DIM)
            s_ref[h] = lax.dot_general(k_ref[0, :, sl], q_ref[0, :, sl], _NT,
                                       preferred_element_type=F32)

        def heads(weights):
            for h in range(DSA_LOOKAHEAD):
                raw_logits(h)
            set_mask(-bound)
            for h in range(N_HEADS_B):
                if h + DSA_LOOKAHEAD < N_HEADS_B:
                    raw_logits(h + DSA_LOOKAHEAD)
                p, gain = weights(h)
                l_ref[h] += gain * jnp.sum(p, axis=0, keepdims=True)
                acc_ref[h] += gain * jnp.dot(values(h), p.astype(BF16), preferred_element_type=F32)

        @pl.when(far)
        def _():
            heads(lambda h: (jnp.exp2(s_ref[h] + mask_ref[...]),
                             jnp.exp2(b_ref[h, DSA_NEAR + 1, 0:1, :])))

        @pl.when(jnp.logical_not(far))
        def _():
            heads(lambda h: (jnp.exp2(s_ref[h] + mask_ref[...] + near_bias(h)), 1.0))

    @pl.when(jnp.logical_not(bounded))
    def _():
        set_mask(0.0)

        @pl.when(far)
        def _():
            for h in range(N_HEADS_B):
                s = logits(h)
                s_ref[h] = s
                cm_ref[h] = jnp.max(s, axis=0, keepdims=True) + b_ref[h, DSA_NEAR + 1, 0:1, :]

        @pl.when(jnp.logical_not(far))
        def _():
            for h in range(N_HEADS_B):
                s = logits(h) + near_bias(h)
                s_ref[h] = s
                cm_ref[h] = jnp.max(s, axis=0, keepdims=True)

        for h in range(N_HEADS_B):
            m_old = m_ref[h]
            m_new = jnp.maximum(m_old, cm_ref[h])
            alpha = jnp.exp2(m_old - m_new)
            m_ref[h] = m_new
            shift = jnp.where(far, m_new - b_ref[h, DSA_NEAR + 1, 0:1, :], m_new)
            p = jnp.exp2(s_ref[h] - shift)
            l_ref[h] = alpha * l_ref[h] + jnp.sum(p, axis=0, keepdims=True)
            pv = jnp.dot(values(h), p.astype(BF16), preferred_element_type=F32)
            acc_ref[h] = alpha * acc_ref[h] + pv

    last = (qb * DSA_TQ + DSA_TQ - 1) // DSA_TK

    @pl.when(kc == last)
    def _():
        for h in range(N_HEADS_B):
            o_ref[0, h * HEAD_DIM:(h + 1) * HEAD_DIM, :] = (acc_ref[h] / l_ref[h]).astype(BF16)


def _dsa(main_b, ki, wi_t, bias, logit_bound):
    bsz, s, _ = main_b.shape
    w = WIDTH_B
    qs, ks = _dsa_schedule(s)
    npairs = len(qs)
    qi_t = jnp.swapaxes(main_b[:, :, 3 * PROJ_BLK:4 * PROJ_BLK], 1, 2)
    scores, thr = pl.pallas_call(
        _dsa_score_kernel,
        grid_spec=pltpu.PrefetchScalarGridSpec(
            num_scalar_prefetch=2,
            grid=(bsz, npairs),
            in_specs=[
                pl.BlockSpec((1, N_IDX_HEADS * IDX_DIM, DSA_TQ), lambda b, t, qt, kt: (b, 0, qt[t])),
                pl.BlockSpec((1, DSA_TK, IDX_DIM), lambda b, t, qt, kt: (b, kt[t], 0)),
                pl.BlockSpec((1, N_IDX_HEADS, DSA_TQ), lambda b, t, qt, kt: (b, 0, qt[t])),
            ],
            out_specs=[
                pl.BlockSpec((1, DSA_TK, DSA_TQ), lambda b, t, qt, kt: (b, kt[t], qt[t])),
                pl.BlockSpec((1, 1, DSA_TQ), lambda b, t, qt, kt: (b, 0, qt[t])),
            ],
            scratch_shapes=[pltpu.VMEM((WORD, s // WORD, DSA_TQ), I32),
                            pltpu.VMEM((s // WORD, DSA_TQ), I32),
                            pltpu.VMEM((DSA_TK, DSA_TQ), I32)],
        ),
        out_shape=[jax.ShapeDtypeStruct((bsz, s, s), F32),
                   jax.ShapeDtypeStruct((bsz, 1, s), F32)],
        compiler_params=_cparams(("arbitrary", "arbitrary")),
        name="dsa_scores",
    )(qs, ks, qi_t, ki, wi_t)

    vt = jnp.swapaxes(main_b[:, :, 2 * PROJ_BLK:3 * PROJ_BLK], 1, 2)
    ntile = bias.shape[1]
    out_t = pl.pallas_call(
        _dsa_attn_kernel,
        grid_spec=pltpu.PrefetchScalarGridSpec(
            num_scalar_prefetch=2,
            grid=(bsz, npairs),
            in_specs=[
                pl.BlockSpec((1, DSA_TK, DSA_TQ), lambda b, t, qt, kt: (b, kt[t], qt[t])),
                pl.BlockSpec((1, 1, DSA_TQ), lambda b, t, qt, kt: (b, 0, qt[t])),
                pl.BlockSpec((1, DSA_TQ, w), lambda b, t, qt, kt: (b, qt[t], 0)),
                pl.BlockSpec((1, DSA_TK, w), lambda b, t, qt, kt: (b, kt[t], 1)),
                pl.BlockSpec((1, w, DSA_TK), lambda b, t, qt, kt: (b, 0, kt[t])),
                pl.BlockSpec((N_HEADS_B, ntile, DSA_TQ, DSA_TQ), lambda b, t, qt, kt: (0, 0, 0, 0),
                             pipeline_mode=pl.Buffered(1)),
                pl.BlockSpec(memory_space=pltpu.SMEM),
            ],
            out_specs=pl.BlockSpec((1, w, DSA_TQ), lambda b, t, qt, kt: (b, 0, qt[t])),
            scratch_shapes=[pltpu.VMEM((N_HEADS_B, HEAD_DIM, DSA_TQ), F32),
                            pltpu.VMEM((N_HEADS_B, 1, DSA_TQ), F32),
                            pltpu.VMEM((N_HEADS_B, 1, DSA_TQ), F32),
                            pltpu.VMEM((N_HEADS_B, DSA_TK, DSA_TQ), F32),
                            pltpu.VMEM((DSA_TK, DSA_TQ), F32),
                            pltpu.VMEM((N_HEADS_B, 1, DSA_TQ), F32)],
        ),
        out_shape=jax.ShapeDtypeStruct((bsz, w, s), BF16),
        compiler_params=_cparams(("arbitrary", "arbitrary")),
        name="dsa_attn",
    )(qs, ks, scores, thr, main_b, main_b, vt, bias, logit_bound)
    return jnp.swapaxes(out_t, 1, 2).reshape(bsz * s, w)


def _out_kernel(x_ref, ma_ref, mb_ref, wa_ref, wb_ref, o_ref):
    o_ref[...] = (x_ref[...]
                  + jnp.dot(ma_ref[...], wa_ref[...], preferred_element_type=F32)
                  + jnp.dot(mb_ref[...], wb_ref[...], preferred_element_type=F32))


def _out_proj(x2, mix_a, mix_b, w_out, tm=512):
    m, d = x2.shape
    row = lambda width: pl.BlockSpec((tm, width), lambda i: (i, 0))
    return pl.pallas_call(
        _out_kernel,
        grid=(m // tm,),
        in_specs=[row(d), row(WIDTH_A), row(WIDTH_B),
                  pl.BlockSpec((WIDTH_A, d), lambda i: (0, 0)),
                  pl.BlockSpec((WIDTH_B, d), lambda i: (1, 0))],
        out_specs=row(d),
        out_shape=jax.ShapeDtypeStruct((m, d), F32),
        compiler_params=_cparams(("parallel",)),
        name="out_proj",
    )(x2, mix_a, mix_b, w_out, w_out)


def kernel(x, rel_bias, norm_ffn1, w_ffn1_in, w_ffn1_out, norm_mix, w_in, q_norm_a, k_norm_a,
           q_norm_b, k_norm_b, w_out, norm_ffn2, w_ffn2_in, w_ffn2_out):
    bsz, s, d = x.shape
    m = bsz * s
    depth = norm_ffn1.shape[0]
    scale = HEAD_DIM ** -0.5
    n_main = PROJ_NBLK * PROJ_BLK
    dil_bias, dsa_bias = _bias_tiles(rel_bias)
    x2 = x.reshape(m, d)
    ones = jnp.ones((PROJ_BLK,), F32)
    tile8 = lambda g: jnp.tile(g, PROJ_BLK // HEAD_DIM)
    for l in range(depth):
        x2 = _ffn(x2, norm_ffn1[l], w_ffn1_in[l].astype(BF16), w_ffn1_out[l].astype(BF16))

        w_main = w_in[l, :, :n_main].astype(BF16)
        w_idx = jnp.pad(w_in[l, :, n_main:], ((0, 0), (0, IDX_PAD - IDX_DIM - N_IDX_HEADS))).astype(BF16)
        head_gain = jnp.stack([tile8(q_norm_a[l] * scale), tile8(k_norm_a[l]), ones,
                               tile8(q_norm_b[l] * (scale * LOG2E)), tile8(k_norm_b[l]), ones, ones])
        main_a, main_b, idx = _proj(x2, norm_mix[l], w_main, w_idx,
                                    head_gain.reshape(PROJ_NBLK, 1, PROJ_BLK))
        main_a = main_a.reshape(bsz, s, PROJ_NA * PROJ_BLK)
        main_b = main_b.reshape(bsz, s, (PROJ_NBLK - PROJ_NA) * PROJ_BLK)
        idx = idx.reshape(bsz, s, IDX_PAD)
        ki = idx[:, :, :IDX_DIM].astype(BF16)
        wi_t = jnp.swapaxes(idx[:, :, IDX_DIM:IDX_DIM + N_IDX_HEADS], 1, 2)

        mix_a = _dilated(main_a, dil_bias)
        logit_bound = (1.02 * HEAD_DIM * jnp.max(jnp.abs(q_norm_b[l] * (scale * LOG2E)))
                       * jnp.max(jnp.abs(k_norm_b[l]))
                       + LOG2E * jnp.max(jnp.abs(rel_bias[:, N_HEADS_A:]))).reshape(1)
        mix_b = _dsa(main_b, ki, wi_t, dsa_bias, logit_bound)

        x2 = _out_proj(x2, mix_a, mix_b, w_out[l].astype(BF16))
        x2 = _ffn(x2, norm_ffn2[l], w_ffn2_in[l].astype(BF16), w_ffn2_out[l].astype(BF16))
    return x2.reshape(bsz, s, d)
```

```python
import functools
import math

import jax
import jax.numpy as jnp
import numpy as np
from jax import lax
from jax.experimental import pallas as pl
from jax.experimental.pallas import tpu as pltpu

F32 = jnp.float32
BF16 = jnp.bfloat16
I32 = jnp.int32

HEAD_DIM = 128
N_HEADS_A = 8
N_HEADS_B = 8
WIDTH_A = N_HEADS_A * HEAD_DIM
WIDTH_B = N_HEADS_B * HEAD_DIM
DILATIONS = (1, 4, 16)
DIL_BLK = 128
N_IDX_HEADS = 16
IDX_DIM = 64
TOPK_MAX = 256
N_BUCKETS = 32
REL_MAX_DIST = 2048
EPS = 1e-6
IDX_PAD = 128
IDX_SCALE = (IDX_DIM ** -0.5) * (N_IDX_HEADS ** -0.5)
LOG2E = math.log2(math.e)

VMEM_LIMIT = 56 * 1024 * 1024

DSA_TQ = 256
DSA_TK = 512
DSA_HALVES = DSA_TK // DSA_TQ
WORD = 32
CHUNK_WORDS = DSA_TK // WORD
SCORE_ROWS = 128
NEG_BIG = -1e30
DSA_SAFE_BOUND = 40.0
DSA_LOOKAHEAD = 2
INT_MIN = -2 ** 31


def _near_tiles():
    exact = N_BUCKETS // 2
    j = 1
    while True:
        d = j * DSA_TQ - (DSA_TQ - 1)
        b = exact + int(math.log(d / exact) / math.log(REL_MAX_DIST / exact) * (N_BUCKETS - exact) - 0.02)
        if d >= exact and b >= N_BUCKETS - 1:
            return j
        j += 1


DSA_NEAR = _near_tiles()


def _cparams(sem):
    return pltpu.CompilerParams(dimension_semantics=sem, vmem_limit_bytes=VMEM_LIMIT)


def _rel_bucket(dist):
    exact = N_BUCKETS // 2
    df = jnp.maximum(dist, 1).astype(F32)
    large = exact + (jnp.log(df / exact) / math.log(REL_MAX_DIST / exact)
                     * (N_BUCKETS - exact)).astype(I32)
    large = jnp.minimum(large, N_BUCKETS - 1)
    return jnp.where(dist < exact, dist, large)


def _lookup(tab_ref, bucket, col):
    val = jnp.zeros(bucket.shape, F32)
    for b in range(N_BUCKETS):
        val = jnp.where(bucket == b, tab_ref[b, col], val)
    return val


def _dil_bias_kernel(tab_ref, o_ref):
    dil = jnp.left_shift(1, 2 * pl.program_id(0))
    i = lax.broadcasted_iota(I32, (DIL_BLK, 2 * DIL_BLK), 0)
    j = lax.broadcasted_iota(I32, (DIL_BLK, 2 * DIL_BLK), 1)
    bucket = _rel_bucket(jnp.clip(DIL_BLK + i - j, 0, DIL_BLK) * dil)
    for h in range(N_HEADS_A):
        o_ref[0, h] = _lookup(tab_ref, bucket, h)


def _dsa_bias_kernel(tab_ref, o_ref):
    jj = pl.program_id(0)
    s = lax.broadcasted_iota(I32, (DSA_TQ, DSA_TQ), 0)
    t = lax.broadcasted_iota(I32, (DSA_TQ, DSA_TQ), 1)
    bucket = _rel_bucket(jnp.maximum((jj - 1) * DSA_TQ + t - s, 0))
    for h in range(N_HEADS_B):
        o_ref[h, 0] = _lookup(tab_ref, bucket, N_HEADS_A + h) * LOG2E


def _bias_tiles(rel_bias):
    smem = pl.BlockSpec(memory_space=pltpu.SMEM)
    dil = pl.pallas_call(
        _dil_bias_kernel,
        grid=(len(DILATIONS),),
        in_specs=[smem],
        out_specs=pl.BlockSpec((1, N_HEADS_A, DIL_BLK, 2 * DIL_BLK), lambda g: (g, 0, 0, 0)),
        out_shape=jax.ShapeDtypeStruct((len(DILATIONS), N_HEADS_A, DIL_BLK, 2 * DIL_BLK), F32),
        name="dil_bias",
    )(rel_bias)
    ntile = DSA_NEAR + 2
    dsa = pl.pallas_call(
        _dsa_bias_kernel,
        grid=(ntile,),
        in_specs=[smem],
        out_specs=pl.BlockSpec((N_HEADS_B, 1, DSA_TQ, DSA_TQ), lambda g: (0, g, 0, 0)),
        out_shape=jax.ShapeDtypeStruct((N_HEADS_B, ntile, DSA_TQ, DSA_TQ), F32),
        name="dsa_bias",
    )(rel_bias)
    return dil, dsa


def _ffn_kernel(x_ref, g_ref, wg_ref, wu_ref, wo_ref, o_ref, h_ref, acc_ref):
    f = pl.program_id(1)

    @pl.when(f == 0)
    def _():
        x = x_ref[...]
        ms = jnp.mean(x * x, axis=-1, keepdims=True)
        h_ref[...] = (x * lax.rsqrt(ms + EPS) * g_ref[...]).astype(BF16)
        acc_ref[...] = jnp.zeros_like(acc_ref)

    h = h_ref[...]
    gate = jnp.dot(h, wg_ref[...], preferred_element_type=F32)
    up = jnp.dot(h, wu_ref[...], preferred_element_type=F32)
    act = (gate * jax.nn.sigmoid(gate) * up).astype(BF16)
    acc_ref[...] += jnp.dot(act, wo_ref[...], preferred_element_type=F32)

    @pl.when(f == pl.num_programs(1) - 1)
    def _():
        o_ref[...] = x_ref[...] + 0.5 * acc_ref[...]


def _ffn(x2, gain, w_in, w_out, tm=512, tf=512):
    m, d = x2.shape
    d_ff = w_out.shape[0]
    nf = d_ff // tf
    return pl.pallas_call(
        _ffn_kernel,
        grid=(m // tm, nf),
        in_specs=[
            pl.BlockSpec((tm, d), lambda i, f: (i, 0)),
            pl.BlockSpec((1, d), lambda i, f: (0, 0)),
            pl.BlockSpec((d, tf), lambda i, f: (0, f)),
            pl.BlockSpec((d, tf), lambda i, f: (0, f + nf)),
            pl.BlockSpec((tf, d), lambda i, f: (f, 0)),
        ],
        out_specs=pl.BlockSpec((tm, d), lambda i, f: (i, 0)),
        out_shape=jax.ShapeDtypeStruct((m, d), F32),
        scratch_shapes=[pltpu.VMEM((tm, d), BF16), pltpu.VMEM((tm, d), F32)],
        compiler_params=_cparams(("parallel", "arbitrary")),
        name="ffn",
    )(x2, gain.reshape(1, d), w_in, w_in, w_out)


PROJ_BLK = 1024
PROJ_NBLK = 7
PROJ_NA = 3
_NORM_BLOCKS = (0, 1, 3, 4)


def _proj_kernel(x_ref, g_ref, w_ref, widx_ref, hg_ref, oa_ref, ob_ref, idx_ref, h_ref):
    j = pl.program_id(1)

    @pl.when(j == 0)
    def _():
        x = x_ref[...]
        ms = jnp.mean(x * x, axis=-1, keepdims=True)
        h_ref[...] = (x * lax.rsqrt(ms + EPS) * g_ref[...]).astype(BF16)

    h = h_ref[...]
    y = jnp.dot(h, w_ref[...], preferred_element_type=F32)
    is_norm = functools.reduce(jnp.logical_or, [j == b for b in _NORM_BLOCKS])

    def emit(dst_ref):
        @pl.when(is_norm)
        def _():
            for hd in range(PROJ_BLK // HEAD_DIM):
                sl = slice(hd * HEAD_DIM, (hd + 1) * HEAD_DIM)
                yh = y[:, sl]
                ms = jnp.mean(yh * yh, axis=-1, keepdims=True)
                dst_ref[:, sl] = (yh * lax.rsqrt(ms + EPS) * hg_ref[0, :, sl]).astype(BF16)

        @pl.when(jnp.logical_not(is_norm))
        def _():
            dst_ref[...] = y.astype(BF16)

    pl.when(j < PROJ_NA)(lambda: emit(oa_ref))
    pl.when(j >= PROJ_NA)(lambda: emit(ob_ref))

    @pl.when(j == PROJ_NBLK - 1)
    def _():
        idx_ref[...] = jnp.dot(h, widx_ref[...], preferred_element_type=F32)


def _proj(x2, gain, w_main, w_idx, head_gain, tm=1024):
    m, d = x2.shape
    return pl.pallas_call(
        _proj_kernel,
        grid=(m // tm, PROJ_NBLK),
        in_specs=[
            pl.BlockSpec((tm, d), lambda i, j: (i, 0)),
            pl.BlockSpec((1, d), lambda i, j: (0, 0)),
            pl.BlockSpec((d, PROJ_BLK), lambda i, j: (0, j)),
            pl.BlockSpec((d, IDX_PAD), lambda i, j: (0, 0)),
            pl.BlockSpec((1, 1, PROJ_BLK), lambda i, j: (j, 0, 0)),
        ],
        out_specs=[
            pl.BlockSpec((tm, PROJ_BLK), lambda i, j: (i, jnp.minimum(j, PROJ_NA - 1))),
            pl.BlockSpec((tm, PROJ_BLK), lambda i, j: (i, jnp.maximum(j - PROJ_NA, 0))),
            pl.BlockSpec((tm, IDX_PAD), lambda i, j: (i, 0)),
        ],
        out_shape=[
            jax.ShapeDtypeStruct((m, PROJ_NA * PROJ_BLK), BF16),
            jax.ShapeDtypeStruct((m, (PROJ_NBLK - PROJ_NA) * PROJ_BLK), BF16),
            jax.ShapeDtypeStruct((m, IDX_PAD), F32),
        ],
        scratch_shapes=[pltpu.VMEM((tm, d), BF16)],
        compiler_params=_cparams(("parallel", "arbitrary")),
        name="proj",
    )(x2, gain.reshape(1, d), w_main, w_idx, head_gain)


_NT = (((1,), (1,)), ((), ()))


DIL_SPAN = DIL_BLK * max(DILATIONS)
DIL_G = 2
DIL_UNITS = DIL_SPAN // DIL_BLK
DIL_UNROLL = 4


def _dilated_kernel(q_ref, kp_ref, kc_ref, vp_ref, vc_ref, b_ref, o_ref, qf, kf, vf, ob, lse_ref):
    n = pl.program_id(1)
    h0 = pl.program_id(2) * DIL_G
    for g in range(DIL_G):
        sl = slice(g * HEAD_DIM, (g + 1) * HEAD_DIM)
        qf[g] = q_ref[0, :, sl].astype(F32)
        kf[g, :DIL_SPAN] = kp_ref[0, :, sl].astype(F32)
        kf[g, DIL_SPAN:] = kc_ref[0, :, sl].astype(F32)
        vf[g, :DIL_SPAN] = vp_ref[0, :, sl].astype(F32)
        vf[g, DIL_SPAN:] = vc_ref[0, :, sl].astype(F32)
    i = lax.broadcasted_iota(I32, (DIL_BLK, 2 * DIL_BLK), 0)
    j = lax.broadcasted_iota(I32, (DIL_BLK, 2 * DIL_BLK), 1)
    band = jnp.logical_and(j >= i, j <= i + DIL_BLK)
    first = j >= DIL_BLK

    for br, dil in enumerate(DILATIONS):
        span = DIL_BLK * dil

        def unit(idx, carry, br=br, dil=dil, span=span):
            blk = idx // dil
            base = blk * span + idx % dil
            valid = jnp.logical_and(band, jnp.logical_or(first, n * (DIL_SPAN // span) + blk > 0))
            qrows = pl.ds(base, DIL_BLK, stride=dil)
            krows = pl.ds(DIL_SPAN + base - span, 2 * DIL_BLK, stride=dil)
            for g in range(DIL_G):
                q = qf[g, qrows, :].astype(BF16)
                k = kf[g, krows, :].astype(BF16)
                v = vf[g, krows, :].astype(BF16)
                s = lax.dot_general(q, k, _NT, preferred_element_type=F32) + b_ref[br, h0 + g]
                s = jnp.where(valid, s, -jnp.inf)
                mx = jnp.max(s, axis=-1, keepdims=True)
                p = jnp.exp(s - mx)
                den = jnp.sum(p, axis=-1, keepdims=True)
                ob[br, g, qrows, :] = jnp.dot(p.astype(BF16), v, preferred_element_type=F32) / den
                lse_ref[br, g, qrows, :] = jnp.broadcast_to(mx + jnp.log(den), (DIL_BLK, HEAD_DIM))
            return carry

        lax.fori_loop(0, DIL_UNITS, unit, 0, unroll=DIL_UNROLL)

    for g in range(DIL_G):
        l1, l2, l3 = lse_ref[0, g], lse_ref[1, g], lse_ref[2, g]
        mx = jnp.maximum(jnp.maximum(l1, l2), l3)
        e1, e2, e3 = jnp.exp(l1 - mx), jnp.exp(l2 - mx), jnp.exp(l3 - mx)
        mix = (e1 * ob[0, g] + e2 * ob[1, g] + e3 * ob[2, g]) / (e1 + e2 + e3)
        o_ref[0, :, g * HEAD_DIM:(g + 1) * HEAD_DIM] = mix.astype(BF16)


def _dilated(main_a, bias):
    bsz, s, _ = main_a.shape
    gw = DIL_G * HEAD_DIM
    ng = WIDTH_A // gw
    cur = lambda c: pl.BlockSpec((1, DIL_SPAN, gw), lambda b, n, g: (b, n, c * ng + g))
    prev = lambda c: pl.BlockSpec((1, DIL_SPAN, gw), lambda b, n, g: (b, jnp.maximum(n - 1, 0), c * ng + g))
    out = pl.pallas_call(
        _dilated_kernel,
        grid=(bsz, s // DIL_SPAN, ng),
        in_specs=[cur(0), prev(1), cur(1), prev(2), cur(2),
                  pl.BlockSpec(bias.shape, lambda b, n, g: (0, 0, 0, 0), pipeline_mode=pl.Buffered(1))],
        out_specs=pl.BlockSpec((1, DIL_SPAN, gw), lambda b, n, g: (b, n, g)),
        out_shape=jax.ShapeDtypeStruct((bsz, s, WIDTH_A), BF16),
        scratch_shapes=[pltpu.VMEM((DIL_G, DIL_SPAN, HEAD_DIM), F32),
                        pltpu.VMEM((DIL_G, 2 * DIL_SPAN, HEAD_DIM), F32),
                        pltpu.VMEM((DIL_G, 2 * DIL_SPAN, HEAD_DIM), F32),
                        pltpu.VMEM((len(DILATIONS), DIL_G, DIL_SPAN, HEAD_DIM), F32),
                        pltpu.VMEM((len(DILATIONS), DIL_G, DIL_SPAN, HEAD_DIM), F32)],
        compiler_params=_cparams(("parallel", "parallel", "parallel")),
        name="dilated",
    )(main_a, main_a, main_a, main_a, main_a, bias)
    return out.reshape(bsz * s, WIDTH_A)


def _dsa_schedule(s):
    qs, ks = [], []
    for i in range(s // DSA_TQ):
        last = (i * DSA_TQ + DSA_TQ - 1) // DSA_TK
        for k in range(last + 1):
            qs.append(i)
            ks.append(k)
    return np.asarray(qs, np.int32), np.asarray(ks, np.int32)


def _sortable(bits):
    return bits ^ (lax.shift_right_arithmetic(bits, 31) & 0x7FFFFFFF)


def _bit_transpose(words):
    a = list(words)
    mask, j = 0x0000FFFF, 16
    while j:
        k = 0
        while k < WORD:
            t = (a[k] ^ lax.shift_right_logical(a[k + j], j)) & mask
            a[k] = a[k] ^ t
            a[k + j] = a[k + j] ^ lax.shift_left(t, j)
            k = (k + j + 1) & ~j
        j >>= 1
        mask = (mask ^ (mask << j)) & 0xFFFFFFFF
    return a


def _dsa_score_kernel(qtab, ktab, qit_ref, ki_ref, w_ref, sc_ref, thr_ref,
                      planes_ref, alive_ref, stage_ref):
    step = pl.program_id(1)
    qb = qtab[step]
    kc = ktab[step]
    w = w_ref[0] * IDX_SCALE
    srow = lax.broadcasted_iota(I32, (SCORE_ROWS, DSA_TQ), 0)
    tcol = lax.broadcasted_iota(I32, (SCORE_ROWS, DSA_TQ), 1)
    row0 = pl.multiple_of(kc * CHUNK_WORDS, CHUNK_WORDS)
    group = WORD * 8
    for g in range(DSA_TK // group):
        for r0 in range(g * group, (g + 1) * group, SCORE_ROWS):
            kidx = ki_ref[0, r0:r0 + SCORE_ROWS, :]
            acc = jnp.zeros((SCORE_ROWS, DSA_TQ), F32)
            for h in range(N_IDX_HEADS):
                d = jnp.dot(kidx, qit_ref[0, h * IDX_DIM:(h + 1) * IDX_DIM, :],
                            preferred_element_type=F32)
                acc = acc + jnp.maximum(d, 0.0) * w[h:h + 1, :]
            causal = srow - tcol <= qb * DSA_TQ - kc * DSA_TK - r0
            acc = jnp.where(causal, acc, -jnp.inf)
            sc_ref[r0:r0 + SCORE_ROWS, :] = acc
            stage_ref[r0:r0 + SCORE_ROWS, :] = _sortable(lax.bitcast_convert_type(acc, I32)) ^ INT_MIN
        planes = _bit_transpose([stage_ref[g * group + 8 * n:g * group + 8 * (n + 1), :]
                                 for n in range(WORD)])
        for n in range(WORD):
            planes_ref[n, pl.ds(pl.multiple_of(row0 + 8 * g, 8), 8), :] = planes[n]

    last = (qb * DSA_TQ + DSA_TQ - 1) // DSA_TK

    @pl.when(kc == last)
    def _():
        nwords = alive_ref.shape[0]
        rb = min(128, nwords)
        nrows = (last + 1) * CHUNK_WORDS
        nblk = (nrows + rb - 1) // rb
        tpos = qb * DSA_TQ + lax.broadcasted_iota(I32, (1, DSA_TQ), 1)
        want0 = jnp.minimum(tpos + 1, TOPK_MAX)
        rows = lambda blk: pl.ds(pl.multiple_of(blk * rb, rb), rb)

        def hits(alive, n, blk):
            hit = lax.population_count(alive & planes_ref[n, rows(blk), :])
            return hit.reshape(rb // 8, 8, DSA_TQ).sum(axis=0)

        def init(blk, cnt):
            ridx = blk * rb + lax.broadcasted_iota(I32, (rb, DSA_TQ), 0)
            alive = jnp.where(ridx < nrows, -1, 0)
            alive_ref[rows(blk), :] = alive
            return cnt + hits(alive, 0, blk)
        zero = jnp.zeros((8, DSA_TQ), I32)
        cnt0 = lax.fori_loop(0, nblk, init, zero)

        def bit_step(n, carry):
            r, want, cnt8 = carry
            cnt = cnt8.sum(axis=0, keepdims=True)
            take = cnt >= want
            r = r | jnp.where(take, lax.shift_left(jnp.int32(1), 31 - n), 0)
            want = jnp.where(take, want, want - cnt)
            flip = jnp.where(take, 0, -1)
            nxt = jnp.minimum(n + 1, WORD - 1)

            def sweep(blk, acc):
                alive = alive_ref[rows(blk), :] & (planes_ref[n, rows(blk), :] ^ flip)
                alive_ref[rows(blk), :] = alive
                return acc + hits(alive, nxt, blk)
            return r, want, lax.fori_loop(0, nblk, sweep, zero)

        r, _, _ = lax.fori_loop(0, WORD, bit_step, (jnp.zeros((1, DSA_TQ), I32), want0, cnt0))
        thr_ref[0] = lax.bitcast_convert_type(_sortable(r ^ INT_MIN), F32)


def _dsa_attn_kernel(qtab, ktab, sc_ref, thr_ref, q_ref, k_ref, vt_ref, b_ref, bound_ref, o_ref,
                     acc_ref, m_ref, l_ref, s_ref, mask_ref, cm_ref):
    step = pl.program_id(1)
    qb = qtab[step]
    kc = ktab[step]

    @pl.when(kc == 0)
    def _():
        acc_ref[...] = jnp.zeros_like(acc_ref)
        m_ref[...] = jnp.full_like(m_ref, NEG_BIG)
        l_ref[...] = jnp.zeros_like(l_ref)

    bound = bound_ref[0]
    bounded = bound <= DSA_SAFE_BOUND

    def set_mask(kept):
        mask_ref[...] = jnp.where(sc_ref[...] >= thr_ref[0], kept, -jnp.inf)

    tiles = [jnp.clip(qb - (kc * DSA_HALVES + hh), -1, DSA_NEAR) + 1 for hh in range(DSA_HALVES)]
    far = qb - (kc * DSA_HALVES + DSA_HALVES - 1) >= DSA_NEAR

    def logits(h):
        sl = slice(h * HEAD_DIM, (h + 1) * HEAD_DIM)
        s = lax.dot_general(k_ref[0, :, sl], q_ref[0, :, sl], _NT, preferred_element_type=F32)
        return s + mask_ref[...]

    def near_bias(h):
        return jnp.concatenate([b_ref[h, tiles[hh]] for hh in range(DSA_HALVES)], axis=0)

    def values(h):
        return vt_ref[h * HEAD_DIM:(h + 1) * HEAD_DIM, :]

    @pl.when(bounded)
    def _():
        def raw_logits(h):
            sl = slice(h * HEAD_DIM, (h + 1) * HEAD_DIM)
            s_ref[h] = lax.dot_general(k_ref[0, :, sl], q_ref[0, :, sl], _NT,
                                       preferred_element_type=F32)

        def heads(weights):
            for h in range(DSA_LOOKAHEAD):
                raw_logits(h)
            set_mask(-bound)
            for h in range(N_HEADS_B):
                if h + DSA_LOOKAHEAD < N_HEADS_B:
                    raw_logits(h + DSA_LOOKAHEAD)
                p, gain = weights(h)
                l_ref[h] += gain * jnp.sum(p, axis=0, keepdims=True)
                acc_ref[h] += gain * jnp.dot(values(h), p.astype(BF16), preferred_element_type=F32)

        @pl.when(far)
        def _():
            heads(lambda h: (jnp.exp2(s_ref[h] + mask_ref[...]),
                             jnp.exp2(b_ref[h, DSA_NEAR + 1, 0:1, :])))

        @pl.when(jnp.logical_not(far))
        def _():
            heads(lambda h: (jnp.exp2(s_ref[h] + mask_ref[...] + near_bias(h)), 1.0))

    @pl.when(jnp.logical_not(bounded))
    def _():
        set_mask(0.0)

        @pl.when(far)
        def _():
            for h in range(N_HEADS_B):
                s = logits(h)
                s_ref[h] = s
                cm_ref[h] = jnp.max(s, axis=0, keepdims=True) + b_ref[h, DSA_NEAR + 1, 0:1, :]

        @pl.when(jnp.logical_not(far))
        def _():
            for h in range(N_HEADS_B):
                s = logits(h) + near_bias(h)
                s_ref[h] = s
                cm_ref[h] = jnp.max(s, axis=0, keepdims=True)

        for h in range(N_HEADS_B):
            m_old = m_ref[h]
            m_new = jnp.maximum(m_old, cm_ref[h])
            alpha = jnp.exp2(m_old - m_new)
            m_ref[h] = m_new
            shift = jnp.where(far, m_new - b_ref[h, DSA_NEAR + 1, 0:1, :], m_new)
            p = jnp.exp2(s_ref[h] - shift)
            l_ref[h] = alpha * l_ref[h] + jnp.sum(p, axis=0, keepdims=True)
            pv = jnp.dot(values(h), p.astype(BF16), preferred_element_type=F32)
            acc_ref[h] = alpha * acc_ref[h] + pv

    last = (qb * DSA_TQ + DSA_TQ - 1) // DSA_TK

    @pl.when(kc == last)
    def _():
        for h in range(N_HEADS_B):
            o_ref[0, h * HEAD_DIM:(h + 1) * HEAD_DIM, :] = (acc_ref[h] / l_ref[h]).astype(BF16)


def _dsa(main_b, ki, wi_t, bias, logit_bound):
    bsz, s, _ = main_b.shape
    w = WIDTH_B
    qs, ks = _dsa_schedule(s)
    npairs = len(qs)
    qi_t = jnp.swapaxes(main_b[:, :, 3 * PROJ_BLK:4 * PROJ_BLK], 1, 2)
    scores, thr = pl.pallas_call(
        _dsa_score_kernel,
        grid_spec=pltpu.PrefetchScalarGridSpec(
            num_scalar_prefetch=2,
            grid=(bsz, npairs),
            in_specs=[
                pl.BlockSpec((1, N_IDX_HEADS * IDX_DIM, DSA_TQ), lambda b, t, qt, kt: (b, 0, qt[t])),
                pl.BlockSpec((1, DSA_TK, IDX_DIM), lambda b, t, qt, kt: (b, kt[t], 0)),
                pl.BlockSpec((1, N_IDX_HEADS, DSA_TQ), lambda b, t, qt, kt: (b, 0, qt[t])),
            ],
            out_specs=[
                pl.BlockSpec((None, None, None, DSA_TK, DSA_TQ),
                             lambda b, t, qt, kt: (b, kt[t], qt[t], 0, 0)),
                pl.BlockSpec((1, 1, DSA_TQ), lambda b, t, qt, kt: (b, 0, qt[t])),
            ],
            scratch_shapes=[pltpu.VMEM((WORD, s // WORD, DSA_TQ), I32),
                            pltpu.VMEM((s // WORD, DSA_TQ), I32),
                            pltpu.VMEM((DSA_TK, DSA_TQ), I32)],
        ),
        out_shape=[jax.ShapeDtypeStruct((bsz, s // DSA_TK, s // DSA_TQ, DSA_TK, DSA_TQ), F32),
                   jax.ShapeDtypeStruct((bsz, 1, s), F32)],
        compiler_params=_cparams(("arbitrary", "arbitrary")),
        name="dsa_scores",
    )(qs, ks, qi_t, ki, wi_t)

    vt = jnp.swapaxes(main_b[:, :, 2 * PROJ_BLK:3 * PROJ_BLK].reshape(bsz, s // DSA_TK, DSA_TK, w), 2, 3)
    ntile = bias.shape[1]
    out_t = pl.pallas_call(
        _dsa_attn_kernel,
        grid_spec=pltpu.PrefetchScalarGridSpec(
            num_scalar_prefetch=2,
            grid=(bsz, npairs),
            in_specs=[
                pl.BlockSpec((None, None, None, DSA_TK, DSA_TQ),
                             lambda b, t, qt, kt: (b, kt[t], qt[t], 0, 0)),
                pl.BlockSpec((1, 1, DSA_TQ), lambda b, t, qt, kt: (b, 0, qt[t])),
                pl.BlockSpec((1, DSA_TQ, w), lambda b, t, qt, kt: (b, qt[t], 0)),
                pl.BlockSpec((1, DSA_TK, w), lambda b, t, qt, kt: (b, kt[t], 1)),
                pl.BlockSpec((None, None, w, DSA_TK), lambda b, t, qt, kt: (b, kt[t], 0, 0)),
                pl.BlockSpec((N_HEADS_B, ntile, DSA_TQ, DSA_TQ), lambda b, t, qt, kt: (0, 0, 0, 0),
                             pipeline_mode=pl.Buffered(1)),
                pl.BlockSpec(memory_space=pltpu.SMEM),
            ],
            out_specs=pl.BlockSpec((1, w, DSA_TQ), lambda b, t, qt, kt: (b, 0, qt[t])),
            scratch_shapes=[pltpu.VMEM((N_HEADS_B, HEAD_DIM, DSA_TQ), F32),
                            pltpu.VMEM((N_HEADS_B, 1, DSA_TQ), F32),
                            pltpu.VMEM((N_HEADS_B, 1, DSA_TQ), F32),
                            pltpu.VMEM((N_HEADS_B, DSA_TK, DSA_TQ), F32),
                            pltpu.VMEM((DSA_TK, DSA_TQ), F32),
                            pltpu.VMEM((N_HEADS_B, 1, DSA_TQ), F32)],
        ),
        out_shape=jax.ShapeDtypeStruct((bsz, w, s), BF16),
        compiler_params=_cparams(("arbitrary", "arbitrary")),
        name="dsa_attn",
    )(qs, ks, scores, thr, main_b, main_b, vt, bias, logit_bound)
    return jnp.swapaxes(out_t, 1, 2).reshape(bsz * s, w)


def _out_kernel(x_ref, ma_ref, mb_ref, wa_ref, wb_ref, o_ref):
    o_ref[...] = (x_ref[...]
                  + jnp.dot(ma_ref[...], wa_ref[...], preferred_element_type=F32)
                  + jnp.dot(mb_ref[...], wb_ref[...], preferred_element_type=F32))


def _out_proj(x2, mix_a, mix_b, w_out, tm=512):
    m, d = x2.shape
    row = lambda width: pl.BlockSpec((tm, width), lambda i: (i, 0))
    return pl.pallas_call(
        _out_kernel,
        grid=(m // tm,),
        in_specs=[row(d), row(WIDTH_A), row(WIDTH_B),
                  pl.BlockSpec((WIDTH_A, d), lambda i: (0, 0)),
                  pl.BlockSpec((WIDTH_B, d), lambda i: (1, 0))],
        out_specs=row(d),
        out_shape=jax.ShapeDtypeStruct((m, d), F32),
        compiler_params=_cparams(("parallel",)),
        name="out_proj",
    )(x2, mix_a, mix_b, w_out, w_out)


def kernel(x, rel_bias, norm_ffn1, w_ffn1_in, w_ffn1_out, norm_mix, w_in, q_norm_a, k_norm_a,
           q_norm_b, k_norm_b, w_out, norm_ffn2, w_ffn2_in, w_ffn2_out):
    bsz, s, d = x.shape
    m = bsz * s
    depth = norm_ffn1.shape[0]
    scale = HEAD_DIM ** -0.5
    n_main = PROJ_NBLK * PROJ_BLK
    dil_bias, dsa_bias = _bias_tiles(rel_bias)
    x2 = x.reshape(m, d)
    ones = jnp.ones((PROJ_BLK,), F32)
    tile8 = lambda g: jnp.tile(g, PROJ_BLK // HEAD_DIM)
    for l in range(depth):
        x2 = _ffn(x2, norm_ffn1[l], w_ffn1_in[l].astype(BF16), w_ffn1_out[l].astype(BF16))

        w_main = w_in[l, :, :n_main].astype(BF16)
        w_idx = jnp.pad(w_in[l, :, n_main:], ((0, 0), (0, IDX_PAD - IDX_DIM - N_IDX_HEADS))).astype(BF16)
        head_gain = jnp.stack([tile8(q_norm_a[l] * scale), tile8(k_norm_a[l]), ones,
                               tile8(q_norm_b[l] * (scale * LOG2E)), tile8(k_norm_b[l]), ones, ones])
        main_a, main_b, idx = _proj(x2, norm_mix[l], w_main, w_idx,
                                    head_gain.reshape(PROJ_NBLK, 1, PROJ_BLK))
        main_a = main_a.reshape(bsz, s, PROJ_NA * PROJ_BLK)
        main_b = main_b.reshape(bsz, s, (PROJ_NBLK - PROJ_NA) * PROJ_BLK)
        idx = idx.reshape(bsz, s, IDX_PAD)
        ki = idx[:, :, :IDX_DIM].astype(BF16)
        wi_t = jnp.swapaxes(idx[:, :, IDX_DIM:IDX_DIM + N_IDX_HEADS], 1, 2)

        mix_a = _dilated(main_a, dil_bias)
        logit_bound = (1.02 * HEAD_DIM * jnp.max(jnp.abs(q_norm_b[l] * (scale * LOG2E)))
                       * jnp.max(jnp.abs(k_norm_b[l]))
                       + LOG2E * jnp.max(jnp.abs(rel_bias[:, N_HEADS_A:]))).reshape(1)
        mix_b = _dsa(main_b, ki, wi_t, dsa_bias, logit_bound)

        x2 = _out_proj(x2, mix_a, mix_b, w_out[l].astype(BF16))
        x2 = _ffn(x2, norm_ffn2[l], w_ffn2_in[l].astype(BF16), w_ffn2_out[l].astype(BF16))
    return x2.reshape(bsz, s, d)
```

```python
import functools
import math

import jax
import jax.numpy as jnp
import numpy as np
from jax import lax
from jax.experimental import pallas as pl
from jax.experimental.pallas import tpu as pltpu

F32 = jnp.float32
BF16 = jnp.bfloat16
I32 = jnp.int32

HEAD_DIM = 128
N_HEADS_A = 8
N_HEADS_B = 8
WIDTH_A = N_HEADS_A * HEAD_DIM
WIDTH_B = N_HEADS_B * HEAD_DIM
DILATIONS = (1, 4, 16)
DIL_BLK = 128
N_IDX_HEADS = 16
IDX_DIM = 64
TOPK_MAX = 256
N_BUCKETS = 32
REL_MAX_DIST = 2048
EPS = 1e-6
IDX_PAD = 128
IDX_SCALE = (IDX_DIM ** -0.5) * (N_IDX_HEADS ** -0.5)
LOG2E = math.log2(math.e)

VMEM_LIMIT = 56 * 1024 * 1024

DSA_TQ = 256
DSA_TK = 512
DSA_HALVES = DSA_TK // DSA_TQ
ATT_TQ = 512
ATT_QT = ATT_TQ // DSA_TQ
WORD = 32
CHUNK_WORDS = DSA_TK // WORD
SCORE_ROWS = 128
NEG_BIG = -1e30
DSA_SAFE_BOUND = 40.0
DSA_LOOKAHEAD = 2
INT_MIN = -2 ** 31


def _near_tiles():
    exact = N_BUCKETS // 2
    j = 1
    while True:
        d = j * DSA_TQ - (DSA_TQ - 1)
        b = exact + int(math.log(d / exact) / math.log(REL_MAX_DIST / exact) * (N_BUCKETS - exact) - 0.02)
        if d >= exact and b >= N_BUCKETS - 1:
            return j
        j += 1


DSA_NEAR = _near_tiles()


def _cparams(sem):
    return pltpu.CompilerParams(dimension_semantics=sem, vmem_limit_bytes=VMEM_LIMIT)


def _rel_bucket(dist):
    exact = N_BUCKETS // 2
    df = jnp.maximum(dist, 1).astype(F32)
    large = exact + (jnp.log(df / exact) / math.log(REL_MAX_DIST / exact)
                     * (N_BUCKETS - exact)).astype(I32)
    large = jnp.minimum(large, N_BUCKETS - 1)
    return jnp.where(dist < exact, dist, large)


def _lookup(tab_ref, bucket, col):
    val = jnp.zeros(bucket.shape, F32)
    for b in range(N_BUCKETS):
        val = jnp.where(bucket == b, tab_ref[b, col], val)
    return val


def _dil_bias_kernel(tab_ref, o_ref):
    dil = jnp.left_shift(1, 2 * pl.program_id(0))
    i = lax.broadcasted_iota(I32, (DIL_BLK, 2 * DIL_BLK), 0)
    j = lax.broadcasted_iota(I32, (DIL_BLK, 2 * DIL_BLK), 1)
    bucket = _rel_bucket(jnp.clip(DIL_BLK + i - j, 0, DIL_BLK) * dil)
    for h in range(N_HEADS_A):
        o_ref[0, h] = _lookup(tab_ref, bucket, h)


def _dsa_bias_kernel(tab_ref, o_ref):
    jj = pl.program_id(0)
    s = lax.broadcasted_iota(I32, (DSA_TQ, DSA_TQ), 0)
    t = lax.broadcasted_iota(I32, (DSA_TQ, DSA_TQ), 1)
    bucket = _rel_bucket(jnp.maximum((jj - 1) * DSA_TQ + t - s, 0))
    for h in range(N_HEADS_B):
        o_ref[h, 0] = _lookup(tab_ref, bucket, N_HEADS_A + h) * LOG2E


def _bias_tiles(rel_bias):
    smem = pl.BlockSpec(memory_space=pltpu.SMEM)
    dil = pl.pallas_call(
        _dil_bias_kernel,
        grid=(len(DILATIONS),),
        in_specs=[smem],
        out_specs=pl.BlockSpec((1, N_HEADS_A, DIL_BLK, 2 * DIL_BLK), lambda g: (g, 0, 0, 0)),
        out_shape=jax.ShapeDtypeStruct((len(DILATIONS), N_HEADS_A, DIL_BLK, 2 * DIL_BLK), F32),
        name="dil_bias",
    )(rel_bias)
    ntile = DSA_NEAR + 2
    dsa = pl.pallas_call(
        _dsa_bias_kernel,
        grid=(ntile,),
        in_specs=[smem],
        out_specs=pl.BlockSpec((N_HEADS_B, 1, DSA_TQ, DSA_TQ), lambda g: (0, g, 0, 0)),
        out_shape=jax.ShapeDtypeStruct((N_HEADS_B, ntile, DSA_TQ, DSA_TQ), F32),
        name="dsa_bias",
    )(rel_bias)
    return dil, dsa


def _ffn_kernel(x_ref, g_ref, wg_ref, wu_ref, wo_ref, o_ref, h_ref, acc_ref):
    f = pl.program_id(1)

    @pl.when(f == 0)
    def _():
        x = x_ref[...]
        ms = jnp.mean(x * x, axis=-1, keepdims=True)
        h_ref[...] = (x * lax.rsqrt(ms + EPS) * g_ref[...]).astype(BF16)
        acc_ref[...] = jnp.zeros_like(acc_ref)

    h = h_ref[...]
    gate = jnp.dot(h, wg_ref[...], preferred_element_type=F32)
    up = jnp.dot(h, wu_ref[...], preferred_element_type=F32)
    act = (gate * jax.nn.sigmoid(gate) * up).astype(BF16)
    acc_ref[...] += jnp.dot(act, wo_ref[...], preferred_element_type=F32)

    @pl.when(f == pl.num_programs(1) - 1)
    def _():
        o_ref[...] = x_ref[...] + 0.5 * acc_ref[...]


def _ffn(x2, gain, w_in, w_out, tm=512, tf=512):
    m, d = x2.shape
    d_ff = w_out.shape[0]
    nf = d_ff // tf
    return pl.pallas_call(
        _ffn_kernel,
        grid=(m // tm, nf),
        in_specs=[
            pl.BlockSpec((tm, d), lambda i, f: (i, 0)),
            pl.BlockSpec((1, d), lambda i, f: (0, 0)),
            pl.BlockSpec((d, tf), lambda i, f: (0, f)),
            pl.BlockSpec((d, tf), lambda i, f: (0, f + nf)),
            pl.BlockSpec((tf, d), lambda i, f: (f, 0)),
        ],
        out_specs=pl.BlockSpec((tm, d), lambda i, f: (i, 0)),
        out_shape=jax.ShapeDtypeStruct((m, d), F32),
        scratch_shapes=[pltpu.VMEM((tm, d), BF16), pltpu.VMEM((tm, d), F32)],
        compiler_params=_cparams(("parallel", "arbitrary")),
        name="ffn",
    )(x2, gain.reshape(1, d), w_in, w_in, w_out)


PROJ_BLK = 1024
PROJ_NBLK = 7
PROJ_NA = 3
_NORM_BLOCKS = (0, 1, 3, 4)


def _proj_kernel(x_ref, g_ref, w_ref, widx_ref, hg_ref, oa_ref, ob_ref, idx_ref, h_ref):
    j = pl.program_id(1)

    @pl.when(j == 0)
    def _():
        x = x_ref[...]
        ms = jnp.mean(x * x, axis=-1, keepdims=True)
        h_ref[...] = (x * lax.rsqrt(ms + EPS) * g_ref[...]).astype(BF16)

    h = h_ref[...]
    y = jnp.dot(h, w_ref[...], preferred_element_type=F32)
    is_norm = functools.reduce(jnp.logical_or, [j == b for b in _NORM_BLOCKS])

    def emit(dst_ref):
        @pl.when(is_norm)
        def _():
            for hd in range(PROJ_BLK // HEAD_DIM):
                sl = slice(hd * HEAD_DIM, (hd + 1) * HEAD_DIM)
                yh = y[:, sl]
                ms = jnp.mean(yh * yh, axis=-1, keepdims=True)
                dst_ref[:, sl] = (yh * lax.rsqrt(ms + EPS) * hg_ref[0, :, sl]).astype(BF16)

        @pl.when(jnp.logical_not(is_norm))
        def _():
            dst_ref[...] = y.astype(BF16)

    pl.when(j < PROJ_NA)(lambda: emit(oa_ref))
    pl.when(j >= PROJ_NA)(lambda: emit(ob_ref))

    @pl.when(j == PROJ_NBLK - 1)
    def _():
        idx_ref[...] = jnp.dot(h, widx_ref[...], preferred_element_type=F32)


def _proj(x2, gain, w_main, w_idx, head_gain, tm=1024):
    m, d = x2.shape
    return pl.pallas_call(
        _proj_kernel,
        grid=(m // tm, PROJ_NBLK),
        in_specs=[
            pl.BlockSpec((tm, d), lambda i, j: (i, 0)),
            pl.BlockSpec((1, d), lambda i, j: (0, 0)),
            pl.BlockSpec((d, PROJ_BLK), lambda i, j: (0, j)),
            pl.BlockSpec((d, IDX_PAD), lambda i, j: (0, 0)),
            pl.BlockSpec((1, 1, PROJ_BLK), lambda i, j: (j, 0, 0)),
        ],
        out_specs=[
            pl.BlockSpec((tm, PROJ_BLK), lambda i, j: (i, jnp.minimum(j, PROJ_NA - 1))),
            pl.BlockSpec((tm, PROJ_BLK), lambda i, j: (i, jnp.maximum(j - PROJ_NA, 0))),
            pl.BlockSpec((tm, IDX_PAD), lambda i, j: (i, 0)),
        ],
        out_shape=[
            jax.ShapeDtypeStruct((m, PROJ_NA * PROJ_BLK), BF16),
            jax.ShapeDtypeStruct((m, (PROJ_NBLK - PROJ_NA) * PROJ_BLK), BF16),
            jax.ShapeDtypeStruct((m, IDX_PAD), F32),
        ],
        scratch_shapes=[pltpu.VMEM((tm, d), BF16)],
        compiler_params=_cparams(("parallel", "arbitrary")),
        name="proj",
    )(x2, gain.reshape(1, d), w_main, w_idx, head_gain)


_NT = (((1,), (1,)), ((), ()))


DIL_SPAN = DIL_BLK * max(DILATIONS)
DIL_G = 2
DIL_UNITS = DIL_SPAN // DIL_BLK
DIL_UNROLL = 4


def _dilated_kernel(q_ref, kp_ref, kc_ref, vp_ref, vc_ref, b_ref, o_ref, qf, kf, vf, ob, lse_ref):
    n = pl.program_id(1)
    h0 = pl.program_id(2) * DIL_G
    for g in range(DIL_G):
        sl = slice(g * HEAD_DIM, (g + 1) * HEAD_DIM)
        qf[g] = q_ref[0, :, sl].astype(F32)
        kf[g, :DIL_SPAN] = kp_ref[0, :, sl].astype(F32)
        kf[g, DIL_SPAN:] = kc_ref[0, :, sl].astype(F32)
        vf[g, :DIL_SPAN] = vp_ref[0, :, sl].astype(F32)
        vf[g, DIL_SPAN:] = vc_ref[0, :, sl].astype(F32)
    i = lax.broadcasted_iota(I32, (DIL_BLK, 2 * DIL_BLK), 0)
    j = lax.broadcasted_iota(I32, (DIL_BLK, 2 * DIL_BLK), 1)
    band = jnp.logical_and(j >= i, j <= i + DIL_BLK)
    first = j >= DIL_BLK

    for br, dil in enumerate(DILATIONS):
        span = DIL_BLK * dil

        def unit(idx, carry, br=br, dil=dil, span=span):
            blk = idx // dil
            base = blk * span + idx % dil
            valid = jnp.logical_and(band, jnp.logical_or(first, n * (DIL_SPAN // span) + blk > 0))
            qrows = pl.ds(base, DIL_BLK, stride=dil)
            krows = pl.ds(DIL_SPAN + base - span, 2 * DIL_BLK, stride=dil)
            for g in range(DIL_G):
                q = qf[g, qrows, :].astype(BF16)
                k = kf[g, krows, :].astype(BF16)
                v = vf[g, krows, :].astype(BF16)
                s = lax.dot_general(q, k, _NT, preferred_element_type=F32) + b_ref[br, h0 + g]
                s = jnp.where(valid, s, -jnp.inf)
                mx = jnp.max(s, axis=-1, keepdims=True)
                p = jnp.exp(s - mx)
                den = jnp.sum(p, axis=-1, keepdims=True)
                ob[br, g, qrows, :] = jnp.dot(p.astype(BF16), v, preferred_element_type=F32) / den
                lse_ref[br, g, qrows, :] = jnp.broadcast_to(mx + jnp.log(den), (DIL_BLK, HEAD_DIM))
            return carry

        lax.fori_loop(0, DIL_UNITS, unit, 0, unroll=DIL_UNROLL)

    for g in range(DIL_G):
        l1, l2, l3 = lse_ref[0, g], lse_ref[1, g], lse_ref[2, g]
        mx = jnp.maximum(jnp.maximum(l1, l2), l3)
        e1, e2, e3 = jnp.exp(l1 - mx), jnp.exp(l2 - mx), jnp.exp(l3 - mx)
        mix = (e1 * ob[0, g] + e2 * ob[1, g] + e3 * ob[2, g]) / (e1 + e2 + e3)
        o_ref[0, :, g * HEAD_DIM:(g + 1) * HEAD_DIM] = mix.astype(BF16)


def _dilated(main_a, bias):
    bsz, s, _ = main_a.shape
    gw = DIL_G * HEAD_DIM
    ng = WIDTH_A // gw
    cur = lambda c: pl.BlockSpec((1, DIL_SPAN, gw), lambda b, n, g: (b, n, c * ng + g))
    prev = lambda c: pl.BlockSpec((1, DIL_SPAN, gw), lambda b, n, g: (b, jnp.maximum(n - 1, 0), c * ng + g))
    out = pl.pallas_call(
        _dilated_kernel,
        grid=(bsz, s // DIL_SPAN, ng),
        in_specs=[cur(0), prev(1), cur(1), prev(2), cur(2),
                  pl.BlockSpec(bias.shape, lambda b, n, g: (0, 0, 0, 0), pipeline_mode=pl.Buffered(1))],
        out_specs=pl.BlockSpec((1, DIL_SPAN, gw), lambda b, n, g: (b, n, g)),
        out_shape=jax.ShapeDtypeStruct((bsz, s, WIDTH_A), BF16),
        scratch_shapes=[pltpu.VMEM((DIL_G, DIL_SPAN, HEAD_DIM), F32),
                        pltpu.VMEM((DIL_G, 2 * DIL_SPAN, HEAD_DIM), F32),
                        pltpu.VMEM((DIL_G, 2 * DIL_SPAN, HEAD_DIM), F32),
                        pltpu.VMEM((len(DILATIONS), DIL_G, DIL_SPAN, HEAD_DIM), F32),
                        pltpu.VMEM((len(DILATIONS), DIL_G, DIL_SPAN, HEAD_DIM), F32)],
        compiler_params=_cparams(("parallel", "parallel", "parallel")),
        name="dilated",
    )(main_a, main_a, main_a, main_a, main_a, bias)
    return out.reshape(bsz * s, WIDTH_A)


def _dsa_schedule(s, tq):
    qs, ks = [], []
    for i in range(s // tq):
        last = (i * tq + tq - 1) // DSA_TK
        for k in range(last + 1):
            qs.append(i)
            ks.append(k)
    return np.asarray(qs, np.int32), np.asarray(ks, np.int32)


def _sortable(bits):
    return bits ^ (lax.shift_right_arithmetic(bits, 31) & 0x7FFFFFFF)


def _bit_transpose(words):
    a = list(words)
    mask, j = 0x0000FFFF, 16
    while j:
        k = 0
        while k < WORD:
            t = (a[k] ^ lax.shift_right_logical(a[k + j], j)) & mask
            a[k] = a[k] ^ t
            a[k + j] = a[k + j] ^ lax.shift_left(t, j)
            k = (k + j + 1) & ~j
        j >>= 1
        mask = (mask ^ (mask << j)) & 0xFFFFFFFF
    return a


def _dsa_score_kernel(qtab, ktab, qit_ref, ki_ref, w_ref, sc_ref, thr_ref,
                      planes_ref, alive_ref, stage_ref):
    step = pl.program_id(1)
    qb = qtab[step]
    kc = ktab[step]
    w = w_ref[0] * IDX_SCALE
    srow = lax.broadcasted_iota(I32, (SCORE_ROWS, DSA_TQ), 0)
    tcol = lax.broadcasted_iota(I32, (SCORE_ROWS, DSA_TQ), 1)
    row0 = pl.multiple_of(kc * CHUNK_WORDS, CHUNK_WORDS)
    group = WORD * 8
    for g in range(DSA_TK // group):
        for r0 in range(g * group, (g + 1) * group, SCORE_ROWS):
            kidx = ki_ref[0, r0:r0 + SCORE_ROWS, :]
            acc = jnp.zeros((SCORE_ROWS, DSA_TQ), F32)
            for h in range(N_IDX_HEADS):
                d = jnp.dot(kidx, qit_ref[0, h * IDX_DIM:(h + 1) * IDX_DIM, :],
                            preferred_element_type=F32)
                acc = acc + jnp.maximum(d, 0.0) * w[h:h + 1, :]
            causal = srow - tcol <= qb * DSA_TQ - kc * DSA_TK - r0
            acc = jnp.where(causal, acc, -jnp.inf)
            sc_ref[r0:r0 + SCORE_ROWS, :] = acc
            stage_ref[r0:r0 + SCORE_ROWS, :] = _sortable(lax.bitcast_convert_type(acc, I32)) ^ INT_MIN
        planes = _bit_transpose([stage_ref[g * group + 8 * n:g * group + 8 * (n + 1), :]
                                 for n in range(WORD)])
        for n in range(WORD):
            planes_ref[n, pl.ds(pl.multiple_of(row0 + 8 * g, 8), 8), :] = planes[n]

    last = (qb * DSA_TQ + DSA_TQ - 1) // DSA_TK

    @pl.when(kc == last)
    def _():
        nwords = alive_ref.shape[0]
        rb = min(128, nwords)
        nrows = (last + 1) * CHUNK_WORDS
        nblk = (nrows + rb - 1) // rb
        tpos = qb * DSA_TQ + lax.broadcasted_iota(I32, (1, DSA_TQ), 1)
        want0 = jnp.minimum(tpos + 1, TOPK_MAX)
        rows = lambda blk: pl.ds(pl.multiple_of(blk * rb, rb), rb)

        def hits(alive, n, blk):
            hit = lax.population_count(alive & planes_ref[n, rows(blk), :])
            return hit.reshape(rb // 8, 8, DSA_TQ).sum(axis=0)

        def init(blk, cnt):
            ridx = blk * rb + lax.broadcasted_iota(I32, (rb, DSA_TQ), 0)
            alive = jnp.where(ridx < nrows, -1, 0)
            alive_ref[rows(blk), :] = alive
            return cnt + hits(alive, 0, blk)
        zero = jnp.zeros((8, DSA_TQ), I32)
        cnt0 = lax.fori_loop(0, nblk, init, zero)

        def bit_step(n, carry):
            r, want, cnt8 = carry
            cnt = cnt8.sum(axis=0, keepdims=True)
            take = cnt >= want
            r = r | jnp.where(take, lax.shift_left(jnp.int32(1), 31 - n), 0)
            want = jnp.where(take, want, want - cnt)
            flip = jnp.where(take, 0, -1)
            nxt = jnp.minimum(n + 1, WORD - 1)

            def sweep(blk, acc):
                alive = alive_ref[rows(blk), :] & (planes_ref[n, rows(blk), :] ^ flip)
                alive_ref[rows(blk), :] = alive
                return acc + hits(alive, nxt, blk)
            return r, want, lax.fori_loop(0, nblk, sweep, zero)

        r, _, _ = lax.fori_loop(0, WORD, bit_step, (jnp.zeros((1, DSA_TQ), I32), want0, cnt0))
        thr_ref[0] = lax.bitcast_convert_type(_sortable(r ^ INT_MIN), F32)


def _dsa_attn_kernel(qtab, ktab, sc_ref, thr_ref, q_ref, k_ref, vt_ref, b_ref, bound_ref, o_ref,
                     acc_ref, m_ref, l_ref, s_ref, mask_ref, cm_ref):
    step = pl.program_id(1)
    qb = qtab[step]
    kc = ktab[step]

    @pl.when(kc == 0)
    def _():
        acc_ref[...] = jnp.zeros_like(acc_ref)
        m_ref[...] = jnp.full_like(m_ref, NEG_BIG)
        l_ref[...] = jnp.zeros_like(l_ref)

    bound = bound_ref[0]
    bounded = bound <= DSA_SAFE_BOUND

    def set_mask(kept):
        for b in range(ATT_QT):
            cols = slice(b * DSA_TQ, (b + 1) * DSA_TQ)
            mask_ref[:, cols] = jnp.where(sc_ref[b] >= thr_ref[0, :, cols], kept, -jnp.inf)

    tiles = [[jnp.clip((qb * ATT_QT + b) - (kc * DSA_HALVES + a), -1, DSA_NEAR) + 1
              for b in range(ATT_QT)] for a in range(DSA_HALVES)]
    far = qb * ATT_QT - (kc * DSA_HALVES + DSA_HALVES - 1) >= DSA_NEAR
    far_bias = lambda h: b_ref[h, DSA_NEAR + 1, 0:1, 0:1]

    def logits(h):
        sl = slice(h * HEAD_DIM, (h + 1) * HEAD_DIM)
        s = lax.dot_general(k_ref[0, :, sl], q_ref[0, :, sl], _NT, preferred_element_type=F32)
        return s + mask_ref[...]

    def near_bias(h):
        return jnp.concatenate(
            [jnp.concatenate([b_ref[h, tiles[a][b]] for b in range(ATT_QT)], axis=1)
             for a in range(DSA_HALVES)], axis=0)

    def values(h):
        return vt_ref[h * HEAD_DIM:(h + 1) * HEAD_DIM, :]

    @pl.when(bounded)
    def _():
        def raw_logits(h):
            sl = slice(h * HEAD_DIM, (h + 1) * HEAD_DIM)
            s_ref[h] = lax.dot_general(k_ref[0, :, sl], q_ref[0, :, sl], _NT,
                                       preferred_element_type=F32)

        def heads(weights):
            for h in range(DSA_LOOKAHEAD):
                raw_logits(h)
            set_mask(-bound)
            for h in range(N_HEADS_B):
                if h + DSA_LOOKAHEAD < N_HEADS_B:
                    raw_logits(h + DSA_LOOKAHEAD)
                p, gain = weights(h)
                l_ref[h] += gain * jnp.sum(p, axis=0, keepdims=True)
                acc_ref[h] += gain * jnp.dot(values(h), p.astype(BF16), preferred_element_type=F32)

        @pl.when(far)
        def _():
            heads(lambda h: (jnp.exp2(s_ref[h] + mask_ref[...]), jnp.exp2(far_bias(h))))

        @pl.when(jnp.logical_not(far))
        def _():
            heads(lambda h: (jnp.exp2(s_ref[h] + mask_ref[...] + near_bias(h)), 1.0))

    @pl.when(jnp.logical_not(bounded))
    def _():
        set_mask(0.0)

        @pl.when(far)
        def _():
            for h in range(N_HEADS_B):
                s = logits(h)
                s_ref[h] = s
                cm_ref[h] = jnp.max(s, axis=0, keepdims=True) + far_bias(h)

        @pl.when(jnp.logical_not(far))
        def _():
            for h in range(N_HEADS_B):
                s = logits(h) + near_bias(h)
                s_ref[h] = s
                cm_ref[h] = jnp.max(s, axis=0, keepdims=True)

        for h in range(N_HEADS_B):
            m_old = m_ref[h]
            m_new = jnp.maximum(m_old, cm_ref[h])
            alpha = jnp.exp2(m_old - m_new)
            m_ref[h] = m_new
            shift = jnp.where(far, m_new - far_bias(h), m_new)
            p = jnp.exp2(s_ref[h] - shift)
            l_ref[h] = alpha * l_ref[h] + jnp.sum(p, axis=0, keepdims=True)
            pv = jnp.dot(values(h), p.astype(BF16), preferred_element_type=F32)
            acc_ref[h] = alpha * acc_ref[h] + pv

    last = (qb * ATT_TQ + ATT_TQ - 1) // DSA_TK

    @pl.when(kc == last)
    def _():
        for h in range(N_HEADS_B):
            o_ref[0, h * HEAD_DIM:(h + 1) * HEAD_DIM, :] = (acc_ref[h] / l_ref[h]).astype(BF16)


def _dsa(main_b, ki, wi_t, bias, logit_bound):
    bsz, s, _ = main_b.shape
    w = WIDTH_B
    qs, ks = _dsa_schedule(s, DSA_TQ)
    npairs = len(qs)
    qi_t =jnp.swapaxes(main_b[:, :, 3 * PROJ_BLK:4 * PROJ_BLK], 1, 2)
    scores, thr = pl.pallas_call(
        _dsa_score_kernel,
        grid_spec=pltpu.PrefetchScalarGridSpec(
            num_scalar_prefetch=2,
            grid=(bsz, npairs),
            in_specs=[
                pl.BlockSpec((1, N_IDX_HEADS * IDX_DIM, DSA_TQ), lambda b, t, qt, kt: (b, 0, qt[t])),
                pl.BlockSpec((1, DSA_TK, IDX_DIM), lambda b, t, qt, kt: (b, kt[t], 0)),
                pl.BlockSpec((1, N_IDX_HEADS, DSA_TQ), lambda b, t, qt, kt: (b, 0, qt[t])),
            ],
            out_specs=[
                pl.BlockSpec((None, None, None, DSA_TK, DSA_TQ),
                             lambda b, t, qt, kt: (b, kt[t], qt[t], 0, 0)),
                pl.BlockSpec((1, 1, DSA_TQ), lambda b, t, qt, kt: (b, 0, qt[t])),
            ],
            scratch_shapes=[pltpu.VMEM((WORD, s // WORD, DSA_TQ), I32),
                            pltpu.VMEM((s // WORD, DSA_TQ), I32),
                            pltpu.VMEM((DSA_TK, DSA_TQ), I32)],
        ),
        out_shape=[jax.ShapeDtypeStruct((bsz, s // DSA_TK, s // DSA_TQ, DSA_TK, DSA_TQ), F32),
                   jax.ShapeDtypeStruct((bsz, 1, s), F32)],
        compiler_params=_cparams(("arbitrary", "arbitrary")),
        name="dsa_scores",
    )(qs, ks, qi_t, ki, wi_t)

    vt = jnp.swapaxes(main_b[:, :, 2 * PROJ_BLK:3 * PROJ_BLK].reshape(bsz, s // DSA_TK, DSA_TK, w), 2, 3)
    ntile = bias.shape[1]
    qs, ks = _dsa_schedule(s, ATT_TQ)
    out_t = pl.pallas_call(
        _dsa_attn_kernel,
        grid_spec=pltpu.PrefetchScalarGridSpec(
            num_scalar_prefetch=2,
            grid=(bsz, len(qs)),
            in_specs=[
                pl.BlockSpec((None, None, ATT_QT, DSA_TK, DSA_TQ),
                             lambda b, t, qt, kt: (b, kt[t], qt[t], 0, 0)),
                pl.BlockSpec((1, 1, ATT_TQ), lambda b, t, qt, kt: (b, 0, qt[t])),
                pl.BlockSpec((1, ATT_TQ, w), lambda b, t, qt, kt: (b, qt[t], 0)),
                pl.BlockSpec((1, DSA_TK, w), lambda b, t, qt, kt: (b, kt[t], 1)),
                pl.BlockSpec((None, None, w, DSA_TK), lambda b, t, qt, kt: (b, kt[t], 0, 0)),
                pl.BlockSpec((N_HEADS_B, ntile, DSA_TQ, DSA_TQ), lambda b, t, qt, kt: (0, 0, 0, 0),
                             pipeline_mode=pl.Buffered(1)),
                pl.BlockSpec(memory_space=pltpu.SMEM),
            ],
            out_specs=pl.BlockSpec((1, w, ATT_TQ), lambda b, t, qt, kt: (b, 0, qt[t])),
            scratch_shapes=[pltpu.VMEM((N_HEADS_B, HEAD_DIM, ATT_TQ), F32),
                            pltpu.VMEM((N_HEADS_B, 1, ATT_TQ), F32),
                            pltpu.VMEM((N_HEADS_B, 1, ATT_TQ), F32),
                            pltpu.VMEM((N_HEADS_B, DSA_TK, ATT_TQ), F32),
                            pltpu.VMEM((DSA_TK, ATT_TQ), F32),
                            pltpu.VMEM((N_HEADS_B, 1, ATT_TQ), F32)],
        ),
        out_shape=jax.ShapeDtypeStruct((bsz, w, s), BF16),
        compiler_params=_cparams(("arbitrary", "arbitrary")),
        name="dsa_attn",
    )(qs, ks, scores, thr, main_b, main_b, vt, bias, logit_bound)
    return jnp.swapaxes(out_t, 1, 2).reshape(bsz * s, w)


def _out_kernel(x_ref, ma_ref, mb_ref, wa_ref, wb_ref, o_ref):
    o_ref[...] = (x_ref[...]
                  + jnp.dot(ma_ref[...], wa_ref[...], preferred_element_type=F32)
                  + jnp.dot(mb_ref[...], wb_ref[...], preferred_element_type=F32))


def _out_proj(x2, mix_a, mix_b, w_out, tm=512):
    m, d = x2.shape
    row = lambda width: pl.BlockSpec((tm, width), lambda i: (i, 0))
    return pl.pallas_call(
        _out_kernel,
        grid=(m // tm,),
        in_specs=[row(d), row(WIDTH_A), row(WIDTH_B),
                  pl.BlockSpec((WIDTH_A, d), lambda i: (0, 0)),
                  pl.BlockSpec((WIDTH_B, d), lambda i: (1, 0))],
        out_specs=row(d),
        out_shape=jax.ShapeDtypeStruct((m, d), F32),
        compiler_params=_cparams(("parallel",)),
        name="out_proj",
    )(x2, mix_a, mix_b, w_out, w_out)


def kernel(x, rel_bias, norm_ffn1, w_ffn1_in, w_ffn1_out, norm_mix, w_in, q_norm_a, k_norm_a,
           q_norm_b, k_norm_b, w_out, norm_ffn2, w_ffn2_in, w_ffn2_out):
    bsz, s, d = x.shape
    m = bsz * s
    depth = norm_ffn1.shape[0]
    scale = HEAD_DIM ** -0.5
    n_main = PROJ_NBLK * PROJ_BLK
    dil_bias, dsa_bias = _bias_tiles(rel_bias)
    x2 = x.reshape(m, d)
    ones = jnp.ones((PROJ_BLK,), F32)
    tile8 = lambda g: jnp.tile(g, PROJ_BLK // HEAD_DIM)
    for l in range(depth):
        x2 = _ffn(x2, norm_ffn1[l], w_ffn1_in[l].astype(BF16), w_ffn1_out[l].astype(BF16))

        w_main = w_in[l, :, :n_main].astype(BF16)
        w_idx = jnp.pad(w_in[l, :, n_main:], ((0, 0), (0, IDX_PAD - IDX_DIM - N_IDX_HEADS))).astype(BF16)
        head_gain = jnp.stack([tile8(q_norm_a[l] * scale), tile8(k_norm_a[l]), ones,
                               tile8(q_norm_b[l] * (scale * LOG2E)), tile8(k_norm_b[l]), ones, ones])
        main_a, main_b, idx = _proj(x2, norm_mix[l], w_main, w_idx,
                                    head_gain.reshape(PROJ_NBLK, 1, PROJ_BLK))
        main_a = main_a.reshape(bsz, s, PROJ_NA * PROJ_BLK)
        main_b = main_b.reshape(bsz, s, (PROJ_NBLK - PROJ_NA) * PROJ_BLK)
        idx = idx.reshape(bsz, s, IDX_PAD)
        ki = idx[:, :, :IDX_DIM].astype(BF16)
        wi_t = jnp.swapaxes(idx[:, :, IDX_DIM:IDX_DIM + N_IDX_HEADS], 1, 2)

        mix_a = _dilated(main_a, dil_bias)
        logit_bound = (1.02 * HEAD_DIM * jnp.max(jnp.abs(q_norm_b[l] * (scale * LOG2E)))
                       * jnp.max(jnp.abs(k_norm_b[l]))
                       + LOG2E * jnp.max(jnp.abs(rel_bias[:, N_HEADS_A:]))).reshape(1)
        mix_b = _dsa(main_b, ki, wi_t, dsa_bias, logit_bound)

        x2 = _out_proj(x2, mix_a, mix_b, w_out[l].astype(BF16))
        x2 = _ffn(x2, norm_ffn2[l], w_ffn2_in[l].astype(BF16), w_ffn2_out[l].astype(BF16))
    return x2.reshape(bsz, s, d)
```

```python
import functools
import math

import jax
import jax.numpy as jnp
import numpy as np
from jax import lax
from jax.experimental import pallas as pl
from jax.experimental.pallas import tpu as pltpu

F32 = jnp.float32
BF16 = jnp.bfloat16
I32 = jnp.int32

HEAD_DIM = 128
N_HEADS_A = 8
N_HEADS_B = 8
WIDTH_A = N_HEADS_A * HEAD_DIM
WIDTH_B = N_HEADS_B * HEAD_DIM
DILATIONS = (1, 4, 16)
DIL_BLK = 128
N_IDX_HEADS = 16
IDX_DIM = 64
TOPK_MAX = 256
N_BUCKETS = 32
REL_MAX_DIST = 2048
EPS = 1e-6
IDX_PAD = 128
IDX_SCALE = (IDX_DIM ** -0.5) * (N_IDX_HEADS ** -0.5)
LOG2E = math.log2(math.e)

VMEM_LIMIT = 56 * 1024 * 1024

DSA_TQ = 256
DSA_TK = 512
DSA_HALVES = DSA_TK // DSA_TQ
ATT_TQ = 512
ATT_QT = ATT_TQ // DSA_TQ
WORD = 32
SC_TK = 1024
SC_PER_ATT = SC_TK // DSA_TK
CHUNK_WORDS = SC_TK // WORD
SCORE_ROWS = 64
NEG_BIG = -1e30
DSA_SAFE_BOUND = 40.0
DSA_LOOKAHEAD = 2
INT_MIN = -2 ** 31


def _near_tiles():
    exact = N_BUCKETS // 2
    j = 1
    while True:
        d = j * DSA_TQ - (DSA_TQ - 1)
        b = exact + int(math.log(d / exact) / math.log(REL_MAX_DIST / exact) * (N_BUCKETS - exact) - 0.02)
        if d >= exact and b >= N_BUCKETS - 1:
            return j
        j += 1


DSA_NEAR = _near_tiles()


def _cparams(sem):
    return pltpu.CompilerParams(dimension_semantics=sem, vmem_limit_bytes=VMEM_LIMIT)


def _rel_bucket(dist):
    exact = N_BUCKETS // 2
    df = jnp.maximum(dist, 1).astype(F32)
    large = exact + (jnp.log(df / exact) / math.log(REL_MAX_DIST / exact)
                     * (N_BUCKETS - exact)).astype(I32)
    large = jnp.minimum(large, N_BUCKETS - 1)
    return jnp.where(dist < exact, dist, large)


def _lookup(tab_ref, bucket, col):
    val = jnp.zeros(bucket.shape, F32)
    for b in range(N_BUCKETS):
        val = jnp.where(bucket == b, tab_ref[b, col], val)
    return val


def _dil_bias_kernel(tab_ref, o_ref):
    dil = jnp.left_shift(1, 2 * pl.program_id(0))
    i = lax.broadcasted_iota(I32, (DIL_BLK, 2 * DIL_BLK), 0)
    j = lax.broadcasted_iota(I32, (DIL_BLK, 2 * DIL_BLK), 1)
    bucket = _rel_bucket(jnp.clip(DIL_BLK + i - j, 0, DIL_BLK) * dil)
    for h in range(N_HEADS_A):
        o_ref[0, h] = _lookup(tab_ref, bucket, h)


def _dsa_bias_kernel(tab_ref, o_ref):
    jj = pl.program_id(0)
    s = lax.broadcasted_iota(I32, (DSA_TQ, DSA_TQ), 0)
    t = lax.broadcasted_iota(I32, (DSA_TQ, DSA_TQ), 1)
    bucket = _rel_bucket(jnp.maximum((jj - 1) * DSA_TQ + t - s, 0))
    for h in range(N_HEADS_B):
        o_ref[h, 0] = _lookup(tab_ref, bucket, N_HEADS_A + h) * LOG2E


def _bias_tiles(rel_bias):
    smem = pl.BlockSpec(memory_space=pltpu.SMEM)
    dil = pl.pallas_call(
        _dil_bias_kernel,
        grid=(len(DILATIONS),),
        in_specs=[smem],
        out_specs=pl.BlockSpec((1, N_HEADS_A, DIL_BLK, 2 * DIL_BLK), lambda g: (g, 0, 0, 0)),
        out_shape=jax.ShapeDtypeStruct((len(DILATIONS), N_HEADS_A, DIL_BLK, 2 * DIL_BLK), F32),
        name="dil_bias",
    )(rel_bias)
    ntile = DSA_NEAR + 2
    dsa = pl.pallas_call(
        _dsa_bias_kernel,
        grid=(ntile,),
        in_specs=[smem],
        out_specs=pl.BlockSpec((N_HEADS_B, 1, DSA_TQ, DSA_TQ), lambda g: (0, g, 0, 0)),
        out_shape=jax.ShapeDtypeStruct((N_HEADS_B, ntile, DSA_TQ, DSA_TQ), F32),
        name="dsa_bias",
    )(rel_bias)
    return dil, dsa


def _ffn_kernel(x_ref, g_ref, wg_ref, wu_ref, wo_ref, o_ref, h_ref, acc_ref):
    f = pl.program_id(1)

    @pl.when(f == 0)
    def _():
        x = x_ref[...]
        ms = jnp.mean(x * x, axis=-1, keepdims=True)
        h_ref[...] = (x * lax.rsqrt(ms + EPS) * g_ref[...]).astype(BF16)
        acc_ref[...] = jnp.zeros_like(acc_ref)

    h = h_ref[...]
    gate = jnp.dot(h, wg_ref[...], preferred_element_type=F32)
    up = jnp.dot(h, wu_ref[...], preferred_element_type=F32)
    act = (gate * jax.nn.sigmoid(gate) * up).astype(BF16)
    acc_ref[...] += jnp.dot(act, wo_ref[...], preferred_element_type=F32)

    @pl.when(f == pl.num_programs(1) - 1)
    def _():
        o_ref[...] = x_ref[...] + 0.5 * acc_ref[...]


def _ffn(x2, gain, w_in, w_out, tm=512, tf=512):
    m, d = x2.shape
    d_ff = w_out.shape[0]
    nf = d_ff // tf
    return pl.pallas_call(
        _ffn_kernel,
        grid=(m // tm, nf),
        in_specs=[
            pl.BlockSpec((tm, d), lambda i, f: (i, 0)),
            pl.BlockSpec((1, d), lambda i, f: (0, 0)),
            pl.BlockSpec((d, tf), lambda i, f: (0, f)),
            pl.BlockSpec((d, tf), lambda i, f: (0, f + nf)),
            pl.BlockSpec((tf, d), lambda i, f: (f, 0)),
        ],
        out_specs=pl.BlockSpec((tm, d), lambda i, f: (i, 0)),
        out_shape=jax.ShapeDtypeStruct((m, d), F32),
        scratch_shapes=[pltpu.VMEM((tm, d), BF16), pltpu.VMEM((tm, d), F32)],
        compiler_params=_cparams(("parallel", "arbitrary")),
        name="ffn",
    )(x2, gain.reshape(1, d), w_in, w_in, w_out)


PROJ_BLK = 1024
PROJ_NBLK = 7
PROJ_NA = 3
_NORM_BLOCKS = (0, 1, 3, 4)


def _proj_kernel(x_ref, g_ref, w_ref, widx_ref, hg_ref, oa_ref, ob_ref, idx_ref, h_ref):
    j = pl.program_id(1)

    @pl.when(j == 0)
    def _():
        x = x_ref[...]
        ms = jnp.mean(x * x, axis=-1, keepdims=True)
        h_ref[...] = (x * lax.rsqrt(ms + EPS) * g_ref[...]).astype(BF16)

    h = h_ref[...]
    y = jnp.dot(h, w_ref[...], preferred_element_type=F32)
    is_norm = functools.reduce(jnp.logical_or, [j == b for b in _NORM_BLOCKS])

    def emit(dst_ref):
        @pl.when(is_norm)
        def _():
            for hd in range(PROJ_BLK // HEAD_DIM):
                sl = slice(hd * HEAD_DIM, (hd + 1) * HEAD_DIM)
                yh = y[:, sl]
                ms = jnp.mean(yh * yh, axis=-1, keepdims=True)
                dst_ref[:, sl] = (yh * lax.rsqrt(ms + EPS) * hg_ref[0, :, sl]).astype(BF16)

        @pl.when(jnp.logical_not(is_norm))
        def _():
            dst_ref[...] = y.astype(BF16)

    pl.when(j < PROJ_NA)(lambda: emit(oa_ref))
    pl.when(j >= PROJ_NA)(lambda: emit(ob_ref))

    @pl.when(j == PROJ_NBLK - 1)
    def _():
        idx_ref[...] = jnp.dot(h, widx_ref[...], preferred_element_type=F32)


def _proj(x2, gain, w_main, w_idx, head_gain, tm=1024):
    m, d = x2.shape
    return pl.pallas_call(
        _proj_kernel,
        grid=(m // tm, PROJ_NBLK),
        in_specs=[
            pl.BlockSpec((tm, d), lambda i, j: (i, 0)),
            pl.BlockSpec((1, d), lambda i, j: (0, 0)),
            pl.BlockSpec((d, PROJ_BLK), lambda i, j: (0, j)),
            pl.BlockSpec((d, IDX_PAD), lambda i, j: (0, 0)),
            pl.BlockSpec((1, 1, PROJ_BLK), lambda i, j: (j, 0, 0)),
        ],
        out_specs=[
            pl.BlockSpec((tm, PROJ_BLK), lambda i, j: (i, jnp.minimum(j, PROJ_NA - 1))),
            pl.BlockSpec((tm, PROJ_BLK), lambda i, j: (i, jnp.maximum(j - PROJ_NA, 0))),
            pl.BlockSpec((tm, IDX_PAD), lambda i, j: (i, 0)),
        ],
        out_shape=[
            jax.ShapeDtypeStruct((m, PROJ_NA * PROJ_BLK), BF16),
            jax.ShapeDtypeStruct((m, (PROJ_NBLK - PROJ_NA) * PROJ_BLK), BF16),
            jax.ShapeDtypeStruct((m, IDX_PAD), F32),
        ],
        scratch_shapes=[pltpu.VMEM((tm, d), BF16)],
        compiler_params=_cparams(("parallel", "arbitrary")),
        name="proj",
    )(x2, gain.reshape(1, d), w_main, w_idx, head_gain)


_NT = (((1,), (1,)), ((), ()))


DIL_SPAN = DIL_BLK * max(DILATIONS)
DIL_G = 2
DIL_UNITS = DIL_SPAN // DIL_BLK
DIL_UNROLL = 4


def _dilated_kernel(q_ref, kp_ref, kc_ref, vp_ref, vc_ref, b_ref, o_ref, qf, kf, vf, ob, lse_ref):
    n = pl.program_id(1)
    h0 = pl.program_id(2) * DIL_G
    for g in range(DIL_G):
        sl = slice(g * HEAD_DIM, (g + 1) * HEAD_DIM)
        qf[g] = q_ref[0, :, sl].astype(F32)
        kf[g, :DIL_SPAN] = kp_ref[0, :, sl].astype(F32)
        kf[g, DIL_SPAN:] = kc_ref[0, :, sl].astype(F32)
        vf[g, :DIL_SPAN] = vp_ref[0, :, sl].astype(F32)
        vf[g, DIL_SPAN:] = vc_ref[0, :, sl].astype(F32)
    i = lax.broadcasted_iota(I32, (DIL_BLK, 2 * DIL_BLK), 0)
    j = lax.broadcasted_iota(I32, (DIL_BLK, 2 * DIL_BLK), 1)
    band = jnp.logical_and(j >= i, j <= i + DIL_BLK)
    first = j >= DIL_BLK

    for br, dil in enumerate(DILATIONS):
        span = DIL_BLK * dil

        def unit(idx, carry, br=br, dil=dil, span=span):
            blk = idx // dil
            base = blk * span + idx % dil
            valid = jnp.logical_and(band, jnp.logical_or(first, n * (DIL_SPAN // span) + blk > 0))
            qrows = pl.ds(base, DIL_BLK, stride=dil)
            krows = pl.ds(DIL_SPAN + base - span, 2 * DIL_BLK, stride=dil)
            for g in range(DIL_G):
                q = qf[g, qrows, :].astype(BF16)
                k = kf[g, krows, :].astype(BF16)
                v = vf[g, krows, :].astype(BF16)
                s = lax.dot_general(q, k, _NT, preferred_element_type=F32) + b_ref[br, h0 + g]
                s = jnp.where(valid, s, -jnp.inf)
                mx = jnp.max(s, axis=-1, keepdims=True)
                p = jnp.exp(s - mx)
                den = jnp.sum(p, axis=-1, keepdims=True)
                ob[br, g, qrows, :] = jnp.dot(p.astype(BF16), v, preferred_element_type=F32) / den
                lse_ref[br, g, qrows, :] = jnp.broadcast_to(mx + jnp.log(den), (DIL_BLK, HEAD_DIM))
            return carry

        lax.fori_loop(0, DIL_UNITS, unit, 0, unroll=DIL_UNROLL)

    for g in range(DIL_G):
        l1, l2, l3 = lse_ref[0, g], lse_ref[1, g], lse_ref[2, g]
        mx = jnp.maximum(jnp.maximum(l1, l2), l3)
        e1, e2, e3 = jnp.exp(l1 - mx), jnp.exp(l2 - mx), jnp.exp(l3 - mx)
        mix = (e1 * ob[0, g] + e2 * ob[1, g] + e3 * ob[2, g]) / (e1 + e2 + e3)
        o_ref[0, :, g * HEAD_DIM:(g + 1) * HEAD_DIM] = mix.astype(BF16)


def _dilated(main_a, bias):
    bsz, s, _ = main_a.shape
    gw = DIL_G * HEAD_DIM
    ng = WIDTH_A // gw
    cur = lambda c: pl.BlockSpec((1, DIL_SPAN, gw), lambda b, n, g: (b, n, c * ng + g))
    prev = lambda c: pl.BlockSpec((1, DIL_SPAN, gw), lambda b, n, g: (b, jnp.maximum(n - 1, 0), c * ng + g))
    out = pl.pallas_call(
        _dilated_kernel,
        grid=(bsz, s // DIL_SPAN, ng),
        in_specs=[cur(0), prev(1), cur(1), prev(2), cur(2),
                  pl.BlockSpec(bias.shape, lambda b, n, g: (0, 0, 0, 0), pipeline_mode=pl.Buffered(1))],
        out_specs=pl.BlockSpec((1, DIL_SPAN, gw), lambda b, n, g: (b, n, g)),
        out_shape=jax.ShapeDtypeStruct((bsz, s, WIDTH_A), BF16),
        scratch_shapes=[pltpu.VMEM((DIL_G, DIL_SPAN, HEAD_DIM), F32),
                        pltpu.VMEM((DIL_G, 2 * DIL_SPAN, HEAD_DIM), F32),
                        pltpu.VMEM((DIL_G, 2 * DIL_SPAN, HEAD_DIM), F32),
                        pltpu.VMEM((len(DILATIONS), DIL_G, DIL_SPAN, HEAD_DIM), F32),
                        pltpu.VMEM((len(DILATIONS), DIL_G, DIL_SPAN, HEAD_DIM), F32)],
        compiler_params=_cparams(("parallel", "parallel", "parallel")),
        name="dilated",
    )(main_a, main_a, main_a, main_a, main_a, bias)
    return out.reshape(bsz * s, WIDTH_A)


def _dsa_schedule(s, tq, tk):
    qs, ks = [], []
    for i in range(s // tq):
        last = (i * tq + tq - 1) // tk
        for k in range(last + 1):
            qs.append(i)
            ks.append(k)
    return np.asarray(qs, np.int32), np.asarray(ks, np.int32)


def _sortable(bits):
    return bits ^ (lax.shift_right_arithmetic(bits, 31) & 0x7FFFFFFF)


def _bit_transpose(words):
    a = list(words)
    mask, j = 0x0000FFFF, 16
    while j:
        k = 0
        while k < WORD:
            t = (a[k] ^ lax.shift_right_logical(a[k + j], j)) & mask
            a[k] = a[k] ^ t
            a[k + j] = a[k + j] ^ lax.shift_left(t, j)
            k = (k + j + 1) & ~j
        j >>= 1
        mask = (mask ^ (mask << j)) & 0xFFFFFFFF
    return a


def _dsa_score_kernel(qtab, ktab, qit_ref, ki_ref, w_ref, sc_ref, thr_ref,
                      planes_ref, alive_ref, stage_ref):
    step = pl.program_id(1)
    qb = qtab[step]
    kc = ktab[step]
    w = w_ref[0] * IDX_SCALE
    srow = lax.broadcasted_iota(I32, (SCORE_ROWS, DSA_TQ), 0)
    tcol = lax.broadcasted_iota(I32, (SCORE_ROWS, DSA_TQ), 1)
    row0 = pl.multiple_of(kc * CHUNK_WORDS, CHUNK_WORDS)
    group = WORD * 8
    for g in range(SC_TK // group):
        for r0 in range(g * group, (g + 1) * group, SCORE_ROWS):
            kidx = ki_ref[0, r0:r0 + SCORE_ROWS, :]
            acc = jnp.zeros((SCORE_ROWS, DSA_TQ), F32)
            for h in range(N_IDX_HEADS):
                d = jnp.dot(kidx, qit_ref[0, h * IDX_DIM:(h + 1) * IDX_DIM, :],
                            preferred_element_type=F32)
                acc = acc + jnp.maximum(d, 0.0) * w[h:h + 1, :]
            causal = srow - tcol <= qb * DSA_TQ - kc * SC_TK - r0
            acc = jnp.where(causal, acc, -jnp.inf)
            sc_ref[r0:r0 + SCORE_ROWS, :] = acc
            stage_ref[r0:r0 + SCORE_ROWS, :] = _sortable(lax.bitcast_convert_type(acc, I32)) ^ INT_MIN
        for lanes in (slice(c, c + 128) for c in range(0, DSA_TQ, 128)):
            planes = _bit_transpose([stage_ref[g * group + 8 * n:g * group + 8 * (n + 1), lanes]
                                     for n in range(WORD)])
            for n in range(WORD):
                planes_ref[n, pl.ds(pl.multiple_of(row0 + 8 * g, 8), 8), lanes] = planes[n]

    last = (qb * DSA_TQ + DSA_TQ - 1) // SC_TK

    @pl.when(kc == last)
    def _():
        nwords = alive_ref.shape[0]
        rb = min(128, nwords)
        nrows = (last + 1) * CHUNK_WORDS
        nblk = (nrows + rb - 1) // rb
        tpos = qb * DSA_TQ + lax.broadcasted_iota(I32, (1, DSA_TQ), 1)
        want0 = jnp.minimum(tpos + 1, TOPK_MAX)
        rows = lambda blk: pl.ds(pl.multiple_of(blk * rb, rb), rb)

        def hits(alive, n, blk):
            hit = lax.population_count(alive & planes_ref[n, rows(blk), :])
            return hit.reshape(rb // 8, 8, DSA_TQ).sum(axis=0)

        def init(blk, cnt):
            ridx = blk * rb + lax.broadcasted_iota(I32, (rb, DSA_TQ), 0)
            alive = jnp.where(ridx < nrows, -1, 0)
            alive_ref[rows(blk), :] = alive
            return cnt + hits(alive, 0, blk)
        zero = jnp.zeros((8, DSA_TQ), I32)
        cnt0 = lax.fori_loop(0, nblk, init, zero)

        def bit_step(n, carry):
            r, want, cnt8 = carry
            cnt = cnt8.sum(axis=0, keepdims=True)
            take = cnt >= want
            r = r | jnp.where(take, lax.shift_left(jnp.int32(1), 31 - n), 0)
            want = jnp.where(take, want, want - cnt)
            flip = jnp.where(take, 0, -1)
            nxt = jnp.minimum(n + 1, WORD - 1)

            def sweep(blk, acc):
                alive = alive_ref[rows(blk), :] & (planes_ref[n, rows(blk), :] ^ flip)
                alive_ref[rows(blk), :] = alive
                return acc + hits(alive, nxt, blk)
            return r, want, lax.fori_loop(0, nblk, sweep, zero)

        r, _, _ = lax.fori_loop(0, WORD, bit_step, (jnp.zeros((1, DSA_TQ), I32), want0, cnt0))
        thr_ref[0] = lax.bitcast_convert_type(_sortable(r ^ INT_MIN), F32)


def _dsa_attn_kernel(qtab, ktab, sc_ref, thr_ref, q_ref, k_ref, vt_ref, b_ref, bound_ref, o_ref,
                     acc_ref, m_ref, l_ref, s_ref, mask_ref, cm_ref):
    step = pl.program_id(1)
    qb = qtab[step]
    kc = ktab[step]

    @pl.when(kc == 0)
    def _():
        acc_ref[...] = jnp.zeros_like(acc_ref)
        m_ref[...] = jnp.full_like(m_ref, NEG_BIG)
        l_ref[...] = jnp.zeros_like(l_ref)

    bound = bound_ref[0]
    bounded = bound <= DSA_SAFE_BOUND

    def set_mask(kept):
        for b in range(ATT_QT):
            cols = slice(b * DSA_TQ, (b + 1) * DSA_TQ)
            mask_ref[:, cols] = jnp.where(sc_ref[b] >= thr_ref[0, :, cols], kept, -jnp.inf)

    tiles = [[jnp.clip((qb * ATT_QT + b) - (kc * DSA_HALVES + a), -1, DSA_NEAR) + 1
              for b in range(ATT_QT)] for a in range(DSA_HALVES)]
    far = qb * ATT_QT - (kc * DSA_HALVES + DSA_HALVES - 1) >= DSA_NEAR
    far_bias = lambda h: b_ref[h, DSA_NEAR + 1, 0:1, 0:1]

    def logits(h):
        sl = slice(h * HEAD_DIM, (h + 1) * HEAD_DIM)
        s = lax.dot_general(k_ref[0, :, sl], q_ref[0, :, sl], _NT, preferred_element_type=F32)
        return s + mask_ref[...]

    def near_bias(h):
        return jnp.concatenate(
            [jnp.concatenate([b_ref[h, tiles[a][b]] for b in range(ATT_QT)], axis=1)
             for a in range(DSA_HALVES)], axis=0)

    def values(h):
        return vt_ref[h * HEAD_DIM:(h + 1) * HEAD_DIM, :]

    @pl.when(bounded)
    def _():
        def raw_logits(h):
            sl = slice(h * HEAD_DIM, (h + 1) * HEAD_DIM)
            s_ref[h] = lax.dot_general(k_ref[0, :, sl], q_ref[0, :, sl], _NT,
                                       preferred_element_type=F32)

        def heads(weights):
            for h in range(DSA_LOOKAHEAD):
                raw_logits(h)
            set_mask(-bound)
            for h in range(N_HEADS_B):
                if h + DSA_LOOKAHEAD < N_HEADS_B:
                    raw_logits(h + DSA_LOOKAHEAD)
                p, gain = weights(h)
                l_ref[h] += gain * jnp.sum(p, axis=0, keepdims=True)
                acc_ref[h] += gain * jnp.dot(values(h), p.astype(BF16), preferred_element_type=F32)

        @pl.when(far)
        def _():
            heads(lambda h: (jnp.exp2(s_ref[h] + mask_ref[...]), jnp.exp2(far_bias(h))))

        @pl.when(jnp.logical_not(far))
        def _():
            heads(lambda h: (jnp.exp2(s_ref[h] + mask_ref[...] + near_bias(h)), 1.0))

    @pl.when(jnp.logical_not(bounded))
    def _():
        set_mask(0.0)

        @pl.when(far)
        def _():
            for h in range(N_HEADS_B):
                s = logits(h)
                s_ref[h] = s
                cm_ref[h] = jnp.max(s, axis=0, keepdims=True) + far_bias(h)

        @pl.when(jnp.logical_not(far))
        def _():
            for h in range(N_HEADS_B):
                s = logits(h) + near_bias(h)
                s_ref[h] = s
                cm_ref[h] = jnp.max(s, axis=0, keepdims=True)

        for h in range(N_HEADS_B):
            m_old = m_ref[h]
            m_new = jnp.maximum(m_old, cm_ref[h])
            alpha = jnp.exp2(m_old - m_new)
            m_ref[h] = m_new
            shift = jnp.where(far, m_new - far_bias(h), m_new)
            p = jnp.exp2(s_ref[h] - shift)
            l_ref[h] = alpha * l_ref[h] + jnp.sum(p, axis=0, keepdims=True)
            pv = jnp.dot(values(h), p.astype(BF16), preferred_element_type=F32)
            acc_ref[h] = alpha * acc_ref[h] + pv

    last = (qb * ATT_TQ + ATT_TQ - 1) // DSA_TK

    @pl.when(kc == last)
    def _():
        for h in range(N_HEADS_B):
            o_ref[0, h * HEAD_DIM:(h + 1) * HEAD_DIM, :] = (acc_ref[h] / l_ref[h]).astype(BF16)


def _dsa(main_b, ki, wi_t, bias, logit_bound):
    bsz, s, _ = main_b.shape
    w = WIDTH_B
    qs, ks = _dsa_schedule(s, DSA_TQ, SC_TK)
    npairs = len(qs)
    qi_t =jnp.swapaxes(main_b[:, :, 3 * PROJ_BLK:4 * PROJ_BLK], 1, 2)
    scores, thr = pl.pallas_call(
        _dsa_score_kernel,
        grid_spec=pltpu.PrefetchScalarGridSpec(
            num_scalar_prefetch=2,
            grid=(bsz, npairs),
            in_specs=[
                pl.BlockSpec((1, N_IDX_HEADS * IDX_DIM, DSA_TQ), lambda b, t, qt, kt: (b, 0, qt[t])),
                pl.BlockSpec((1, SC_TK, IDX_DIM), lambda b, t, qt, kt: (b, kt[t], 0)),
                pl.BlockSpec((1, N_IDX_HEADS, DSA_TQ), lambda b, t, qt, kt: (b, 0, qt[t])),
            ],
            out_specs=[
                pl.BlockSpec((None, None, None, SC_TK, DSA_TQ),
                             lambda b, t, qt, kt: (b, kt[t], qt[t], 0, 0)),
                pl.BlockSpec((1, 1, DSA_TQ), lambda b, t, qt, kt: (b, 0, qt[t])),
            ],
            scratch_shapes=[pltpu.VMEM((WORD, s // WORD, DSA_TQ), I32),
                            pltpu.VMEM((s // WORD, DSA_TQ), I32),
                            pltpu.VMEM((SC_TK, DSA_TQ), I32)],
        ),
        out_shape=[jax.ShapeDtypeStruct((bsz, s // SC_TK, s // DSA_TQ, SC_TK, DSA_TQ), F32),
                   jax.ShapeDtypeStruct((bsz, 1, s), F32)],
        compiler_params=_cparams(("arbitrary", "arbitrary")),
        name="dsa_scores",
    )(qs, ks, qi_t, ki, wi_t)

    vt = jnp.swapaxes(main_b[:, :, 2 * PROJ_BLK:3 * PROJ_BLK].reshape(bsz, s // DSA_TK, DSA_TK, w), 2, 3)
    ntile = bias.shape[1]
    qs, ks = _dsa_schedule(s, ATT_TQ, DSA_TK)
    out_t = pl.pallas_call(
        _dsa_attn_kernel,
        grid_spec=pltpu.PrefetchScalarGridSpec(
            num_scalar_prefetch=2,
            grid=(bsz, len(qs)),
            in_specs=[
                pl.BlockSpec((None, None, ATT_QT, DSA_TK, DSA_TQ),
                             lambda b, t, qt, kt: (b, kt[t] // SC_PER_ATT, qt[t], kt[t] % SC_PER_ATT, 0)),
                pl.BlockSpec((1, 1, ATT_TQ), lambda b, t, qt, kt: (b, 0, qt[t])),
                pl.BlockSpec((1, ATT_TQ, w), lambda b, t, qt, kt: (b, qt[t], 0)),
                pl.BlockSpec((1, DSA_TK, w), lambda b, t, qt, kt: (b, kt[t], 1)),
                pl.BlockSpec((None, None, w, DSA_TK), lambda b, t, qt, kt: (b, kt[t], 0, 0)),
                pl.BlockSpec((N_HEADS_B, ntile, DSA_TQ, DSA_TQ), lambda b, t, qt, kt: (0, 0, 0, 0),
                             pipeline_mode=pl.Buffered(1)),
                pl.BlockSpec(memory_space=pltpu.SMEM),
            ],
            out_specs=pl.BlockSpec((1, w, ATT_TQ), lambda b, t, qt, kt: (b, 0, qt[t])),
            scratch_shapes=[pltpu.VMEM((N_HEADS_B, HEAD_DIM, ATT_TQ), F32),
                            pltpu.VMEM((N_HEADS_B, 1, ATT_TQ), F32),
                            pltpu.VMEM((N_HEADS_B, 1, ATT_TQ), F32),
                            pltpu.VMEM((N_HEADS_B, DSA_TK, ATT_TQ), F32),
                            pltpu.VMEM((DSA_TK, ATT_TQ), F32),
                            pltpu.VMEM((N_HEADS_B, 1, ATT_TQ), F32)],
        ),
        out_shape=jax.ShapeDtypeStruct((bsz, w, s), BF16),
        compiler_params=_cparams(("arbitrary", "arbitrary")),
        name="dsa_attn",
    )(qs, ks, scores, thr, main_b, main_b, vt, bias, logit_bound)
    return jnp.swapaxes(out_t, 1, 2).reshape(bsz * s, w)


def _out_kernel(x_ref, ma_ref, mb_ref, wa_ref, wb_ref, o_ref):
    o_ref[...] = (x_ref[...]
                  + jnp.dot(ma_ref[...], wa_ref[...], preferred_element_type=F32)
                  + jnp.dot(mb_ref[...], wb_ref[...], preferred_element_type=F32))


def _out_proj(x2, mix_a, mix_b, w_out, tm=512):
    m, d = x2.shape
    row = lambda width: pl.BlockSpec((tm, width), lambda i: (i, 0))
    return pl.pallas_call(
        _out_kernel,
        grid=(m // tm,),
        in_specs=[row(d), row(WIDTH_A), row(WIDTH_B),
                  pl.BlockSpec((WIDTH_A, d), lambda i: (0, 0)),
                  pl.BlockSpec((WIDTH_B, d), lambda i: (1, 0))],
        out_specs=row(d),
        out_shape=jax.ShapeDtypeStruct((m, d), F32),
        compiler_params=_cparams(("parallel",)),
        name="out_proj",
    )(x2, mix_a, mix_b, w_out, w_out)


def kernel(x, rel_bias, norm_ffn1, w_ffn1_in, w_ffn1_out, norm_mix, w_in, q_norm_a, k_norm_a,
           q_norm_b, k_norm_b, w_out, norm_ffn2, w_ffn2_in, w_ffn2_out):
    bsz, s, d = x.shape
    m = bsz * s
    depth = norm_ffn1.shape[0]
    scale = HEAD_DIM ** -0.5
    n_main = PROJ_NBLK * PROJ_BLK
    dil_bias, dsa_bias = _bias_tiles(rel_bias)
    x2 = x.reshape(m, d)
    ones = jnp.ones((PROJ_BLK,), F32)
    tile8 = lambda g: jnp.tile(g, PROJ_BLK // HEAD_DIM)
    for l in range(depth):
        x2 = _ffn(x2, norm_ffn1[l], w_ffn1_in[l].astype(BF16), w_ffn1_out[l].astype(BF16))

        w_main = w_in[l, :, :n_main].astype(BF16)
        w_idx = jnp.pad(w_in[l, :, n_main:], ((0, 0), (0, IDX_PAD - IDX_DIM - N_IDX_HEADS))).astype(BF16)
        head_gain = jnp.stack([tile8(q_norm_a[l] * scale), tile8(k_norm_a[l]), ones,
                               tile8(q_norm_b[l] * (scale * LOG2E)), tile8(k_norm_b[l]), ones, ones])
        main_a, main_b, idx = _proj(x2, norm_mix[l], w_main, w_idx,
                                    head_gain.reshape(PROJ_NBLK, 1, PROJ_BLK))
        main_a = main_a.reshape(bsz, s, PROJ_NA * PROJ_BLK)
        main_b = main_b.reshape(bsz, s, (PROJ_NBLK - PROJ_NA) * PROJ_BLK)
        idx = idx.reshape(bsz, s, IDX_PAD)
        ki = idx[:, :, :IDX_DIM].astype(BF16)
        wi_t = jnp.swapaxes(idx[:, :, IDX_DIM:IDX_DIM + N_IDX_HEADS], 1, 2)

        mix_a = _dilated(main_a, dil_bias)
        logit_bound = (1.02 * HEAD_DIM * jnp.max(jnp.abs(q_norm_b[l] * (scale * LOG2E)))
                       * jnp.max(jnp.abs(k_norm_b[l]))
                       + LOG2E * jnp.max(jnp.abs(rel_bias[:, N_HEADS_A:]))).reshape(1)
        mix_b = _dsa(main_b, ki, wi_t, dsa_bias, logit_bound)

        x2 = _out_proj(x2, mix_a, mix_b, w_out[l].astype(BF16))
        x2 = _ffn(x2, norm_ffn2[l], w_ffn2_in[l].astype(BF16), w_ffn2_out[l].astype(BF16))
    return x2.reshape(bsz, s, d)
```

```python
import functools
import math

import jax
import jax.numpy as jnp
import numpy as np
from jax import lax
from jax.experimental import pallas as pl
from jax.experimental.pallas import tpu as pltpu

F32 = jnp.float32
BF16 = jnp.bfloat16
I32 = jnp.int32

HEAD_DIM = 128
N_HEADS_A = 8
N_HEADS_B = 8
WIDTH_A = N_HEADS_A * HEAD_DIM
WIDTH_B = N_HEADS_B * HEAD_DIM
DILATIONS = (1, 4, 16)
DIL_BLK = 128
N_IDX_HEADS = 16
IDX_DIM = 64
TOPK_MAX = 256
N_BUCKETS = 32
REL_MAX_DIST = 2048
EPS = 1e-6
IDX_PAD = 128
IDX_SCALE = (IDX_DIM ** -0.5) * (N_IDX_HEADS ** -0.5)
LOG2E = math.log2(math.e)

VMEM_LIMIT = 56 * 1024 * 1024

DSA_TQ = 256
DSA_TK = 512
DSA_HALVES = DSA_TK // DSA_TQ
ATT_TQ = 512
ATT_QT = ATT_TQ // DSA_TQ
WORD = 32
SC_TK = 1024
SC_PER_ATT = SC_TK // DSA_TK
CHUNK_WORDS = SC_TK // WORD
SCORE_ROWS = 64
NEG_BIG = -1e30
DSA_SAFE_BOUND = 40.0
DSA_LOOKAHEAD = 2
INT_MIN = -2 ** 31


def _near_tiles():
    exact = N_BUCKETS // 2
    j = 1
    while True:
        d = j * DSA_TQ - (DSA_TQ - 1)
        b = exact + int(math.log(d / exact) / math.log(REL_MAX_DIST / exact) * (N_BUCKETS - exact) - 0.02)
        if d >= exact and b >= N_BUCKETS - 1:
            return j
        j += 1


DSA_NEAR = _near_tiles()


def _cparams(sem):
    return pltpu.CompilerParams(dimension_semantics=sem, vmem_limit_bytes=VMEM_LIMIT)


def _rel_bucket(dist):
    exact = N_BUCKETS // 2
    df = jnp.maximum(dist, 1).astype(F32)
    large = exact + (jnp.log(df / exact) / math.log(REL_MAX_DIST / exact)
                     * (N_BUCKETS - exact)).astype(I32)
    large = jnp.minimum(large, N_BUCKETS - 1)
    return jnp.where(dist < exact, dist, large)


def _lookup(tab_ref, bucket, col):
    val = jnp.zeros(bucket.shape, F32)
    for b in range(N_BUCKETS):
        val = jnp.where(bucket == b, tab_ref[b, col], val)
    return val


def _dil_bias_kernel(tab_ref, o_ref):
    dil = jnp.left_shift(1, 2 * pl.program_id(0))
    i = lax.broadcasted_iota(I32, (DIL_BLK, 2 * DIL_BLK), 0)
    j = lax.broadcasted_iota(I32, (DIL_BLK, 2 * DIL_BLK), 1)
    bucket = _rel_bucket(jnp.clip(DIL_BLK + i - j, 0, DIL_BLK) * dil)
    for h in range(N_HEADS_A):
        o_ref[0, h] = _lookup(tab_ref, bucket, h)


def _dsa_bias_kernel(tab_ref, o_ref):
    jj = pl.program_id(0)
    s = lax.broadcasted_iota(I32, (DSA_TQ, DSA_TQ), 0)
    t = lax.broadcasted_iota(I32, (DSA_TQ, DSA_TQ), 1)
    bucket = _rel_bucket(jnp.maximum((jj - 1) * DSA_TQ + t - s, 0))
    for h in range(N_HEADS_B):
        o_ref[h, 0] = _lookup(tab_ref, bucket, N_HEADS_A + h) * LOG2E


def _bias_tiles(rel_bias):
    smem = pl.BlockSpec(memory_space=pltpu.SMEM)
    dil = pl.pallas_call(
        _dil_bias_kernel,
        grid=(len(DILATIONS),),
        in_specs=[smem],
        out_specs=pl.BlockSpec((1, N_HEADS_A, DIL_BLK, 2 * DIL_BLK), lambda g: (g, 0, 0, 0)),
        out_shape=jax.ShapeDtypeStruct((len(DILATIONS), N_HEADS_A, DIL_BLK, 2 * DIL_BLK), F32),
        name="dil_bias",
    )(rel_bias)
    ntile = DSA_NEAR + 2
    dsa = pl.pallas_call(
        _dsa_bias_kernel,
        grid=(ntile,),
        in_specs=[smem],
        out_specs=pl.BlockSpec((N_HEADS_B, 1, DSA_TQ, DSA_TQ), lambda g: (0, g, 0, 0)),
        out_shape=jax.ShapeDtypeStruct((N_HEADS_B, ntile, DSA_TQ, DSA_TQ), F32),
        name="dsa_bias",
    )(rel_bias)
    return dil, dsa


def _ffn_kernel(x_ref, g_ref, wg_ref, wu_ref, wo_ref, o_ref, h_ref, act_ref):
    f = pl.program_id(1)
    nf = pl.num_programs(1) - 1

    def gate_up(slot):
        h = h_ref[...]
        gate = jnp.dot(h, wg_ref[...], preferred_element_type=F32)
        up = jnp.dot(h, wu_ref[...], preferred_element_type=F32)
        act_ref[slot] = (gate * jax.nn.sigmoid(gate) * (0.5 * up)).astype(BF16)

    def down(slot):
        o_ref[...] += jnp.dot(act_ref[slot], wo_ref[...], preferred_element_type=F32)

    @pl.when(f == 0)
    def _():
        x = x_ref[...]
        ms = jnp.mean(x * x, axis=-1, keepdims=True)
        h_ref[...] = (x * lax.rsqrt(ms + EPS) * g_ref[...]).astype(BF16)
        o_ref[...] = x
        gate_up(0)

    for parity in (0, 1):
        @pl.when(jnp.logical_and(jnp.logical_and(f > 0, f < nf), f % 2 == parity))
        def _(parity=parity):
            down(1 - parity)
            gate_up(parity)

        @pl.when(jnp.logical_and(f == nf, f % 2 == parity))
        def _(parity=parity):
            down(1 - parity)


def _ffn(x2, gain, w_in, w_out, tm=512, tf=512):
    m, d = x2.shape
    d_ff = w_out.shape[0]
    nf = d_ff // tf
    return pl.pallas_call(
        _ffn_kernel,
        grid=(m // tm, nf + 1),
        in_specs=[
            pl.BlockSpec((tm, d), lambda i, f: (i, 0)),
            pl.BlockSpec((1, d), lambda i, f: (0, 0)),
            pl.BlockSpec((d, tf), lambda i, f: (0, jnp.minimum(f, nf - 1))),
            pl.BlockSpec((d, tf), lambda i, f: (0, jnp.minimum(f, nf - 1) + nf)),
            pl.BlockSpec((tf, d), lambda i, f: (jnp.maximum(f - 1, 0), 0)),
        ],
        out_specs=pl.BlockSpec((tm, d), lambda i, f: (i, 0)),
        out_shape=jax.ShapeDtypeStruct((m, d), F32),
        scratch_shapes=[pltpu.VMEM((tm, d), BF16), pltpu.VMEM((2, tm, tf), BF16)],
        compiler_params=_cparams(("parallel", "arbitrary")),
        name="ffn",
    )(x2, gain.reshape(1, d), w_in, w_in, w_out)


PROJ_BLK = 1024
PROJ_NBLK = 7
PROJ_NA = 3
_NORM_BLOCKS = (0, 1, 3, 4)


def _proj_kernel(x_ref, g_ref, w_ref, widx_ref, hg_ref, oa_ref, ob_ref, idx_ref, h_ref):
    j = pl.program_id(1)

    @pl.when(j == 0)
    def _():
        x = x_ref[...]
        ms = jnp.mean(x * x, axis=-1, keepdims=True)
        h_ref[...] = (x * lax.rsqrt(ms + EPS) * g_ref[...]).astype(BF16)

    h = h_ref[...]
    y = jnp.dot(h, w_ref[...], preferred_element_type=F32)
    is_norm = functools.reduce(jnp.logical_or, [j == b for b in _NORM_BLOCKS])

    def emit(dst_ref):
        @pl.when(is_norm)
        def _():
            for hd in range(PROJ_BLK // HEAD_DIM):
                sl = slice(hd * HEAD_DIM, (hd + 1) * HEAD_DIM)
                yh = y[:, sl]
                ms = jnp.mean(yh * yh, axis=-1, keepdims=True)
                dst_ref[:, sl] = (yh * lax.rsqrt(ms + EPS) * hg_ref[0, :, sl]).astype(BF16)

        @pl.when(jnp.logical_not(is_norm))
        def _():
            dst_ref[...] = y.astype(BF16)

    pl.when(j < PROJ_NA)(lambda: emit(oa_ref))
    pl.when(j >= PROJ_NA)(lambda: emit(ob_ref))

    @pl.when(j == PROJ_NBLK - 1)
    def _():
        idx_ref[...] = jnp.dot(h, widx_ref[...], preferred_element_type=F32)


def _proj(x2, gain, w_main, w_idx, head_gain, tm=1024):
    m, d = x2.shape
    return pl.pallas_call(
        _proj_kernel,
        grid=(m // tm, PROJ_NBLK),
        in_specs=[
            pl.BlockSpec((tm, d), lambda i, j: (i, 0)),
            pl.BlockSpec((1, d), lambda i, j: (0, 0)),
            pl.BlockSpec((d, PROJ_BLK), lambda i, j: (0, j)),
            pl.BlockSpec((d, IDX_PAD), lambda i, j: (0, 0)),
            pl.BlockSpec((1, 1, PROJ_BLK), lambda i, j: (j, 0, 0)),
        ],
        out_specs=[
            pl.BlockSpec((tm, PROJ_BLK), lambda i, j: (i, jnp.minimum(j, PROJ_NA - 1))),
            pl.BlockSpec((tm, PROJ_BLK), lambda i, j: (i, jnp.maximum(j - PROJ_NA, 0))),
            pl.BlockSpec((tm, IDX_PAD), lambda i, j: (i, 0)),
        ],
        out_shape=[
            jax.ShapeDtypeStruct((m, PROJ_NA * PROJ_BLK), BF16),
            jax.ShapeDtypeStruct((m, (PROJ_NBLK - PROJ_NA) * PROJ_BLK), BF16),
            jax.ShapeDtypeStruct((m, IDX_PAD), F32),
        ],
        scratch_shapes=[pltpu.VMEM((tm, d), BF16)],
        compiler_params=_cparams(("parallel", "arbitrary")),
        name="proj",
    )(x2, gain.reshape(1, d), w_main, w_idx, head_gain)


_NT = (((1,), (1,)), ((), ()))


DIL_SPAN = DIL_BLK * max(DILATIONS)
DIL_G = 2
DIL_UNITS = DIL_SPAN // DIL_BLK
DIL_UNROLL = 4


def _dilated_kernel(q_ref, kp_ref, kc_ref, vp_ref, vc_ref, b_ref, o_ref, qf, kf, vf, ob, lse_ref):
    n = pl.program_id(1)
    h0 = pl.program_id(2) * DIL_G
    for g in range(DIL_G):
        sl = slice(g * HEAD_DIM, (g + 1) * HEAD_DIM)
        qf[g] = q_ref[0, :, sl].astype(F32)
        kf[g, :DIL_SPAN] = kp_ref[0, :, sl].astype(F32)
        kf[g, DIL_SPAN:] = kc_ref[0, :, sl].astype(F32)
        vf[g, :DIL_SPAN] = vp_ref[0, :, sl].astype(F32)
        vf[g, DIL_SPAN:] = vc_ref[0, :, sl].astype(F32)
    i = lax.broadcasted_iota(I32, (DIL_BLK, 2 * DIL_BLK), 0)
    j = lax.broadcasted_iota(I32, (DIL_BLK, 2 * DIL_BLK), 1)
    band = jnp.logical_and(j >= i, j <= i + DIL_BLK)
    first = j >= DIL_BLK

    for br, dil in enumerate(DILATIONS):
        span = DIL_BLK * dil

        def unit(idx, carry, br=br, dil=dil, span=span):
            blk = idx // dil
            base = blk * span + idx % dil
            valid = jnp.logical_and(band, jnp.logical_or(first, n * (DIL_SPAN // span) + blk > 0))
            qrows = pl.ds(base, DIL_BLK, stride=dil)
            krows = pl.ds(DIL_SPAN + base - span, 2 * DIL_BLK, stride=dil)
            for g in range(DIL_G):
                q = qf[g, qrows, :].astype(BF16)
                k = kf[g, krows, :].astype(BF16)
                v = vf[g, krows, :].astype(BF16)
                s = lax.dot_general(q, k, _NT, preferred_element_type=F32) + b_ref[br, h0 + g]
                s = jnp.where(valid, s, -jnp.inf)
                mx = jnp.max(s, axis=-1, keepdims=True)
                p = jnp.exp(s - mx)
                den = jnp.sum(p, axis=-1, keepdims=True)
                ob[br, g, qrows, :] = jnp.dot(p.astype(BF16), v, preferred_element_type=F32) / den
                lse_ref[br, g, qrows, :] = jnp.broadcast_to(mx + jnp.log(den), (DIL_BLK, HEAD_DIM))
            return carry

        lax.fori_loop(0, DIL_UNITS, unit, 0, unroll=DIL_UNROLL)

    for g in range(DIL_G):
        l1, l2, l3 = lse_ref[0, g], lse_ref[1, g], lse_ref[2, g]
        mx = jnp.maximum(jnp.maximum(l1, l2), l3)
        e1, e2, e3 = jnp.exp(l1 - mx), jnp.exp(l2 - mx), jnp.exp(l3 - mx)
        mix = (e1 * ob[0, g] + e2 * ob[1, g] + e3 * ob[2, g]) / (e1 + e2 + e3)
        o_ref[0, :, g * HEAD_DIM:(g + 1) * HEAD_DIM] = mix.astype(BF16)


def _dilated(main_a, bias):
    bsz, s, _ = main_a.shape
    gw = DIL_G * HEAD_DIM
    ng = WIDTH_A // gw
    cur = lambda c: pl.BlockSpec((1, DIL_SPAN, gw), lambda b, n, g: (b, n, c * ng + g))
    prev = lambda c: pl.BlockSpec((1, DIL_SPAN, gw), lambda b, n, g: (b, jnp.maximum(n - 1, 0), c * ng + g))
    out = pl.pallas_call(
        _dilated_kernel,
        grid=(bsz, s // DIL_SPAN, ng),
        in_specs=[cur(0), prev(1), cur(1), prev(2), cur(2),
                  pl.BlockSpec(bias.shape, lambda b, n, g: (0, 0, 0, 0), pipeline_mode=pl.Buffered(1))],
        out_specs=pl.BlockSpec((1, DIL_SPAN, gw), lambda b, n, g: (b, n, g)),
        out_shape=jax.ShapeDtypeStruct((bsz, s, WIDTH_A), BF16),
        scratch_shapes=[pltpu.VMEM((DIL_G, DIL_SPAN, HEAD_DIM), F32),
                        pltpu.VMEM((DIL_G, 2 * DIL_SPAN, HEAD_DIM), F32),
                        pltpu.VMEM((DIL_G, 2 * DIL_SPAN, HEAD_DIM), F32),
                        pltpu.VMEM((len(DILATIONS), DIL_G, DIL_SPAN, HEAD_DIM), F32),
                        pltpu.VMEM((len(DILATIONS), DIL_G, DIL_SPAN, HEAD_DIM), F32)],
        compiler_params=_cparams(("parallel", "parallel", "parallel")),
        name="dilated",
    )(main_a, main_a, main_a, main_a, main_a, bias)
    return out.reshape(bsz * s, WIDTH_A)


def _dsa_schedule(s, tq, tk):
    qs, ks = [], []
    for i in range(s // tq):
        last = (i * tq + tq - 1) // tk
        for k in range(last + 1):
            qs.append(i)
            ks.append(k)
    return np.asarray(qs, np.int32), np.asarray(ks, np.int32)


def _sortable(bits):
    return bits ^ (lax.shift_right_arithmetic(bits, 31) & 0x7FFFFFFF)


def _bit_transpose(words):
    a = list(words)
    mask, j = 0x0000FFFF, 16
    while j:
        k = 0
        while k < WORD:
            t = (a[k] ^ lax.shift_right_logical(a[k + j], j)) & mask
            a[k] = a[k] ^ t
            a[k + j] = a[k + j] ^ lax.shift_left(t, j)
            k = (k + j + 1) & ~j
        j >>= 1
        mask = (mask ^ (mask << j)) & 0xFFFFFFFF
    return a


def _dsa_score_kernel(qtab, ktab, qit_ref, ki_ref, w_ref, sc_ref, thr_ref,
                      planes_ref, alive_ref, stage_ref):
    step = pl.program_id(1)
    qb = qtab[step]
    kc = ktab[step]
    w = w_ref[0] * IDX_SCALE
    srow = lax.broadcasted_iota(I32, (SCORE_ROWS, DSA_TQ), 0)
    tcol = lax.broadcasted_iota(I32, (SCORE_ROWS, DSA_TQ), 1)
    row0 = pl.multiple_of(kc * CHUNK_WORDS, CHUNK_WORDS)
    group = WORD * 8
    for g in range(SC_TK // group):
        for r0 in range(g * group, (g + 1) * group, SCORE_ROWS):
            kidx = ki_ref[0, r0:r0 + SCORE_ROWS, :]
            acc = jnp.zeros((SCORE_ROWS, DSA_TQ), F32)
            for h in range(N_IDX_HEADS):
                d = jnp.dot(kidx, qit_ref[0, h * IDX_DIM:(h + 1) * IDX_DIM, :],
                            preferred_element_type=F32)
                acc = acc + jnp.maximum(d, 0.0) * w[h:h + 1, :]
            causal = srow - tcol <= qb * DSA_TQ - kc * SC_TK - r0
            acc = jnp.where(causal, acc, -jnp.inf)
            sc_ref[r0:r0 + SCORE_ROWS, :] = acc
            stage_ref[r0:r0 + SCORE_ROWS, :] = _sortable(lax.bitcast_convert_type(acc, I32)) ^ INT_MIN
        for lanes in (slice(c, c + 128) for c in range(0, DSA_TQ, 128)):
            planes = _bit_transpose([stage_ref[g * group + 8 * n:g * group + 8 * (n + 1), lanes]
                                     for n in range(WORD)])
            for n in range(WORD):
                planes_ref[n, pl.ds(pl.multiple_of(row0 + 8 * g, 8), 8), lanes] = planes[n]

    last = (qb * DSA_TQ + DSA_TQ - 1) // SC_TK

    @pl.when(kc == last)
    def _():
        nwords = alive_ref.shape[0]
        rb = min(128, nwords)
        nrows = (last + 1) * CHUNK_WORDS
        nblk = (nrows + rb - 1) // rb
        tpos = qb * DSA_TQ + lax.broadcasted_iota(I32, (1, DSA_TQ), 1)
        want0 = jnp.minimum(tpos + 1, TOPK_MAX)
        rows = lambda blk: pl.ds(pl.multiple_of(blk * rb, rb), rb)

        def hits(alive, n, blk):
            hit = lax.population_count(alive & planes_ref[n, rows(blk), :])
            return hit.reshape(rb // 8, 8, DSA_TQ).sum(axis=0)

        def init(blk, cnt):
            ridx = blk * rb + lax.broadcasted_iota(I32, (rb, DSA_TQ), 0)
            alive = jnp.where(ridx < nrows, -1, 0)
            alive_ref[rows(blk), :] = alive
            return cnt + hits(alive, 0, blk)
        zero = jnp.zeros((8, DSA_TQ), I32)
        cnt0 = lax.fori_loop(0, nblk, init, zero)

        def bit_step(n, carry):
            r, want, cnt8 = carry
            cnt = cnt8.sum(axis=0, keepdims=True)
            take = cnt >= want
            r = r | jnp.where(take, lax.shift_left(jnp.int32(1), 31 - n), 0)
            want = jnp.where(take, want, want - cnt)
            flip = jnp.where(take, 0, -1)
            nxt = jnp.minimum(n + 1, WORD - 1)

            def sweep(blk, acc):
                alive = alive_ref[rows(blk), :] & (planes_ref[n, rows(blk), :] ^ flip)
                alive_ref[rows(blk), :] = alive
                return acc + hits(alive, nxt, blk)
            return r, want, lax.fori_loop(0, nblk, sweep, zero)

        r, _, _ = lax.fori_loop(0, WORD, bit_step, (jnp.zeros((1, DSA_TQ), I32), want0, cnt0))
        thr_ref[0] = lax.bitcast_convert_type(_sortable(r ^ INT_MIN), F32)


def _dsa_attn_kernel(qtab, ktab, sc_ref, thr_ref, q_ref, k_ref, vt_ref, b_ref, bound_ref, o_ref,
                     acc_ref, m_ref, l_ref, s_ref, mask_ref, cm_ref):
    step = pl.program_id(1)
    qb = qtab[step]
    kc = ktab[step]

    @pl.when(kc == 0)
    def _():
        acc_ref[...] = jnp.zeros_like(acc_ref)
        m_ref[...] = jnp.full_like(m_ref, NEG_BIG)
        l_ref[...] = jnp.zeros_like(l_ref)

    bound = bound_ref[0]
    bounded = bound <= DSA_SAFE_BOUND

    def set_mask(kept):
        for b in range(ATT_QT):
            cols = slice(b * DSA_TQ, (b + 1) * DSA_TQ)
            mask_ref[:, cols] = jnp.where(sc_ref[b] >= thr_ref[0, :, cols], kept, -jnp.inf)

    tiles = [[jnp.clip((qb * ATT_QT + b) - (kc * DSA_HALVES + a), -1, DSA_NEAR) + 1
              for b in range(ATT_QT)] for a in range(DSA_HALVES)]
    far = qb * ATT_QT - (kc * DSA_HALVES + DSA_HALVES - 1) >= DSA_NEAR
    far_bias = lambda h: b_ref[h, DSA_NEAR + 1, 0:1, 0:1]

    def logits(h):
        sl = slice(h * HEAD_DIM, (h + 1) * HEAD_DIM)
        s = lax.dot_general(k_ref[0, :, sl], q_ref[0, :, sl], _NT, preferred_element_type=F32)
        return s + mask_ref[...]

    def near_bias(h):
        return jnp.concatenate(
            [jnp.concatenate([b_ref[h, tiles[a][b]] for b in range(ATT_QT)], axis=1)
             for a in range(DSA_HALVES)], axis=0)

    def values(h):
        return vt_ref[h * HEAD_DIM:(h + 1) * HEAD_DIM, :]

    @pl.when(bounded)
    def _():
        def raw_logits(h):
            sl = slice(h * HEAD_DIM, (h + 1) * HEAD_DIM)
            s_ref[h] = lax.dot_general(k_ref[0, :, sl], q_ref[0, :, sl], _NT,
                                       preferred_element_type=F32)

        def heads(weights):
            for h in range(DSA_LOOKAHEAD):
                raw_logits(h)
            set_mask(-bound)
            for h in range(N_HEADS_B):
                if h + DSA_LOOKAHEAD < N_HEADS_B:
                    raw_logits(h + DSA_LOOKAHEAD)
                p, gain = weights(h)
                l_ref[h] += gain * jnp.sum(p, axis=0, keepdims=True)
                acc_ref[h] += gain * jnp.dot(values(h), p.astype(BF16), preferred_element_type=F32)

        @pl.when(far)
        def _():
            heads(lambda h: (jnp.exp2(s_ref[h] + mask_ref[...]), jnp.exp2(far_bias(h))))

        @pl.when(jnp.logical_not(far))
        def _():
            heads(lambda h: (jnp.exp2(s_ref[h] + mask_ref[...] + near_bias(h)), 1.0))

    @pl.when(jnp.logical_not(bounded))
    def _():
        set_mask(0.0)

        @pl.when(far)
        def _():
            for h in range(N_HEADS_B):
                s = logits(h)
                s_ref[h] = s
                cm_ref[h] = jnp.max(s, axis=0, keepdims=True) + far_bias(h)

        @pl.when(jnp.logical_not(far))
        def _():
            for h in range(N_HEADS_B):
                s = logits(h) + near_bias(h)
                s_ref[h] = s
                cm_ref[h] = jnp.max(s, axis=0, keepdims=True)

        for h in range(N_HEADS_B):
            m_old = m_ref[h]
            m_new = jnp.maximum(m_old, cm_ref[h])
            alpha = jnp.exp2(m_old - m_new)
            m_ref[h] = m_new
            shift = jnp.where(far, m_new - far_bias(h), m_new)
            p = jnp.exp2(s_ref[h] - shift)
            l_ref[h] = alpha * l_ref[h] + jnp.sum(p, axis=0, keepdims=True)
            pv = jnp.dot(values(h), p.astype(BF16), preferred_element_type=F32)
            acc_ref[h] = alpha * acc_ref[h] + pv

    last = (qb * ATT_TQ + ATT_TQ - 1) // DSA_TK

    @pl.when(kc == last)
    def _():
        for h in range(N_HEADS_B):
            o_ref[0, h * HEAD_DIM:(h + 1) * HEAD_DIM, :] = (acc_ref[h] / l_ref[h]).astype(BF16)


def _dsa(main_b, ki, wi_t, bias, logit_bound):
    bsz, s, _ = main_b.shape
    w = WIDTH_B
    qs, ks = _dsa_schedule(s, DSA_TQ, SC_TK)
    npairs = len(qs)
    qi_t =jnp.swapaxes(main_b[:, :, 3 * PROJ_BLK:4 * PROJ_BLK], 1, 2)
    scores, thr = pl.pallas_call(
        _dsa_score_kernel,
        grid_spec=pltpu.PrefetchScalarGridSpec(
            num_scalar_prefetch=2,
            grid=(bsz, npairs),
            in_specs=[
                pl.BlockSpec((1, N_IDX_HEADS * IDX_DIM, DSA_TQ), lambda b, t, qt, kt: (b, 0, qt[t])),
                pl.BlockSpec((1, SC_TK, IDX_DIM), lambda b, t, qt, kt: (b, kt[t], 0)),
                pl.BlockSpec((1, N_IDX_HEADS, DSA_TQ), lambda b, t, qt, kt: (b, 0, qt[t])),
            ],
            out_specs=[
                pl.BlockSpec((None, None, None, SC_TK, DSA_TQ),
                             lambda b, t, qt, kt: (b, kt[t], qt[t], 0, 0)),
                pl.BlockSpec((1, 1, DSA_TQ), lambda b, t, qt, kt: (b, 0, qt[t])),
            ],
            scratch_shapes=[pltpu.VMEM((WORD, s // WORD, DSA_TQ), I32),
                            pltpu.VMEM((s // WORD, DSA_TQ), I32),
                            pltpu.VMEM((SC_TK, DSA_TQ), I32)],
        ),
        out_shape=[jax.ShapeDtypeStruct((bsz, s // SC_TK, s // DSA_TQ, SC_TK, DSA_TQ), F32),
                   jax.ShapeDtypeStruct((bsz, 1, s), F32)],
        compiler_params=_cparams(("arbitrary", "arbitrary")),
        name="dsa_scores",
    )(qs, ks, qi_t, ki, wi_t)

    vt = jnp.swapaxes(main_b[:, :, 2 * PROJ_BLK:3 * PROJ_BLK].reshape(bsz, s // DSA_TK, DSA_TK, w), 2, 3)
    ntile = bias.shape[1]
    qs, ks = _dsa_schedule(s, ATT_TQ, DSA_TK)
    out_t = pl.pallas_call(
        _dsa_attn_kernel,
        grid_spec=pltpu.PrefetchScalarGridSpec(
            num_scalar_prefetch=2,
            grid=(bsz, len(qs)),
            in_specs=[
                pl.BlockSpec((None, None, ATT_QT, DSA_TK, DSA_TQ),
                             lambda b, t, qt, kt: (b, kt[t] // SC_PER_ATT, qt[t], kt[t] % SC_PER_ATT, 0)),
                pl.BlockSpec((1, 1, ATT_TQ), lambda b, t, qt, kt: (b, 0, qt[t])),
                pl.BlockSpec((1, ATT_TQ, w), lambda b, t, qt, kt: (b, qt[t], 0)),
                pl.BlockSpec((1, DSA_TK, w), lambda b, t, qt, kt: (b, kt[t], 1)),
                pl.BlockSpec((None, None, w, DSA_TK), lambda b, t, qt, kt: (b, kt[t], 0, 0)),
                pl.BlockSpec((N_HEADS_B, ntile, DSA_TQ, DSA_TQ), lambda b, t, qt, kt: (0, 0, 0, 0),
                             pipeline_mode=pl.Buffered(1)),
                pl.BlockSpec(memory_space=pltpu.SMEM),
            ],
            out_specs=pl.BlockSpec((1, w, ATT_TQ), lambda b, t, qt, kt: (b, 0, qt[t])),
            scratch_shapes=[pltpu.VMEM((N_HEADS_B, HEAD_DIM, ATT_TQ), F32),
                            pltpu.VMEM((N_HEADS_B, 1, ATT_TQ), F32),
                            pltpu.VMEM((N_HEADS_B, 1, ATT_TQ), F32),
                            pltpu.VMEM((N_HEADS_B, DSA_TK, ATT_TQ), F32),
                            pltpu.VMEM((DSA_TK, ATT_TQ), F32),
                            pltpu.VMEM((N_HEADS_B, 1, ATT_TQ), F32)],
        ),
        out_shape=jax.ShapeDtypeStruct((bsz, w, s), BF16),
        compiler_params=_cparams(("arbitrary", "arbitrary")),
        name="dsa_attn",
    )(qs, ks, scores, thr, main_b, main_b, vt, bias, logit_bound)
    return jnp.swapaxes(out_t, 1, 2).reshape(bsz * s, w)


def _out_kernel(x_ref, ma_ref, mb_ref, wa_ref, wb_ref, o_ref):
    o_ref[...] = (x_ref[...]
                  + jnp.dot(ma_ref[...], wa_ref[...], preferred_element_type=F32)
                  + jnp.dot(mb_ref[...], wb_ref[...], preferred_element_type=F32))


def _out_proj(x2, mix_a, mix_b, w_out, tm=512):
    m, d = x2.shape
    row = lambda width: pl.BlockSpec((tm, width), lambda i: (i, 0))
    return pl.pallas_call(
        _out_kernel,
        grid=(m // tm,),
        in_specs=[row(d), row(WIDTH_A), row(WIDTH_B),
                  pl.BlockSpec((WIDTH_A, d), lambda i: (0, 0)),
                  pl.BlockSpec((WIDTH_B, d), lambda i: (1, 0))],
        out_specs=row(d),
        out_shape=jax.ShapeDtypeStruct((m, d), F32),
        compiler_params=_cparams(("parallel",)),
        name="out_proj",
    )(x2, mix_a, mix_b, w_out, w_out)


def kernel(x, rel_bias, norm_ffn1, w_ffn1_in, w_ffn1_out, norm_mix, w_in, q_norm_a, k_norm_a,
           q_norm_b, k_norm_b, w_out, norm_ffn2, w_ffn2_in, w_ffn2_out):
    bsz, s, d = x.shape
    m = bsz * s
    depth = norm_ffn1.shape[0]
    scale = HEAD_DIM ** -0.5
    n_main = PROJ_NBLK * PROJ_BLK
    dil_bias, dsa_bias = _bias_tiles(rel_bias)
    x2 = x.reshape(m, d)
    ones = jnp.ones((PROJ_BLK,), F32)
    tile8 = lambda g: jnp.tile(g, PROJ_BLK // HEAD_DIM)
    for l in range(depth):
        x2 = _ffn(x2, norm_ffn1[l], w_ffn1_in[l].astype(BF16), w_ffn1_out[l].astype(BF16))

        w_main = w_in[l, :, :n_main].astype(BF16)
        w_idx = jnp.pad(w_in[l, :, n_main:], ((0, 0), (0, IDX_PAD - IDX_DIM - N_IDX_HEADS))).astype(BF16)
        head_gain = jnp.stack([tile8(q_norm_a[l] * scale), tile8(k_norm_a[l]), ones,
                               tile8(q_norm_b[l] * (scale * LOG2E)), tile8(k_norm_b[l]), ones, ones])
        main_a, main_b, idx = _proj(x2, norm_mix[l], w_main, w_idx,
                                    head_gain.reshape(PROJ_NBLK, 1, PROJ_BLK))
        main_a = main_a.reshape(bsz, s, PROJ_NA * PROJ_BLK)
        main_b = main_b.reshape(bsz, s, (PROJ_NBLK - PROJ_NA) * PROJ_BLK)
        idx = idx.reshape(bsz, s, IDX_PAD)
        ki = idx[:, :, :IDX_DIM].astype(BF16)
        wi_t = jnp.swapaxes(idx[:, :, IDX_DIM:IDX_DIM + N_IDX_HEADS], 1, 2)

        mix_a = _dilated(main_a, dil_bias)
        logit_bound = (1.02 * HEAD_DIM * jnp.max(jnp.abs(q_norm_b[l] * (scale * LOG2E)))
                       * jnp.max(jnp.abs(k_norm_b[l]))
                       + LOG2E * jnp.max(jnp.abs(rel_bias[:, N_HEADS_A:]))).reshape(1)
        mix_b = _dsa(main_b, ki, wi_t, dsa_bias, logit_bound)

        x2 = _out_proj(x2, mix_a, mix_b, w_out[l].astype(BF16))
        x2 = _ffn(x2, norm_ffn2[l], w_ffn2_in[l].astype(BF16), w_ffn2_out[l].astype(BF16))
    return x2.reshape(bsz, s, d)
```

```python
import functools
import math

import jax
import jax.numpy as jnp
import numpy as np
from jax import lax
from jax.experimental import pallas as pl
from jax.experimental.pallas import tpu as pltpu

F32 = jnp.float32
BF16 = jnp.bfloat16
I32 = jnp.int32

HEAD_DIM = 128
N_HEADS_A = 8
N_HEADS_B = 8
WIDTH_A = N_HEADS_A * HEAD_DIM
WIDTH_B = N_HEADS_B * HEAD_DIM
DILATIONS = (1, 4, 16)
DIL_BLK = 128
N_IDX_HEADS = 16
IDX_DIM = 64
TOPK_MAX = 256
N_BUCKETS = 32
REL_MAX_DIST = 2048
EPS = 1e-6
IDX_PAD = 128
IDX_SCALE = (IDX_DIM ** -0.5) * (N_IDX_HEADS ** -0.5)
LOG2E = math.log2(math.e)

VMEM_LIMIT = 56 * 1024 * 1024

DSA_TQ = 256
DSA_TK = 512
DSA_HALVES = DSA_TK // DSA_TQ
ATT_TQ = 512
ATT_QT = ATT_TQ // DSA_TQ
WORD = 32
SC_TK = 1024
SC_PER_ATT = SC_TK // DSA_TK
CHUNK_WORDS = SC_TK // WORD
SCORE_ROWS = 64
NEG_BIG = -1e30
DSA_SAFE_BOUND = 40.0
DSA_LOOKAHEAD = 2
INT_MIN = -2 ** 31


def _near_tiles():
    exact = N_BUCKETS // 2
    j = 1
    while True:
        d = j * DSA_TQ - (DSA_TQ - 1)
        b = exact + int(math.log(d / exact) / math.log(REL_MAX_DIST / exact) * (N_BUCKETS - exact) - 0.02)
        if d >= exact and b >= N_BUCKETS - 1:
            return j
        j += 1


DSA_NEAR = _near_tiles()


def _cparams(sem):
    return pltpu.CompilerParams(dimension_semantics=sem, vmem_limit_bytes=VMEM_LIMIT)


def _rel_bucket(dist):
    exact = N_BUCKETS // 2
    df = jnp.maximum(dist, 1).astype(F32)
    large = exact + (jnp.log(df / exact) / math.log(REL_MAX_DIST / exact)
                     * (N_BUCKETS - exact)).astype(I32)
    large = jnp.minimum(large, N_BUCKETS - 1)
    return jnp.where(dist < exact, dist, large)


def _lookup(tab_ref, bucket, col):
    val = jnp.zeros(bucket.shape, F32)
    for b in range(N_BUCKETS):
        val = jnp.where(bucket == b, tab_ref[b, col], val)
    return val


def _dil_bias_kernel(tab_ref, o_ref):
    dil = jnp.left_shift(1, 2 * pl.program_id(0))
    i = lax.broadcasted_iota(I32, (DIL_BLK, 2 * DIL_BLK), 0)
    j = lax.broadcasted_iota(I32, (DIL_BLK, 2 * DIL_BLK), 1)
    bucket = _rel_bucket(jnp.clip(DIL_BLK + i - j, 0, DIL_BLK) * dil)
    for h in range(N_HEADS_A):
        o_ref[0, h] = _lookup(tab_ref, bucket, h)


def _dsa_bias_kernel(tab_ref, o_ref):
    jj = pl.program_id(0)
    s = lax.broadcasted_iota(I32, (DSA_TQ, DSA_TQ), 0)
    t = lax.broadcasted_iota(I32, (DSA_TQ, DSA_TQ), 1)
    bucket = _rel_bucket(jnp.maximum((jj - 1) * DSA_TQ + t - s, 0))
    for h in range(N_HEADS_B):
        o_ref[h, 0] = _lookup(tab_ref, bucket, N_HEADS_A + h) * LOG2E


def _bias_tiles(rel_bias):
    smem = pl.BlockSpec(memory_space=pltpu.SMEM)
    dil = pl.pallas_call(
        _dil_bias_kernel,
        grid=(len(DILATIONS),),
        in_specs=[smem],
        out_specs=pl.BlockSpec((1, N_HEADS_A, DIL_BLK, 2 * DIL_BLK), lambda g: (g, 0, 0, 0)),
        out_shape=jax.ShapeDtypeStruct((len(DILATIONS), N_HEADS_A, DIL_BLK, 2 * DIL_BLK), F32),
        name="dil_bias",
    )(rel_bias)
    ntile = DSA_NEAR + 2
    dsa = pl.pallas_call(
        _dsa_bias_kernel,
        grid=(ntile,),
        in_specs=[smem],
        out_specs=pl.BlockSpec((N_HEADS_B, 1, DSA_TQ, DSA_TQ), lambda g: (0, g, 0, 0)),
        out_shape=jax.ShapeDtypeStruct((N_HEADS_B, ntile, DSA_TQ, DSA_TQ), F32),
        name="dsa_bias",
    )(rel_bias)
    return dil, dsa


def _ffn_kernel(x_ref, g_ref, wg_ref, wu_ref, wo_ref, o_ref, h_ref, acc_ref):
    f = pl.program_id(1)

    @pl.when(f == 0)
    def _():
        x = x_ref[...]
        ms = jnp.mean(x * x, axis=-1, keepdims=True)
        h_ref[...] = (x * lax.rsqrt(ms + EPS) * g_ref[...]).astype(BF16)
        acc_ref[...] = jnp.zeros_like(acc_ref)

    h = h_ref[...]
    gate = jnp.dot(h, wg_ref[...], preferred_element_type=F32)
    up = jnp.dot(h, wu_ref[...], preferred_element_type=F32)
    act = (gate * jax.nn.sigmoid(gate) * up).astype(BF16)
    acc_ref[...] += jnp.dot(act, wo_ref[...], preferred_element_type=F32)

    @pl.when(f == pl.num_programs(1) - 1)
    def _():
        o_ref[...] = x_ref[...] + 0.5 * acc_ref[...]


def _ffn(x2, gain, w_in, w_out, tm=512, tf=512):
    m, d = x2.shape
    d_ff = w_out.shape[0]
    nf = d_ff // tf
    chunk = lambda i, f: jnp.where(i % 2 == 0, f, nf - 1 - f)
    return pl.pallas_call(
        _ffn_kernel,
        grid=(m // tm, nf),
        in_specs=[
            pl.BlockSpec((tm, d), lambda i, f: (i, 0)),
            pl.BlockSpec((1, d), lambda i, f: (0, 0)),
            pl.BlockSpec((d, tf), lambda i, f: (0, chunk(i, f))),
            pl.BlockSpec((d, tf), lambda i, f: (0, chunk(i, f) + nf)),
            pl.BlockSpec((tf, d), lambda i, f: (chunk(i, f), 0)),
        ],
        out_specs=pl.BlockSpec((tm, d), lambda i, f: (i, 0)),
        out_shape=jax.ShapeDtypeStruct((m, d), F32),
        scratch_shapes=[pltpu.VMEM((tm, d), BF16), pltpu.VMEM((tm, d), F32)],
        compiler_params=_cparams(("parallel", "arbitrary")),
        name="ffn",
    )(x2, gain.reshape(1, d), w_in, w_in, w_out)


PROJ_BLK = 1024
PROJ_NBLK = 7
PROJ_NA = 3
_NORM_BLOCKS = (0, 1, 3, 4)


def _proj_block(i, step):
    return jnp.where(i % 2 == 0, step, PROJ_NBLK - 1 - step)


def _proj_kernel(x_ref, g_ref, w_ref, widx_ref, hg_ref, oa_ref, ob_ref, idx_ref, h_ref):
    j = _proj_block(pl.program_id(0), pl.program_id(1))

    @pl.when(pl.program_id(1) == 0)
    def _():
        x = x_ref[...]
        ms = jnp.mean(x * x, axis=-1, keepdims=True)
        h_ref[...] = (x * lax.rsqrt(ms + EPS) * g_ref[...]).astype(BF16)

    h = h_ref[...]
    y = jnp.dot(h, w_ref[...], preferred_element_type=F32)
    is_norm = functools.reduce(jnp.logical_or, [j == b for b in _NORM_BLOCKS])

    def emit(dst_ref):
        @pl.when(is_norm)
        def _():
            for hd in range(PROJ_BLK // HEAD_DIM):
                sl = slice(hd * HEAD_DIM, (hd + 1) * HEAD_DIM)
                yh = y[:, sl]
                ms = jnp.mean(yh * yh, axis=-1, keepdims=True)
                dst_ref[:, sl] = (yh * lax.rsqrt(ms + EPS) * hg_ref[0, :, sl]).astype(BF16)

        @pl.when(jnp.logical_not(is_norm))
        def _():
            dst_ref[...] = y.astype(BF16)

    pl.when(j < PROJ_NA)(lambda: emit(oa_ref))
    pl.when(j >= PROJ_NA)(lambda: emit(ob_ref))

    @pl.when(j == PROJ_NBLK - 1)
    def _():
        idx_ref[...] = jnp.dot(h, widx_ref[...], preferred_element_type=F32)


def _proj(x2, gain, w_main, w_idx, head_gain, tm=1024):
    m, d = x2.shape
    return pl.pallas_call(
        _proj_kernel,
        grid=(m // tm, PROJ_NBLK),
        in_specs=[
            pl.BlockSpec((tm, d), lambda i, j: (i, 0)),
            pl.BlockSpec((1, d), lambda i, j: (0, 0)),
            pl.BlockSpec((d, PROJ_BLK), lambda i, j: (0, _proj_block(i, j))),
            pl.BlockSpec((d, IDX_PAD), lambda i, j: (0, 0)),
            pl.BlockSpec((1, 1, PROJ_BLK), lambda i, j: (_proj_block(i, j), 0, 0)),
        ],
        out_specs=[
            pl.BlockSpec((tm, PROJ_BLK), lambda i, j: (i, jnp.minimum(_proj_block(i, j), PROJ_NA - 1))),
            pl.BlockSpec((tm, PROJ_BLK), lambda i, j: (i, jnp.maximum(_proj_block(i, j) - PROJ_NA, 0))),
            pl.BlockSpec((tm, IDX_PAD), lambda i, j: (i, 0)),
        ],
        out_shape=[
            jax.ShapeDtypeStruct((m, PROJ_NA * PROJ_BLK), BF16),
            jax.ShapeDtypeStruct((m, (PROJ_NBLK - PROJ_NA) * PROJ_BLK), BF16),
            jax.ShapeDtypeStruct((m, IDX_PAD), F32),
        ],
        scratch_shapes=[pltpu.VMEM((tm, d), BF16)],
        compiler_params=_cparams(("parallel", "arbitrary")),
        name="proj",
    )(x2, gain.reshape(1, d), w_main, w_idx, head_gain)


_NT = (((1,), (1,)), ((), ()))


DIL_SPAN = DIL_BLK * max(DILATIONS)
DIL_G = 2
DIL_UNITS = DIL_SPAN // DIL_BLK
DIL_UNROLL = 8


def _dilated_kernel(q_ref, kp_ref, kc_ref, vp_ref, vc_ref, b_ref, o_ref, qf, kf, vf, ob, lse_ref):
    n = pl.program_id(1)
    h0 = pl.program_id(2) * DIL_G
    for g in range(DIL_G):
        sl = slice(g * HEAD_DIM, (g + 1) * HEAD_DIM)
        qf[g] = q_ref[0, :, sl].astype(F32)
        kf[g, :DIL_SPAN] = kp_ref[0, :, sl].astype(F32)
        kf[g, DIL_SPAN:] = kc_ref[0, :, sl].astype(F32)
        vf[g, :DIL_SPAN] = vp_ref[0, :, sl].astype(F32)
        vf[g, DIL_SPAN:] = vc_ref[0, :, sl].astype(F32)
    i = lax.broadcasted_iota(I32, (DIL_BLK, 2 * DIL_BLK), 0)
    j = lax.broadcasted_iota(I32, (DIL_BLK, 2 * DIL_BLK), 1)
    band = jnp.logical_and(j >= i, j <= i + DIL_BLK)
    first = j >= DIL_BLK

    for br, dil in enumerate(DILATIONS):
        span = DIL_BLK * dil

        def unit(idx, carry, br=br, dil=dil, span=span):
            blk = idx // dil
            base = blk * span + idx % dil
            valid = jnp.logical_and(band, jnp.logical_or(first, n * (DIL_SPAN // span) + blk > 0))
            qrows = pl.ds(base, DIL_BLK, stride=dil)
            krows = pl.ds(DIL_SPAN + base - span, 2 * DIL_BLK, stride=dil)
            for g in range(DIL_G):
                q = qf[g, qrows, :].astype(BF16)
                k = kf[g, krows, :].astype(BF16)
                v = vf[g, krows, :].astype(BF16)
                s = lax.dot_general(q, k, _NT, preferred_element_type=F32) + b_ref[br, h0 + g]
                s = jnp.where(valid, s, -jnp.inf)
                mx = jnp.max(s, axis=-1, keepdims=True)
                p = jnp.exp(s - mx)
                den = jnp.sum(p, axis=-1, keepdims=True)
                ob[br, g, qrows, :] = jnp.dot(p.astype(BF16), v, preferred_element_type=F32) / den
                lse_ref[br, g, qrows, :] = jnp.broadcast_to(mx + jnp.log(den), (DIL_BLK, HEAD_DIM))
            return carry

        lax.fori_loop(0, DIL_UNITS, unit, 0, unroll=DIL_UNROLL)

    for g in range(DIL_G):
        l1, l2, l3 = lse_ref[0, g], lse_ref[1, g], lse_ref[2, g]
        mx = jnp.maximum(jnp.maximum(l1, l2), l3)
        e1, e2, e3 = jnp.exp(l1 - mx), jnp.exp(l2 - mx), jnp.exp(l3 - mx)
        mix = (e1 * ob[0, g] + e2 * ob[1, g] + e3 * ob[2, g]) / (e1 + e2 + e3)
        o_ref[0, :, g * HEAD_DIM:(g + 1) * HEAD_DIM] = mix.astype(BF16)


def _dilated(main_a, bias):
    bsz, s, _ = main_a.shape
    gw = DIL_G * HEAD_DIM
    ng = WIDTH_A // gw
    cur = lambda c: pl.BlockSpec((1, DIL_SPAN, gw), lambda b, n, g: (b, n, c * ng + g))
    prev = lambda c: pl.BlockSpec((1, DIL_SPAN, gw), lambda b, n, g: (b, jnp.maximum(n - 1, 0), c * ng + g))
    out = pl.pallas_call(
        _dilated_kernel,
        grid=(bsz, s // DIL_SPAN, ng),
        in_specs=[cur(0), prev(1), cur(1), prev(2), cur(2),
                  pl.BlockSpec(bias.shape, lambda b, n, g: (0, 0, 0, 0), pipeline_mode=pl.Buffered(1))],
        out_specs=pl.BlockSpec((1, DIL_SPAN, gw), lambda b, n, g: (b, n, g)),
        out_shape=jax.ShapeDtypeStruct((bsz, s, WIDTH_A), BF16),
        scratch_shapes=[pltpu.VMEM((DIL_G, DIL_SPAN, HEAD_DIM), F32),
                        pltpu.VMEM((DIL_G, 2 * DIL_SPAN, HEAD_DIM), F32),
                        pltpu.VMEM((DIL_G, 2 * DIL_SPAN, HEAD_DIM), F32),
                        pltpu.VMEM((len(DILATIONS), DIL_G, DIL_SPAN, HEAD_DIM), F32),
                        pltpu.VMEM((len(DILATIONS), DIL_G, DIL_SPAN, HEAD_DIM), F32)],
        compiler_params=_cparams(("parallel", "parallel", "parallel")),
        name="dilated",
    )(main_a, main_a, main_a, main_a, main_a, bias)
    return out.reshape(bsz * s, WIDTH_A)


def _dsa_schedule(s, tq, tk):
    qs, ks = [], []
    for i in range(s // tq):
        last = (i * tq + tq - 1) // tk
        for k in range(last + 1):
            qs.append(i)
            ks.append(k)
    return np.asarray(qs, np.int32), np.asarray(ks, np.int32)


def _sortable(bits):
    return bits ^ (lax.shift_right_arithmetic(bits, 31) & 0x7FFFFFFF)


def _bit_transpose(words):
    a = list(words)
    mask, j = 0x0000FFFF, 16
    while j:
        k = 0
        while k < WORD:
            t = (a[k] ^ lax.shift_right_logical(a[k + j], j)) & mask
            a[k] = a[k] ^ t
            a[k + j] = a[k + j] ^ lax.shift_left(t, j)
            k = (k + j + 1) & ~j
        j >>= 1
        mask = (mask ^ (mask << j)) & 0xFFFFFFFF
    return a


def _dsa_score_kernel(qtab, ktab, qit_ref, ki_ref, w_ref, sc_ref, thr_ref,
                      planes_ref, alive_ref, stage_ref):
    step = pl.program_id(1)
    qb = qtab[step]
    kc = ktab[step]
    w = w_ref[0] * IDX_SCALE
    srow = lax.broadcasted_iota(I32, (SCORE_ROWS, DSA_TQ), 0)
    tcol = lax.broadcasted_iota(I32, (SCORE_ROWS, DSA_TQ), 1)
    row0 = pl.multiple_of(kc * CHUNK_WORDS, CHUNK_WORDS)
    group = WORD * 8
    for g in range(SC_TK // group):
        for r0 in range(g * group, (g + 1) * group, SCORE_ROWS):
            kidx = ki_ref[0, r0:r0 + SCORE_ROWS, :]
            acc = jnp.zeros((SCORE_ROWS, DSA_TQ), F32)
            for h in range(N_IDX_HEADS):
                d = jnp.dot(kidx, qit_ref[0, h * IDX_DIM:(h + 1) * IDX_DIM, :],
                            preferred_element_type=F32)
                acc = acc + jnp.maximum(d, 0.0) * w[h:h + 1, :]
            causal = srow - tcol <= qb * DSA_TQ - kc * SC_TK - r0
            acc = jnp.where(causal, acc, -jnp.inf)
            sc_ref[r0:r0 + SCORE_ROWS, :] = acc
            stage_ref[r0:r0 + SCORE_ROWS, :] = _sortable(lax.bitcast_convert_type(acc, I32)) ^ INT_MIN
        for lanes in (slice(c, c + 128) for c in range(0, DSA_TQ, 128)):
            planes = _bit_transpose([stage_ref[g * group + 8 * n:g * group + 8 * (n + 1), lanes]
                                     for n in range(WORD)])
            for n in range(WORD):
                planes_ref[n, pl.ds(pl.multiple_of(row0 + 8 * g, 8), 8), lanes] = planes[n]

    last = (qb * DSA_TQ + DSA_TQ - 1) // SC_TK

    @pl.when(kc == last)
    def _():
        nwords = alive_ref.shape[0]
        rb = min(128, nwords)
        nrows = (last + 1) * CHUNK_WORDS
        nblk = (nrows + rb - 1) // rb
        tpos = qb * DSA_TQ + lax.broadcasted_iota(I32, (1, DSA_TQ), 1)
        want0 = jnp.minimum(tpos + 1, TOPK_MAX)
        rows = lambda blk: pl.ds(pl.multiple_of(blk * rb, rb), rb)

        def hits(alive, n, blk):
            hit = lax.population_count(alive & planes_ref[n, rows(blk), :])
            return hit.reshape(rb // 8, 8, DSA_TQ).sum(axis=0)

        def init(blk, cnt):
            ridx = blk * rb + lax.broadcasted_iota(I32, (rb, DSA_TQ), 0)
            alive = jnp.where(ridx < nrows, -1, 0)
            alive_ref[rows(blk), :] = alive
            return cnt + hits(alive, 0, blk)
        zero = jnp.zeros((8, DSA_TQ), I32)
        cnt0 = lax.fori_loop(0, nblk, init, zero)

        def bit_step(n, carry):
            r, want, cnt8 = carry
            cnt = cnt8.sum(axis=0, keepdims=True)
            take = cnt >= want
            r = r | jnp.where(take, lax.shift_left(jnp.int32(1), 31 - n), 0)
            want = jnp.where(take, want, want - cnt)
            flip = jnp.where(take, 0, -1)
            nxt = jnp.minimum(n + 1, WORD - 1)

            def sweep(blk, acc):
                alive = alive_ref[rows(blk), :] & (planes_ref[n, rows(blk), :] ^ flip)
                alive_ref[rows(blk), :] = alive
                return acc + hits(alive, nxt, blk)
            return r, want, lax.fori_loop(0, nblk, sweep, zero)

        r, _, _ = lax.fori_loop(0, WORD, bit_step, (jnp.zeros((1, DSA_TQ), I32), want0, cnt0))
        thr_ref[0] = lax.bitcast_convert_type(_sortable(r ^ INT_MIN), F32)


def _dsa_attn_kernel(qtab, ktab, sc_ref, thr_ref, q_ref, k_ref, vt_ref, b_ref, bound_ref, o_ref,
                     acc_ref, m_ref, l_ref, s_ref, mask_ref, cm_ref):
    step = pl.program_id(1)
    qb = qtab[step]
    kc = ktab[step]

    @pl.when(kc == 0)
    def _():
        acc_ref[...] = jnp.zeros_like(acc_ref)
        m_ref[...] = jnp.full_like(m_ref, NEG_BIG)
        l_ref[...] = jnp.zeros_like(l_ref)

    bound = bound_ref[0]
    bounded = bound <= DSA_SAFE_BOUND

    def set_mask(kept):
        for b in range(ATT_QT):
            cols = slice(b * DSA_TQ, (b + 1) * DSA_TQ)
            mask_ref[:, cols] = jnp.where(sc_ref[b] >= thr_ref[0, :, cols], kept, -jnp.inf)

    tiles = [[jnp.clip((qb * ATT_QT + b) - (kc * DSA_HALVES + a), -1, DSA_NEAR) + 1
              for b in range(ATT_QT)] for a in range(DSA_HALVES)]
    far = qb * ATT_QT - (kc * DSA_HALVES + DSA_HALVES - 1) >= DSA_NEAR
    far_bias = lambda h: b_ref[h, DSA_NEAR + 1, 0:1, 0:1]

    def logits(h):
        sl = slice(h * HEAD_DIM, (h + 1) * HEAD_DIM)
        s = lax.dot_general(k_ref[0, :, sl], q_ref[0, :, sl], _NT, preferred_element_type=F32)
        return s + mask_ref[...]

    def near_bias(h):
        return jnp.concatenate(
            [jnp.concatenate([b_ref[h, tiles[a][b]] for b in range(ATT_QT)], axis=1)
             for a in range(DSA_HALVES)], axis=0)

    def values(h):
        return vt_ref[h * HEAD_DIM:(h + 1) * HEAD_DIM, :]

    @pl.when(bounded)
    def _():
        def raw_logits(h):
            sl = slice(h * HEAD_DIM, (h + 1) * HEAD_DIM)
            s_ref[h] = lax.dot_general(k_ref[0, :, sl], q_ref[0, :, sl], _NT,
                                       preferred_element_type=F32)

        def heads(weights):
            for h in range(DSA_LOOKAHEAD):
                raw_logits(h)
            set_mask(-bound)
            for h in range(N_HEADS_B):
                if h + DSA_LOOKAHEAD < N_HEADS_B:
                    raw_logits(h + DSA_LOOKAHEAD)
                p, gain = weights(h)
                l_ref[h] += gain * jnp.sum(p, axis=0, keepdims=True)
                acc_ref[h] += gain * jnp.dot(values(h), p.astype(BF16), preferred_element_type=F32)

        @pl.when(far)
        def _():
            heads(lambda h: (jnp.exp2(s_ref[h] + mask_ref[...]), jnp.exp2(far_bias(h))))

        @pl.when(jnp.logical_not(far))
        def _():
            heads(lambda h: (jnp.exp2(s_ref[h] + mask_ref[...] + near_bias(h)), 1.0))

    @pl.when(jnp.logical_not(bounded))
    def _():
        set_mask(0.0)

        @pl.when(far)
        def _():
            for h in range(N_HEADS_B):
                s = logits(h)
                s_ref[h] = s
                cm_ref[h] = jnp.max(s, axis=0, keepdims=True) + far_bias(h)

        @pl.when(jnp.logical_not(far))
        def _():
            for h in range(N_HEADS_B):
                s = logits(h) + near_bias(h)
                s_ref[h] = s
                cm_ref[h] = jnp.max(s, axis=0, keepdims=True)

        for h in range(N_HEADS_B):
            m_old = m_ref[h]
            m_new = jnp.maximum(m_old, cm_ref[h])
            alpha = jnp.exp2(m_old - m_new)
            m_ref[h] = m_new
            shift = jnp.where(far, m_new - far_bias(h), m_new)
            p = jnp.exp2(s_ref[h] - shift)
            l_ref[h] = alpha * l_ref[h] + jnp.sum(p, axis=0, keepdims=True)
            pv = jnp.dot(values(h), p.astype(BF16), preferred_element_type=F32)
            acc_ref[h] = alpha * acc_ref[h] + pv

    last = (qb * ATT_TQ + ATT_TQ - 1) // DSA_TK

    @pl.when(kc == last)
    def _():
        for h in range(N_HEADS_B):
            o_ref[0, h * HEAD_DIM:(h + 1) * HEAD_DIM, :] = (acc_ref[h] / l_ref[h]).astype(BF16)


def _dsa(main_b, ki, wi_t, bias, logit_bound):
    bsz, s, _ = main_b.shape
    w = WIDTH_B
    qs, ks = _dsa_schedule(s, DSA_TQ, SC_TK)
    npairs = len(qs)
    qi_t =jnp.swapaxes(main_b[:, :, 3 * PROJ_BLK:4 * PROJ_BLK], 1, 2)
    scores, thr = pl.pallas_call(
        _dsa_score_kernel,
        grid_spec=pltpu.PrefetchScalarGridSpec(
            num_scalar_prefetch=2,
            grid=(bsz, npairs),
            in_specs=[
                pl.BlockSpec((1, N_IDX_HEADS * IDX_DIM, DSA_TQ), lambda b, t, qt, kt: (b, 0, qt[t])),
                pl.BlockSpec((1, SC_TK, IDX_DIM), lambda b, t, qt, kt: (b, kt[t], 0)),
                pl.BlockSpec((1, N_IDX_HEADS, DSA_TQ), lambda b, t, qt, kt: (b, 0, qt[t])),
            ],
            out_specs=[
                pl.BlockSpec((None, None, None, SC_TK, DSA_TQ),
                             lambda b, t, qt, kt: (b, kt[t], qt[t], 0, 0)),
                pl.BlockSpec((1, 1, DSA_TQ), lambda b, t, qt, kt: (b, 0, qt[t])),
            ],
            scratch_shapes=[pltpu.VMEM((WORD, s // WORD, DSA_TQ), I32),
                            pltpu.VMEM((s // WORD, DSA_TQ), I32),
                            pltpu.VMEM((SC_TK, DSA_TQ), I32)],
        ),
        out_shape=[jax.ShapeDtypeStruct((bsz, s // SC_TK, s // DSA_TQ, SC_TK, DSA_TQ), F32),
                   jax.ShapeDtypeStruct((bsz, 1, s), F32)],
        compiler_params=_cparams(("arbitrary", "arbitrary")),
        name="dsa_scores",
    )(qs, ks, qi_t, ki, wi_t)

    vt = jnp.swapaxes(main_b[:, :, 2 * PROJ_BLK:3 * PROJ_BLK].reshape(bsz, s // DSA_TK, DSA_TK, w), 2, 3)
    ntile = bias.shape[1]
    qs, ks = _dsa_schedule(s, ATT_TQ, DSA_TK)
    out_t = pl.pallas_call(
        _dsa_attn_kernel,
        grid_spec=pltpu.PrefetchScalarGridSpec(
            num_scalar_prefetch=2,
            grid=(bsz, len(qs)),
            in_specs=[
                pl.BlockSpec((None, None, ATT_QT, DSA_TK, DSA_TQ),
                             lambda b, t, qt, kt: (b, kt[t] // SC_PER_ATT, qt[t], kt[t] % SC_PER_ATT, 0)),
                pl.BlockSpec((1, 1, ATT_TQ), lambda b, t, qt, kt: (b, 0, qt[t])),
                pl.BlockSpec((1, ATT_TQ, w), lambda b, t, qt, kt: (b, qt[t], 0)),
                pl.BlockSpec((1, DSA_TK, w), lambda b, t, qt, kt: (b, kt[t], 1)),
                pl.BlockSpec((None, None, w, DSA_TK), lambda b, t, qt, kt: (b, kt[t], 0, 0)),
                pl.BlockSpec((N_HEADS_B, ntile, DSA_TQ, DSA_TQ), lambda b, t, qt, kt: (0, 0, 0, 0),
                             pipeline_mode=pl.Buffered(1)),
                pl.BlockSpec(memory_space=pltpu.SMEM),
            ],
            out_specs=pl.BlockSpec((1, w, ATT_TQ), lambda b, t, qt, kt: (b, 0, qt[t])),
            scratch_shapes=[pltpu.VMEM((N_HEADS_B, HEAD_DIM, ATT_TQ), F32),
                            pltpu.VMEM((N_HEADS_B, 1, ATT_TQ), F32),
                            pltpu.VMEM((N_HEADS_B, 1, ATT_TQ), F32),
                            pltpu.VMEM((N_HEADS_B, DSA_TK, ATT_TQ), F32),
                            pltpu.VMEM((DSA_TK, ATT_TQ), F32),
                            pltpu.VMEM((N_HEADS_B, 1, ATT_TQ), F32)],
        ),
        out_shape=jax.ShapeDtypeStruct((bsz, w, s), BF16),
        compiler_params=_cparams(("arbitrary", "arbitrary")),
        name="dsa_attn",
    )(qs, ks, scores, thr, main_b, main_b, vt, bias, logit_bound)
    return jnp.swapaxes(out_t, 1, 2).reshape(bsz * s, w)


def _out_kernel(x_ref, ma_ref, mb_ref, wa_ref, wb_ref, o_ref):
    o_ref[...] = (x_ref[...]
                  + jnp.dot(ma_ref[...], wa_ref[...], preferred_element_type=F32)
                  + jnp.dot(mb_ref[...], wb_ref[...], preferred_element_type=F32))


def _out_proj(x2, mix_a, mix_b, w_out, tm=512):
    m, d = x2.shape
    row = lambda width: pl.BlockSpec((tm, width), lambda i: (i, 0))
    return pl.pallas_call(
        _out_kernel,
        grid=(m // tm,),
        in_specs=[row(d), row(WIDTH_A), row(WIDTH_B),
                  pl.BlockSpec((WIDTH_A, d), lambda i: (0, 0)),
                  pl.BlockSpec((WIDTH_B, d), lambda i: (1, 0))],
        out_specs=row(d),
        out_shape=jax.ShapeDtypeStruct((m, d), F32),
        compiler_params=_cparams(("parallel",)),
        name="out_proj",
    )(x2, mix_a, mix_b, w_out, w_out)


def kernel(x, rel_bias, norm_ffn1, w_ffn1_in, w_ffn1_out, norm_mix, w_in, q_norm_a, k_norm_a,
           q_norm_b, k_norm_b, w_out, norm_ffn2, w_ffn2_in, w_ffn2_out):
    bsz, s, d = x.shape
    m = bsz * s
    depth = norm_ffn1.shape[0]
    scale = HEAD_DIM ** -0.5
    n_main = PROJ_NBLK * PROJ_BLK
    dil_bias, dsa_bias = _bias_tiles(rel_bias)
    x2 = x.reshape(m, d)
    ones = jnp.ones((PROJ_BLK,), F32)
    tile8 = lambda g: jnp.tile(g, PROJ_BLK // HEAD_DIM)
    for l in range(depth):
        x2 = _ffn(x2, norm_ffn1[l], w_ffn1_in[l].astype(BF16), w_ffn1_out[l].astype(BF16))

        w_main = w_in[l, :, :n_main].astype(BF16)
        w_idx = jnp.pad(w_in[l, :, n_main:], ((0, 0), (0, IDX_PAD - IDX_DIM - N_IDX_HEADS))).astype(BF16)
        head_gain = jnp.stack([tile8(q_norm_a[l] * scale), tile8(k_norm_a[l]), ones,
                               tile8(q_norm_b[l] * (scale * LOG2E)), tile8(k_norm_b[l]), ones, ones])
        main_a, main_b, idx = _proj(x2, norm_mix[l], w_main, w_idx,
                                    head_gain.reshape(PROJ_NBLK, 1, PROJ_BLK))
        main_a = main_a.reshape(bsz, s, PROJ_NA * PROJ_BLK)
        main_b = main_b.reshape(bsz, s, (PROJ_NBLK - PROJ_NA) * PROJ_BLK)
        idx = idx.reshape(bsz, s, IDX_PAD)
        ki = idx[:, :, :IDX_DIM].astype(BF16)
        wi_t = jnp.swapaxes(idx[:, :, IDX_DIM:IDX_DIM + N_IDX_HEADS], 1, 2)

        mix_a = _dilated(main_a, dil_bias)
        logit_bound = (1.02 * HEAD_DIM * jnp.max(jnp.abs(q_norm_b[l] * (scale * LOG2E)))
                       * jnp.max(jnp.abs(k_norm_b[l]))
                       + LOG2E * jnp.max(jnp.abs(rel_bias[:, N_HEADS_A:]))).reshape(1)
        mix_b = _dsa(main_b, ki, wi_t, dsa_bias, logit_bound)

        x2 = _out_proj(x2, mix_a, mix_b, w_out[l].astype(BF16))
        x2 = _ffn(x2, norm_ffn2[l], w_ffn2_in[l].astype(BF16), w_ffn2_out[l].astype(BF16))
    return x2.reshape(bsz, s, d)
```

```python
import functools
import math

import jax
import jax.numpy as jnp
import numpy as np
from jax import lax
from jax.experimental import pallas as pl
from jax.experimental.pallas import tpu as pltpu

F32 = jnp.float32
BF16 = jnp.bfloat16
I32 = jnp.int32

HEAD_DIM = 128
N_HEADS_A = 8
N_HEADS_B = 8
WIDTH_A = N_HEADS_A * HEAD_DIM
WIDTH_B = N_HEADS_B * HEAD_DIM
DILATIONS = (1, 4, 16)
DIL_BLK = 128
N_IDX_HEADS = 16
IDX_DIM = 64
TOPK_MAX = 256
N_BUCKETS = 32
REL_MAX_DIST = 2048
EPS = 1e-6
IDX_PAD = 128
IDX_SCALE = (IDX_DIM ** -0.5) * (N_IDX_HEADS ** -0.5)
LOG2E = math.log2(math.e)

VMEM_LIMIT = 56 * 1024 * 1024

DSA_TQ = 256
DSA_TK = 512
DSA_HALVES = DSA_TK // DSA_TQ
ATT_TQ = 512
ATT_QT = ATT_TQ // DSA_TQ
WORD = 32
SC_TK = 1024
SC_PER_ATT = SC_TK // DSA_TK
CHUNK_WORDS = SC_TK // WORD
SCORE_ROWS = 64
NEG_BIG = -1e30
DSA_SAFE_BOUND = 40.0
DSA_LOOKAHEAD = 2
INT_MIN = -2 ** 31


def _near_tiles():
    exact = N_BUCKETS // 2
    j = 1
    while True:
        d = j * DSA_TQ - (DSA_TQ - 1)
        b = exact + int(math.log(d / exact) / math.log(REL_MAX_DIST / exact) * (N_BUCKETS - exact) - 0.02)
        if d >= exact and b >= N_BUCKETS - 1:
            return j
        j += 1


DSA_NEAR = _near_tiles()


def _cparams(sem):
    return pltpu.CompilerParams(dimension_semantics=sem, vmem_limit_bytes=VMEM_LIMIT)


def _rel_bucket(dist):
    exact = N_BUCKETS // 2
    df = jnp.maximum(dist, 1).astype(F32)
    large = exact + (jnp.log(df / exact) / math.log(REL_MAX_DIST / exact)
                     * (N_BUCKETS - exact)).astype(I32)
    large = jnp.minimum(large, N_BUCKETS - 1)
    return jnp.where(dist < exact, dist, large)


def _lookup(tab_ref, bucket, col):
    val = jnp.zeros(bucket.shape, F32)
    for b in range(N_BUCKETS):
        val = jnp.where(bucket == b, tab_ref[b, col], val)
    return val


def _dil_bias_kernel(tab_ref, o_ref):
    dil = jnp.left_shift(1, 2 * pl.program_id(0))
    i = lax.broadcasted_iota(I32, (DIL_BLK, 2 * DIL_BLK), 0)
    j = lax.broadcasted_iota(I32, (DIL_BLK, 2 * DIL_BLK), 1)
    bucket = _rel_bucket(jnp.clip(DIL_BLK + i - j, 0, DIL_BLK) * dil)
    for h in range(N_HEADS_A):
        o_ref[0, h] = _lookup(tab_ref, bucket, h)


def _dsa_bias_kernel(tab_ref, o_ref):
    jj = pl.program_id(0)
    s = lax.broadcasted_iota(I32, (DSA_TQ, DSA_TQ), 0)
    t = lax.broadcasted_iota(I32, (DSA_TQ, DSA_TQ), 1)
    bucket = _rel_bucket(jnp.maximum((jj - 1) * DSA_TQ + t - s, 0))
    for h in range(N_HEADS_B):
        o_ref[h, 0] = _lookup(tab_ref, bucket, N_HEADS_A + h) * LOG2E


def _bias_tiles(rel_bias):
    smem = pl.BlockSpec(memory_space=pltpu.SMEM)
    dil = pl.pallas_call(
        _dil_bias_kernel,
        grid=(len(DILATIONS),),
        in_specs=[smem],
        out_specs=pl.BlockSpec((1, N_HEADS_A, DIL_BLK, 2 * DIL_BLK), lambda g: (g, 0, 0, 0)),
        out_shape=jax.ShapeDtypeStruct((len(DILATIONS), N_HEADS_A, DIL_BLK, 2 * DIL_BLK), F32),
        name="dil_bias",
    )(rel_bias)
    ntile = DSA_NEAR + 2
    dsa = pl.pallas_call(
        _dsa_bias_kernel,
        grid=(ntile,),
        in_specs=[smem],
        out_specs=pl.BlockSpec((N_HEADS_B, 1, DSA_TQ, DSA_TQ), lambda g: (0, g, 0, 0)),
        out_shape=jax.ShapeDtypeStruct((N_HEADS_B, ntile, DSA_TQ, DSA_TQ), F32),
        name="dsa_bias",
    )(rel_bias)
    return dil, dsa


def _ffn_kernel(x_ref, g_ref, wg_ref, wu_ref, wo_ref, o_ref, h_ref, acc_ref):
    f = pl.program_id(1)

    @pl.when(f == 0)
    def _():
        x = x_ref[...]
        ms = jnp.mean(x * x, axis=-1, keepdims=True)
        h_ref[...] = (x * lax.rsqrt(ms + EPS) * g_ref[...]).astype(BF16)
        acc_ref[...] = jnp.zeros_like(acc_ref)

    h = h_ref[...]
    gate = jnp.dot(h, wg_ref[...], preferred_element_type=F32)
    up = jnp.dot(h, wu_ref[...], preferred_element_type=F32)
    act = (gate * jax.nn.sigmoid(gate) * up).astype(BF16)
    acc_ref[...] += jnp.dot(act, wo_ref[...], preferred_element_type=F32)

    @pl.when(f == pl.num_programs(1) - 1)
    def _():
        o_ref[...] = x_ref[...] + 0.5 * acc_ref[...]


def _ffn(x2, gain, w_in, w_out, tm=512, tf=512):
    m, d = x2.shape
    d_ff = w_out.shape[0]
    nf = d_ff // tf
    chunk = lambda i, f: jnp.where(i % 2 == 0, f, nf - 1 - f)
    return pl.pallas_call(
        _ffn_kernel,
        grid=(m // tm, nf),
        in_specs=[
            pl.BlockSpec((tm, d), lambda i, f: (i, 0)),
            pl.BlockSpec((1, d), lambda i, f: (0, 0)),
            pl.BlockSpec((d, tf), lambda i, f: (0, chunk(i, f))),
            pl.BlockSpec((d, tf), lambda i, f: (0, chunk(i, f) + nf)),
            pl.BlockSpec((tf, d), lambda i, f: (chunk(i, f), 0)),
        ],
        out_specs=pl.BlockSpec((tm, d), lambda i, f: (i, 0)),
        out_shape=jax.ShapeDtypeStruct((m, d), F32),
        scratch_shapes=[pltpu.VMEM((tm, d), BF16), pltpu.VMEM((tm, d), F32)],
        compiler_params=_cparams(("parallel", "arbitrary")),
        name="ffn",
    )(x2, gain.reshape(1, d), w_in, w_in, w_out)


PROJ_BLK = 1024
PROJ_NBLK = 7
PROJ_NA = 3
_NORM_BLOCKS = (0, 1, 3, 4)


def _proj_block(i, step):
    return jnp.where(i % 2 == 0, step, PROJ_NBLK - 1 - step)


def _proj_kernel(x_ref, g_ref, w_ref, widx_ref, hg_ref, oa_ref, ob_ref, idx_ref, h_ref):
    j = _proj_block(pl.program_id(0), pl.program_id(1))

    @pl.when(pl.program_id(1) == 0)
    def _():
        x = x_ref[...]
        ms = jnp.mean(x * x, axis=-1, keepdims=True)
        h_ref[...] = (x * lax.rsqrt(ms + EPS) * g_ref[...]).astype(BF16)

    h = h_ref[...]
    y = jnp.dot(h, w_ref[...], preferred_element_type=F32)
    is_norm = functools.reduce(jnp.logical_or, [j == b for b in _NORM_BLOCKS])

    def emit(dst_ref):
        @pl.when(is_norm)
        def _():
            for hd in range(PROJ_BLK // HEAD_DIM):
                sl = slice(hd * HEAD_DIM, (hd + 1) * HEAD_DIM)
                yh = y[:, sl]
                ms = jnp.mean(yh * yh, axis=-1, keepdims=True)
                dst_ref[:, sl] = (yh * lax.rsqrt(ms + EPS) * hg_ref[0, :, sl]).astype(BF16)

        @pl.when(jnp.logical_not(is_norm))
        def _():
            dst_ref[...] = y.astype(BF16)

    pl.when(j < PROJ_NA)(lambda: emit(oa_ref))
    pl.when(j >= PROJ_NA)(lambda: emit(ob_ref))

    @pl.when(j == PROJ_NBLK - 1)
    def _():
        idx_ref[...] = jnp.dot(h, widx_ref[...], preferred_element_type=F32)


def _proj(x2, gain, w_main, w_idx, head_gain, tm=1024):
    m, d = x2.shape
    return pl.pallas_call(
        _proj_kernel,
        grid=(m // tm, PROJ_NBLK),
        in_specs=[
            pl.BlockSpec((tm, d), lambda i, j: (i, 0)),
            pl.BlockSpec((1, d), lambda i, j: (0, 0)),
            pl.BlockSpec((d, PROJ_BLK), lambda i, j: (0, _proj_block(i, j))),
            pl.BlockSpec((d, IDX_PAD), lambda i, j: (0, 0)),
            pl.BlockSpec((1, 1, PROJ_BLK), lambda i, j: (_proj_block(i, j), 0, 0)),
        ],
        out_specs=[
            pl.BlockSpec((tm, PROJ_BLK), lambda i, j: (i, jnp.minimum(_proj_block(i, j), PROJ_NA - 1))),
            pl.BlockSpec((tm, PROJ_BLK), lambda i, j: (i, jnp.maximum(_proj_block(i, j) - PROJ_NA, 0))),
            pl.BlockSpec((tm, IDX_PAD), lambda i, j: (i, 0)),
        ],
        out_shape=[
            jax.ShapeDtypeStruct((m, PROJ_NA * PROJ_BLK), BF16),
            jax.ShapeDtypeStruct((m, (PROJ_NBLK - PROJ_NA) * PROJ_BLK), BF16),
            jax.ShapeDtypeStruct((m, IDX_PAD), F32),
        ],
        scratch_shapes=[pltpu.VMEM((tm, d), BF16)],
        compiler_params=_cparams(("parallel", "arbitrary")),
        name="proj",
    )(x2, gain.reshape(1, d), w_main, w_idx, head_gain)


_NT = (((1,), (1,)), ((), ()))


DIL_SPAN = DIL_BLK * max(DILATIONS)
DIL_G = 2
DIL_UNITS = DIL_SPAN // DIL_BLK
DIL_UNROLL = 8


def _dilated_kernel(q_ref, kp_ref, kc_ref, vp_ref, vc_ref, b_ref, o_ref, qf, kf, vf, ob, lse_ref):
    n = pl.program_id(1)
    h0 = pl.program_id(2) * DIL_G
    for g in range(DIL_G):
        sl = slice(g * HEAD_DIM, (g + 1) * HEAD_DIM)
        qf[g] = q_ref[0, :, sl].astype(F32)
        kf[g, :DIL_SPAN] = kp_ref[0, :, sl].astype(F32)
        kf[g, DIL_SPAN:] = kc_ref[0, :, sl].astype(F32)
        vf[g, :DIL_SPAN] = vp_ref[0, :, sl].astype(F32)
        vf[g, DIL_SPAN:] = vc_ref[0, :, sl].astype(F32)
    i = lax.broadcasted_iota(I32, (DIL_BLK, 2 * DIL_BLK), 0)
    j = lax.broadcasted_iota(I32, (DIL_BLK, 2 * DIL_BLK), 1)
    band = jnp.logical_and(j >= i, j <= i + DIL_BLK)
    first = j >= DIL_BLK

    for br, dil in enumerate(DILATIONS):
        span = DIL_BLK * dil

        def unit(idx, carry, br=br, dil=dil, span=span):
            blk = idx // dil
            base = blk * span + idx % dil
            valid = jnp.logical_and(band, jnp.logical_or(first, n * (DIL_SPAN // span) + blk > 0))
            qrows = pl.ds(base, DIL_BLK, stride=dil)
            krows = pl.ds(DIL_SPAN + base - span, 2 * DIL_BLK, stride=dil)
            for g in range(DIL_G):
                q = qf[g, qrows, :].astype(BF16)
                k = kf[g, krows, :].astype(BF16)
                v = vf[g, krows, :].astype(BF16)
                s = lax.dot_general(q, k, _NT, preferred_element_type=F32) + b_ref[br, h0 + g]
                s = jnp.where(valid, s, -jnp.inf)
                mx = jnp.max(s, axis=-1, keepdims=True)
                p = jnp.exp(s - mx)
                den = jnp.sum(p, axis=-1, keepdims=True)
                ob[br, g, qrows, :] = jnp.dot(p.astype(BF16), v, preferred_element_type=F32) / den
                lse_ref[br, g, qrows, :] = jnp.broadcast_to(mx + jnp.log(den), (DIL_BLK, HEAD_DIM))
            return carry

        lax.fori_loop(0, DIL_UNITS, unit, 0, unroll=DIL_UNROLL)

    for g in range(DIL_G):
        l1, l2, l3 = lse_ref[0, g], lse_ref[1, g], lse_ref[2, g]
        mx = jnp.maximum(jnp.maximum(l1, l2), l3)
        e1, e2, e3 = jnp.exp(l1 - mx), jnp.exp(l2 - mx), jnp.exp(l3 - mx)
        mix = (e1 * ob[0, g] + e2 * ob[1, g] + e3 * ob[2, g]) / (e1 + e2 + e3)
        o_ref[0, :, g * HEAD_DIM:(g + 1) * HEAD_DIM] = mix.astype(BF16)


def _dilated(main_a, bias):
    bsz, s, _ = main_a.shape
    gw = DIL_G * HEAD_DIM
    ng = WIDTH_A // gw
    cur = lambda c: pl.BlockSpec((1, DIL_SPAN, gw), lambda b, n, g: (b, n, c * ng + g))
    prev = lambda c: pl.BlockSpec((1, DIL_SPAN, gw), lambda b, n, g: (b, jnp.maximum(n - 1, 0), c * ng + g))
    out = pl.pallas_call(
        _dilated_kernel,
        grid=(bsz, s // DIL_SPAN, ng),
        in_specs=[cur(0), prev(1), cur(1), prev(2), cur(2),
                  pl.BlockSpec(bias.shape, lambda b, n, g: (0, 0, 0, 0), pipeline_mode=pl.Buffered(1))],
        out_specs=pl.BlockSpec((1, DIL_SPAN, gw), lambda b, n, g: (b, n, g)),
        out_shape=jax.ShapeDtypeStruct((bsz, s, WIDTH_A), BF16),
        scratch_shapes=[pltpu.VMEM((DIL_G, DIL_SPAN, HEAD_DIM), F32),
                        pltpu.VMEM((DIL_G, 2 * DIL_SPAN, HEAD_DIM), F32),
                        pltpu.VMEM((DIL_G, 2 * DIL_SPAN, HEAD_DIM), F32),
                        pltpu.VMEM((len(DILATIONS), DIL_G, DIL_SPAN, HEAD_DIM), F32),
                        pltpu.VMEM((len(DILATIONS), DIL_G, DIL_SPAN, HEAD_DIM), F32)],
        compiler_params=_cparams(("parallel", "parallel", "parallel")),
        name="dilated",
    )(main_a, main_a, main_a, main_a, main_a, bias)
    return out.reshape(bsz * s, WIDTH_A)


def _dsa_schedule(s, tq, tk):
    qs, ks = [], []
    for i in range(s // tq):
        last = (i * tq + tq - 1) // tk
        for k in range(last + 1):
            qs.append(i)
            ks.append(k)
    return np.asarray(qs, np.int32), np.asarray(ks, np.int32)


def _sortable(bits):
    return bits ^ (lax.shift_right_arithmetic(bits, 31) & 0x7FFFFFFF)


def _bit_transpose(words):
    a = list(words)
    mask, j = 0x0000FFFF, 16
    while j:
        k = 0
        while k < WORD:
            t = (a[k] ^ lax.shift_right_logical(a[k + j], j)) & mask
            a[k] = a[k] ^ t
            a[k + j] = a[k + j] ^ lax.shift_left(t, j)
            k = (k + j + 1) & ~j
        j >>= 1
        mask = (mask ^ (mask << j)) & 0xFFFFFFFF
    return a


def _dsa_score_kernel(qtab, ktab, qit_ref, ki_ref, w_ref, sc_ref, thr_ref,
                      planes_ref, alive_ref, stage_ref):
    step = pl.program_id(1)
    qb = qtab[step]
    kc = ktab[step]
    w = w_ref[0] * IDX_SCALE
    srow = lax.broadcasted_iota(I32, (SCORE_ROWS, DSA_TQ), 0)
    tcol = lax.broadcasted_iota(I32, (SCORE_ROWS, DSA_TQ), 1)
    row0 = pl.multiple_of(kc * CHUNK_WORDS, CHUNK_WORDS)
    group = WORD * 8
    for g in range(SC_TK // group):
        for r0 in range(g * group, (g + 1) * group, SCORE_ROWS):
            kidx = ki_ref[0, r0:r0 + SCORE_ROWS, :]
            acc = jnp.zeros((SCORE_ROWS, DSA_TQ), F32)
            for h in range(N_IDX_HEADS):
                d = jnp.dot(kidx, qit_ref[0, h * IDX_DIM:(h + 1) * IDX_DIM, :],
                            preferred_element_type=F32)
                acc = acc + jnp.maximum(d, 0.0) * w[h:h + 1, :]
            causal = srow - tcol <= qb * DSA_TQ - kc * SC_TK - r0
            acc = jnp.where(causal, acc, -jnp.inf)
            sc_ref[r0:r0 + SCORE_ROWS, :] = acc
            stage_ref[r0:r0 + SCORE_ROWS, :] = _sortable(lax.bitcast_convert_type(acc, I32)) ^ INT_MIN
        for lanes in (slice(c, c + 128) for c in range(0, DSA_TQ, 128)):
            planes = _bit_transpose([stage_ref[g * group + 8 * n:g * group + 8 * (n + 1), lanes]
                                     for n in range(WORD)])
            for n in range(WORD):
                planes_ref[n, pl.ds(pl.multiple_of(row0 + 8 * g, 8), 8), lanes] = planes[n]

    last = (qb * DSA_TQ + DSA_TQ - 1) // SC_TK

    @pl.when(kc == last)
    def _():
        nwords = alive_ref.shape[0]
        rb = min(128, nwords)
        nrows = (last + 1) * CHUNK_WORDS
        nblk = (nrows + rb - 1) // rb
        tpos = qb * DSA_TQ + lax.broadcasted_iota(I32, (1, DSA_TQ), 1)
        want0 = jnp.minimum(tpos + 1, TOPK_MAX)
        rows = lambda blk: pl.ds(pl.multiple_of(blk * rb, rb), rb)

        def hits(alive, n, blk):
            hit = lax.population_count(alive & planes_ref[n, rows(blk), :])
            return hit.reshape(rb // 8, 8, DSA_TQ).sum(axis=0)

        def init(blk, cnt):
            ridx = blk * rb + lax.broadcasted_iota(I32, (rb, DSA_TQ), 0)
            alive = jnp.where(ridx < nrows, -1, 0)
            alive_ref[rows(blk), :] = alive
            return cnt + hits(alive, 0, blk)
        zero = jnp.zeros((8, DSA_TQ), I32)
        cnt0 = lax.fori_loop(0, nblk, init, zero)

        def bit_step(n, carry):
            r, want, cnt8 = carry
            cnt = cnt8.sum(axis=0, keepdims=True)
            take = cnt >= want
            r = r | jnp.where(take, lax.shift_left(jnp.int32(1), 31 - n), 0)
            want = jnp.where(take, want, want - cnt)
            flip = jnp.where(take, 0, -1)
            nxt = jnp.minimum(n + 1, WORD - 1)

            def sweep(blk, acc):
                alive = alive_ref[rows(blk), :] & (planes_ref[n, rows(blk), :] ^ flip)
                alive_ref[rows(blk), :] = alive
                return acc + hits(alive, nxt, blk)
            return r, want, lax.fori_loop(0, nblk, sweep, zero)

        r, want, _ = lax.fori_loop(0, WORD, bit_step, (jnp.zeros((1, DSA_TQ), I32), want0, cnt0))
        thr_ref[0, 0:1, :] = lax.bitcast_convert_type(_sortable(r ^ INT_MIN), F32)
        thr_ref[0, 1:2, :] = jnp.full((1, DSA_TQ), nwords * WORD, F32)

        def live(blk, acc):
            hit = lax.population_count(alive_ref[rows(blk), :])
            return acc + hit.reshape(rb // 8, 8, DSA_TQ).sum(axis=0)
        tied = lax.fori_loop(0, nblk, live, zero).sum(axis=0, keepdims=True)

        @pl.when(jnp.max(tied - want) > 0)
        def _():
            blk_bits = max(1, (nwords // 8 - 1).bit_length())
            lane_patterns = (-0x10000, -0xFF0100, -0xF0F0F10, -0x33333334, -0x55555556)
            nsec = blk_bits + len(lane_patterns) + 3

            def sec_plane(k, blk):
                ridx = blk * rb + lax.broadcasted_iota(I32, (rb, 1), 0)
                if k < blk_bits:
                    bit = 3 + blk_bits - 1 - k
                elif k < blk_bits + len(lane_patterns):
                    return jnp.full((rb, 1), lane_patterns[k - blk_bits], I32)
                else:
                    bit = nsec - 1 - k
                return jnp.where((lax.shift_right_logical(ridx, bit) & 1) == 0, -1, 0)

            def sec_count(k):
                def body(blk, acc):
                    hit = lax.population_count(alive_ref[rows(blk), :] & sec_plane(k, blk))
                    return acc + hit.reshape(rb // 8, 8, DSA_TQ).sum(axis=0)
                return lax.fori_loop(0, nblk, body, zero).sum(axis=0, keepdims=True)

            left = want
            code = jnp.zeros((1, DSA_TQ), I32)
            for k in range(nsec):
                cnt = sec_count(k)
                take = cnt >= left
                code = code | jnp.where(take, 1 << (nsec - 1 - k), 0)
                left = jnp.where(take, left, left - cnt)
                flip = jnp.where(take, 0, -1)

                def update(blk, c, k=k, flip=flip):
                    alive_ref[rows(blk), :] = alive_ref[rows(blk), :] & (sec_plane(k, blk) ^ flip)
                    return c
                lax.fori_loop(0, nblk, update, 0)
            inv = ~code
            cutoff = ((lax.shift_right_logical(inv, 8) & ((1 << blk_bits) - 1)) * (8 * WORD)
                      + 8 * ((code >> 3) & (WORD - 1) ^ (WORD - 1)) + (inv & 7))
            thr_ref[0, 1:2, :] = cutoff.astype(F32)


def _dsa_attn_kernel(qtab, ktab, sc_ref, thr_ref, q_ref, k_ref, vt_ref, b_ref, bound_ref, o_ref,
                     acc_ref, m_ref, l_ref, s_ref, mask_ref, cm_ref):
    step = pl.program_id(1)
    qb = qtab[step]
    kc = ktab[step]

    @pl.when(kc == 0)
    def _():
        acc_ref[...] = jnp.zeros_like(acc_ref)
        m_ref[...] = jnp.full_like(m_ref, NEG_BIG)
        l_ref[...] = jnp.zeros_like(l_ref)

    bound = bound_ref[0]
    bounded = bound <= DSA_SAFE_BOUND

    def set_mask(kept):
        key = (kc * DSA_TK + lax.broadcasted_iota(I32, (DSA_TK, DSA_TQ), 0)).astype(F32)
        for b in range(ATT_QT):
            cols = slice(b * DSA_TQ, (b + 1) * DSA_TQ)
            thr = thr_ref[0, 0:1, cols]
            tied = jnp.logical_and(sc_ref[b] == thr, key <= thr_ref[0, 1:2, cols])
            mask_ref[:, cols] = jnp.where(jnp.logical_or(sc_ref[b] > thr, tied), kept, -jnp.inf)

    tiles = [[jnp.clip((qb * ATT_QT + b) - (kc * DSA_HALVES + a), -1, DSA_NEAR) + 1
              for b in range(ATT_QT)] for a in range(DSA_HALVES)]
    far = qb * ATT_QT - (kc * DSA_HALVES + DSA_HALVES - 1) >= DSA_NEAR
    far_bias = lambda h: b_ref[h, DSA_NEAR + 1, 0:1, 0:1]

    def logits(h):
        sl = slice(h * HEAD_DIM, (h + 1) * HEAD_DIM)
        s = lax.dot_general(k_ref[0, :, sl], q_ref[0, :, sl], _NT, preferred_element_type=F32)
        return s + mask_ref[...]

    def near_bias(h):
        return jnp.concatenate(
            [jnp.concatenate([b_ref[h, tiles[a][b]] for b in range(ATT_QT)], axis=1)
             for a in range(DSA_HALVES)], axis=0)

    def values(h):
        return vt_ref[h * HEAD_DIM:(h + 1) * HEAD_DIM, :]

    @pl.when(bounded)
    def _():
        def raw_logits(h):
            sl = slice(h * HEAD_DIM, (h + 1) * HEAD_DIM)
            s_ref[h] = lax.dot_general(k_ref[0, :, sl], q_ref[0, :, sl], _NT,
                                       preferred_element_type=F32)

        def heads(weights):
            for h in range(DSA_LOOKAHEAD):
                raw_logits(h)
            set_mask(-bound)
            for h in range(N_HEADS_B):
                if h + DSA_LOOKAHEAD < N_HEADS_B:
                    raw_logits(h + DSA_LOOKAHEAD)
                p, gain = weights(h)
                l_ref[h] += gain * jnp.sum(p, axis=0, keepdims=True)
                acc_ref[h] += gain * jnp.dot(values(h), p.astype(BF16), preferred_element_type=F32)

        @pl.when(far)
        def _():
            heads(lambda h: (jnp.exp2(s_ref[h] + mask_ref[...]), jnp.exp2(far_bias(h))))

        @pl.when(jnp.logical_not(far))
        def _():
            heads(lambda h: (jnp.exp2(s_ref[h] + mask_ref[...] + near_bias(h)), 1.0))

    @pl.when(jnp.logical_not(bounded))
    def _():
        set_mask(0.0)

        @pl.when(far)
        def _():
            for h in range(N_HEADS_B):
                s = logits(h)
                s_ref[h] = s
                cm_ref[h] = jnp.max(s, axis=0, keepdims=True) + far_bias(h)

        @pl.when(jnp.logical_not(far))
        def _():
            for h in range(N_HEADS_B):
                s = logits(h) + near_bias(h)
                s_ref[h] = s
                cm_ref[h] = jnp.max(s, axis=0, keepdims=True)

        for h in range(N_HEADS_B):
            m_old = m_ref[h]
            m_new = jnp.maximum(m_old, cm_ref[h])
            alpha = jnp.exp2(m_old - m_new)
            m_ref[h] = m_new
            shift = jnp.where(far, m_new - far_bias(h), m_new)
            p = jnp.exp2(s_ref[h] - shift)
            l_ref[h] = alpha * l_ref[h] + jnp.sum(p, axis=0, keepdims=True)
            pv = jnp.dot(values(h), p.astype(BF16), preferred_element_type=F32)
            acc_ref[h] = alpha * acc_ref[h] + pv

    last = (qb * ATT_TQ + ATT_TQ - 1) // DSA_TK

    @pl.when(kc == last)
    def _():
        for h in range(N_HEADS_B):
            o_ref[0, h * HEAD_DIM:(h + 1) * HEAD_DIM, :] = (acc_ref[h] / l_ref[h]).astype(BF16)


def _dsa_scores(qi_t, ki, wi_t):
    bsz, _, s = qi_t.shape
    qs, ks = _dsa_schedule(s, DSA_TQ, SC_TK)
    npairs = len(qs)
    return pl.pallas_call(
        _dsa_score_kernel,
        grid_spec=pltpu.PrefetchScalarGridSpec(
            num_scalar_prefetch=2,
            grid=(bsz, npairs),
            in_specs=[
                pl.BlockSpec((1, N_IDX_HEADS * IDX_DIM, DSA_TQ), lambda b, t, qt, kt: (b, 0, qt[t])),
                pl.BlockSpec((1, SC_TK, IDX_DIM), lambda b, t, qt, kt: (b, kt[t], 0)),
                pl.BlockSpec((1, N_IDX_HEADS, DSA_TQ), lambda b, t, qt, kt: (b, 0, qt[t])),
            ],
            out_specs=[
                pl.BlockSpec((None, None, None, SC_TK, DSA_TQ),
                             lambda b, t, qt, kt: (b, kt[t], qt[t], 0, 0)),
                pl.BlockSpec((1, 2, DSA_TQ), lambda b, t, qt, kt: (b, 0, qt[t])),
            ],
            scratch_shapes=[pltpu.VMEM((WORD, s // WORD, DSA_TQ), I32),
                            pltpu.VMEM((s // WORD, DSA_TQ), I32),
                            pltpu.VMEM((SC_TK, DSA_TQ), I32)],
        ),
        out_shape=[jax.ShapeDtypeStruct((bsz, s // SC_TK, s // DSA_TQ, SC_TK, DSA_TQ), F32),
                   jax.ShapeDtypeStruct((bsz, 2, s), F32)],
        compiler_params=_cparams(("arbitrary", "arbitrary")),
        name="dsa_scores",
    )(qs, ks, qi_t, ki, wi_t)


def _dsa(main_b, ki, wi_t, bias, logit_bound):
    bsz, s, _ = main_b.shape
    w = WIDTH_B
    scores, thr = _dsa_scores(jnp.swapaxes(main_b[:, :, 3 * PROJ_BLK:4 * PROJ_BLK], 1, 2), ki, wi_t)

    vt = jnp.swapaxes(main_b[:, :, 2 * PROJ_BLK:3 * PROJ_BLK].reshape(bsz, s // DSA_TK, DSA_TK, w), 2, 3)
    ntile = bias.shape[1]
    qs, ks = _dsa_schedule(s, ATT_TQ, DSA_TK)
    out_t = pl.pallas_call(
        _dsa_attn_kernel,
        grid_spec=pltpu.PrefetchScalarGridSpec(
            num_scalar_prefetch=2,
            grid=(bsz, len(qs)),
            in_specs=[
                pl.BlockSpec((None, None, ATT_QT, DSA_TK, DSA_TQ),
                             lambda b, t, qt, kt: (b, kt[t] // SC_PER_ATT, qt[t], kt[t] % SC_PER_ATT, 0)),
                pl.BlockSpec((1, 2, ATT_TQ), lambda b, t, qt, kt: (b, 0, qt[t])),
                pl.BlockSpec((1, ATT_TQ, w), lambda b, t, qt, kt: (b, qt[t], 0)),
                pl.BlockSpec((1, DSA_TK, w), lambda b, t, qt, kt: (b, kt[t], 1)),
                pl.BlockSpec((None, None, w, DSA_TK), lambda b, t, qt, kt: (b, kt[t], 0, 0)),
                pl.BlockSpec((N_HEADS_B, ntile, DSA_TQ, DSA_TQ), lambda b, t, qt, kt: (0, 0, 0, 0),
                             pipeline_mode=pl.Buffered(1)),
                pl.BlockSpec(memory_space=pltpu.SMEM),
            ],
            out_specs=pl.BlockSpec((1, w, ATT_TQ), lambda b, t, qt, kt: (b, 0, qt[t])),
            scratch_shapes=[pltpu.VMEM((N_HEADS_B, HEAD_DIM, ATT_TQ), F32),
                            pltpu.VMEM((N_HEADS_B, 1, ATT_TQ), F32),
                            pltpu.VMEM((N_HEADS_B, 1, ATT_TQ), F32),
                            pltpu.VMEM((N_HEADS_B, DSA_TK, ATT_TQ), F32),
                            pltpu.VMEM((DSA_TK, ATT_TQ), F32),
                            pltpu.VMEM((N_HEADS_B, 1, ATT_TQ), F32)],
        ),
        out_shape=jax.ShapeDtypeStruct((bsz, w, s), BF16),
        compiler_params=_cparams(("arbitrary", "arbitrary")),
        name="dsa_attn",
    )(qs, ks, scores, thr, main_b, main_b, vt, bias, logit_bound)
    return jnp.swapaxes(out_t, 1, 2).reshape(bsz * s, w)


def _out_kernel(x_ref, ma_ref, mb_ref, wa_ref, wb_ref, o_ref):
    o_ref[...] = (x_ref[...]
                  + jnp.dot(ma_ref[...], wa_ref[...], preferred_element_type=F32)
                  + jnp.dot(mb_ref[...], wb_ref[...], preferred_element_type=F32))


def _out_proj(x2, mix_a, mix_b, w_out, tm=512):
    m, d = x2.shape
    row = lambda width: pl.BlockSpec((tm, width), lambda i: (i, 0))
    return pl.pallas_call(
        _out_kernel,
        grid=(m // tm,),
        in_specs=[row(d), row(WIDTH_A), row(WIDTH_B),
                  pl.BlockSpec((WIDTH_A, d), lambda i: (0, 0)),
                  pl.BlockSpec((WIDTH_B, d), lambda i: (1, 0))],
        out_specs=row(d),
        out_shape=jax.ShapeDtypeStruct((m, d), F32),
        compiler_params=_cparams(("parallel",)),
        name="out_proj",
    )(x2, mix_a, mix_b, w_out, w_out)


def kernel(x, rel_bias, norm_ffn1, w_ffn1_in, w_ffn1_out, norm_mix, w_in, q_norm_a, k_norm_a,
           q_norm_b, k_norm_b, w_out, norm_ffn2, w_ffn2_in, w_ffn2_out):
    bsz, s, d = x.shape
    m = bsz * s
    depth = norm_ffn1.shape[0]
    scale = HEAD_DIM ** -0.5
    n_main = PROJ_NBLK * PROJ_BLK
    dil_bias, dsa_bias = _bias_tiles(rel_bias)
    x2 = x.reshape(m, d)
    ones = jnp.ones((PROJ_BLK,), F32)
    tile8 = lambda g: jnp.tile(g, PROJ_BLK // HEAD_DIM)
    for l in range(depth):
        x2 = _ffn(x2, norm_ffn1[l], w_ffn1_in[l].astype(BF16), w_ffn1_out[l].astype(BF16))

        w_main = w_in[l, :, :n_main].astype(BF16)
        w_idx = jnp.pad(w_in[l, :, n_main:], ((0, 0), (0, IDX_PAD - IDX_DIM - N_IDX_HEADS))).astype(BF16)
        head_gain = jnp.stack([tile8(q_norm_a[l] * scale), tile8(k_norm_a[l]), ones,
                               tile8(q_norm_b[l] * (scale * LOG2E)), tile8(k_norm_b[l]), ones, ones])
        main_a, main_b, idx = _proj(x2, norm_mix[l], w_main, w_idx,
                                    head_gain.reshape(PROJ_NBLK, 1, PROJ_BLK))
        main_a = main_a.reshape(bsz, s, PROJ_NA * PROJ_BLK)
        main_b = main_b.reshape(bsz, s, (PROJ_NBLK - PROJ_NA) * PROJ_BLK)
        idx = idx.reshape(bsz, s, IDX_PAD)
        ki = idx[:, :, :IDX_DIM].astype(BF16)
        wi_t = jnp.swapaxes(idx[:, :, IDX_DIM:IDX_DIM + N_IDX_HEADS], 1, 2)

        mix_a = _dilated(main_a, dil_bias)
        logit_bound = (1.02 * HEAD_DIM * jnp.max(jnp.abs(q_norm_b[l] * (scale * LOG2E)))
                       * jnp.max(jnp.abs(k_norm_b[l]))
                       + LOG2E * jnp.max(jnp.abs(rel_bias[:, N_HEADS_A:]))).reshape(1)
        mix_b = _dsa(main_b, ki, wi_t, dsa_bias, logit_bound)

        x2 = _out_proj(x2, mix_a, mix_b, w_out[l].astype(BF16))
        x2 = _ffn(x2, norm_ffn2[l], w_ffn2_in[l].astype(BF16), w_ffn2_out[l].astype(BF16))
    return x2.reshape(bsz, s, d)
```

```python
import functools
import math

import jax
import jax.numpy as jnp
import numpy as np
from jax import lax
from jax.experimental import pallas as pl
from jax.experimental.pallas import tpu as pltpu

F32 = jnp.float32
BF16 = jnp.bfloat16
I32 = jnp.int32

HEAD_DIM = 128
N_HEADS_A = 8
N_HEADS_B = 8
WIDTH_A = N_HEADS_A * HEAD_DIM
WIDTH_B = N_HEADS_B * HEAD_DIM
DILATIONS = (1, 4, 16)
DIL_BLK = 128
N_IDX_HEADS = 16
IDX_DIM = 64
TOPK_MAX = 256
N_BUCKETS = 32
REL_MAX_DIST = 2048
EPS = 1e-6
IDX_PAD = 128
IDX_SCALE = (IDX_DIM ** -0.5) * (N_IDX_HEADS ** -0.5)
LOG2E = math.log2(math.e)

VMEM_LIMIT = 56 * 1024 * 1024

DSA_TQ = 256
DSA_TK = 512
DSA_HALVES = DSA_TK // DSA_TQ
ATT_TQ = 512
ATT_QT = ATT_TQ // DSA_TQ
WORD = 32
SC_TK = 1024
SC_PER_ATT = SC_TK // DSA_TK
CHUNK_WORDS = SC_TK // WORD
SCORE_ROWS = 64
NEG_BIG = -1e30
DSA_SAFE_BOUND = 40.0
DSA_LOOKAHEAD = 2
INT_MIN = -2 ** 31


def _near_tiles():
    exact = N_BUCKETS // 2
    j = 1
    while True:
        d = j * DSA_TQ - (DSA_TQ - 1)
        b = exact + int(math.log(d / exact) / math.log(REL_MAX_DIST / exact) * (N_BUCKETS - exact) - 0.02)
        if d >= exact and b >= N_BUCKETS - 1:
            return j
        j += 1


DSA_NEAR = _near_tiles()


def _cparams(sem):
    return pltpu.CompilerParams(dimension_semantics=sem, vmem_limit_bytes=VMEM_LIMIT)


def _rel_bucket(dist):
    exact = N_BUCKETS // 2
    df = jnp.maximum(dist, 1).astype(F32)
    large = exact + (jnp.log(df / exact) / math.log(REL_MAX_DIST / exact)
                     * (N_BUCKETS - exact)).astype(I32)
    large = jnp.minimum(large, N_BUCKETS - 1)
    return jnp.where(dist < exact, dist, large)


def _lookup(tab_ref, bucket, col):
    val = jnp.zeros(bucket.shape, F32)
    for b in range(N_BUCKETS):
        val = jnp.where(bucket == b, tab_ref[b, col], val)
    return val


def _dil_bias_kernel(tab_ref, o_ref):
    dil = jnp.left_shift(1, 2 * pl.program_id(0))
    i = lax.broadcasted_iota(I32, (DIL_BLK, 2 * DIL_BLK), 0)
    j = lax.broadcasted_iota(I32, (DIL_BLK, 2 * DIL_BLK), 1)
    bucket = _rel_bucket(jnp.clip(DIL_BLK + i - j, 0, DIL_BLK) * dil)
    for h in range(N_HEADS_A):
        o_ref[0, h] = _lookup(tab_ref, bucket, h)


def _dsa_bias_kernel(tab_ref, o_ref):
    jj = pl.program_id(0)
    s = lax.broadcasted_iota(I32, (DSA_TQ, DSA_TQ), 0)
    t = lax.broadcasted_iota(I32, (DSA_TQ, DSA_TQ), 1)
    bucket = _rel_bucket(jnp.maximum((jj - 1) * DSA_TQ + t - s, 0))
    for h in range(N_HEADS_B):
        o_ref[h, 0] = _lookup(tab_ref, bucket, N_HEADS_A + h) * LOG2E


def _bias_tiles(rel_bias):
    smem = pl.BlockSpec(memory_space=pltpu.SMEM)
    dil = pl.pallas_call(
        _dil_bias_kernel,
        grid=(len(DILATIONS),),
        in_specs=[smem],
        out_specs=pl.BlockSpec((1, N_HEADS_A, DIL_BLK, 2 * DIL_BLK), lambda g: (g, 0, 0, 0)),
        out_shape=jax.ShapeDtypeStruct((len(DILATIONS), N_HEADS_A, DIL_BLK, 2 * DIL_BLK), F32),
        name="dil_bias",
    )(rel_bias)
    ntile = DSA_NEAR + 2
    dsa = pl.pallas_call(
        _dsa_bias_kernel,
        grid=(ntile,),
        in_specs=[smem],
        out_specs=pl.BlockSpec((N_HEADS_B, 1, DSA_TQ, DSA_TQ), lambda g: (0, g, 0, 0)),
        out_shape=jax.ShapeDtypeStruct((N_HEADS_B, ntile, DSA_TQ, DSA_TQ), F32),
        name="dsa_bias",
    )(rel_bias)
    return dil, dsa


def _ffn_kernel(x_ref, g_ref, wg_ref, wu_ref, wo_ref, o_ref, h_ref, acc_ref):
    f = pl.program_id(1)

    @pl.when(f == 0)
    def _():
        x = x_ref[...]
        ms = jnp.mean(x * x, axis=-1, keepdims=True)
        h_ref[...] = (x * lax.rsqrt(ms + EPS) * g_ref[...]).astype(BF16)
        acc_ref[...] = jnp.zeros_like(acc_ref)

    h = h_ref[...]
    gate = jnp.dot(h, wg_ref[...], preferred_element_type=F32)
    up = jnp.dot(h, wu_ref[...], preferred_element_type=F32)
    act = (gate * jax.nn.sigmoid(gate) * up).astype(BF16)
    acc_ref[...] += jnp.dot(act, wo_ref[...], preferred_element_type=F32)

    @pl.when(f == pl.num_programs(1) - 1)
    def _():
        o_ref[...] = x_ref[...] + 0.5 * acc_ref[...]


def _ffn(x2, gain, w_in, w_out, tm=512, tf=512):
    m, d = x2.shape
    d_ff = w_out.shape[0]
    nf = d_ff // tf
    return pl.pallas_call(
        _ffn_kernel,
        grid=(m // tm, nf),
        in_specs=[
            pl.BlockSpec((tm, d), lambda i, f: (i, 0)),
            pl.BlockSpec((1, d), lambda i, f: (0, 0)),
            pl.BlockSpec((d, tf), lambda i, f: (0, f)),
            pl.BlockSpec((d, tf), lambda i, f: (0, f + nf)),
            pl.BlockSpec((tf, d), lambda i, f: (f, 0)),
        ],
        out_specs=pl.BlockSpec((tm, d), lambda i, f: (i, 0)),
        out_shape=jax.ShapeDtypeStruct((m, d), F32),
        scratch_shapes=[pltpu.VMEM((tm, d), BF16), pltpu.VMEM((tm, d), F32)],
        compiler_params=_cparams(("parallel", "arbitrary")),
        name="ffn",
    )(x2, gain.reshape(1, d), w_in, w_in, w_out)


PROJ_BLK = 1024
PROJ_NBLK = 7
PROJ_B0 = 3
PROJ_PARTS = 4
_NORM_BLOCKS = (0, 1, 3, 4)


def _proj_kernel(x_ref, g_ref, w_ref, widx_ref, hg_ref, o_ref, idx_ref, h_ref):
    j = pl.program_id(1)

    @pl.when(j == 0)
    def _():
        x = x_ref[...]
        ms = jnp.mean(x * x, axis=-1, keepdims=True)
        h_ref[...] = (x * lax.rsqrt(ms + EPS) * g_ref[...]).astype(BF16)

    is_norm = functools.reduce(jnp.logical_or, [j == b for b in _NORM_BLOCKS])
    rows = h_ref.shape[0] // PROJ_PARTS
    for part in range(PROJ_PARTS):
        rs = slice(part * rows, (part + 1) * rows)
        y = jnp.dot(h_ref[rs, :], w_ref[...], preferred_element_type=F32)
        for hd in range(PROJ_BLK // HEAD_DIM):
            sl = slice(hd * HEAD_DIM, (hd + 1) * HEAD_DIM)
            yh = y[:, sl]
            ms = jnp.mean(yh * yh, axis=-1, keepdims=True)
            inv = jnp.where(is_norm, lax.rsqrt(ms + EPS), 1.0)
            o_ref[rs, sl] = (yh * inv * hg_ref[0, :, sl]).astype(BF16)

    @pl.when(j == PROJ_NBLK - 1)
    def _():
        idx_ref[...] = jnp.dot(h_ref[...], widx_ref[...], preferred_element_type=F32)


def _proj(x2, gain, w_main, w_idx, head_gain, tm=1024):
    m, d = x2.shape
    return pl.pallas_call(
        _proj_kernel,
        grid=(m // tm, PROJ_NBLK),
        in_specs=[
            pl.BlockSpec((tm, d), lambda i, j: (i, 0)),
            pl.BlockSpec((1, d), lambda i, j: (0, 0)),
            pl.BlockSpec((d, PROJ_BLK), lambda i, j: (0, j)),
            pl.BlockSpec((d, IDX_PAD), lambda i, j: (0, 0)),
            pl.BlockSpec((1, 1, PROJ_BLK), lambda i, j: (j, 0, 0)),
        ],
        out_specs=[
            pl.BlockSpec((tm, PROJ_BLK), lambda i, j: (i, j)),
            pl.BlockSpec((tm, IDX_PAD), lambda i, j: (i, 0)),
        ],
        out_shape=[
            jax.ShapeDtypeStruct((m, PROJ_NBLK * PROJ_BLK), BF16),
            jax.ShapeDtypeStruct((m, IDX_PAD), F32),
        ],
        scratch_shapes=[pltpu.VMEM((tm, d), BF16)],
        compiler_params=_cparams(("parallel", "arbitrary")),
        name="proj",
    )(x2, gain.reshape(1, d), w_main, w_idx, head_gain)


_NT = (((1,), (1,)), ((), ()))


DIL_SPAN = DIL_BLK * max(DILATIONS)
DIL_G = 2
DIL_UNITS = DIL_SPAN // DIL_BLK
DIL_UNROLL = 8


def _dilated_kernel(q_ref, kp_ref, kc_ref, vp_ref, vc_ref, b_ref, o_ref, qf, kf, vf, ob, lse_ref):
    n = pl.program_id(1)
    h0 = pl.program_id(2) * DIL_G
    for g in range(DIL_G):
        sl = slice(g * HEAD_DIM, (g + 1) * HEAD_DIM)
        qf[g] = q_ref[0, :, sl].astype(F32)
        kf[g, :DIL_SPAN] = kp_ref[0, :, sl].astype(F32)
        kf[g, DIL_SPAN:] = kc_ref[0, :, sl].astype(F32)
        vf[g, :DIL_SPAN] = vp_ref[0, :, sl].astype(F32)
        vf[g, DIL_SPAN:] = vc_ref[0, :, sl].astype(F32)
    i = lax.broadcasted_iota(I32, (DIL_BLK, 2 * DIL_BLK), 0)
    j = lax.broadcasted_iota(I32, (DIL_BLK, 2 * DIL_BLK), 1)
    band = jnp.logical_and(j >= i, j <= i + DIL_BLK)
    first = j >= DIL_BLK

    for br, dil in enumerate(DILATIONS):
        span = DIL_BLK * dil

        def unit(idx, carry, br=br, dil=dil, span=span):
            blk = idx // dil
            base = blk * span + idx % dil
            valid = jnp.logical_and(band, jnp.logical_or(first, n * (DIL_SPAN // span) + blk > 0))
            qrows = pl.ds(base, DIL_BLK, stride=dil)
            krows = pl.ds(DIL_SPAN + base - span, 2 * DIL_BLK, stride=dil)
            for g in range(DIL_G):
                q = qf[g, qrows, :].astype(BF16)
                k = kf[g, krows, :].astype(BF16)
                v = vf[g, krows, :].astype(BF16)
                s = lax.dot_general(q, k, _NT, preferred_element_type=F32) + b_ref[br, h0 + g]
                s = jnp.where(valid, s, -jnp.inf)
                mx = jnp.max(s, axis=-1, keepdims=True)
                p = jnp.exp(s - mx)
                den = jnp.sum(p, axis=-1, keepdims=True)
                ob[br, g, qrows, :] = jnp.dot(p.astype(BF16), v, preferred_element_type=F32) / den
                lse_ref[br, g, qrows, :] = jnp.broadcast_to(mx + jnp.log(den), (DIL_BLK, HEAD_DIM))
            return carry

        lax.fori_loop(0, DIL_UNITS, unit, 0, unroll=DIL_UNROLL)

    for g in range(DIL_G):
        l1, l2, l3 = lse_ref[0, g], lse_ref[1, g], lse_ref[2, g]
        mx = jnp.maximum(jnp.maximum(l1, l2), l3)
        e1, e2, e3 = jnp.exp(l1 - mx), jnp.exp(l2 - mx), jnp.exp(l3 - mx)
        mix = (e1 * ob[0, g] + e2 * ob[1, g] + e3 * ob[2, g]) / (e1 + e2 + e3)
        o_ref[0, :, g * HEAD_DIM:(g + 1) * HEAD_DIM] = mix.astype(BF16)


def _dilated(main, bias):
    bsz, s, _ = main.shape
    gw = DIL_G * HEAD_DIM
    ng = WIDTH_A // gw
    cur = lambda c: pl.BlockSpec((1, DIL_SPAN, gw), lambda b, n, g: (b, n, c * ng + g))
    prev = lambda c: pl.BlockSpec((1, DIL_SPAN, gw), lambda b, n, g: (b, jnp.maximum(n - 1, 0), c * ng + g))
    out = pl.pallas_call(
        _dilated_kernel,
        grid=(bsz, s // DIL_SPAN, ng),
        in_specs=[cur(0), prev(1), cur(1), prev(2), cur(2),
                  pl.BlockSpec(bias.shape, lambda b, n, g: (0, 0, 0, 0), pipeline_mode=pl.Buffered(1))],
        out_specs=pl.BlockSpec((1, DIL_SPAN, gw), lambda b, n, g: (b, n, g)),
        out_shape=jax.ShapeDtypeStruct((bsz, s, WIDTH_A), BF16),
        scratch_shapes=[pltpu.VMEM((DIL_G, DIL_SPAN, HEAD_DIM), F32),
                        pltpu.VMEM((DIL_G, 2 * DIL_SPAN, HEAD_DIM), F32),
                        pltpu.VMEM((DIL_G, 2 * DIL_SPAN, HEAD_DIM), F32),
                        pltpu.VMEM((len(DILATIONS), DIL_G, DIL_SPAN, HEAD_DIM), F32),
                        pltpu.VMEM((len(DILATIONS), DIL_G, DIL_SPAN, HEAD_DIM), F32)],
        compiler_params=_cparams(("parallel", "parallel", "parallel")),
        name="dilated",
    )(main, main, main, main, main, bias)
    return out.reshape(bsz * s, WIDTH_A)


def _dsa_schedule(s, tq, tk):
    qs, ks = [], []
    for i in range(s // tq):
        last = (i * tq + tq - 1) // tk
        for k in range(last + 1):
            qs.append(i)
            ks.append(k)
    return np.asarray(qs, np.int32), np.asarray(ks, np.int32)


def _sortable(bits):
    return bits ^ (lax.shift_right_arithmetic(bits, 31) & 0x7FFFFFFF)


def _bit_transpose(words):
    a = list(words)
    mask, j = 0x0000FFFF, 16
    while j:
        k = 0
        while k < WORD:
            t = (a[k] ^ lax.shift_right_logical(a[k + j], j)) & mask
            a[k] = a[k] ^ t
            a[k + j] = a[k + j] ^ lax.shift_left(t, j)
            k = (k + j + 1) & ~j
        j >>= 1
        mask = (mask ^ (mask << j)) & 0xFFFFFFFF
    return a


def _dsa_score_kernel(qtab, ktab, qit_ref, ki_ref, w_ref, sc_ref, thr_ref,
                      planes_ref, alive_ref, stage_ref):
    step = pl.program_id(1)
    qb = qtab[step]
    kc = ktab[step]
    w = w_ref[0] * IDX_SCALE
    srow = lax.broadcasted_iota(I32, (SCORE_ROWS, DSA_TQ), 0)
    tcol = lax.broadcasted_iota(I32, (SCORE_ROWS, DSA_TQ), 1)
    row0 = pl.multiple_of(kc * CHUNK_WORDS, CHUNK_WORDS)
    group = WORD * 8
    for g in range(SC_TK // group):
        for r0 in range(g * group, (g + 1) * group, SCORE_ROWS):
            kidx = ki_ref[0, r0:r0 + SCORE_ROWS, :]
            acc = jnp.zeros((SCORE_ROWS, DSA_TQ), F32)
            for h in range(N_IDX_HEADS):
                d = jnp.dot(kidx, qit_ref[0, h * IDX_DIM:(h + 1) * IDX_DIM, :],
                            preferred_element_type=F32)
                acc = acc + jnp.maximum(d, 0.0) * w[h:h + 1, :]
            causal = srow - tcol <= qb * DSA_TQ - kc * SC_TK - r0
            acc = jnp.where(causal, acc, -jnp.inf)
            sc_ref[r0:r0 + SCORE_ROWS, :] = acc
            stage_ref[r0:r0 + SCORE_ROWS, :] = _sortable(lax.bitcast_convert_type(acc, I32)) ^ INT_MIN
        for lanes in (slice(c, c + 128) for c in range(0, DSA_TQ, 128)):
            planes = _bit_transpose([stage_ref[g * group + 8 * n:g * group + 8 * (n + 1), lanes]
                                     for n in range(WORD)])
            for n in range(WORD):
                planes_ref[n, pl.ds(pl.multiple_of(row0 + 8 * g, 8), 8), lanes] = planes[n]

    last = (qb * DSA_TQ + DSA_TQ - 1) // SC_TK

    @pl.when(kc == last)
    def _():
        nwords = alive_ref.shape[0]
        rb = min(128, nwords)
        nrows = (last + 1) * CHUNK_WORDS
        nblk = (nrows + rb - 1) // rb
        tpos = qb * DSA_TQ + lax.broadcasted_iota(I32, (1, DSA_TQ), 1)
        want0 = jnp.minimum(tpos + 1, TOPK_MAX)
        rows = lambda blk: pl.ds(pl.multiple_of(blk * rb, rb), rb)

        def hits(alive, n, blk):
            hit = lax.population_count(alive & planes_ref[n, rows(blk), :])
            return hit.reshape(rb // 8, 8, DSA_TQ).sum(axis=0)

        def init(blk, cnt):
            ridx = blk * rb + lax.broadcasted_iota(I32, (rb, DSA_TQ), 0)
            alive = jnp.where(ridx < nrows, -1, 0)
            alive_ref[rows(blk), :] = alive
            return cnt + hits(alive, 0, blk)
        zero = jnp.zeros((8, DSA_TQ), I32)
        cnt0 = lax.fori_loop(0, nblk, init, zero)

        def bit_step(n, carry):
            r, want, cnt8 = carry
            cnt = cnt8.sum(axis=0, keepdims=True)
            take = cnt >= want
            r = r | jnp.where(take, lax.shift_left(jnp.int32(1), 31 - n), 0)
            want = jnp.where(take, want, want - cnt)
            flip = jnp.where(take, 0, -1)
            nxt = jnp.minimum(n + 1, WORD - 1)

            def sweep(blk, acc):
                alive = alive_ref[rows(blk), :] & (planes_ref[n, rows(blk), :] ^ flip)
                alive_ref[rows(blk), :] = alive
                return acc + hits(alive, nxt, blk)
            return r, want, lax.fori_loop(0, nblk, sweep, zero)

        r, want, _ = lax.fori_loop(0, WORD, bit_step, (jnp.zeros((1, DSA_TQ), I32), want0, cnt0))
        thr_ref[0, 0:1, :] = lax.bitcast_convert_type(_sortable(r ^ INT_MIN), F32)
        thr_ref[0, 1:2, :] = jnp.full((1, DSA_TQ), nwords * WORD, F32)

        def live(blk, acc):
            hit = lax.population_count(alive_ref[rows(blk), :])
            return acc + hit.reshape(rb // 8, 8, DSA_TQ).sum(axis=0)
        tied = lax.fori_loop(0, nblk, live, zero).sum(axis=0, keepdims=True)

        @pl.when(jnp.max(tied - want) > 0)
        def _():
            blk_bits = max(1, (nwords // 8 - 1).bit_length())
            lane_patterns = (-0x10000, -0xFF0100, -0xF0F0F10, -0x33333334, -0x55555556)
            nsec = blk_bits + len(lane_patterns) + 3

            def sec_plane(k, blk):
                ridx = blk * rb + lax.broadcasted_iota(I32, (rb, 1), 0)
                if k < blk_bits:
                    bit = 3 + blk_bits - 1 - k
                elif k < blk_bits + len(lane_patterns):
                    return jnp.full((rb, 1), lane_patterns[k - blk_bits], I32)
                else:
                    bit = nsec - 1 - k
                return jnp.where((lax.shift_right_logical(ridx, bit) & 1) == 0, -1, 0)

            def sec_count(k):
                def body(blk, acc):
                    hit = lax.population_count(alive_ref[rows(blk), :] & sec_plane(k, blk))
                    return acc + hit.reshape(rb // 8, 8, DSA_TQ).sum(axis=0)
                return lax.fori_loop(0, nblk, body, zero).sum(axis=0, keepdims=True)

            left = want
            code = jnp.zeros((1, DSA_TQ), I32)
            for k in range(nsec):
                cnt = sec_count(k)
                take = cnt >= left
                code = code | jnp.where(take, 1 << (nsec - 1 - k), 0)
                left = jnp.where(take, left, left - cnt)
                flip = jnp.where(take, 0, -1)

                def update(blk, c, k=k, flip=flip):
                    alive_ref[rows(blk), :] = alive_ref[rows(blk), :] & (sec_plane(k, blk) ^ flip)
                    return c
                lax.fori_loop(0, nblk, update, 0)
            inv = ~code
            cutoff = ((lax.shift_right_logical(inv, 8) & ((1 << blk_bits) - 1)) * (8 * WORD)
                      + 8 * ((code >> 3) & (WORD - 1) ^ (WORD - 1)) + (inv & 7))
            thr_ref[0, 1:2, :] = cutoff.astype(F32)


def _dsa_attn_kernel(qtab, ktab, sc_ref, thr_ref, q_ref, k_ref, vt_ref, b_ref, bound_ref, o_ref,
                     acc_ref, m_ref, l_ref, s_ref, mask_ref, cm_ref):
    step = pl.program_id(1)
    qb = qtab[step]
    kc = ktab[step]

    @pl.when(kc == 0)
    def _():
        acc_ref[...] = jnp.zeros_like(acc_ref)
        m_ref[...] = jnp.full_like(m_ref, NEG_BIG)
        l_ref[...] = jnp.zeros_like(l_ref)

    bound = bound_ref[0]
    bounded = bound <= DSA_SAFE_BOUND

    def set_mask(kept):
        key = (kc * DSA_TK + lax.broadcasted_iota(I32, (DSA_TK, DSA_TQ), 0)).astype(F32)
        for b in range(ATT_QT):
            cols = slice(b * DSA_TQ, (b + 1) * DSA_TQ)
            thr = thr_ref[0, 0:1, cols]
            tied = jnp.logical_and(sc_ref[b] == thr, key <= thr_ref[0, 1:2, cols])
            mask_ref[:, cols] = jnp.where(jnp.logical_or(sc_ref[b] > thr, tied), kept, -jnp.inf)

    tiles = [[jnp.clip((qb * ATT_QT + b) - (kc * DSA_HALVES + a), -1, DSA_NEAR) + 1
              for b in range(ATT_QT)] for a in range(DSA_HALVES)]
    far = qb * ATT_QT - (kc * DSA_HALVES + DSA_HALVES - 1) >= DSA_NEAR
    far_bias = lambda h: b_ref[h, DSA_NEAR + 1, 0:1, 0:1]

    def logits(h):
        sl = slice(h * HEAD_DIM, (h + 1) * HEAD_DIM)
        s = lax.dot_general(k_ref[0, :, sl], q_ref[0, :, sl], _NT, preferred_element_type=F32)
        return s + mask_ref[...]

    def near_bias(h):
        return jnp.concatenate(
            [jnp.concatenate([b_ref[h, tiles[a][b]] for b in range(ATT_QT)], axis=1)
             for a in range(DSA_HALVES)], axis=0)

    def values(h):
        return vt_ref[h * HEAD_DIM:(h + 1) * HEAD_DIM, :]

    @pl.when(bounded)
    def _():
        def raw_logits(h):
            sl = slice(h * HEAD_DIM, (h + 1) * HEAD_DIM)
            s_ref[h] = lax.dot_general(k_ref[0, :, sl], q_ref[0, :, sl], _NT,
                                       preferred_element_type=F32)

        def heads(weights):
            for h in range(DSA_LOOKAHEAD):
                raw_logits(h)
            set_mask(-bound)
            for h in range(N_HEADS_B):
                if h + DSA_LOOKAHEAD < N_HEADS_B:
                    raw_logits(h + DSA_LOOKAHEAD)
                p, gain = weights(h)
                l_ref[h] += gain * jnp.sum(p, axis=0, keepdims=True)
                acc_ref[h] += gain * jnp.dot(values(h), p.astype(BF16), preferred_element_type=F32)

        @pl.when(far)
        def _():
            heads(lambda h: (jnp.exp2(s_ref[h] + mask_ref[...]), jnp.exp2(far_bias(h))))

        @pl.when(jnp.logical_not(far))
        def _():
            heads(lambda h: (jnp.exp2(s_ref[h] + mask_ref[...] + near_bias(h)), 1.0))

    @pl.when(jnp.logical_not(bounded))
    def _():
        set_mask(0.0)

        @pl.when(far)
        def _():
            for h in range(N_HEADS_B):
                s = logits(h)
                s_ref[h] = s
                cm_ref[h] = jnp.max(s, axis=0, keepdims=True) + far_bias(h)

        @pl.when(jnp.logical_not(far))
        def _():
            for h in range(N_HEADS_B):
                s = logits(h) + near_bias(h)
                s_ref[h] = s
                cm_ref[h] = jnp.max(s, axis=0, keepdims=True)

        for h in range(N_HEADS_B):
            m_old = m_ref[h]
            m_new = jnp.maximum(m_old, cm_ref[h])
            alpha = jnp.exp2(m_old - m_new)
            m_ref[h] = m_new
            shift = jnp.where(far, m_new - far_bias(h), m_new)
            p = jnp.exp2(s_ref[h] - shift)
            l_ref[h] = alpha * l_ref[h] + jnp.sum(p, axis=0, keepdims=True)
            pv = jnp.dot(values(h), p.astype(BF16), preferred_element_type=F32)
            acc_ref[h] = alpha * acc_ref[h] + pv

    last = (qb * ATT_TQ + ATT_TQ - 1) // DSA_TK

    @pl.when(kc == last)
    def _():
        for h in range(N_HEADS_B):
            o_ref[0, h * HEAD_DIM:(h + 1) * HEAD_DIM, :] = (acc_ref[h] / l_ref[h]).astype(BF16)


def _dsa_scores(qi_t, ki, wi_t):
    bsz, _, s = qi_t.shape
    qs, ks = _dsa_schedule(s, DSA_TQ, SC_TK)
    npairs = len(qs)
    return pl.pallas_call(
        _dsa_score_kernel,
        grid_spec=pltpu.PrefetchScalarGridSpec(
            num_scalar_prefetch=2,
            grid=(bsz, npairs),
            in_specs=[
                pl.BlockSpec((1, N_IDX_HEADS * IDX_DIM, DSA_TQ), lambda b, t, qt, kt: (b, 0, qt[t])),
                pl.BlockSpec((1, SC_TK, IDX_DIM), lambda b, t, qt, kt: (b, kt[t], 0)),
                pl.BlockSpec((1, N_IDX_HEADS, DSA_TQ), lambda b, t, qt, kt: (b, 0, qt[t])),
            ],
            out_specs=[
                pl.BlockSpec((None, None, None, SC_TK, DSA_TQ),
                             lambda b, t, qt, kt: (b, kt[t], qt[t], 0, 0)),
                pl.BlockSpec((1, 2, DSA_TQ), lambda b, t, qt, kt: (b, 0, qt[t])),
            ],
            scratch_shapes=[pltpu.VMEM((WORD, s // WORD, DSA_TQ), I32),
                            pltpu.VMEM((s // WORD, DSA_TQ), I32),
                            pltpu.VMEM((SC_TK, DSA_TQ), I32)],
        ),
        out_shape=[jax.ShapeDtypeStruct((bsz, s // SC_TK, s // DSA_TQ, SC_TK, DSA_TQ), F32),
                   jax.ShapeDtypeStruct((bsz, 2, s), F32)],
        compiler_params=_cparams(("arbitrary", "arbitrary")),
        name="dsa_scores",
    )(qs, ks, qi_t, ki, wi_t)


def _dsa(main, ki, wi_t, bias, logit_bound):
    bsz, s, _ = main.shape
    w = WIDTH_B
    col = lambda c: main[:, :, (PROJ_B0 + c) * PROJ_BLK:(PROJ_B0 + c + 1) * PROJ_BLK]
    scores, thr = _dsa_scores(jnp.swapaxes(col(3), 1, 2), ki, wi_t)

    vt = jnp.swapaxes(col(2).reshape(bsz, s // DSA_TK, DSA_TK, w), 2, 3)
    ntile = bias.shape[1]
    qs, ks = _dsa_schedule(s, ATT_TQ, DSA_TK)
    out_t = pl.pallas_call(
        _dsa_attn_kernel,
        grid_spec=pltpu.PrefetchScalarGridSpec(
            num_scalar_prefetch=2,
            grid=(bsz, len(qs)),
            in_specs=[
                pl.BlockSpec((None, None, ATT_QT, DSA_TK, DSA_TQ),
                             lambda b, t, qt, kt: (b, kt[t] // SC_PER_ATT, qt[t], kt[t] % SC_PER_ATT, 0)),
                pl.BlockSpec((1, 2, ATT_TQ), lambda b, t, qt, kt: (b, 0, qt[t])),
                pl.BlockSpec((1, ATT_TQ, w), lambda b, t, qt, kt: (b, qt[t], PROJ_B0)),
                pl.BlockSpec((1, DSA_TK, w), lambda b, t, qt, kt: (b, kt[t], PROJ_B0 + 1)),
                pl.BlockSpec((None, None, w, DSA_TK), lambda b, t, qt, kt: (b, kt[t], 0, 0)),
                pl.BlockSpec((N_HEADS_B, ntile, DSA_TQ, DSA_TQ), lambda b, t, qt, kt: (0, 0, 0, 0),
                             pipeline_mode=pl.Buffered(1)),
                pl.BlockSpec(memory_space=pltpu.SMEM),
            ],
            out_specs=pl.BlockSpec((1, w, ATT_TQ), lambda b, t, qt, kt: (b, 0, qt[t])),
            scratch_shapes=[pltpu.VMEM((N_HEADS_B, HEAD_DIM, ATT_TQ), F32),
                            pltpu.VMEM((N_HEADS_B, 1, ATT_TQ), F32),
                            pltpu.VMEM((N_HEADS_B, 1, ATT_TQ), F32),
                            pltpu.VMEM((N_HEADS_B, DSA_TK, ATT_TQ), F32),
                            pltpu.VMEM((DSA_TK, ATT_TQ), F32),
                            pltpu.VMEM((N_HEADS_B, 1, ATT_TQ), F32)],
        ),
        out_shape=jax.ShapeDtypeStruct((bsz, w, s), BF16),
        compiler_params=_cparams(("arbitrary", "arbitrary")),
        name="dsa_attn",
    )(qs, ks, scores, thr, main, main, vt, bias, logit_bound)
    return jnp.swapaxes(out_t, 1, 2).reshape(bsz * s, w)


def _out_kernel(x_ref, ma_ref, mb_ref, wa_ref, wb_ref, o_ref):
    o_ref[...] = (x_ref[...]
                  + jnp.dot(ma_ref[...], wa_ref[...], preferred_element_type=F32)
                  + jnp.dot(mb_ref[...], wb_ref[...], preferred_element_type=F32))


def _out_proj(x2, mix_a, mix_b, w_out, tm=512):
    m, d = x2.shape
    row = lambda width: pl.BlockSpec((tm, width), lambda i: (i, 0))
    return pl.pallas_call(
        _out_kernel,
        grid=(m // tm,),
        in_specs=[row(d), row(WIDTH_A), row(WIDTH_B),
                  pl.BlockSpec((WIDTH_A, d), lambda i: (0, 0)),
                  pl.BlockSpec((WIDTH_B, d), lambda i: (1, 0))],
        out_specs=row(d),
        out_shape=jax.ShapeDtypeStruct((m, d), F32),
        compiler_params=_cparams(("parallel",)),
        name="out_proj",
    )(x2, mix_a, mix_b, w_out, w_out)


def kernel(x, rel_bias, norm_ffn1, w_ffn1_in, w_ffn1_out, norm_mix, w_in, q_norm_a, k_norm_a,
           q_norm_b, k_norm_b, w_out, norm_ffn2, w_ffn2_in, w_ffn2_out):
    bsz, s, d = x.shape
    m = bsz * s
    depth = norm_ffn1.shape[0]
    scale = HEAD_DIM ** -0.5
    n_main = PROJ_NBLK * PROJ_BLK
    dil_bias, dsa_bias = _bias_tiles(rel_bias)
    x2 = x.reshape(m, d)
    ones = jnp.ones((PROJ_BLK,), F32)
    tile8 = lambda g: jnp.tile(g, PROJ_BLK // HEAD_DIM)
    for l in range(depth):
        x2 = _ffn(x2, norm_ffn1[l], w_ffn1_in[l].astype(BF16), w_ffn1_out[l].astype(BF16))

        w_main = w_in[l, :, :n_main].astype(BF16)
        w_idx = jnp.pad(w_in[l, :, n_main:], ((0, 0), (0, IDX_PAD - IDX_DIM - N_IDX_HEADS))).astype(BF16)
        head_gain = jnp.stack([tile8(q_norm_a[l] * scale), tile8(k_norm_a[l]), ones,
                               tile8(q_norm_b[l] * (scale * LOG2E)), tile8(k_norm_b[l]), ones, ones])
        main, idx = _proj(x2, norm_mix[l], w_main, w_idx, head_gain.reshape(PROJ_NBLK, 1, PROJ_BLK))
        main = main.reshape(bsz, s, n_main)
        idx = idx.reshape(bsz, s, IDX_PAD)
        ki = idx[:, :, :IDX_DIM].astype(BF16)
        wi_t = jnp.swapaxes(idx[:, :, IDX_DIM:IDX_DIM + N_IDX_HEADS], 1, 2)

        mix_a = _dilated(main, dil_bias)
        logit_bound = (1.02 * HEAD_DIM * jnp.max(jnp.abs(q_norm_b[l] * (scale * LOG2E)))
                       * jnp.max(jnp.abs(k_norm_b[l]))
                       + LOG2E * jnp.max(jnp.abs(rel_bias[:, N_HEADS_A:]))).reshape(1)
        mix_b = _dsa(main, ki, wi_t, dsa_bias, logit_bound)

        x2 = _out_proj(x2, mix_a, mix_b, w_out[l].astype(BF16))
        x2 = _ffn(x2, norm_ffn2[l], w_ffn2_in[l].astype(BF16), w_ffn2_out[l].astype(BF16))
    return x2.reshape(bsz, s, d)
```

```python
import functools
import math

import jax
import jax.numpy as jnp
import numpy as np
from jax import lax
from jax.experimental import pallas as pl
from jax.experimental.pallas import tpu as pltpu

F32 = jnp.float32
BF16 = jnp.bfloat16
I32 = jnp.int32

HEAD_DIM = 128
N_HEADS_A = 8
N_HEADS_B = 8
WIDTH_A = N_HEADS_A * HEAD_DIM
WIDTH_B = N_HEADS_B * HEAD_DIM
DILATIONS = (1, 4, 16)
DIL_BLK = 128
N_IDX_HEADS = 16
IDX_DIM = 64
TOPK_MAX = 256
N_BUCKETS = 32
REL_MAX_DIST = 2048
EPS = 1e-6
IDX_PAD = 128
IDX_SCALE = (IDX_DIM ** -0.5) * (N_IDX_HEADS ** -0.5)
LOG2E = math.log2(math.e)

VMEM_LIMIT = 56 * 1024 * 1024

DSA_TQ = 256
DSA_TK = 512
DSA_HALVES = DSA_TK // DSA_TQ
ATT_TQ = 512
ATT_QT = ATT_TQ // DSA_TQ
WORD = 32
SC_TK = 1024
SC_PER_ATT = SC_TK // DSA_TK
CHUNK_WORDS = SC_TK // WORD
SCORE_ROWS = 64
NEG_BIG = -1e30
DSA_SAFE_BOUND = 40.0
DSA_LOOKAHEAD = 2
INT_MIN = -2 ** 31


def _near_tiles():
    exact = N_BUCKETS // 2
    j = 1
    while True:
        d = j * DSA_TQ - (DSA_TQ - 1)
        b = exact + int(math.log(d / exact) / math.log(REL_MAX_DIST / exact) * (N_BUCKETS - exact) - 0.02)
        if d >= exact and b >= N_BUCKETS - 1:
            return j
        j += 1


DSA_NEAR = _near_tiles()


def _cparams(sem):
    return pltpu.CompilerParams(dimension_semantics=sem, vmem_limit_bytes=VMEM_LIMIT)


def _rel_bucket(dist):
    exact = N_BUCKETS // 2
    df = jnp.maximum(dist, 1).astype(F32)
    large = exact + (jnp.log(df / exact) / math.log(REL_MAX_DIST / exact)
                     * (N_BUCKETS - exact)).astype(I32)
    large = jnp.minimum(large, N_BUCKETS - 1)
    return jnp.where(dist < exact, dist, large)


def _lookup(tab_ref, bucket, col):
    val = jnp.zeros(bucket.shape, F32)
    for b in range(N_BUCKETS):
        val = jnp.where(bucket == b, tab_ref[b, col], val)
    return val


def _dil_bias_kernel(tab_ref, o_ref):
    dil = jnp.left_shift(1, 2 * pl.program_id(0))
    i = lax.broadcasted_iota(I32, (DIL_BLK, 2 * DIL_BLK), 0)
    j = lax.broadcasted_iota(I32, (DIL_BLK, 2 * DIL_BLK), 1)
    bucket = _rel_bucket(jnp.clip(DIL_BLK + i - j, 0, DIL_BLK) * dil)
    for h in range(N_HEADS_A):
        o_ref[0, h] = _lookup(tab_ref, bucket, h)


def _dsa_bias_kernel(tab_ref, o_ref):
    jj = pl.program_id(0)
    s = lax.broadcasted_iota(I32, (DSA_TQ, DSA_TQ), 0)
    t = lax.broadcasted_iota(I32, (DSA_TQ, DSA_TQ), 1)
    bucket = _rel_bucket(jnp.maximum((jj - 1) * DSA_TQ + t - s, 0))
    for h in range(N_HEADS_B):
        o_ref[h, 0] = _lookup(tab_ref, bucket, N_HEADS_A + h) * LOG2E


def _bias_tiles(rel_bias):
    smem = pl.BlockSpec(memory_space=pltpu.SMEM)
    dil = pl.pallas_call(
        _dil_bias_kernel,
        grid=(len(DILATIONS),),
        in_specs=[smem],
        out_specs=pl.BlockSpec((1, N_HEADS_A, DIL_BLK, 2 * DIL_BLK), lambda g: (g, 0, 0, 0)),
        out_shape=jax.ShapeDtypeStruct((len(DILATIONS), N_HEADS_A, DIL_BLK, 2 * DIL_BLK), F32),
        name="dil_bias",
    )(rel_bias)
    ntile = DSA_NEAR + 2
    dsa = pl.pallas_call(
        _dsa_bias_kernel,
        grid=(ntile,),
        in_specs=[smem],
        out_specs=pl.BlockSpec((N_HEADS_B, 1, DSA_TQ, DSA_TQ), lambda g: (0, g, 0, 0)),
        out_shape=jax.ShapeDtypeStruct((N_HEADS_B, ntile, DSA_TQ, DSA_TQ), F32),
        name="dsa_bias",
    )(rel_bias)
    return dil, dsa


CAST_BLOCK_BYTES = 8 * 1024 * 1024


def _cast_kernel(w_ref, o_ref):
    o_ref[...] = w_ref[...].astype(BF16)


def _layer_bf16(w, layer):
    _, rows, cols = w.shape
    rb = rows
    while rb * cols * 4 > CAST_BLOCK_BYTES and rb % 32 == 0:
        rb //= 2
    return pl.pallas_call(
        _cast_kernel,
        grid=(rows // rb,),
        in_specs=[pl.BlockSpec((None, rb, cols), lambda i: (layer, i, 0))],
        out_specs=pl.BlockSpec((rb, cols), lambda i: (i, 0)),
        out_shape=jax.ShapeDtypeStruct((rows, cols), BF16),
        compiler_params=_cparams(("parallel",)),
        name="cast_bf16",
    )(w)


def _ffn_kernel(x_ref, g_ref, wg_ref, wu_ref, wo_ref, o_ref, h_ref, acc_ref):
    f = pl.program_id(1)

    @pl.when(f == 0)
    def _():
        x = x_ref[...]
        ms = jnp.mean(x * x, axis=-1, keepdims=True)
        h_ref[...] = (x * lax.rsqrt(ms + EPS) * g_ref[...]).astype(BF16)
        acc_ref[...] = jnp.zeros_like(acc_ref)

    h = h_ref[...]
    gate = jnp.dot(h, wg_ref[...], preferred_element_type=F32)
    up = jnp.dot(h, wu_ref[...], preferred_element_type=F32)
    act = (gate * jax.nn.sigmoid(gate) * up).astype(BF16)
    acc_ref[...] += jnp.dot(act, wo_ref[...], preferred_element_type=F32)

    @pl.when(f == pl.num_programs(1) - 1)
    def _():
        o_ref[...] = x_ref[...] + 0.5 * acc_ref[...]


def _ffn(x2, gain, w_in, w_out, tm=512, tf=512):
    m, d = x2.shape
    d_ff = w_out.shape[0]
    nf = d_ff // tf
    return pl.pallas_call(
        _ffn_kernel,
        grid=(m // tm, nf),
        in_specs=[
            pl.BlockSpec((tm, d), lambda i, f: (i, 0)),
            pl.BlockSpec((1, d), lambda i, f: (0, 0)),
            pl.BlockSpec((d, tf), lambda i, f: (0, f)),
            pl.BlockSpec((d, tf), lambda i, f: (0, f + nf)),
            pl.BlockSpec((tf, d), lambda i, f: (f, 0)),
        ],
        out_specs=pl.BlockSpec((tm, d), lambda i, f: (i, 0)),
        out_shape=jax.ShapeDtypeStruct((m, d), F32),
        scratch_shapes=[pltpu.VMEM((tm, d), BF16), pltpu.VMEM((tm, d), F32)],
        compiler_params=_cparams(("parallel", "arbitrary")),
        name="ffn",
    )(x2, gain.reshape(1, d), w_in, w_in, w_out)


PROJ_BLK = 1024
PROJ_NBLK = 7
PROJ_B0 = 3
PROJ_PARTS = 4
_NORM_BLOCKS = (0, 1, 3, 4)


def _proj_kernel(x_ref, g_ref, w_ref, widx_ref, hg_ref, o_ref, idx_ref, h_ref):
    j = pl.program_id(1)

    @pl.when(j == 0)
    def _():
        x = x_ref[...]
        ms = jnp.mean(x * x, axis=-1, keepdims=True)
        h_ref[...] = (x * lax.rsqrt(ms + EPS) * g_ref[...]).astype(BF16)

    is_norm = functools.reduce(jnp.logical_or, [j == b for b in _NORM_BLOCKS])
    rows = h_ref.shape[0] // PROJ_PARTS
    for part in range(PROJ_PARTS):
        rs = slice(part * rows, (part + 1) * rows)
        y = jnp.dot(h_ref[rs, :], w_ref[...], preferred_element_type=F32)
        for hd in range(PROJ_BLK // HEAD_DIM):
            sl = slice(hd * HEAD_DIM, (hd + 1) * HEAD_DIM)
            yh = y[:, sl]
            ms = jnp.mean(yh * yh, axis=-1, keepdims=True)
            inv = jnp.where(is_norm, lax.rsqrt(ms + EPS), 1.0)
            o_ref[rs, sl] = (yh * inv * hg_ref[0, :, sl]).astype(BF16)

    @pl.when(j == PROJ_NBLK - 1)
    def _():
        idx_ref[...] = jnp.dot(h_ref[...], widx_ref[...], preferred_element_type=F32)


def _proj(x2, gain, w_main, w_idx, head_gain, tm=1024):
    m, d = x2.shape
    return pl.pallas_call(
        _proj_kernel,
        grid=(m // tm, PROJ_NBLK),
        in_specs=[
            pl.BlockSpec((tm, d), lambda i, j: (i, 0)),
            pl.BlockSpec((1, d), lambda i, j: (0, 0)),
            pl.BlockSpec((d, PROJ_BLK), lambda i, j: (0, j)),
            pl.BlockSpec((d, IDX_PAD), lambda i, j: (0, 0)),
            pl.BlockSpec((1, 1, PROJ_BLK), lambda i, j: (j, 0, 0)),
        ],
        out_specs=[
            pl.BlockSpec((tm, PROJ_BLK), lambda i, j: (i, j)),
            pl.BlockSpec((tm, IDX_PAD), lambda i, j: (i, 0)),
        ],
        out_shape=[
            jax.ShapeDtypeStruct((m, PROJ_NBLK * PROJ_BLK), BF16),
            jax.ShapeDtypeStruct((m, IDX_PAD), F32),
        ],
        scratch_shapes=[pltpu.VMEM((tm, d), BF16)],
        compiler_params=_cparams(("parallel", "arbitrary")),
        name="proj",
    )(x2, gain.reshape(1, d), w_main, w_idx, head_gain)


_NT = (((1,), (1,)), ((), ()))


DIL_SPAN = DIL_BLK * max(DILATIONS)
DIL_G = 2
DIL_UNITS = DIL_SPAN // DIL_BLK
DIL_UNROLL = 8


def _dilated_kernel(q_ref, kp_ref, kc_ref, vp_ref, vc_ref, b_ref, o_ref, qf, kf, vf, ob, lse_ref):
    n = pl.program_id(1)
    h0 = pl.program_id(2) * DIL_G
    for g in range(DIL_G):
        sl = slice(g * HEAD_DIM, (g + 1) * HEAD_DIM)
        qf[g] = q_ref[0, :, sl].astype(F32)
        kf[g, :DIL_SPAN] = kp_ref[0, :, sl].astype(F32)
        kf[g, DIL_SPAN:] = kc_ref[0, :, sl].astype(F32)
        vf[g, :DIL_SPAN] = vp_ref[0, :, sl].astype(F32)
        vf[g, DIL_SPAN:] = vc_ref[0, :, sl].astype(F32)
    i = lax.broadcasted_iota(I32, (DIL_BLK, 2 * DIL_BLK), 0)
    j = lax.broadcasted_iota(I32, (DIL_BLK, 2 * DIL_BLK), 1)
    band = jnp.logical_and(j >= i, j <= i + DIL_BLK)
    first = j >= DIL_BLK

    for br, dil in enumerate(DILATIONS):
        span = DIL_BLK * dil

        def unit(idx, carry, br=br, dil=dil, span=span):
            blk = idx // dil
            base = blk * span + idx % dil
            valid = jnp.logical_and(band, jnp.logical_or(first, n * (DIL_SPAN // span) + blk > 0))
            qrows = pl.ds(base, DIL_BLK, stride=dil)
            krows = pl.ds(DIL_SPAN + base - span, 2 * DIL_BLK, stride=dil)
            for g in range(DIL_G):
                q = qf[g, qrows, :].astype(BF16)
                k = kf[g, krows, :].astype(BF16)
                v = vf[g, krows, :].astype(BF16)
                s = lax.dot_general(q, k, _NT, preferred_element_type=F32) + b_ref[br, h0 + g]
                s = jnp.where(valid, s, -jnp.inf)
                mx = jnp.max(s, axis=-1, keepdims=True)
                p = jnp.exp(s - mx)
                den = jnp.sum(p, axis=-1, keepdims=True)
                ob[br, g, qrows, :] = jnp.dot(p.astype(BF16), v, preferred_element_type=F32) / den
                lse_ref[br, g, qrows, :] = jnp.broadcast_to(mx + jnp.log(den), (DIL_BLK, HEAD_DIM))
            return carry

        lax.fori_loop(0, DIL_UNITS, unit, 0, unroll=DIL_UNROLL)

    for g in range(DIL_G):
        l1, l2, l3 = lse_ref[0, g], lse_ref[1, g], lse_ref[2, g]
        mx = jnp.maximum(jnp.maximum(l1, l2), l3)
        e1, e2, e3 = jnp.exp(l1 - mx), jnp.exp(l2 - mx), jnp.exp(l3 - mx)
        mix = (e1 * ob[0, g] + e2 * ob[1, g] + e3 * ob[2, g]) / (e1 + e2 + e3)
        o_ref[0, :, g * HEAD_DIM:(g + 1) * HEAD_DIM] = mix.astype(BF16)


def _dilated(main, bias):
    bsz, s, _ = main.shape
    gw = DIL_G * HEAD_DIM
    ng = WIDTH_A // gw
    cur = lambda c: pl.BlockSpec((1, DIL_SPAN, gw), lambda b, n, g: (b, n, c * ng + g))
    prev = lambda c: pl.BlockSpec((1, DIL_SPAN, gw), lambda b, n, g: (b, jnp.maximum(n - 1, 0), c * ng + g))
    out = pl.pallas_call(
        _dilated_kernel,
        grid=(bsz, s // DIL_SPAN, ng),
        in_specs=[cur(0), prev(1), cur(1), prev(2), cur(2),
                  pl.BlockSpec(bias.shape, lambda b, n, g: (0, 0, 0, 0), pipeline_mode=pl.Buffered(1))],
        out_specs=pl.BlockSpec((1, DIL_SPAN, gw), lambda b, n, g: (b, n, g)),
        out_shape=jax.ShapeDtypeStruct((bsz, s, WIDTH_A), BF16),
        scratch_shapes=[pltpu.VMEM((DIL_G, DIL_SPAN, HEAD_DIM), F32),
                        pltpu.VMEM((DIL_G, 2 * DIL_SPAN, HEAD_DIM), F32),
                        pltpu.VMEM((DIL_G, 2 * DIL_SPAN, HEAD_DIM), F32),
                        pltpu.VMEM((len(DILATIONS), DIL_G, DIL_SPAN, HEAD_DIM), F32),
                        pltpu.VMEM((len(DILATIONS), DIL_G, DIL_SPAN, HEAD_DIM), F32)],
        compiler_params=_cparams(("parallel", "parallel", "parallel")),
        name="dilated",
    )(main, main, main, main, main, bias)
    return out.reshape(bsz * s, WIDTH_A)


def _dsa_schedule(s, tq, tk):
    qs, ks = [], []
    for i in range(s // tq):
        last = (i * tq + tq - 1) // tk
        for k in range(last + 1):
            qs.append(i)
            ks.append(k)
    return np.asarray(qs, np.int32), np.asarray(ks, np.int32)


def _sortable(bits):
    return bits ^ (lax.shift_right_arithmetic(bits, 31) & 0x7FFFFFFF)


def _bit_transpose(words):
    a = list(words)
    mask, j = 0x0000FFFF, 16
    while j:
        k = 0
        while k < WORD:
            t = (a[k] ^ lax.shift_right_logical(a[k + j], j)) & mask
            a[k] = a[k] ^ t
            a[k + j] = a[k + j] ^ lax.shift_left(t, j)
            k = (k + j + 1) & ~j
        j >>= 1
        mask = (mask ^ (mask << j)) & 0xFFFFFFFF
    return a


def _dsa_score_kernel(qtab, ktab, qit_ref, ki_ref, w_ref, sc_ref, thr_ref,
                      planes_ref, alive_ref, stage_ref):
    step = pl.program_id(1)
    qb = qtab[step]
    kc = ktab[step]
    w = w_ref[0] * IDX_SCALE
    srow = lax.broadcasted_iota(I32, (SCORE_ROWS, DSA_TQ), 0)
    tcol = lax.broadcasted_iota(I32, (SCORE_ROWS, DSA_TQ), 1)
    row0 = pl.multiple_of(kc * CHUNK_WORDS, CHUNK_WORDS)
    group = WORD * 8
    for g in range(SC_TK // group):
        for r0 in range(g * group, (g + 1) * group, SCORE_ROWS):
            kidx = ki_ref[0, r0:r0 + SCORE_ROWS, :]
            acc = jnp.zeros((SCORE_ROWS, DSA_TQ), F32)
            for h in range(N_IDX_HEADS):
                d = jnp.dot(kidx, qit_ref[0, h * IDX_DIM:(h + 1) * IDX_DIM, :],
                            preferred_element_type=F32)
                acc = acc + jnp.maximum(d, 0.0) * w[h:h + 1, :]
            causal = srow - tcol <= qb * DSA_TQ - kc * SC_TK - r0
            acc = jnp.where(causal, acc, -jnp.inf)
            sc_ref[r0:r0 + SCORE_ROWS, :] = acc
            stage_ref[r0:r0 + SCORE_ROWS, :] = _sortable(lax.bitcast_convert_type(acc, I32)) ^ INT_MIN
        for lanes in (slice(c, c + 128) for c in range(0, DSA_TQ, 128)):
            planes = _bit_transpose([stage_ref[g * group + 8 * n:g * group + 8 * (n + 1), lanes]
                                     for n in range(WORD)])
            for n in range(WORD):
                planes_ref[n, pl.ds(pl.multiple_of(row0 + 8 * g, 8), 8), lanes] = planes[n]

    last = (qb * DSA_TQ + DSA_TQ - 1) // SC_TK

    @pl.when(kc == last)
    def _():
        nwords = alive_ref.shape[0]
        rb = min(128, nwords)
        nrows = (last + 1) * CHUNK_WORDS
        nblk = (nrows + rb - 1) // rb
        tpos = qb * DSA_TQ + lax.broadcasted_iota(I32, (1, DSA_TQ), 1)
        want0 = jnp.minimum(tpos + 1, TOPK_MAX)
        rows = lambda blk: pl.ds(pl.multiple_of(blk * rb, rb), rb)

        def hits(alive, n, blk):
            hit = lax.population_count(alive & planes_ref[n, rows(blk), :])
            return hit.reshape(rb // 8, 8, DSA_TQ).sum(axis=0)

        def init(blk, cnt):
            ridx = blk * rb + lax.broadcasted_iota(I32, (rb, DSA_TQ), 0)
            alive = jnp.where(ridx < nrows, -1, 0)
            alive_ref[rows(blk), :] = alive
            return cnt + hits(alive, 0, blk)
        zero = jnp.zeros((8, DSA_TQ), I32)
        cnt0 = lax.fori_loop(0, nblk, init, zero)

        def bit_step(n, carry):
            r, want, cnt8 = carry
            cnt = cnt8.sum(axis=0, keepdims=True)
            take = cnt >= want
            r = r | jnp.where(take, lax.shift_left(jnp.int32(1), 31 - n), 0)
            want = jnp.where(take, want, want - cnt)
            flip = jnp.where(take, 0, -1)
            nxt = jnp.minimum(n + 1, WORD - 1)

            def sweep(blk, acc):
                alive = alive_ref[rows(blk), :] & (planes_ref[n, rows(blk), :] ^ flip)
                alive_ref[rows(blk), :] = alive
                return acc + hits(alive, nxt, blk)
            return r, want, lax.fori_loop(0, nblk, sweep, zero)

        r, want, _ = lax.fori_loop(0, WORD, bit_step, (jnp.zeros((1, DSA_TQ), I32), want0, cnt0))
        thr_ref[0, 0:1, :] = lax.bitcast_convert_type(_sortable(r ^ INT_MIN), F32)
        thr_ref[0, 1:2, :] = jnp.full((1, DSA_TQ), nwords * WORD, F32)

        def live(blk, acc):
            hit = lax.population_count(alive_ref[rows(blk), :])
            return acc + hit.reshape(rb // 8, 8, DSA_TQ).sum(axis=0)
        tied = lax.fori_loop(0, nblk, live, zero).sum(axis=0, keepdims=True)

        @pl.when(jnp.max(tied - want) > 0)
        def _():
            blk_bits = max(1, (nwords // 8 - 1).bit_length())
            lane_patterns = (-0x10000, -0xFF0100, -0xF0F0F10, -0x33333334, -0x55555556)
            nsec = blk_bits + len(lane_patterns) + 3

            def sec_plane(k, blk):
                ridx = blk * rb + lax.broadcasted_iota(I32, (rb, 1), 0)
                if k < blk_bits:
                    bit = 3 + blk_bits - 1 - k
                elif k < blk_bits + len(lane_patterns):
                    return jnp.full((rb, 1), lane_patterns[k - blk_bits], I32)
                else:
                    bit = nsec - 1 - k
                return jnp.where((lax.shift_right_logical(ridx, bit) & 1) == 0, -1, 0)

            def sec_count(k):
                def body(blk, acc):
                    hit = lax.population_count(alive_ref[rows(blk), :] & sec_plane(k, blk))
                    return acc + hit.reshape(rb // 8, 8, DSA_TQ).sum(axis=0)
                return lax.fori_loop(0, nblk, body, zero).sum(axis=0, keepdims=True)

            left = want
            code = jnp.zeros((1, DSA_TQ), I32)
            for k in range(nsec):
                cnt = sec_count(k)
                take = cnt >= left
                code = code | jnp.where(take, 1 << (nsec - 1 - k), 0)
                left = jnp.where(take, left, left - cnt)
                flip = jnp.where(take, 0, -1)

                def update(blk, c, k=k, flip=flip):
                    alive_ref[rows(blk), :] = alive_ref[rows(blk), :] & (sec_plane(k, blk) ^ flip)
                    return c
                lax.fori_loop(0, nblk, update, 0)
            inv = ~code
            cutoff = ((lax.shift_right_logical(inv, 8) & ((1 << blk_bits) - 1)) * (8 * WORD)
                      + 8 * ((code >> 3) & (WORD - 1) ^ (WORD - 1)) + (inv & 7))
            thr_ref[0, 1:2, :] = cutoff.astype(F32)


def _dsa_attn_kernel(qtab, ktab, sc_ref, thr_ref, q_ref, k_ref, vt_ref, b_ref, bound_ref, o_ref,
                     acc_ref, m_ref, l_ref, s_ref, mask_ref, cm_ref):
    step = pl.program_id(1)
    qb = qtab[step]
    kc = ktab[step]

    @pl.when(kc == 0)
    def _():
        acc_ref[...] = jnp.zeros_like(acc_ref)
        m_ref[...] = jnp.full_like(m_ref, NEG_BIG)
        l_ref[...] = jnp.zeros_like(l_ref)

    bound = bound_ref[0]
    bounded = bound <= DSA_SAFE_BOUND

    def set_mask(kept):
        key = (kc * DSA_TK + lax.broadcasted_iota(I32, (DSA_TK, DSA_TQ), 0)).astype(F32)
        for b in range(ATT_QT):
            cols = slice(b * DSA_TQ, (b + 1) * DSA_TQ)
            thr = thr_ref[0, 0:1, cols]
            tied = jnp.logical_and(sc_ref[b] == thr, key <= thr_ref[0, 1:2, cols])
            mask_ref[:, cols] = jnp.where(jnp.logical_or(sc_ref[b] > thr, tied), kept, -jnp.inf)

    tiles = [[jnp.clip((qb * ATT_QT + b) - (kc * DSA_HALVES + a), -1, DSA_NEAR) + 1
              for b in range(ATT_QT)] for a in range(DSA_HALVES)]
    far = qb * ATT_QT - (kc * DSA_HALVES + DSA_HALVES - 1) >= DSA_NEAR
    far_bias = lambda h: b_ref[h, DSA_NEAR + 1, 0:1, 0:1]

    def logits(h):
        sl = slice(h * HEAD_DIM, (h + 1) * HEAD_DIM)
        s = lax.dot_general(k_ref[0, :, sl], q_ref[0, :, sl], _NT, preferred_element_type=F32)
        return s + mask_ref[...]

    def near_bias(h):
        return jnp.concatenate(
            [jnp.concatenate([b_ref[h, tiles[a][b]] for b in range(ATT_QT)], axis=1)
             for a in range(DSA_HALVES)], axis=0)

    def values(h):
        return vt_ref[h * HEAD_DIM:(h + 1) * HEAD_DIM, :]

    @pl.when(bounded)
    def _():
        def raw_logits(h):
            sl = slice(h * HEAD_DIM, (h + 1) * HEAD_DIM)
            s_ref[h] = lax.dot_general(k_ref[0, :, sl], q_ref[0, :, sl], _NT,
                                       preferred_element_type=F32)

        def heads(weights):
            for h in range(DSA_LOOKAHEAD):
                raw_logits(h)
            set_mask(-bound)
            for h in range(N_HEADS_B):
                if h + DSA_LOOKAHEAD < N_HEADS_B:
                    raw_logits(h + DSA_LOOKAHEAD)
                p, gain = weights(h)
                l_ref[h] += gain * jnp.sum(p, axis=0, keepdims=True)
                acc_ref[h] += gain * jnp.dot(values(h), p.astype(BF16), preferred_element_type=F32)

        @pl.when(far)
        def _():
            heads(lambda h: (jnp.exp2(s_ref[h] + mask_ref[...]), jnp.exp2(far_bias(h))))

        @pl.when(jnp.logical_not(far))
        def _():
            heads(lambda h: (jnp.exp2(s_ref[h] + mask_ref[...] + near_bias(h)), 1.0))

    @pl.when(jnp.logical_not(bounded))
    def _():
        set_mask(0.0)

        @pl.when(far)
        def _():
            for h in range(N_HEADS_B):
                s = logits(h)
                s_ref[h] = s
                cm_ref[h] = jnp.max(s, axis=0, keepdims=True) + far_bias(h)

        @pl.when(jnp.logical_not(far))
        def _():
            for h in range(N_HEADS_B):
                s = logits(h) + near_bias(h)
                s_ref[h] = s
                cm_ref[h] = jnp.max(s, axis=0, keepdims=True)

        for h in range(N_HEADS_B):
            m_old = m_ref[h]
            m_new = jnp.maximum(m_old, cm_ref[h])
            alpha = jnp.exp2(m_old - m_new)
            m_ref[h] = m_new
            shift = jnp.where(far, m_new - far_bias(h), m_new)
            p = jnp.exp2(s_ref[h] - shift)
            l_ref[h] = alpha * l_ref[h] + jnp.sum(p, axis=0, keepdims=True)
            pv = jnp.dot(values(h), p.astype(BF16), preferred_element_type=F32)
            acc_ref[h] = alpha * acc_ref[h] + pv

    last = (qb * ATT_TQ + ATT_TQ - 1) // DSA_TK

    @pl.when(kc == last)
    def _():
        for h in range(N_HEADS_B):
            o_ref[0, h * HEAD_DIM:(h + 1) * HEAD_DIM, :] = (acc_ref[h] / l_ref[h]).astype(BF16)


def _dsa_scores(qi_t, ki, wi_t):
    bsz, _, s = qi_t.shape
    qs, ks = _dsa_schedule(s, DSA_TQ, SC_TK)
    npairs = len(qs)
    return pl.pallas_call(
        _dsa_score_kernel,
        grid_spec=pltpu.PrefetchScalarGridSpec(
            num_scalar_prefetch=2,
            grid=(bsz, npairs),
            in_specs=[
                pl.BlockSpec((1, N_IDX_HEADS * IDX_DIM, DSA_TQ), lambda b, t, qt, kt: (b, 0, qt[t])),
                pl.BlockSpec((1, SC_TK, IDX_DIM), lambda b, t, qt, kt: (b, kt[t], 0)),
                pl.BlockSpec((1, N_IDX_HEADS, DSA_TQ), lambda b, t, qt, kt: (b, 0, qt[t])),
            ],
            out_specs=[
                pl.BlockSpec((None, None, None, SC_TK, DSA_TQ),
                             lambda b, t, qt, kt: (b, kt[t], qt[t], 0, 0)),
                pl.BlockSpec((1, 2, DSA_TQ), lambda b, t, qt, kt: (b, 0, qt[t])),
            ],
            scratch_shapes=[pltpu.VMEM((WORD, s // WORD, DSA_TQ), I32),
                            pltpu.VMEM((s // WORD, DSA_TQ), I32),
                            pltpu.VMEM((SC_TK, DSA_TQ), I32)],
        ),
        out_shape=[jax.ShapeDtypeStruct((bsz, s // SC_TK, s // DSA_TQ, SC_TK, DSA_TQ), F32),
                   jax.ShapeDtypeStruct((bsz, 2, s), F32)],
        compiler_params=_cparams(("arbitrary", "arbitrary")),
        name="dsa_scores",
    )(qs, ks, qi_t, ki, wi_t)


def _dsa(main, ki, wi_t, bias, logit_bound):
    bsz, s, _ = main.shape
    w = WIDTH_B
    col = lambda c: main[:, :, (PROJ_B0 + c) * PROJ_BLK:(PROJ_B0 + c + 1) * PROJ_BLK]
    scores, thr = _dsa_scores(jnp.swapaxes(col(3), 1, 2), ki, wi_t)

    vt = jnp.swapaxes(col(2).reshape(bsz, s // DSA_TK, DSA_TK, w), 2, 3)
    ntile = bias.shape[1]
    qs, ks = _dsa_schedule(s, ATT_TQ, DSA_TK)
    out_t = pl.pallas_call(
        _dsa_attn_kernel,
        grid_spec=pltpu.PrefetchScalarGridSpec(
            num_scalar_prefetch=2,
            grid=(bsz, len(qs)),
            in_specs=[
                pl.BlockSpec((None, None, ATT_QT, DSA_TK, DSA_TQ),
                             lambda b, t, qt, kt: (b, kt[t] // SC_PER_ATT, qt[t], kt[t] % SC_PER_ATT, 0)),
                pl.BlockSpec((1, 2, ATT_TQ), lambda b, t, qt, kt: (b, 0, qt[t])),
                pl.BlockSpec((1, ATT_TQ, w), lambda b, t, qt, kt: (b, qt[t], PROJ_B0)),
                pl.BlockSpec((1, DSA_TK, w), lambda b, t, qt, kt: (b, kt[t], PROJ_B0 + 1)),
                pl.BlockSpec((None, None, w, DSA_TK), lambda b, t, qt, kt: (b, kt[t], 0, 0)),
                pl.BlockSpec((N_HEADS_B, ntile, DSA_TQ, DSA_TQ), lambda b, t, qt, kt: (0, 0, 0, 0),
                             pipeline_mode=pl.Buffered(1)),
                pl.BlockSpec(memory_space=pltpu.SMEM),
            ],
            out_specs=pl.BlockSpec((1, w, ATT_TQ), lambda b, t, qt, kt: (b, 0, qt[t])),
            scratch_shapes=[pltpu.VMEM((N_HEADS_B, HEAD_DIM, ATT_TQ), F32),
                            pltpu.VMEM((N_HEADS_B, 1, ATT_TQ), F32),
                            pltpu.VMEM((N_HEADS_B, 1, ATT_TQ), F32),
                            pltpu.VMEM((N_HEADS_B, DSA_TK, ATT_TQ), F32),
                            pltpu.VMEM((DSA_TK, ATT_TQ), F32),
                            pltpu.VMEM((N_HEADS_B, 1, ATT_TQ), F32)],
        ),
        out_shape=jax.ShapeDtypeStruct((bsz, w, s), BF16),
        compiler_params=_cparams(("arbitrary", "arbitrary")),
        name="dsa_attn",
    )(qs, ks, scores, thr, main, main, vt, bias, logit_bound)
    return jnp.swapaxes(out_t, 1, 2).reshape(bsz * s, w)


def _out_kernel(x_ref, ma_ref, mb_ref, wa_ref, wb_ref, o_ref):
    o_ref[...] = (x_ref[...]
                  + jnp.dot(ma_ref[...], wa_ref[...], preferred_element_type=F32)
                  + jnp.dot(mb_ref[...], wb_ref[...], preferred_element_type=F32))


def _out_proj(x2, mix_a, mix_b, w_out, tm=512):
    m, d = x2.shape
    row = lambda width: pl.BlockSpec((tm, width), lambda i: (i, 0))
    return pl.pallas_call(
        _out_kernel,
        grid=(m // tm,),
        in_specs=[row(d), row(WIDTH_A), row(WIDTH_B),
                  pl.BlockSpec((WIDTH_A, d), lambda i: (0, 0)),
                  pl.BlockSpec((WIDTH_B, d), lambda i: (1, 0))],
        out_specs=row(d),
        out_shape=jax.ShapeDtypeStruct((m, d), F32),
        compiler_params=_cparams(("parallel",)),
        name="out_proj",
    )(x2, mix_a, mix_b, w_out, w_out)


def kernel(x, rel_bias, norm_ffn1, w_ffn1_in, w_ffn1_out, norm_mix, w_in, q_norm_a, k_norm_a,
           q_norm_b, k_norm_b, w_out, norm_ffn2, w_ffn2_in, w_ffn2_out):
    bsz, s, d = x.shape
    m = bsz * s
    depth = norm_ffn1.shape[0]
    scale = HEAD_DIM ** -0.5
    n_main = PROJ_NBLK * PROJ_BLK
    dil_bias, dsa_bias = _bias_tiles(rel_bias)
    x2 = x.reshape(m, d)
    ones = jnp.ones((PROJ_BLK,), F32)
    tile8 = lambda g: jnp.tile(g, PROJ_BLK // HEAD_DIM)
    for l in range(depth):
        x2 = _ffn(x2, norm_ffn1[l], _layer_bf16(w_ffn1_in, l), _layer_bf16(w_ffn1_out, l))

        w_main = _layer_bf16(w_in, l)
        w_idx =jnp.pad(w_in[l, :, n_main:], ((0, 0), (0, IDX_PAD - IDX_DIM - N_IDX_HEADS))).astype(BF16)
        head_gain = jnp.stack([tile8(q_norm_a[l] * scale), tile8(k_norm_a[l]), ones,
                               tile8(q_norm_b[l] * (scale * LOG2E)), tile8(k_norm_b[l]), ones, ones])
        main, idx = _proj(x2, norm_mix[l], w_main, w_idx, head_gain.reshape(PROJ_NBLK, 1, PROJ_BLK))
        main = main.reshape(bsz, s, n_main)
        idx = idx.reshape(bsz, s, IDX_PAD)
        ki = idx[:, :, :IDX_DIM].astype(BF16)
        wi_t = jnp.swapaxes(idx[:, :, IDX_DIM:IDX_DIM + N_IDX_HEADS], 1, 2)

        mix_a = _dilated(main, dil_bias)
        logit_bound = (1.02 * HEAD_DIM * jnp.max(jnp.abs(q_norm_b[l] * (scale * LOG2E)))
                       * jnp.max(jnp.abs(k_norm_b[l]))
                       + LOG2E * jnp.max(jnp.abs(rel_bias[:, N_HEADS_A:]))).reshape(1)
        mix_b = _dsa(main, ki, wi_t, dsa_bias, logit_bound)

        x2 = _out_proj(x2, mix_a, mix_b, _layer_bf16(w_out, l))
        x2 = _ffn(x2, norm_ffn2[l], _layer_bf16(w_ffn2_in, l), _layer_bf16(w_ffn2_out, l))
    return x2.reshape(bsz, s, d)
```

```python
import functools
import math

import jax
import jax.numpy as jnp
import numpy as np
from jax import lax
from jax.experimental import pallas as pl
from jax.experimental.pallas import tpu as pltpu

F32 = jnp.float32
BF16 = jnp.bfloat16
I32 = jnp.int32

HEAD_DIM = 128
N_HEADS_A = 8
N_HEADS_B = 8
WIDTH_A = N_HEADS_A * HEAD_DIM
WIDTH_B = N_HEADS_B * HEAD_DIM
DILATIONS = (1, 4, 16)
DIL_BLK = 128
N_IDX_HEADS = 16
IDX_DIM = 64
TOPK_MAX = 256
N_BUCKETS = 32
REL_MAX_DIST = 2048
EPS = 1e-6
IDX_PAD = 128
IDX_SCALE = (IDX_DIM ** -0.5) * (N_IDX_HEADS ** -0.5)
LOG2E = math.log2(math.e)

VMEM_LIMIT = 56 * 1024 * 1024

DSA_TQ = 256
DSA_TK = 512
DSA_HALVES = DSA_TK // DSA_TQ
ATT_TQ = 512
ATT_QT = ATT_TQ // DSA_TQ
WORD = 32
SC_TK = 1024
SC_PER_ATT = SC_TK // DSA_TK
CHUNK_WORDS = SC_TK // WORD
SCORE_ROWS = 64
NEG_BIG = -1e30
DSA_SAFE_BOUND = 40.0
DSA_LOOKAHEAD = 2
INT_MIN = -2 ** 31


def _near_tiles():
    exact = N_BUCKETS // 2
    j = 1
    while True:
        d = j * DSA_TQ - (DSA_TQ - 1)
        b = exact + int(math.log(d / exact) / math.log(REL_MAX_DIST / exact) * (N_BUCKETS - exact) - 0.02)
        if d >= exact and b >= N_BUCKETS - 1:
            return j
        j += 1


DSA_NEAR = _near_tiles()


def _cparams(sem):
    return pltpu.CompilerParams(dimension_semantics=sem, vmem_limit_bytes=VMEM_LIMIT)


def _rel_bucket(dist):
    exact = N_BUCKETS // 2
    df = jnp.maximum(dist, 1).astype(F32)
    large = exact + (jnp.log(df / exact) / math.log(REL_MAX_DIST / exact)
                     * (N_BUCKETS - exact)).astype(I32)
    large = jnp.minimum(large, N_BUCKETS - 1)
    return jnp.where(dist < exact, dist, large)


def _lookup(tab_ref, bucket, col):
    val = jnp.zeros(bucket.shape, F32)
    for b in range(N_BUCKETS):
        val = jnp.where(bucket == b, tab_ref[b, col], val)
    return val


def _dil_bias_kernel(tab_ref, o_ref):
    dil = jnp.left_shift(1, 2 * pl.program_id(0))
    i = lax.broadcasted_iota(I32, (DIL_BLK, 2 * DIL_BLK), 0)
    j = lax.broadcasted_iota(I32, (DIL_BLK, 2 * DIL_BLK), 1)
    bucket = _rel_bucket(jnp.clip(DIL_BLK + i - j, 0, DIL_BLK) * dil)
    for h in range(N_HEADS_A):
        o_ref[0, h] = _lookup(tab_ref, bucket, h)


def _dsa_bias_kernel(tab_ref, o_ref):
    jj = pl.program_id(0)
    s = lax.broadcasted_iota(I32, (DSA_TQ, DSA_TQ), 0)
    t = lax.broadcasted_iota(I32, (DSA_TQ, DSA_TQ), 1)
    bucket = _rel_bucket(jnp.maximum((jj - 1) * DSA_TQ + t - s, 0))
    for h in range(N_HEADS_B):
        o_ref[h, 0] = _lookup(tab_ref, bucket, N_HEADS_A + h) * LOG2E


def _bias_tiles(rel_bias):
    smem = pl.BlockSpec(memory_space=pltpu.SMEM)
    dil = pl.pallas_call(
        _dil_bias_kernel,
        grid=(len(DILATIONS),),
        in_specs=[smem],
        out_specs=pl.BlockSpec((1, N_HEADS_A, DIL_BLK, 2 * DIL_BLK), lambda g: (g, 0, 0, 0)),
        out_shape=jax.ShapeDtypeStruct((len(DILATIONS), N_HEADS_A, DIL_BLK, 2 * DIL_BLK), F32),
        name="dil_bias",
    )(rel_bias)
    ntile = DSA_NEAR + 2
    dsa = pl.pallas_call(
        _dsa_bias_kernel,
        grid=(ntile,),
        in_specs=[smem],
        out_specs=pl.BlockSpec((N_HEADS_B, 1, DSA_TQ, DSA_TQ), lambda g: (0, g, 0, 0)),
        out_shape=jax.ShapeDtypeStruct((N_HEADS_B, ntile, DSA_TQ, DSA_TQ), F32),
        name="dsa_bias",
    )(rel_bias)
    return dil, dsa


def _ffn_kernel(x_ref, g_ref, wg_ref, wu_ref, wo_ref, o_ref, h_ref, acc_ref):
    f = pl.program_id(1)

    @pl.when(f == 0)
    def _():
        x = x_ref[...]
        ms = jnp.mean(x * x, axis=-1, keepdims=True)
        h_ref[...] = (x * lax.rsqrt(ms + EPS) * g_ref[...]).astype(BF16)
        acc_ref[...] = jnp.zeros_like(acc_ref)

    h = h_ref[...]
    gate = jnp.dot(h, wg_ref[...], preferred_element_type=F32)
    up = jnp.dot(h, wu_ref[...], preferred_element_type=F32)
    act = (gate * jax.nn.sigmoid(gate) * up).astype(BF16)
    acc_ref[...] += jnp.dot(act, wo_ref[...], preferred_element_type=F32)

    @pl.when(f == pl.num_programs(1) - 1)
    def _():
        o_ref[...] = x_ref[...] + 0.5 * acc_ref[...]


def _ffn(x2, gain, w_in, w_out, tm=512, tf=512):
    m, d = x2.shape
    d_ff = w_out.shape[0]
    nf = d_ff // tf
    return pl.pallas_call(
        _ffn_kernel,
        grid=(m // tm, nf),
        in_specs=[
            pl.BlockSpec((tm, d), lambda i, f: (i, 0)),
            pl.BlockSpec((1, d), lambda i, f: (0, 0)),
            pl.BlockSpec((d, tf), lambda i, f: (0, f)),
            pl.BlockSpec((d, tf), lambda i, f: (0, f + nf)),
            pl.BlockSpec((tf, d), lambda i, f: (f, 0)),
        ],
        out_specs=pl.BlockSpec((tm, d), lambda i, f: (i, 0)),
        out_shape=jax.ShapeDtypeStruct((m, d), F32),
        scratch_shapes=[pltpu.VMEM((tm, d), BF16), pltpu.VMEM((tm, d), F32)],
        compiler_params=_cparams(("parallel", "arbitrary")),
        name="ffn",
    )(x2, gain.reshape(1, d), w_in, w_in, w_out)


PROJ_BLK = 1024
PROJ_NBLK = 7
PROJ_B0 = 3
PROJ_NMAIN = 5
PROJ_PARTS = 4
_NORM_BLOCKS = (0, 1, 3, 4)


def _proj_kernel(x_ref, g_ref, w_ref, widx_ref, hg_ref, o_ref, vt_ref, qit_ref, idx_ref, h_ref):
    j = pl.program_id(1)

    @pl.when(j == 0)
    def _():
        x = x_ref[...]
        ms = jnp.mean(x * x, axis=-1, keepdims=True)
        h_ref[...] = (x * lax.rsqrt(ms + EPS) * g_ref[...]).astype(BF16)

    rows = h_ref.shape[0] // PROJ_PARTS
    parts = [slice(part * rows, (part + 1) * rows) for part in range(PROJ_PARTS)]
    project = lambda rs: jnp.dot(h_ref[rs, :], w_ref[...], preferred_element_type=F32)

    @pl.when(j < PROJ_NMAIN)
    def _():
        is_norm = functools.reduce(jnp.logical_or, [j == b for b in _NORM_BLOCKS])
        for rs in parts:
            y = project(rs)
            for hd in range(PROJ_BLK // HEAD_DIM):
                sl = slice(hd * HEAD_DIM, (hd + 1) * HEAD_DIM)
                yh = y[:, sl]
                ms = jnp.mean(yh * yh, axis=-1, keepdims=True)
                inv = jnp.where(is_norm, lax.rsqrt(ms + EPS), 1.0)
                o_ref[rs, sl] = (yh * inv * hg_ref[0, :, sl]).astype(BF16)

    @pl.when(j == PROJ_NMAIN)
    def _():
        for rs in parts:
            chunk, off = divmod(rs.start, DSA_TK)
            vt_ref[chunk, :, off:off + rows] = project(rs).astype(BF16).T

    @pl.when(j == PROJ_NMAIN + 1)
    def _():
        for rs in parts:
            qit_ref[0, :, rs] = project(rs).astype(BF16).T
        idx_ref[...] = jnp.dot(h_ref[...], widx_ref[...], preferred_element_type=F32)


def _proj(x2, s, gain, w_main, w_idx, head_gain, tm=1024):
    m, d = x2.shape
    per_seq = s // tm
    return pl.pallas_call(
        _proj_kernel,
        grid=(m // tm, PROJ_NBLK),
        in_specs=[
            pl.BlockSpec((tm, d), lambda i, j: (i, 0)),
            pl.BlockSpec((1, d), lambda i, j: (0, 0)),
            pl.BlockSpec((d, PROJ_BLK), lambda i, j: (0, j)),
            pl.BlockSpec((d, IDX_PAD), lambda i, j: (0, 0)),
            pl.BlockSpec((1, 1, PROJ_BLK), lambda i, j: (j, 0, 0)),
        ],
        out_specs=[
            pl.BlockSpec((tm, PROJ_BLK), lambda i, j: (i, jnp.minimum(j, PROJ_NMAIN - 1))),
            pl.BlockSpec((tm // DSA_TK, PROJ_BLK, DSA_TK), lambda i, j: (i, 0, 0)),
            pl.BlockSpec((1, PROJ_BLK, tm), lambda i, j: (i // per_seq, 0, i % per_seq)),
            pl.BlockSpec((tm, IDX_PAD), lambda i, j: (i, 0)),
        ],
        out_shape=[
            jax.ShapeDtypeStruct((m, PROJ_NMAIN * PROJ_BLK), BF16),
            jax.ShapeDtypeStruct((m // DSA_TK, PROJ_BLK, DSA_TK), BF16),
            jax.ShapeDtypeStruct((m // s, PROJ_BLK, s), BF16),
            jax.ShapeDtypeStruct((m, IDX_PAD), F32),
        ],
        scratch_shapes=[pltpu.VMEM((tm, d), BF16)],
        compiler_params=_cparams(("parallel", "arbitrary")),
        name="proj",
    )(x2, gain.reshape(1, d), w_main, w_idx, head_gain)


_NT = (((1,), (1,)), ((), ()))


DIL_SPAN = DIL_BLK * max(DILATIONS)
DIL_G = 2
DIL_UNITS = DIL_SPAN // DIL_BLK
DIL_UNROLL = 8


def _dilated_kernel(q_ref, kp_ref, kc_ref, vp_ref, vc_ref, b_ref, o_ref, qf, kf, vf, ob, lse_ref):
    n = pl.program_id(1)
    h0 = pl.program_id(2) * DIL_G
    for g in range(DIL_G):
        sl = slice(g * HEAD_DIM, (g + 1) * HEAD_DIM)
        qf[g] = q_ref[0, :, sl].astype(F32)
        kf[g, :DIL_SPAN] = kp_ref[0, :, sl].astype(F32)
        kf[g, DIL_SPAN:] = kc_ref[0, :, sl].astype(F32)
        vf[g, :DIL_SPAN] = vp_ref[0, :, sl].astype(F32)
        vf[g, DIL_SPAN:] = vc_ref[0, :, sl].astype(F32)
    i = lax.broadcasted_iota(I32, (DIL_BLK, 2 * DIL_BLK), 0)
    j = lax.broadcasted_iota(I32, (DIL_BLK, 2 * DIL_BLK), 1)
    band = jnp.logical_and(j >= i, j <= i + DIL_BLK)
    first = j >= DIL_BLK

    for br, dil in enumerate(DILATIONS):
        span = DIL_BLK * dil

        def unit(idx, carry, br=br, dil=dil, span=span):
            blk = idx // dil
            base = blk * span + idx % dil
            valid = jnp.logical_and(band, jnp.logical_or(first, n * (DIL_SPAN // span) + blk > 0))
            qrows = pl.ds(base, DIL_BLK, stride=dil)
            krows = pl.ds(DIL_SPAN + base - span, 2 * DIL_BLK, stride=dil)
            for g in range(DIL_G):
                q = qf[g, qrows, :].astype(BF16)
                k = kf[g, krows, :].astype(BF16)
                v = vf[g, krows, :].astype(BF16)
                s = lax.dot_general(q, k, _NT, preferred_element_type=F32) + b_ref[br, h0 + g]
                s = jnp.where(valid, s, -jnp.inf)
                mx = jnp.max(s, axis=-1, keepdims=True)
                p = jnp.exp(s - mx)
                den = jnp.sum(p, axis=-1, keepdims=True)
                ob[br, g, qrows, :] = jnp.dot(p.astype(BF16), v, preferred_element_type=F32) / den
                lse_ref[br, g, qrows, :] = jnp.broadcast_to(mx + jnp.log(den), (DIL_BLK, HEAD_DIM))
            return carry

        lax.fori_loop(0, DIL_UNITS, unit, 0, unroll=DIL_UNROLL)

    for g in range(DIL_G):
        l1, l2, l3 = lse_ref[0, g], lse_ref[1, g], lse_ref[2, g]
        mx = jnp.maximum(jnp.maximum(l1, l2), l3)
        e1, e2, e3 = jnp.exp(l1 - mx), jnp.exp(l2 - mx), jnp.exp(l3 - mx)
        mix = (e1 * ob[0, g] + e2 * ob[1, g] + e3 * ob[2, g]) / (e1 + e2 + e3)
        o_ref[0, :, g * HEAD_DIM:(g + 1) * HEAD_DIM] = mix.astype(BF16)


def _dilated(main, bias):
    bsz, s, _ = main.shape
    gw = DIL_G * HEAD_DIM
    ng = WIDTH_A // gw
    cur = lambda c: pl.BlockSpec((1, DIL_SPAN, gw), lambda b, n, g: (b, n, c * ng + g))
    prev = lambda c: pl.BlockSpec((1, DIL_SPAN, gw), lambda b, n, g: (b, jnp.maximum(n - 1, 0), c * ng + g))
    out = pl.pallas_call(
        _dilated_kernel,
        grid=(bsz, s // DIL_SPAN, ng),
        in_specs=[cur(0), prev(1), cur(1), prev(2), cur(2),
                  pl.BlockSpec(bias.shape, lambda b, n, g: (0, 0, 0, 0), pipeline_mode=pl.Buffered(1))],
        out_specs=pl.BlockSpec((1, DIL_SPAN, gw), lambda b, n, g: (b, n, g)),
        out_shape=jax.ShapeDtypeStruct((bsz, s, WIDTH_A), BF16),
        scratch_shapes=[pltpu.VMEM((DIL_G, DIL_SPAN, HEAD_DIM), F32),
                        pltpu.VMEM((DIL_G, 2 * DIL_SPAN, HEAD_DIM), F32),
                        pltpu.VMEM((DIL_G, 2 * DIL_SPAN, HEAD_DIM), F32),
                        pltpu.VMEM((len(DILATIONS), DIL_G, DIL_SPAN, HEAD_DIM), F32),
                        pltpu.VMEM((len(DILATIONS), DIL_G, DIL_SPAN, HEAD_DIM), F32)],
        compiler_params=_cparams(("parallel", "parallel", "parallel")),
        name="dilated",
    )(main, main, main, main, main, bias)
    return out.reshape(bsz * s, WIDTH_A)


def _dsa_schedule(s, tq, tk):
    qs, ks = [], []
    for i in range(s // tq):
        last = (i * tq + tq - 1) // tk
        for k in range(last + 1):
            qs.append(i)
            ks.append(k)
    return np.asarray(qs, np.int32), np.asarray(ks, np.int32)


def _sortable(bits):
    return bits ^ (lax.shift_right_arithmetic(bits, 31) & 0x7FFFFFFF)


def _bit_transpose(words):
    a = list(words)
    mask, j = 0x0000FFFF, 16
    while j:
        k = 0
        while k < WORD:
            t = (a[k] ^ lax.shift_right_logical(a[k + j], j)) & mask
            a[k] = a[k] ^ t
            a[k + j] = a[k + j] ^ lax.shift_left(t, j)
            k = (k + j + 1) & ~j
        j >>= 1
        mask = (mask ^ (mask << j)) & 0xFFFFFFFF
    return a


def _dsa_score_kernel(qtab, ktab, qit_ref, ki_ref, w_ref, sc_ref, thr_ref,
                      planes_ref, alive_ref, stage_ref):
    step = pl.program_id(1)
    qb = qtab[step]
    kc = ktab[step]
    w = w_ref[0] * IDX_SCALE
    srow = lax.broadcasted_iota(I32, (SCORE_ROWS, DSA_TQ), 0)
    tcol = lax.broadcasted_iota(I32, (SCORE_ROWS, DSA_TQ), 1)
    row0 = pl.multiple_of(kc * CHUNK_WORDS, CHUNK_WORDS)
    group = WORD * 8
    for g in range(SC_TK // group):
        for r0 in range(g * group, (g + 1) * group, SCORE_ROWS):
            kidx = ki_ref[0, r0:r0 + SCORE_ROWS, :]
            acc = jnp.zeros((SCORE_ROWS, DSA_TQ), F32)
            for h in range(N_IDX_HEADS):
                d = jnp.dot(kidx, qit_ref[0, h * IDX_DIM:(h + 1) * IDX_DIM, :],
                            preferred_element_type=F32)
                acc = acc + jnp.maximum(d, 0.0) * w[h:h + 1, :]
            causal = srow - tcol <= qb * DSA_TQ - kc * SC_TK - r0
            acc = jnp.where(causal, acc, -jnp.inf)
            sc_ref[r0:r0 + SCORE_ROWS, :] = acc
            stage_ref[r0:r0 + SCORE_ROWS, :] = _sortable(lax.bitcast_convert_type(acc, I32)) ^ INT_MIN
        for lanes in (slice(c, c + 128) for c in range(0, DSA_TQ, 128)):
            planes = _bit_transpose([stage_ref[g * group + 8 * n:g * group + 8 * (n + 1), lanes]
                                     for n in range(WORD)])
            for n in range(WORD):
                planes_ref[n, pl.ds(pl.multiple_of(row0 + 8 * g, 8), 8), lanes] = planes[n]

    last = (qb * DSA_TQ + DSA_TQ - 1) // SC_TK

    @pl.when(kc == last)
    def _():
        nwords = alive_ref.shape[0]
        rb = min(128, nwords)
        nrows = (last + 1) * CHUNK_WORDS
        nblk = (nrows + rb - 1) // rb
        tpos = qb * DSA_TQ + lax.broadcasted_iota(I32, (1, DSA_TQ), 1)
        want0 = jnp.minimum(tpos + 1, TOPK_MAX)
        rows = lambda blk: pl.ds(pl.multiple_of(blk * rb, rb), rb)

        def hits(alive, n, blk):
            hit = lax.population_count(alive & planes_ref[n, rows(blk), :])
            return hit.reshape(rb // 8, 8, DSA_TQ).sum(axis=0)

        def init(blk, cnt):
            ridx = blk * rb + lax.broadcasted_iota(I32, (rb, DSA_TQ), 0)
            alive = jnp.where(ridx < nrows, -1, 0)
            alive_ref[rows(blk), :] = alive
            return cnt + hits(alive, 0, blk)
        zero = jnp.zeros((8, DSA_TQ), I32)
        cnt0 = lax.fori_loop(0, nblk, init, zero)

        def bit_step(n, carry):
            r, want, cnt8 = carry
            cnt = cnt8.sum(axis=0, keepdims=True)
            take = cnt >= want
            r = r | jnp.where(take, lax.shift_left(jnp.int32(1), 31 - n), 0)
            want = jnp.where(take, want, want - cnt)
            flip = jnp.where(take, 0, -1)
            nxt = jnp.minimum(n + 1, WORD - 1)

            def sweep(blk, acc):
                alive = alive_ref[rows(blk), :] & (planes_ref[n, rows(blk), :] ^ flip)
                alive_ref[rows(blk), :] = alive
                return acc + hits(alive, nxt, blk)
            return r, want, lax.fori_loop(0, nblk, sweep, zero)

        r, want, _ = lax.fori_loop(0, WORD, bit_step, (jnp.zeros((1, DSA_TQ), I32), want0, cnt0))
        thr_ref[0, 0:1, :] = lax.bitcast_convert_type(_sortable(r ^ INT_MIN), F32)
        thr_ref[0, 1:2, :] = jnp.full((1, DSA_TQ), nwords * WORD, F32)

        def live(blk, acc):
            hit = lax.population_count(alive_ref[rows(blk), :])
            return acc + hit.reshape(rb // 8, 8, DSA_TQ).sum(axis=0)
        tied = lax.fori_loop(0, nblk, live, zero).sum(axis=0, keepdims=True)

        @pl.when(jnp.max(tied - want) > 0)
        def _():
            blk_bits = max(1, (nwords // 8 - 1).bit_length())
            lane_patterns = (-0x10000, -0xFF0100, -0xF0F0F10, -0x33333334, -0x55555556)
            nsec = blk_bits + len(lane_patterns) + 3

            def sec_plane(k, blk):
                ridx = blk * rb + lax.broadcasted_iota(I32, (rb, 1), 0)
                if k < blk_bits:
                    bit = 3 + blk_bits - 1 - k
                elif k < blk_bits + len(lane_patterns):
                    return jnp.full((rb, 1), lane_patterns[k - blk_bits], I32)
                else:
                    bit = nsec - 1 - k
                return jnp.where((lax.shift_right_logical(ridx, bit) & 1) == 0, -1, 0)

            def sec_count(k):
                def body(blk, acc):
                    hit = lax.population_count(alive_ref[rows(blk), :] & sec_plane(k, blk))
                    return acc + hit.reshape(rb // 8, 8, DSA_TQ).sum(axis=0)
                return lax.fori_loop(0, nblk, body, zero).sum(axis=0, keepdims=True)

            left = want
            code = jnp.zeros((1, DSA_TQ), I32)
            for k in range(nsec):
                cnt = sec_count(k)
                take = cnt >= left
                code = code | jnp.where(take, 1 << (nsec - 1 - k), 0)
                left = jnp.where(take, left, left - cnt)
                flip = jnp.where(take, 0, -1)

                def update(blk, c, k=k, flip=flip):
                    alive_ref[rows(blk), :] = alive_ref[rows(blk), :] & (sec_plane(k, blk) ^ flip)
                    return c
                lax.fori_loop(0, nblk, update, 0)
            inv = ~code
            cutoff = ((lax.shift_right_logical(inv, 8) & ((1 << blk_bits) - 1)) * (8 * WORD)
                      + 8 * ((code >> 3) & (WORD - 1) ^ (WORD - 1)) + (inv & 7))
            thr_ref[0, 1:2, :] = cutoff.astype(F32)


def _dsa_attn_kernel(qtab, ktab, sc_ref, thr_ref, q_ref, k_ref, vt_ref, b_ref, bound_ref, o_ref,
                     acc_ref, m_ref, l_ref, s_ref, mask_ref, cm_ref):
    step = pl.program_id(1)
    qb = qtab[step]
    kc = ktab[step]

    @pl.when(kc == 0)
    def _():
        acc_ref[...] = jnp.zeros_like(acc_ref)
        m_ref[...] = jnp.full_like(m_ref, NEG_BIG)
        l_ref[...] = jnp.zeros_like(l_ref)

    bound = bound_ref[0]
    bounded = bound <= DSA_SAFE_BOUND

    def set_mask(kept):
        key = (kc * DSA_TK + lax.broadcasted_iota(I32, (DSA_TK, DSA_TQ), 0)).astype(F32)
        for b in range(ATT_QT):
            cols = slice(b * DSA_TQ, (b + 1) * DSA_TQ)
            thr = thr_ref[0, 0:1, cols]
            tied = jnp.logical_and(sc_ref[b] == thr, key <= thr_ref[0, 1:2, cols])
            mask_ref[:, cols] = jnp.where(jnp.logical_or(sc_ref[b] > thr, tied), kept, -jnp.inf)

    tiles = [[jnp.clip((qb * ATT_QT + b) - (kc * DSA_HALVES + a), -1, DSA_NEAR) + 1
              for b in range(ATT_QT)] for a in range(DSA_HALVES)]
    far = qb * ATT_QT - (kc * DSA_HALVES + DSA_HALVES - 1) >= DSA_NEAR
    far_bias = lambda h: b_ref[h, DSA_NEAR + 1, 0:1, 0:1]

    def logits(h):
        sl = slice(h * HEAD_DIM, (h + 1) * HEAD_DIM)
        s = lax.dot_general(k_ref[0, :, sl], q_ref[0, :, sl], _NT, preferred_element_type=F32)
        return s + mask_ref[...]

    def near_bias(h):
        return jnp.concatenate(
            [jnp.concatenate([b_ref[h, tiles[a][b]] for b in range(ATT_QT)], axis=1)
             for a in range(DSA_HALVES)], axis=0)

    def values(h):
        return vt_ref[h * HEAD_DIM:(h + 1) * HEAD_DIM, :]

    @pl.when(bounded)
    def _():
        def raw_logits(h):
            sl = slice(h * HEAD_DIM, (h + 1) * HEAD_DIM)
            s_ref[h] = lax.dot_general(k_ref[0, :, sl], q_ref[0, :, sl], _NT,
                                       preferred_element_type=F32)

        def heads(weights):
            for h in range(DSA_LOOKAHEAD):
                raw_logits(h)
            set_mask(-bound)
            for h in range(N_HEADS_B):
                if h + DSA_LOOKAHEAD < N_HEADS_B:
                    raw_logits(h + DSA_LOOKAHEAD)
                p, gain = weights(h)
                l_ref[h] += gain * jnp.sum(p, axis=0, keepdims=True)
                acc_ref[h] += gain * jnp.dot(values(h), p.astype(BF16), preferred_element_type=F32)

        @pl.when(far)
        def _():
            heads(lambda h: (jnp.exp2(s_ref[h] + mask_ref[...]), jnp.exp2(far_bias(h))))

        @pl.when(jnp.logical_not(far))
        def _():
            heads(lambda h: (jnp.exp2(s_ref[h] + mask_ref[...] + near_bias(h)), 1.0))

    @pl.when(jnp.logical_not(bounded))
    def _():
        set_mask(0.0)

        @pl.when(far)
        def _():
            for h in range(N_HEADS_B):
                s = logits(h)
                s_ref[h] = s
                cm_ref[h] = jnp.max(s, axis=0, keepdims=True) + far_bias(h)

        @pl.when(jnp.logical_not(far))
        def _():
            for h in range(N_HEADS_B):
                s = logits(h) + near_bias(h)
                s_ref[h] = s
                cm_ref[h] = jnp.max(s, axis=0, keepdims=True)

        for h in range(N_HEADS_B):
            m_old = m_ref[h]
            m_new = jnp.maximum(m_old, cm_ref[h])
            alpha = jnp.exp2(m_old - m_new)
            m_ref[h] = m_new
            shift = jnp.where(far, m_new - far_bias(h), m_new)
            p = jnp.exp2(s_ref[h] - shift)
            l_ref[h] = alpha * l_ref[h] + jnp.sum(p, axis=0, keepdims=True)
            pv = jnp.dot(values(h), p.astype(BF16), preferred_element_type=F32)
            acc_ref[h] = alpha * acc_ref[h] + pv

    last = (qb * ATT_TQ + ATT_TQ - 1) // DSA_TK

    @pl.when(kc == last)
    def _():
        for h in range(N_HEADS_B):
            o_ref[0, :, h * HEAD_DIM:(h + 1) * HEAD_DIM] = (acc_ref[h] / l_ref[h]).T.astype(BF16)


def _dsa_scores(qi_t, ki, wi_t):
    bsz, _, s = qi_t.shape
    qs, ks = _dsa_schedule(s, DSA_TQ, SC_TK)
    npairs = len(qs)
    return pl.pallas_call(
        _dsa_score_kernel,
        grid_spec=pltpu.PrefetchScalarGridSpec(
            num_scalar_prefetch=2,
            grid=(bsz, npairs),
            in_specs=[
                pl.BlockSpec((1, N_IDX_HEADS * IDX_DIM, DSA_TQ), lambda b, t, qt, kt: (b, 0, qt[t])),
                pl.BlockSpec((1, SC_TK, IDX_DIM), lambda b, t, qt, kt: (b, kt[t], 0)),
                pl.BlockSpec((1, N_IDX_HEADS, DSA_TQ), lambda b, t, qt, kt: (b, 0, qt[t])),
            ],
            out_specs=[
                pl.BlockSpec((None, None, None, SC_TK, DSA_TQ),
                             lambda b, t, qt, kt: (b, kt[t], qt[t], 0, 0)),
                pl.BlockSpec((1, 2, DSA_TQ), lambda b, t, qt, kt: (b, 0, qt[t])),
            ],
            scratch_shapes=[pltpu.VMEM((WORD, s // WORD, DSA_TQ), I32),
                            pltpu.VMEM((s // WORD, DSA_TQ), I32),
                            pltpu.VMEM((SC_TK, DSA_TQ), I32)],
        ),
        out_shape=[jax.ShapeDtypeStruct((bsz, s // SC_TK, s // DSA_TQ, SC_TK, DSA_TQ), F32),
                   jax.ShapeDtypeStruct((bsz, 2, s), F32)],
        compiler_params=_cparams(("arbitrary", "arbitrary")),
        name="dsa_scores",
    )(qs, ks, qi_t, ki, wi_t)


def _dsa(main, vt, qi_t, ki, wi_t, bias, logit_bound):
    bsz, s, _ = main.shape
    w = WIDTH_B
    scores, thr = _dsa_scores(qi_t, ki, wi_t)
    ntile = bias.shape[1]
    qs, ks = _dsa_schedule(s, ATT_TQ, DSA_TK)
    out = pl.pallas_call(
        _dsa_attn_kernel,
        grid_spec=pltpu.PrefetchScalarGridSpec(
            num_scalar_prefetch=2,
            grid=(bsz, len(qs)),
            in_specs=[
                pl.BlockSpec((None, None, ATT_QT, DSA_TK, DSA_TQ),
                             lambda b, t, qt, kt: (b, kt[t] // SC_PER_ATT, qt[t], kt[t] % SC_PER_ATT, 0)),
                pl.BlockSpec((1, 2, ATT_TQ), lambda b, t, qt, kt: (b, 0, qt[t])),
                pl.BlockSpec((1, ATT_TQ, w), lambda b, t, qt, kt: (b, qt[t], PROJ_B0)),
                pl.BlockSpec((1, DSA_TK, w), lambda b, t, qt, kt: (b, kt[t], PROJ_B0 + 1)),
                pl.BlockSpec((None, None, w, DSA_TK), lambda b, t, qt, kt: (b, kt[t], 0, 0)),
                pl.BlockSpec((N_HEADS_B, ntile, DSA_TQ, DSA_TQ), lambda b, t, qt, kt: (0, 0, 0, 0),
                             pipeline_mode=pl.Buffered(1)),
                pl.BlockSpec(memory_space=pltpu.SMEM),
            ],
            out_specs=pl.BlockSpec((1, ATT_TQ, w), lambda b, t, qt, kt: (b, qt[t], 0)),
            scratch_shapes=[pltpu.VMEM((N_HEADS_B, HEAD_DIM, ATT_TQ), F32),
                            pltpu.VMEM((N_HEADS_B, 1, ATT_TQ), F32),
                            pltpu.VMEM((N_HEADS_B, 1, ATT_TQ), F32),
                            pltpu.VMEM((N_HEADS_B, DSA_TK, ATT_TQ), F32),
                            pltpu.VMEM((DSA_TK, ATT_TQ), F32),
                            pltpu.VMEM((N_HEADS_B, 1, ATT_TQ), F32)],
        ),
        out_shape=jax.ShapeDtypeStruct((bsz, s, w), BF16),
        compiler_params=_cparams(("arbitrary", "arbitrary")),
        name="dsa_attn",
    )(qs, ks, scores, thr, main, main, vt, bias, logit_bound)
    return out.reshape(bsz * s, w)


def _out_kernel(x_ref, ma_ref, mb_ref, wa_ref, wb_ref, o_ref):
    o_ref[...] = (x_ref[...]
                  + jnp.dot(ma_ref[...], wa_ref[...], preferred_element_type=F32)
                  + jnp.dot(mb_ref[...], wb_ref[...], preferred_element_type=F32))


def _out_proj(x2, mix_a, mix_b, w_out, tm=512):
    m, d = x2.shape
    row = lambda width: pl.BlockSpec((tm, width), lambda i: (i, 0))
    return pl.pallas_call(
        _out_kernel,
        grid=(m // tm,),
        in_specs=[row(d), row(WIDTH_A), row(WIDTH_B),
                  pl.BlockSpec((WIDTH_A, d), lambda i: (0, 0)),
                  pl.BlockSpec((WIDTH_B, d), lambda i: (1, 0))],
        out_specs=row(d),
        out_shape=jax.ShapeDtypeStruct((m, d), F32),
        compiler_params=_cparams(("parallel",)),
        name="out_proj",
    )(x2, mix_a, mix_b, w_out, w_out)


def kernel(x, rel_bias, norm_ffn1, w_ffn1_in, w_ffn1_out, norm_mix, w_in, q_norm_a, k_norm_a,
           q_norm_b, k_norm_b, w_out, norm_ffn2, w_ffn2_in, w_ffn2_out):
    bsz, s, d = x.shape
    m = bsz * s
    depth = norm_ffn1.shape[0]
    scale = HEAD_DIM ** -0.5
    n_main = PROJ_NBLK * PROJ_BLK
    dil_bias, dsa_bias = _bias_tiles(rel_bias)
    x2 = x.reshape(m, d)
    ones = jnp.ones((PROJ_BLK,), F32)
    tile8 = lambda g: jnp.tile(g, PROJ_BLK // HEAD_DIM)
    for l in range(depth):
        x2 = _ffn(x2, norm_ffn1[l], w_ffn1_in[l].astype(BF16), w_ffn1_out[l].astype(BF16))

        w_main = w_in[l, :, :n_main].astype(BF16)
        w_idx = jnp.pad(w_in[l, :, n_main:], ((0, 0), (0, IDX_PAD - IDX_DIM - N_IDX_HEADS))).astype(BF16)
        head_gain = jnp.stack([tile8(q_norm_a[l] * scale), tile8(k_norm_a[l]), ones,
                               tile8(q_norm_b[l] * (scale * LOG2E)), tile8(k_norm_b[l]), ones, ones])
        main, vt, qi_t, idx = _proj(x2, s, norm_mix[l], w_main, w_idx,
                                    head_gain.reshape(PROJ_NBLK, 1, PROJ_BLK))
        main = main.reshape(bsz, s, PROJ_NMAIN * PROJ_BLK)
        vt = vt.reshape(bsz, s // DSA_TK, WIDTH_B, DSA_TK)
        idx = idx.reshape(bsz, s, IDX_PAD)
        ki = idx[:, :, :IDX_DIM].astype(BF16)
        wi_t = jnp.swapaxes(idx[:, :, IDX_DIM:IDX_DIM + N_IDX_HEADS], 1, 2)

        mix_a = _dilated(main, dil_bias)
        logit_bound = (1.02 * HEAD_DIM * jnp.max(jnp.abs(q_norm_b[l] * (scale * LOG2E)))
                       * jnp.max(jnp.abs(k_norm_b[l]))
                       + LOG2E * jnp.max(jnp.abs(rel_bias[:, N_HEADS_A:]))).reshape(1)
        mix_b = _dsa(main, vt, qi_t, ki, wi_t, dsa_bias, logit_bound)

        x2 = _out_proj(x2, mix_a, mix_b, w_out[l].astype(BF16))
        x2 = _ffn(x2, norm_ffn2[l], w_ffn2_in[l].astype(BF16), w_ffn2_out[l].astype(BF16))
    return x2.reshape(bsz, s, d)
```

```python
import functools
import math

import jax
import jax.numpy as jnp
import numpy as np
from jax import lax
from jax.experimental import pallas as pl
from jax.experimental.pallas import tpu as pltpu

F32 = jnp.float32
BF16 = jnp.bfloat16
I32 = jnp.int32

HEAD_DIM = 128
N_HEADS_A = 8
N_HEADS_B = 8
WIDTH_A = N_HEADS_A * HEAD_DIM
WIDTH_B = N_HEADS_B * HEAD_DIM
DILATIONS = (1, 4, 16)
DIL_BLK = 128
N_IDX_HEADS = 16
IDX_DIM = 64
TOPK_MAX = 256
N_BUCKETS = 32
REL_MAX_DIST = 2048
EPS = 1e-6
IDX_PAD = 128
IDX_SCALE = (IDX_DIM ** -0.5) * (N_IDX_HEADS ** -0.5)
LOG2E = math.log2(math.e)

VMEM_LIMIT = 56 * 1024 * 1024

DSA_TQ = 256
DSA_TK = 512
DSA_HALVES = DSA_TK // DSA_TQ
ATT_TQ = 512
ATT_QT = ATT_TQ // DSA_TQ
WORD = 32
SC_TK = 1024
SC_PER_ATT = SC_TK // DSA_TK
CHUNK_WORDS = SC_TK // WORD
SCORE_ROWS = 64
NEG_BIG = -1e30
DSA_SAFE_BOUND = 40.0
DSA_LOOKAHEAD = 2
INT_MIN = -2 ** 31


def _near_tiles():
    exact = N_BUCKETS // 2
    j = 1
    while True:
        d = j * DSA_TQ - (DSA_TQ - 1)
        b = exact + int(math.log(d / exact) / math.log(REL_MAX_DIST / exact) * (N_BUCKETS - exact) - 0.02)
        if d >= exact and b >= N_BUCKETS - 1:
            return j
        j += 1


DSA_NEAR = _near_tiles()


def _cparams(sem):
    return pltpu.CompilerParams(dimension_semantics=sem, vmem_limit_bytes=VMEM_LIMIT)


def _rel_bucket(dist):
    exact = N_BUCKETS // 2
    df = jnp.maximum(dist, 1).astype(F32)
    large = exact + (jnp.log(df / exact) / math.log(REL_MAX_DIST / exact)
                     * (N_BUCKETS - exact)).astype(I32)
    large = jnp.minimum(large, N_BUCKETS - 1)
    return jnp.where(dist < exact, dist, large)


def _lookup(tab_ref, bucket, col):
    val = jnp.zeros(bucket.shape, F32)
    for b in range(N_BUCKETS):
        val = jnp.where(bucket == b, tab_ref[b, col], val)
    return val


def _dil_bias_kernel(tab_ref, o_ref):
    dil = jnp.left_shift(1, 2 * pl.program_id(0))
    i = lax.broadcasted_iota(I32, (DIL_BLK, 2 * DIL_BLK), 0)
    j = lax.broadcasted_iota(I32, (DIL_BLK, 2 * DIL_BLK), 1)
    bucket = _rel_bucket(jnp.clip(DIL_BLK + i - j, 0, DIL_BLK) * dil)
    for h in range(N_HEADS_A):
        o_ref[0, h] = _lookup(tab_ref, bucket, h)


def _dsa_bias_kernel(tab_ref, o_ref):
    jj = pl.program_id(0)
    s = lax.broadcasted_iota(I32, (DSA_TQ, DSA_TQ), 0)
    t = lax.broadcasted_iota(I32, (DSA_TQ, DSA_TQ), 1)
    bucket = _rel_bucket(jnp.maximum((jj - 1) * DSA_TQ + t - s, 0))
    for h in range(N_HEADS_B):
        o_ref[h, 0] = _lookup(tab_ref, bucket, N_HEADS_A + h) * LOG2E


def _bias_tiles(rel_bias):
    smem = pl.BlockSpec(memory_space=pltpu.SMEM)
    dil = pl.pallas_call(
        _dil_bias_kernel,
        grid=(len(DILATIONS),),
        in_specs=[smem],
        out_specs=pl.BlockSpec((1, N_HEADS_A, DIL_BLK, 2 * DIL_BLK), lambda g: (g, 0, 0, 0)),
        out_shape=jax.ShapeDtypeStruct((len(DILATIONS), N_HEADS_A, DIL_BLK, 2 * DIL_BLK), F32),
        name="dil_bias",
    )(rel_bias)
    ntile = DSA_NEAR + 2
    dsa = pl.pallas_call(
        _dsa_bias_kernel,
        grid=(ntile,),
        in_specs=[smem],
        out_specs=pl.BlockSpec((N_HEADS_B, 1, DSA_TQ, DSA_TQ), lambda g: (0, g, 0, 0)),
        out_shape=jax.ShapeDtypeStruct((N_HEADS_B, ntile, DSA_TQ, DSA_TQ), F32),
        name="dsa_bias",
    )(rel_bias)
    return dil, dsa


def _ffn_kernel(x_ref, g_ref, wg_ref, wu_ref, wo_ref, o_ref, h_ref, acc_ref):
    f = pl.program_id(1)

    @pl.when(f == 0)
    def _():
        x = x_ref[...]
        ms = jnp.mean(x * x, axis=-1, keepdims=True)
        h_ref[...] = (x * lax.rsqrt(ms + EPS) * g_ref[...]).astype(BF16)
        acc_ref[...] = jnp.zeros_like(acc_ref)

    h = h_ref[...]
    gate = jnp.dot(h, wg_ref[...], preferred_element_type=F32)
    up = jnp.dot(h, wu_ref[...], preferred_element_type=F32)
    act = (gate * jax.nn.sigmoid(gate) * up).astype(BF16)
    acc_ref[...] += jnp.dot(act, wo_ref[...], preferred_element_type=F32)

    @pl.when(f == pl.num_programs(1) - 1)
    def _():
        o_ref[...] = x_ref[...] + 0.5 * acc_ref[...]


def _ffn(x2, gain, w_in, w_out, tm=512, tf=512):
    m, d = x2.shape
    d_ff = w_out.shape[0]
    nf = d_ff // tf
    return pl.pallas_call(
        _ffn_kernel,
        grid=(m // tm, nf),
        in_specs=[
            pl.BlockSpec((tm, d), lambda i, f: (i, 0)),
            pl.BlockSpec((1, d), lambda i, f: (0, 0)),
            pl.BlockSpec((d, tf), lambda i, f: (0, f)),
            pl.BlockSpec((d, tf), lambda i, f: (0, f + nf)),
            pl.BlockSpec((tf, d), lambda i, f: (f, 0)),
        ],
        out_specs=pl.BlockSpec((tm, d), lambda i, f: (i, 0)),
        out_shape=jax.ShapeDtypeStruct((m, d), F32),
        scratch_shapes=[pltpu.VMEM((tm, d), BF16), pltpu.VMEM((tm, d), F32)],
        compiler_params=_cparams(("parallel", "arbitrary")),
        name="ffn",
    )(x2, gain.reshape(1, d), w_in, w_in, w_out)


PROJ_BLK = 1024
PROJ_NBLK = 7
PROJ_B0 = 3
PROJ_NMAIN = 5
PROJ_PARTS = 4
_NORM_BLOCKS = (0, 1, 3, 4)


def _proj_kernel(x_ref, g_ref, w_ref, widx_ref, hg_ref, o_ref, vt_ref, qit_ref, idx_ref, h_ref):
    j = pl.program_id(1)

    @pl.when(j == 0)
    def _():
        x = x_ref[...]
        ms = jnp.mean(x * x, axis=-1, keepdims=True)
        h_ref[...] = (x * lax.rsqrt(ms + EPS) * g_ref[...]).astype(BF16)

    rows = h_ref.shape[0] // PROJ_PARTS
    parts = [slice(part * rows, (part + 1) * rows) for part in range(PROJ_PARTS)]
    project = lambda rs: jnp.dot(h_ref[rs, :], w_ref[...], preferred_element_type=F32)

    @pl.when(j < PROJ_NMAIN)
    def _():
        is_norm = functools.reduce(jnp.logical_or, [j == b for b in _NORM_BLOCKS])
        for rs in parts:
            y = project(rs)
            for hd in range(PROJ_BLK // HEAD_DIM):
                sl = slice(hd * HEAD_DIM, (hd + 1) * HEAD_DIM)
                yh = y[:, sl]
                ms = jnp.mean(yh * yh, axis=-1, keepdims=True)
                inv = jnp.where(is_norm, lax.rsqrt(ms + EPS), 1.0)
                o_ref[rs, sl] = (yh * inv * hg_ref[0, :, sl]).astype(BF16)

    @pl.when(j == PROJ_NMAIN)
    def _():
        for rs in parts:
            chunk, off = divmod(rs.start, DSA_TK)
            vt_ref[chunk, :, off:off + rows] = project(rs).astype(BF16).T

    @pl.when(j == PROJ_NMAIN + 1)
    def _():
        for rs in parts:
            qit_ref[0, :, rs] = project(rs).astype(BF16).T
        idx_ref[...] = jnp.dot(h_ref[...], widx_ref[...], preferred_element_type=F32)


def _proj(x2, s, gain, w_main, w_idx, head_gain, tm=1024):
    m, d = x2.shape
    per_seq = s // tm
    return pl.pallas_call(
        _proj_kernel,
        grid=(m // tm, PROJ_NBLK),
        in_specs=[
            pl.BlockSpec((tm, d), lambda i, j: (i, 0)),
            pl.BlockSpec((1, d), lambda i, j: (0, 0)),
            pl.BlockSpec((d, PROJ_BLK), lambda i, j: (0, j)),
            pl.BlockSpec((d, IDX_PAD), lambda i, j: (0, 0)),
            pl.BlockSpec((1, 1, PROJ_BLK), lambda i, j: (j, 0, 0)),
        ],
        out_specs=[
            pl.BlockSpec((tm, PROJ_BLK), lambda i, j: (i, jnp.minimum(j, PROJ_NMAIN - 1))),
            pl.BlockSpec((tm // DSA_TK, PROJ_BLK, DSA_TK), lambda i, j: (i, 0, 0)),
            pl.BlockSpec((1, PROJ_BLK, tm), lambda i, j: (i // per_seq, 0, i % per_seq)),
            pl.BlockSpec((tm, IDX_PAD), lambda i, j: (i, 0)),
        ],
        out_shape=[
            jax.ShapeDtypeStruct((m, PROJ_NMAIN * PROJ_BLK), BF16),
            jax.ShapeDtypeStruct((m // DSA_TK, PROJ_BLK, DSA_TK), BF16),
            jax.ShapeDtypeStruct((m // s, PROJ_BLK, s), BF16),
            jax.ShapeDtypeStruct((m, IDX_PAD), F32),
        ],
        scratch_shapes=[pltpu.VMEM((tm, d), BF16)],
        compiler_params=_cparams(("parallel", "arbitrary")),
        name="proj",
    )(x2, gain.reshape(1, d), w_main, w_idx, head_gain)


_NT = (((1,), (1,)), ((), ()))


DIL_SPAN = DIL_BLK * max(DILATIONS)
DIL_G = 2
DIL_UNITS = DIL_SPAN // DIL_BLK
DIL_UNROLL = DIL_UNITS


def _dilated_kernel(q_ref, kp_ref, kc_ref, vp_ref, vc_ref, b_ref, o_ref, qf, kf, vf, ob, lse_ref):
    n = pl.program_id(1)
    h0 = pl.program_id(2) * DIL_G
    for g in range(DIL_G):
        sl = slice(g * HEAD_DIM, (g + 1) * HEAD_DIM)
        qf[g] = q_ref[0, :, sl].astype(F32)
        kf[g, :DIL_SPAN] = kp_ref[0, :, sl].astype(F32)
        kf[g, DIL_SPAN:] = kc_ref[0, :, sl].astype(F32)
        vf[g, :DIL_SPAN] = vp_ref[0, :, sl].astype(F32)
        vf[g, DIL_SPAN:] = vc_ref[0, :, sl].astype(F32)
    i = lax.broadcasted_iota(I32, (DIL_BLK, 2 * DIL_BLK), 0)
    j = lax.broadcasted_iota(I32, (DIL_BLK, 2 * DIL_BLK), 1)
    band = jnp.logical_and(j >= i, j <= i + DIL_BLK)
    first = j >= DIL_BLK

    for br, dil in enumerate(DILATIONS):
        span = DIL_BLK * dil

        def unit(idx, carry, br=br, dil=dil, span=span):
            blk = idx // dil
            base = blk * span + idx % dil
            valid = jnp.logical_and(band, jnp.logical_or(first, n * (DIL_SPAN // span) + blk > 0))
            qrows = pl.ds(base, DIL_BLK, stride=dil)
            krows = pl.ds(DIL_SPAN + base - span, 2 * DIL_BLK, stride=dil)
            for g in range(DIL_G):
                q = qf[g, qrows, :].astype(BF16)
                k = kf[g, krows, :].astype(BF16)
                v = vf[g, krows, :].astype(BF16)
                s = lax.dot_general(q, k, _NT, preferred_element_type=F32) + b_ref[br, h0 + g]
                s = jnp.where(valid, s, -jnp.inf)
                mx = jnp.max(s, axis=-1, keepdims=True)
                p = jnp.exp(s - mx)
                den = jnp.sum(p, axis=-1, keepdims=True)
                ob[br, g, qrows, :] = jnp.dot(p.astype(BF16), v, preferred_element_type=F32) / den
                lse_ref[br, g, qrows, :] = jnp.broadcast_to(mx + jnp.log(den), (DIL_BLK, HEAD_DIM))
            return carry

        lax.fori_loop(0, DIL_UNITS, unit, 0, unroll=DIL_UNROLL)

    for g in range(DIL_G):
        l1, l2, l3 = lse_ref[0, g], lse_ref[1, g], lse_ref[2, g]
        mx = jnp.maximum(jnp.maximum(l1, l2), l3)
        e1, e2, e3 = jnp.exp(l1 - mx), jnp.exp(l2 - mx), jnp.exp(l3 - mx)
        mix = (e1 * ob[0, g] + e2 * ob[1, g] + e3 * ob[2, g]) / (e1 + e2 + e3)
        o_ref[0, :, g * HEAD_DIM:(g + 1) * HEAD_DIM] = mix.astype(BF16)


def _dilated(main, bias):
    bsz, s, _ = main.shape
    gw = DIL_G * HEAD_DIM
    ng = WIDTH_A // gw
    cur = lambda c: pl.BlockSpec((1, DIL_SPAN, gw), lambda b, n, g: (b, n, c * ng + g))
    prev = lambda c: pl.BlockSpec((1, DIL_SPAN, gw), lambda b, n, g: (b, jnp.maximum(n - 1, 0), c * ng + g))
    out = pl.pallas_call(
        _dilated_kernel,
        grid=(bsz, s // DIL_SPAN, ng),
        in_specs=[cur(0), prev(1), cur(1), prev(2), cur(2),
                  pl.BlockSpec(bias.shape, lambda b, n, g: (0, 0, 0, 0), pipeline_mode=pl.Buffered(1))],
        out_specs=pl.BlockSpec((1, DIL_SPAN, gw), lambda b, n, g: (b, n, g)),
        out_shape=jax.ShapeDtypeStruct((bsz, s, WIDTH_A), BF16),
        scratch_shapes=[pltpu.VMEM((DIL_G, DIL_SPAN, HEAD_DIM), F32),
                        pltpu.VMEM((DIL_G, 2 * DIL_SPAN, HEAD_DIM), F32),
                        pltpu.VMEM((DIL_G, 2 * DIL_SPAN, HEAD_DIM), F32),
                        pltpu.VMEM((len(DILATIONS), DIL_G, DIL_SPAN, HEAD_DIM), F32),
                        pltpu.VMEM((len(DILATIONS), DIL_G, DIL_SPAN, HEAD_DIM), F32)],
        compiler_params=_cparams(("parallel", "parallel", "parallel")),
        name="dilated",
    )(main, main, main, main, main, bias)
    return out.reshape(bsz * s, WIDTH_A)


def _dsa_schedule(s, tq, tk):
    qs, ks = [], []
    for i in range(s // tq):
        last = (i * tq + tq - 1) // tk
        for k in range(last + 1):
            qs.append(i)
            ks.append(k)
    return np.asarray(qs, np.int32), np.asarray(ks, np.int32)


def _sortable(bits):
    return bits ^ (lax.shift_right_arithmetic(bits, 31) & 0x7FFFFFFF)


def _bit_transpose(words):
    a = list(words)
    mask, j = 0x0000FFFF, 16
    while j:
        k = 0
        while k < WORD:
            t = (a[k] ^ lax.shift_right_logical(a[k + j], j)) & mask
            a[k] = a[k] ^ t
            a[k + j] = a[k + j] ^ lax.shift_left(t, j)
            k = (k + j + 1) & ~j
        j >>= 1
        mask = (mask ^ (mask << j)) & 0xFFFFFFFF
    return a


def _dsa_score_kernel(qtab, ktab, qit_ref, ki_ref, w_ref, sc_ref, thr_ref,
                      planes_ref, alive_ref, stage_ref):
    step = pl.program_id(1)
    qb = qtab[step]
    kc = ktab[step]
    w = w_ref[0] * IDX_SCALE
    srow = lax.broadcasted_iota(I32, (SCORE_ROWS, DSA_TQ), 0)
    tcol = lax.broadcasted_iota(I32, (SCORE_ROWS, DSA_TQ), 1)
    row0 = pl.multiple_of(kc * CHUNK_WORDS, CHUNK_WORDS)
    group = WORD * 8
    for g in range(SC_TK // group):
        for r0 in range(g * group, (g + 1) * group, SCORE_ROWS):
            kidx = ki_ref[0, r0:r0 + SCORE_ROWS, :]
            acc = jnp.zeros((SCORE_ROWS, DSA_TQ), F32)
            for h in range(N_IDX_HEADS):
                d = jnp.dot(kidx, qit_ref[0, h * IDX_DIM:(h + 1) * IDX_DIM, :],
                            preferred_element_type=F32)
                acc = acc + jnp.maximum(d, 0.0) * w[h:h + 1, :]
            causal = srow - tcol <= qb * DSA_TQ - kc * SC_TK - r0
            acc = jnp.where(causal, acc, -jnp.inf)
            sc_ref[r0:r0 + SCORE_ROWS, :] = acc
            stage_ref[r0:r0 + SCORE_ROWS, :] = _sortable(lax.bitcast_convert_type(acc, I32)) ^ INT_MIN
        for lanes in (slice(c, c + 128) for c in range(0, DSA_TQ, 128)):
            planes = _bit_transpose([stage_ref[g * group + 8 * n:g * group + 8 * (n + 1), lanes]
                                     for n in range(WORD)])
            for n in range(WORD):
                planes_ref[n, pl.ds(pl.multiple_of(row0 + 8 * g, 8), 8), lanes] = planes[n]

    last = (qb * DSA_TQ + DSA_TQ - 1) // SC_TK

    @pl.when(kc == last)
    def _():
        nwords = alive_ref.shape[0]
        rb = min(128, nwords)
        nrows = (last + 1) * CHUNK_WORDS
        nblk = (nrows + rb - 1) // rb
        tpos = qb * DSA_TQ + lax.broadcasted_iota(I32, (1, DSA_TQ), 1)
        want0 = jnp.minimum(tpos + 1, TOPK_MAX)
        rows = lambda blk: pl.ds(pl.multiple_of(blk * rb, rb), rb)

        def hits(alive, n, blk):
            hit = lax.population_count(alive & planes_ref[n, rows(blk), :])
            return hit.reshape(rb // 8, 8, DSA_TQ).sum(axis=0)

        def init(blk, cnt):
            ridx = blk * rb + lax.broadcasted_iota(I32, (rb, DSA_TQ), 0)
            alive = jnp.where(ridx < nrows, -1, 0)
            alive_ref[rows(blk), :] = alive
            return cnt + hits(alive, 0, blk)
        zero = jnp.zeros((8, DSA_TQ), I32)
        cnt0 = lax.fori_loop(0, nblk, init, zero)

        def bit_step(n, carry):
            r, want, cnt8 = carry
            cnt = cnt8.sum(axis=0, keepdims=True)
            take = cnt >= want
            r = r | jnp.where(take, lax.shift_left(jnp.int32(1), 31 - n), 0)
            want = jnp.where(take, want, want - cnt)
            flip = jnp.where(take, 0, -1)
            nxt = jnp.minimum(n + 1, WORD - 1)

            def sweep(blk, acc):
                alive = alive_ref[rows(blk), :] & (planes_ref[n, rows(blk), :] ^ flip)
                alive_ref[rows(blk), :] = alive
                return acc + hits(alive, nxt, blk)
            return r, want, lax.fori_loop(0, nblk, sweep, zero)

        r, want, _ = lax.fori_loop(0, WORD, bit_step, (jnp.zeros((1, DSA_TQ), I32), want0, cnt0))
        thr_ref[0, 0:1, :] = lax.bitcast_convert_type(_sortable(r ^ INT_MIN), F32)
        thr_ref[0, 1:2, :] = jnp.full((1, DSA_TQ), nwords * WORD, F32)

        def live(blk, acc):
            hit = lax.population_count(alive_ref[rows(blk), :])
            return acc + hit.reshape(rb // 8, 8, DSA_TQ).sum(axis=0)
        tied = lax.fori_loop(0, nblk, live, zero).sum(axis=0, keepdims=True)

        @pl.when(jnp.max(tied - want) > 0)
        def _():
            blk_bits = max(1, (nwords // 8 - 1).bit_length())
            lane_patterns = (-0x10000, -0xFF0100, -0xF0F0F10, -0x33333334, -0x55555556)
            nsec = blk_bits + len(lane_patterns) + 3

            def sec_plane(k, blk):
                ridx = blk * rb + lax.broadcasted_iota(I32, (rb, 1), 0)
                if k < blk_bits:
                    bit = 3 + blk_bits - 1 - k
                elif k < blk_bits + len(lane_patterns):
                    return jnp.full((rb, 1), lane_patterns[k - blk_bits], I32)
                else:
                    bit = nsec - 1 - k
                return jnp.where((lax.shift_right_logical(ridx, bit) & 1) == 0, -1, 0)

            def sec_count(k):
                def body(blk, acc):
                    hit = lax.population_count(alive_ref[rows(blk), :] & sec_plane(k, blk))
                    return acc + hit.reshape(rb // 8, 8, DSA_TQ).sum(axis=0)
                return lax.fori_loop(0, nblk, body, zero).sum(axis=0, keepdims=True)

            left = want
            code = jnp.zeros((1, DSA_TQ), I32)
            for k in range(nsec):
                cnt = sec_count(k)
                take = cnt >= left
                code = code | jnp.where(take, 1 << (nsec - 1 - k), 0)
                left = jnp.where(take, left, left - cnt)
                flip = jnp.where(take, 0, -1)

                def update(blk, c, k=k, flip=flip):
                    alive_ref[rows(blk), :] = alive_ref[rows(blk), :] & (sec_plane(k, blk) ^ flip)
                    return c
                lax.fori_loop(0, nblk, update, 0)
            inv = ~code
            cutoff = ((lax.shift_right_logical(inv, 8) & ((1 << blk_bits) - 1)) * (8 * WORD)
                      + 8 * ((code >> 3) & (WORD - 1) ^ (WORD - 1)) + (inv & 7))
            thr_ref[0, 1:2, :] = cutoff.astype(F32)


def _dsa_attn_kernel(qtab, ktab, sc_ref, thr_ref, q_ref, k_ref, vt_ref, b_ref, bound_ref, o_ref,
                     acc_ref, m_ref, l_ref, s_ref, mask_ref, cm_ref):
    step = pl.program_id(1)
    qb = qtab[step]
    kc = ktab[step]

    @pl.when(kc == 0)
    def _():
        acc_ref[...] = jnp.zeros_like(acc_ref)
        m_ref[...] = jnp.full_like(m_ref, NEG_BIG)
        l_ref[...] = jnp.zeros_like(l_ref)

    bound = bound_ref[0]
    bounded = bound <= DSA_SAFE_BOUND

    def set_mask(kept):
        key = (kc * DSA_TK + lax.broadcasted_iota(I32, (DSA_TK, DSA_TQ), 0)).astype(F32)
        for b in range(ATT_QT):
            cols = slice(b * DSA_TQ, (b + 1) * DSA_TQ)
            thr = thr_ref[0, 0:1, cols]
            tied = jnp.logical_and(sc_ref[b] == thr, key <= thr_ref[0, 1:2, cols])
            mask_ref[:, cols] = jnp.where(jnp.logical_or(sc_ref[b] > thr, tied), kept, -jnp.inf)

    tiles = [[jnp.clip((qb * ATT_QT + b) - (kc * DSA_HALVES + a), -1, DSA_NEAR) + 1
              for b in range(ATT_QT)] for a in range(DSA_HALVES)]
    far = qb * ATT_QT - (kc * DSA_HALVES + DSA_HALVES - 1) >= DSA_NEAR
    far_bias = lambda h: b_ref[h, DSA_NEAR + 1, 0:1, 0:1]

    def logits(h):
        sl = slice(h * HEAD_DIM, (h + 1) * HEAD_DIM)
        s = lax.dot_general(k_ref[0, :, sl], q_ref[0, :, sl], _NT, preferred_element_type=F32)
        return s + mask_ref[...]

    def near_bias(h):
        return jnp.concatenate(
            [jnp.concatenate([b_ref[h, tiles[a][b]] for b in range(ATT_QT)], axis=1)
             for a in range(DSA_HALVES)], axis=0)

    def values(h):
        return vt_ref[h * HEAD_DIM:(h + 1) * HEAD_DIM, :]

    @pl.when(bounded)
    def _():
        def raw_logits(h):
            sl = slice(h * HEAD_DIM, (h + 1) * HEAD_DIM)
            s_ref[h] = lax.dot_general(k_ref[0, :, sl], q_ref[0, :, sl], _NT,
                                       preferred_element_type=F32)

        def heads(weights):
            for h in range(DSA_LOOKAHEAD):
                raw_logits(h)
            set_mask(-bound)
            for h in range(N_HEADS_B):
                if h + DSA_LOOKAHEAD < N_HEADS_B:
                    raw_logits(h + DSA_LOOKAHEAD)
                p, gain = weights(h)
                l_ref[h] += gain * jnp.sum(p, axis=0, keepdims=True)
                acc_ref[h] += gain * jnp.dot(values(h), p.astype(BF16), preferred_element_type=F32)

        @pl.when(far)
        def _():
            heads(lambda h: (jnp.exp2(s_ref[h] + mask_ref[...]), jnp.exp2(far_bias(h))))

        @pl.when(jnp.logical_not(far))
        def _():
            heads(lambda h: (jnp.exp2(s_ref[h] + mask_ref[...] + near_bias(h)), 1.0))

    @pl.when(jnp.logical_not(bounded))
    def _():
        set_mask(0.0)

        @pl.when(far)
        def _():
            for h in range(N_HEADS_B):
                s = logits(h)
                s_ref[h] = s
                cm_ref[h] = jnp.max(s, axis=0, keepdims=True) + far_bias(h)

        @pl.when(jnp.logical_not(far))
        def _():
            for h in range(N_HEADS_B):
                s = logits(h) + near_bias(h)
                s_ref[h] = s
                cm_ref[h] = jnp.max(s, axis=0, keepdims=True)

        for h in range(N_HEADS_B):
            m_old = m_ref[h]
            m_new = jnp.maximum(m_old, cm_ref[h])
            alpha = jnp.exp2(m_old - m_new)
            m_ref[h] = m_new
            shift = jnp.where(far, m_new - far_bias(h), m_new)
            p = jnp.exp2(s_ref[h] - shift)
            l_ref[h] = alpha * l_ref[h] + jnp.sum(p, axis=0, keepdims=True)
            pv = jnp.dot(values(h), p.astype(BF16), preferred_element_type=F32)
            acc_ref[h] = alpha * acc_ref[h] + pv

    last = (qb * ATT_TQ + ATT_TQ - 1) // DSA_TK

    @pl.when(kc == last)
    def _():
        for h in range(N_HEADS_B):
            o_ref[0, :, h * HEAD_DIM:(h + 1) * HEAD_DIM] = (acc_ref[h] / l_ref[h]).T.astype(BF16)


def _dsa_scores(qi_t, ki, wi_t):
    bsz, _, s = qi_t.shape
    qs, ks = _dsa_schedule(s, DSA_TQ, SC_TK)
    npairs = len(qs)
    return pl.pallas_call(
        _dsa_score_kernel,
        grid_spec=pltpu.PrefetchScalarGridSpec(
            num_scalar_prefetch=2,
            grid=(bsz, npairs),
            in_specs=[
                pl.BlockSpec((1, N_IDX_HEADS * IDX_DIM, DSA_TQ), lambda b, t, qt, kt: (b, 0, qt[t])),
                pl.BlockSpec((1, SC_TK, IDX_DIM), lambda b, t, qt, kt: (b, kt[t], 0)),
                pl.BlockSpec((1, N_IDX_HEADS, DSA_TQ), lambda b, t, qt, kt: (b, 0, qt[t])),
            ],
            out_specs=[
                pl.BlockSpec((None, None, None, SC_TK, DSA_TQ),
                             lambda b, t, qt, kt: (b, kt[t], qt[t], 0, 0)),
                pl.BlockSpec((1, 2, DSA_TQ), lambda b, t, qt, kt: (b, 0, qt[t])),
            ],
            scratch_shapes=[pltpu.VMEM((WORD, s // WORD, DSA_TQ), I32),
                            pltpu.VMEM((s // WORD, DSA_TQ), I32),
                            pltpu.VMEM((SC_TK, DSA_TQ), I32)],
        ),
        out_shape=[jax.ShapeDtypeStruct((bsz, s // SC_TK, s // DSA_TQ, SC_TK, DSA_TQ), F32),
                   jax.ShapeDtypeStruct((bsz, 2, s), F32)],
        compiler_params=_cparams(("arbitrary", "arbitrary")),
        name="dsa_scores",
    )(qs, ks, qi_t, ki, wi_t)


def _dsa(main, vt, qi_t, ki, wi_t, bias, logit_bound):
    bsz, s, _ = main.shape
    w = WIDTH_B
    scores, thr = _dsa_scores(qi_t, ki, wi_t)
    ntile = bias.shape[1]
    qs, ks = _dsa_schedule(s, ATT_TQ, DSA_TK)
    out = pl.pallas_call(
        _dsa_attn_kernel,
        grid_spec=pltpu.PrefetchScalarGridSpec(
            num_scalar_prefetch=2,
            grid=(bsz, len(qs)),
            in_specs=[
                pl.BlockSpec((None, None, ATT_QT, DSA_TK, DSA_TQ),
                             lambda b, t, qt, kt: (b, kt[t] // SC_PER_ATT, qt[t], kt[t] % SC_PER_ATT, 0)),
                pl.BlockSpec((1, 2, ATT_TQ), lambda b, t, qt, kt: (b, 0, qt[t])),
                pl.BlockSpec((1, ATT_TQ, w), lambda b, t, qt, kt: (b, qt[t], PROJ_B0)),
                pl.BlockSpec((1, DSA_TK, w), lambda b, t, qt, kt: (b, kt[t], PROJ_B0 + 1)),
                pl.BlockSpec((None, None, w, DSA_TK), lambda b, t, qt, kt: (b, kt[t], 0, 0)),
                pl.BlockSpec((N_HEADS_B, ntile, DSA_TQ, DSA_TQ), lambda b, t, qt, kt: (0, 0, 0, 0),
                             pipeline_mode=pl.Buffered(1)),
                pl.BlockSpec(memory_space=pltpu.SMEM),
            ],
            out_specs=pl.BlockSpec((1, ATT_TQ, w), lambda b, t, qt, kt: (b, qt[t], 0)),
            scratch_shapes=[pltpu.VMEM((N_HEADS_B, HEAD_DIM, ATT_TQ), F32),
                            pltpu.VMEM((N_HEADS_B, 1, ATT_TQ), F32),
                            pltpu.VMEM((N_HEADS_B, 1, ATT_TQ), F32),
                            pltpu.VMEM((N_HEADS_B, DSA_TK, ATT_TQ), F32),
                            pltpu.VMEM((DSA_TK, ATT_TQ), F32),
                            pltpu.VMEM((N_HEADS_B, 1, ATT_TQ), F32)],
        ),
        out_shape=jax.ShapeDtypeStruct((bsz, s, w), BF16),
        compiler_params=_cparams(("arbitrary", "arbitrary")),
        name="dsa_attn",
    )(qs, ks, scores, thr, main, main, vt, bias, logit_bound)
    return out.reshape(bsz * s, w)


def _out_kernel(x_ref, ma_ref, mb_ref, wa_ref, wb_ref, o_ref):
    o_ref[...] = (x_ref[...]
                  + jnp.dot(ma_ref[...], wa_ref[...], preferred_element_type=F32)
                  + jnp.dot(mb_ref[...], wb_ref[...], preferred_element_type=F32))


def _out_proj(x2, mix_a, mix_b, w_out, tm=512):
    m, d = x2.shape
    row = lambda width: pl.BlockSpec((tm, width), lambda i: (i, 0))
    return pl.pallas_call(
        _out_kernel,
        grid=(m // tm,),
        in_specs=[row(d), row(WIDTH_A), row(WIDTH_B),
                  pl.BlockSpec((WIDTH_A, d), lambda i: (0, 0)),
                  pl.BlockSpec((WIDTH_B, d), lambda i: (1, 0))],
        out_specs=row(d),
        out_shape=jax.ShapeDtypeStruct((m, d), F32),
        compiler_params=_cparams(("parallel",)),
        name="out_proj",
    )(x2, mix_a, mix_b, w_out, w_out)


def kernel(x, rel_bias, norm_ffn1, w_ffn1_in, w_ffn1_out, norm_mix, w_in, q_norm_a, k_norm_a,
           q_norm_b, k_norm_b, w_out, norm_ffn2, w_ffn2_in, w_ffn2_out):
    bsz, s, d = x.shape
    m = bsz * s
    depth = norm_ffn1.shape[0]
    scale = HEAD_DIM ** -0.5
    n_main = PROJ_NBLK * PROJ_BLK
    dil_bias, dsa_bias = _bias_tiles(rel_bias)
    x2 = x.reshape(m, d)
    ones = jnp.ones((PROJ_BLK,), F32)
    tile8 = lambda g: jnp.tile(g, PROJ_BLK // HEAD_DIM)
    for l in range(depth):
        x2 = _ffn(x2, norm_ffn1[l], w_ffn1_in[l].astype(BF16), w_ffn1_out[l].astype(BF16))

        w_main = w_in[l, :, :n_main].astype(BF16)
        w_idx = jnp.pad(w_in[l, :, n_main:], ((0, 0), (0, IDX_PAD - IDX_DIM - N_IDX_HEADS))).astype(BF16)
        head_gain = jnp.stack([tile8(q_norm_a[l] * scale), tile8(k_norm_a[l]), ones,
                               tile8(q_norm_b[l] * (scale * LOG2E)), tile8(k_norm_b[l]), ones, ones])
        main, vt, qi_t, idx = _proj(x2, s, norm_mix[l], w_main, w_idx,
                                    head_gain.reshape(PROJ_NBLK, 1, PROJ_BLK))
        main = main.reshape(bsz, s, PROJ_NMAIN * PROJ_BLK)
        vt = vt.reshape(bsz, s // DSA_TK, WIDTH_B, DSA_TK)
        idx = idx.reshape(bsz, s, IDX_PAD)
        ki = idx[:, :, :IDX_DIM].astype(BF16)
        wi_t = jnp.swapaxes(idx[:, :, IDX_DIM:IDX_DIM + N_IDX_HEADS], 1, 2)

        mix_a = _dilated(main, dil_bias)
        logit_bound = (1.02 * HEAD_DIM * jnp.max(jnp.abs(q_norm_b[l] * (scale * LOG2E)))
                       * jnp.max(jnp.abs(k_norm_b[l]))
                       + LOG2E * jnp.max(jnp.abs(rel_bias[:, N_HEADS_A:]))).reshape(1)
        mix_b = _dsa(main, vt, qi_t, ki, wi_t, dsa_bias, logit_bound)

        x2 = _out_proj(x2, mix_a, mix_b, w_out[l].astype(BF16))
        x2 = _ffn(x2, norm_ffn2[l], w_ffn2_in[l].astype(BF16), w_ffn2_out[l].astype(BF16))
    return x2.reshape(bsz, s, d)
```

```python
import functools
import math

import jax
import jax.numpy as jnp
import numpy as np
from jax import lax
from jax.experimental import pallas as pl
from jax.experimental.pallas import tpu as pltpu

F32 = jnp.float32
BF16 = jnp.bfloat16
I32 = jnp.int32

HEAD_DIM = 128
N_HEADS_A = 8
N_HEADS_B = 8
WIDTH_A = N_HEADS_A * HEAD_DIM
WIDTH_B = N_HEADS_B * HEAD_DIM
DILATIONS = (1, 4, 16)
DIL_BLK = 128
N_IDX_HEADS = 16
IDX_DIM = 64
TOPK_MAX = 256
N_BUCKETS = 32
REL_MAX_DIST = 2048
EPS = 1e-6
IDX_PAD = 128
IDX_SCALE = (IDX_DIM ** -0.5) * (N_IDX_HEADS ** -0.5)
LOG2E = math.log2(math.e)

VMEM_LIMIT = 56 * 1024 * 1024

DSA_TQ = 256
DSA_TK = 512
DSA_HALVES = DSA_TK // DSA_TQ
ATT_TQ = 512
ATT_QT = ATT_TQ // DSA_TQ
WORD = 32
SC_TK = 1024
SC_PER_ATT = SC_TK // DSA_TK
CHUNK_WORDS = SC_TK // WORD
SCORE_ROWS = 64
NEG_BIG = -1e30
DSA_SAFE_BOUND = 40.0
DSA_LOOKAHEAD = 2
INT_MIN = -2 ** 31


def _near_tiles():
    exact = N_BUCKETS // 2
    j = 1
    while True:
        d = j * DSA_TQ - (DSA_TQ - 1)
        b = exact + int(math.log(d / exact) / math.log(REL_MAX_DIST / exact) * (N_BUCKETS - exact) - 0.02)
        if d >= exact and b >= N_BUCKETS - 1:
            return j
        j += 1


DSA_NEAR = _near_tiles()


def _cparams(sem):
    return pltpu.CompilerParams(dimension_semantics=sem, vmem_limit_bytes=VMEM_LIMIT)


def _rel_bucket(dist):
    exact = N_BUCKETS // 2
    df = jnp.maximum(dist, 1).astype(F32)
    large = exact + (jnp.log(df / exact) / math.log(REL_MAX_DIST / exact)
                     * (N_BUCKETS - exact)).astype(I32)
    large = jnp.minimum(large, N_BUCKETS - 1)
    return jnp.where(dist < exact, dist, large)


def _lookup(tab_ref, bucket, col):
    val = jnp.zeros(bucket.shape, F32)
    for b in range(N_BUCKETS):
        val = jnp.where(bucket == b, tab_ref[b, col], val)
    return val


def _dil_bias_kernel(tab_ref, o_ref):
    dil = jnp.left_shift(1, 2 * pl.program_id(0))
    i = lax.broadcasted_iota(I32, (DIL_BLK, 2 * DIL_BLK), 0)
    j = lax.broadcasted_iota(I32, (DIL_BLK, 2 * DIL_BLK), 1)
    bucket = _rel_bucket(jnp.clip(DIL_BLK + i - j, 0, DIL_BLK) * dil)
    for h in range(N_HEADS_A):
        o_ref[0, h] = _lookup(tab_ref, bucket, h)


def _dsa_bias_kernel(tab_ref, o_ref):
    jj = pl.program_id(0)
    s = lax.broadcasted_iota(I32, (DSA_TQ, DSA_TQ), 0)
    t = lax.broadcasted_iota(I32, (DSA_TQ, DSA_TQ), 1)
    bucket = _rel_bucket(jnp.maximum((jj - 1) * DSA_TQ + t - s, 0))
    for h in range(N_HEADS_B):
        o_ref[h, 0] = _lookup(tab_ref, bucket, N_HEADS_A + h) * LOG2E


def _bias_tiles(rel_bias):
    smem = pl.BlockSpec(memory_space=pltpu.SMEM)
    dil = pl.pallas_call(
        _dil_bias_kernel,
        grid=(len(DILATIONS),),
        in_specs=[smem],
        out_specs=pl.BlockSpec((1, N_HEADS_A, DIL_BLK, 2 * DIL_BLK), lambda g: (g, 0, 0, 0)),
        out_shape=jax.ShapeDtypeStruct((len(DILATIONS), N_HEADS_A, DIL_BLK, 2 * DIL_BLK), F32),
        name="dil_bias",
    )(rel_bias)
    ntile = DSA_NEAR + 2
    dsa = pl.pallas_call(
        _dsa_bias_kernel,
        grid=(ntile,),
        in_specs=[smem],
        out_specs=pl.BlockSpec((N_HEADS_B, 1, DSA_TQ, DSA_TQ), lambda g: (0, g, 0, 0)),
        out_shape=jax.ShapeDtypeStruct((N_HEADS_B, ntile, DSA_TQ, DSA_TQ), F32),
        name="dsa_bias",
    )(rel_bias)
    return dil, dsa


def _ffn_kernel(x_ref, g_ref, wg_ref, wu_ref, wo_ref, o_ref, h_ref, acc_ref):
    f = pl.program_id(1)

    @pl.when(f == 0)
    def _():
        x = x_ref[...]
        ms = jnp.mean(x * x, axis=-1, keepdims=True)
        h_ref[...] = (x * lax.rsqrt(ms + EPS) * g_ref[...]).astype(BF16)
        acc_ref[...] = jnp.zeros_like(acc_ref)

    h = h_ref[...]
    gate = jnp.dot(h, wg_ref[...], preferred_element_type=F32)
    up = jnp.dot(h, wu_ref[...], preferred_element_type=F32)
    act = (gate * jax.nn.sigmoid(gate) * up).astype(BF16)
    acc_ref[...] += jnp.dot(act, wo_ref[...], preferred_element_type=F32)

    @pl.when(f == pl.num_programs(1) - 1)
    def _():
        o_ref[...] = x_ref[...] + 0.5 * acc_ref[...]


def _ffn(x2, gain, w_in, w_out, tm=512, tf=512):
    m, d = x2.shape
    d_ff = w_out.shape[0]
    nf = d_ff // tf
    return pl.pallas_call(
        _ffn_kernel,
        grid=(m // tm, nf),
        in_specs=[
            pl.BlockSpec((tm, d), lambda i, f: (i, 0)),
            pl.BlockSpec((1, d), lambda i, f: (0, 0)),
            pl.BlockSpec((d, tf), lambda i, f: (0, f)),
            pl.BlockSpec((d, tf), lambda i, f: (0, f + nf)),
            pl.BlockSpec((tf, d), lambda i, f: (f, 0)),
        ],
        out_specs=pl.BlockSpec((tm, d), lambda i, f: (i, 0)),
        out_shape=jax.ShapeDtypeStruct((m, d), F32),
        scratch_shapes=[pltpu.VMEM((tm, d), BF16), pltpu.VMEM((tm, d), F32)],
        compiler_params=_cparams(("parallel", "arbitrary")),
        name="ffn",
    )(x2, gain.reshape(1, d), w_in, w_in, w_out)


PROJ_BLK = 1024
PROJ_NBLK = 7
PROJ_B0 = 3
PROJ_NMAIN = 5
PROJ_PARTS = 4
_NORM_BLOCKS = (0, 1, 3, 4)


def _proj_kernel(x_ref, g_ref, w_ref, widx_ref, hg_ref, o_ref, vt_ref, qit_ref, ki_ref, wit_ref, h_ref):
    j = pl.program_id(1)

    @pl.when(j == 0)
    def _():
        x = x_ref[...]
        ms = jnp.mean(x * x, axis=-1, keepdims=True)
        h_ref[...] = (x * lax.rsqrt(ms + EPS) * g_ref[...]).astype(BF16)

    rows = h_ref.shape[0] // PROJ_PARTS
    parts = [slice(part * rows, (part + 1) * rows) for part in range(PROJ_PARTS)]
    project = lambda rs: jnp.dot(h_ref[rs, :], w_ref[...], preferred_element_type=F32)

    @pl.when(j < PROJ_NMAIN)
    def _():
        is_norm = functools.reduce(jnp.logical_or, [j == b for b in _NORM_BLOCKS])
        for rs in parts:
            y = project(rs)
            for hd in range(PROJ_BLK // HEAD_DIM):
                sl = slice(hd * HEAD_DIM, (hd + 1) * HEAD_DIM)
                yh = y[:, sl]
                ms = jnp.mean(yh * yh, axis=-1, keepdims=True)
                inv = jnp.where(is_norm, lax.rsqrt(ms + EPS), 1.0)
                o_ref[rs, sl] = (yh * inv * hg_ref[0, :, sl]).astype(BF16)

    @pl.when(j == PROJ_NMAIN)
    def _():
        for rs in parts:
            chunk, off = divmod(rs.start, DSA_TK)
            vt_ref[chunk, :, off:off + rows] = project(rs).astype(BF16).T

    @pl.when(j == PROJ_NMAIN + 1)
    def _():
        for rs in parts:
            qit_ref[0, :, rs] = project(rs).astype(BF16).T
        idx = jnp.dot(h_ref[...], widx_ref[...], preferred_element_type=F32)
        ki_ref[...] = idx[:, :IDX_DIM].astype(BF16)
        wit_ref[0] = idx.T[IDX_DIM:IDX_DIM + N_IDX_HEADS, :]


def _proj(x2, s, gain, w_main, w_idx, head_gain, tm=1024):
    m, d = x2.shape
    per_seq = s // tm
    return pl.pallas_call(
        _proj_kernel,
        grid=(m // tm, PROJ_NBLK),
        in_specs=[
            pl.BlockSpec((tm, d), lambda i, j: (i, 0)),
            pl.BlockSpec((1, d), lambda i, j: (0, 0)),
            pl.BlockSpec((d, PROJ_BLK), lambda i, j: (0, j)),
            pl.BlockSpec((d, IDX_PAD), lambda i, j: (0, 0)),
            pl.BlockSpec((1, 1, PROJ_BLK), lambda i, j: (j, 0, 0)),
        ],
        out_specs=[
            pl.BlockSpec((tm, PROJ_BLK), lambda i, j: (i, jnp.minimum(j, PROJ_NMAIN - 1))),
            pl.BlockSpec((tm // DSA_TK, PROJ_BLK, DSA_TK), lambda i, j: (i, 0, 0)),
            pl.BlockSpec((1, PROJ_BLK, tm), lambda i, j: (i // per_seq, 0, i % per_seq)),
            pl.BlockSpec((tm, IDX_DIM), lambda i, j: (i, 0)),
            pl.BlockSpec((1, N_IDX_HEADS, tm), lambda i, j: (i // per_seq, 0, i % per_seq)),
        ],
        out_shape=[
            jax.ShapeDtypeStruct((m, PROJ_NMAIN * PROJ_BLK), BF16),
            jax.ShapeDtypeStruct((m // DSA_TK, PROJ_BLK, DSA_TK), BF16),
            jax.ShapeDtypeStruct((m // s, PROJ_BLK, s), BF16),
            jax.ShapeDtypeStruct((m, IDX_DIM), BF16),
            jax.ShapeDtypeStruct((m // s, N_IDX_HEADS, s), F32),
        ],
        scratch_shapes=[pltpu.VMEM((tm, d), BF16)],
        compiler_params=_cparams(("parallel", "arbitrary")),
        name="proj",
    )(x2, gain.reshape(1, d), w_main, w_idx, head_gain)


_NT = (((1,), (1,)), ((), ()))


DIL_SPAN = DIL_BLK * max(DILATIONS)
DIL_G = 2
DIL_UNITS = DIL_SPAN // DIL_BLK
DIL_UNROLL = DIL_UNITS


def _dilated_kernel(q_ref, kp_ref, kc_ref, vp_ref, vc_ref, b_ref, o_ref, qf, kf, vf, ob, lse_ref):
    n = pl.program_id(1)
    h0 = pl.program_id(2) * DIL_G
    for g in range(DIL_G):
        sl = slice(g * HEAD_DIM, (g + 1) * HEAD_DIM)
        qf[g] = q_ref[0, :, sl].astype(F32)
        kf[g, :DIL_SPAN] = kp_ref[0, :, sl].astype(F32)
        kf[g, DIL_SPAN:] = kc_ref[0, :, sl].astype(F32)
        vf[g, :DIL_SPAN] = vp_ref[0, :, sl].astype(F32)
        vf[g, DIL_SPAN:] = vc_ref[0, :, sl].astype(F32)
    i = lax.broadcasted_iota(I32, (DIL_BLK, 2 * DIL_BLK), 0)
    j = lax.broadcasted_iota(I32, (DIL_BLK, 2 * DIL_BLK), 1)
    band = jnp.logical_and(j >= i, j <= i + DIL_BLK)
    first = j >= DIL_BLK

    for br, dil in enumerate(DILATIONS):
        span = DIL_BLK * dil

        def unit(idx, carry, br=br, dil=dil, span=span):
            blk = idx // dil
            base = blk * span + idx % dil
            valid = jnp.logical_and(band, jnp.logical_or(first, n * (DIL_SPAN // span) + blk > 0))
            qrows = pl.ds(base, DIL_BLK, stride=dil)
            krows = pl.ds(DIL_SPAN + base - span, 2 * DIL_BLK, stride=dil)
            for g in range(DIL_G):
                q = qf[g, qrows, :].astype(BF16)
                k = kf[g, krows, :].astype(BF16)
                v = vf[g, krows, :].astype(BF16)
                s = lax.dot_general(q, k, _NT, preferred_element_type=F32) + b_ref[br, h0 + g]
                s = jnp.where(valid, s, -jnp.inf)
                mx = jnp.max(s, axis=-1, keepdims=True)
                p = jnp.exp(s - mx)
                den = jnp.sum(p, axis=-1, keepdims=True)
                ob[br, g, qrows, :] = jnp.dot(p.astype(BF16), v, preferred_element_type=F32) / den
                lse_ref[br, g, qrows, :] = jnp.broadcast_to(mx + jnp.log(den), (DIL_BLK, HEAD_DIM))
            return carry

        lax.fori_loop(0, DIL_UNITS, unit, 0, unroll=DIL_UNROLL)

    for g in range(DIL_G):
        l1, l2, l3 = lse_ref[0, g], lse_ref[1, g], lse_ref[2, g]
        mx = jnp.maximum(jnp.maximum(l1, l2), l3)
        e1, e2, e3 = jnp.exp(l1 - mx), jnp.exp(l2 - mx), jnp.exp(l3 - mx)
        mix = (e1 * ob[0, g] + e2 * ob[1, g] + e3 * ob[2, g]) / (e1 + e2 + e3)
        o_ref[0, :, g * HEAD_DIM:(g + 1) * HEAD_DIM] = mix.astype(BF16)


def _dilated(main, bias):
    bsz, s, _ = main.shape
    gw = DIL_G * HEAD_DIM
    ng = WIDTH_A // gw
    cur = lambda c: pl.BlockSpec((1, DIL_SPAN, gw), lambda b, n, g: (b, n, c * ng + g))
    prev = lambda c: pl.BlockSpec((1, DIL_SPAN, gw), lambda b, n, g: (b, jnp.maximum(n - 1, 0), c * ng + g))
    out = pl.pallas_call(
        _dilated_kernel,
        grid=(bsz, s // DIL_SPAN, ng),
        in_specs=[cur(0), prev(1), cur(1), prev(2), cur(2),
                  pl.BlockSpec(bias.shape, lambda b, n, g: (0, 0, 0, 0), pipeline_mode=pl.Buffered(1))],
        out_specs=pl.BlockSpec((1, DIL_SPAN, gw), lambda b, n, g: (b, n, g)),
        out_shape=jax.ShapeDtypeStruct((bsz, s, WIDTH_A), BF16),
        scratch_shapes=[pltpu.VMEM((DIL_G, DIL_SPAN, HEAD_DIM), F32),
                        pltpu.VMEM((DIL_G, 2 * DIL_SPAN, HEAD_DIM), F32),
                        pltpu.VMEM((DIL_G, 2 * DIL_SPAN, HEAD_DIM), F32),
                        pltpu.VMEM((len(DILATIONS), DIL_G, DIL_SPAN, HEAD_DIM), F32),
                        pltpu.VMEM((len(DILATIONS), DIL_G, DIL_SPAN, HEAD_DIM), F32)],
        compiler_params=_cparams(("parallel", "parallel", "parallel")),
        name="dilated",
    )(main, main, main, main, main, bias)
    return out.reshape(bsz * s, WIDTH_A)


def _dsa_schedule(s, tq, tk):
    qs, ks = [], []
    for i in range(s // tq):
        last = (i * tq + tq - 1) // tk
        for k in range(last + 1):
            qs.append(i)
            ks.append(k)
    return np.asarray(qs, np.int32), np.asarray(ks, np.int32)


def _sortable(bits):
    return bits ^ (lax.shift_right_arithmetic(bits, 31) & 0x7FFFFFFF)


def _bit_transpose(words):
    a = list(words)
    mask, j = 0x0000FFFF, 16
    while j:
        k = 0
        while k < WORD:
            t = (a[k] ^ lax.shift_right_logical(a[k + j], j)) & mask
            a[k] = a[k] ^ t
            a[k + j] = a[k + j] ^ lax.shift_left(t, j)
            k = (k + j + 1) & ~j
        j >>= 1
        mask = (mask ^ (mask << j)) & 0xFFFFFFFF
    return a


def _dsa_score_kernel(qtab, ktab, qit_ref, ki_ref, w_ref, sc_ref, thr_ref,
                      planes_ref, alive_ref, stage_ref):
    step = pl.program_id(1)
    qb = qtab[step]
    kc = ktab[step]
    w = w_ref[0] * IDX_SCALE
    srow = lax.broadcasted_iota(I32, (SCORE_ROWS, DSA_TQ), 0)
    tcol = lax.broadcasted_iota(I32, (SCORE_ROWS, DSA_TQ), 1)
    row0 = pl.multiple_of(kc * CHUNK_WORDS, CHUNK_WORDS)
    group = WORD * 8
    for g in range(SC_TK // group):
        for r0 in range(g * group, (g + 1) * group, SCORE_ROWS):
            kidx = ki_ref[0, r0:r0 + SCORE_ROWS, :]
            acc = jnp.zeros((SCORE_ROWS, DSA_TQ), F32)
            for h in range(N_IDX_HEADS):
                d = jnp.dot(kidx, qit_ref[0, h * IDX_DIM:(h + 1) * IDX_DIM, :],
                            preferred_element_type=F32)
                acc = acc + jnp.maximum(d, 0.0) * w[h:h + 1, :]
            causal = srow - tcol <= qb * DSA_TQ - kc * SC_TK - r0
            acc = jnp.where(causal, acc, -jnp.inf)
            sc_ref[r0:r0 + SCORE_ROWS, :] = acc
            stage_ref[r0:r0 + SCORE_ROWS, :] = _sortable(lax.bitcast_convert_type(acc, I32)) ^ INT_MIN
        for lanes in (slice(c, c + 128) for c in range(0, DSA_TQ, 128)):
            planes = _bit_transpose([stage_ref[g * group + 8 * n:g * group + 8 * (n + 1), lanes]
                                     for n in range(WORD)])
            for n in range(WORD):
                planes_ref[n, pl.ds(pl.multiple_of(row0 + 8 * g, 8), 8), lanes] = planes[n]

    last = (qb * DSA_TQ + DSA_TQ - 1) // SC_TK

    @pl.when(kc == last)
    def _():
        nwords = alive_ref.shape[0]
        rb = min(128, nwords)
        nrows = (last + 1) * CHUNK_WORDS
        nblk = (nrows + rb - 1) // rb
        tpos = qb * DSA_TQ + lax.broadcasted_iota(I32, (1, DSA_TQ), 1)
        want0 = jnp.minimum(tpos + 1, TOPK_MAX)
        rows = lambda blk: pl.ds(pl.multiple_of(blk * rb, rb), rb)

        def hits(alive, n, blk):
            hit = lax.population_count(alive & planes_ref[n, rows(blk), :])
            return hit.reshape(rb // 8, 8, DSA_TQ).sum(axis=0)

        def init(blk, cnt):
            ridx = blk * rb + lax.broadcasted_iota(I32, (rb, DSA_TQ), 0)
            alive = jnp.where(ridx < nrows, -1, 0)
            alive_ref[rows(blk), :] = alive
            return cnt + hits(alive, 0, blk)
        zero = jnp.zeros((8, DSA_TQ), I32)
        cnt0 = lax.fori_loop(0, nblk, init, zero)

        def bit_step(n, carry):
            r, want, cnt8 = carry
            cnt = cnt8.sum(axis=0, keepdims=True)
            take = cnt >= want
            r = r | jnp.where(take, lax.shift_left(jnp.int32(1), 31 - n), 0)
            want = jnp.where(take, want, want - cnt)
            flip = jnp.where(take, 0, -1)
            nxt = jnp.minimum(n + 1, WORD - 1)

            def sweep(blk, acc):
                alive = alive_ref[rows(blk), :] & (planes_ref[n, rows(blk), :] ^ flip)
                alive_ref[rows(blk), :] = alive
                return acc + hits(alive, nxt, blk)
            return r, want, lax.fori_loop(0, nblk, sweep, zero)

        r, want, _ = lax.fori_loop(0, WORD, bit_step, (jnp.zeros((1, DSA_TQ), I32), want0, cnt0))
        thr_ref[0, 0:1, :] = lax.bitcast_convert_type(_sortable(r ^ INT_MIN), F32)
        thr_ref[0, 1:2, :] = jnp.full((1, DSA_TQ), nwords * WORD, F32)

        def live(blk, acc):
            hit = lax.population_count(alive_ref[rows(blk), :])
            return acc + hit.reshape(rb // 8, 8, DSA_TQ).sum(axis=0)
        tied = lax.fori_loop(0, nblk, live, zero).sum(axis=0, keepdims=True)

        @pl.when(jnp.max(tied - want) > 0)
        def _():
            blk_bits = max(1, (nwords // 8 - 1).bit_length())
            lane_patterns = (-0x10000, -0xFF0100, -0xF0F0F10, -0x33333334, -0x55555556)
            nsec = blk_bits + len(lane_patterns) + 3

            def sec_plane(k, blk):
                ridx = blk * rb + lax.broadcasted_iota(I32, (rb, 1), 0)
                if k < blk_bits:
                    bit = 3 + blk_bits - 1 - k
                elif k < blk_bits + len(lane_patterns):
                    return jnp.full((rb, 1), lane_patterns[k - blk_bits], I32)
                else:
                    bit = nsec - 1 - k
                return jnp.where((lax.shift_right_logical(ridx, bit) & 1) == 0, -1, 0)

            def sec_count(k):
                def body(blk, acc):
                    hit = lax.population_count(alive_ref[rows(blk), :] & sec_plane(k, blk))
                    return acc + hit.reshape(rb // 8, 8, DSA_TQ).sum(axis=0)
                return lax.fori_loop(0, nblk, body, zero).sum(axis=0, keepdims=True)

            left = want
            code = jnp.zeros((1, DSA_TQ), I32)
            for k in range(nsec):
                cnt = sec_count(k)
                take = cnt >= left
                code = code | jnp.where(take, 1 << (nsec - 1 - k), 0)
                left = jnp.where(take, left, left - cnt)
                flip = jnp.where(take, 0, -1)

                def update(blk, c, k=k, flip=flip):
                    alive_ref[rows(blk), :] = alive_ref[rows(blk), :] & (sec_plane(k, blk) ^ flip)
                    return c
                lax.fori_loop(0, nblk, update, 0)
            inv = ~code
            cutoff = ((lax.shift_right_logical(inv, 8) & ((1 << blk_bits) - 1)) * (8 * WORD)
                      + 8 * ((code >> 3) & (WORD - 1) ^ (WORD - 1)) + (inv & 7))
            thr_ref[0, 1:2, :] = cutoff.astype(F32)


def _dsa_attn_kernel(qtab, ktab, sc_ref, thr_ref, q_ref, k_ref, vt_ref, b_ref, bound_ref, o_ref,
                     acc_ref, m_ref, l_ref, s_ref, mask_ref, cm_ref):
    step = pl.program_id(1)
    qb = qtab[step]
    kc = ktab[step]

    @pl.when(kc == 0)
    def _():
        acc_ref[...] = jnp.zeros_like(acc_ref)
        m_ref[...] = jnp.full_like(m_ref, NEG_BIG)
        l_ref[...] = jnp.zeros_like(l_ref)

    bound = bound_ref[0]
    bounded = bound <= DSA_SAFE_BOUND

    def set_mask(kept):
        key = (kc * DSA_TK + lax.broadcasted_iota(I32, (DSA_TK, DSA_TQ), 0)).astype(F32)
        for b in range(ATT_QT):
            cols = slice(b * DSA_TQ, (b + 1) * DSA_TQ)
            thr = thr_ref[0, 0:1, cols]
            tied = jnp.logical_and(sc_ref[b] == thr, key <= thr_ref[0, 1:2, cols])
            mask_ref[:, cols] = jnp.where(jnp.logical_or(sc_ref[b] > thr, tied), kept, -jnp.inf)

    tiles = [[jnp.clip((qb * ATT_QT + b) - (kc * DSA_HALVES + a), -1, DSA_NEAR) + 1
              for b in range(ATT_QT)] for a in range(DSA_HALVES)]
    far = qb * ATT_QT - (kc * DSA_HALVES + DSA_HALVES - 1) >= DSA_NEAR
    far_bias = lambda h: b_ref[h, DSA_NEAR + 1, 0:1, 0:1]

    def logits(h):
        sl = slice(h * HEAD_DIM, (h + 1) * HEAD_DIM)
        s = lax.dot_general(k_ref[0, :, sl], q_ref[0, :, sl], _NT, preferred_element_type=F32)
        return s + mask_ref[...]

    def near_bias(h):
        return jnp.concatenate(
            [jnp.concatenate([b_ref[h, tiles[a][b]] for b in range(ATT_QT)], axis=1)
             for a in range(DSA_HALVES)], axis=0)

    def values(h):
        return vt_ref[h * HEAD_DIM:(h + 1) * HEAD_DIM, :]

    @pl.when(bounded)
    def _():
        def raw_logits(h):
            sl = slice(h * HEAD_DIM, (h + 1) * HEAD_DIM)
            s_ref[h] = lax.dot_general(k_ref[0, :, sl], q_ref[0, :, sl], _NT,
                                       preferred_element_type=F32)

        def heads(weights):
            for h in range(DSA_LOOKAHEAD):
                raw_logits(h)
            set_mask(-bound)
            for h in range(N_HEADS_B):
                if h + DSA_LOOKAHEAD < N_HEADS_B:
                    raw_logits(h + DSA_LOOKAHEAD)
                p, gain = weights(h)
                l_ref[h] += gain * jnp.sum(p, axis=0, keepdims=True)
                acc_ref[h] += gain * jnp.dot(values(h), p.astype(BF16), preferred_element_type=F32)

        @pl.when(far)
        def _():
            heads(lambda h: (jnp.exp2(s_ref[h] + mask_ref[...]), jnp.exp2(far_bias(h))))

        @pl.when(jnp.logical_not(far))
        def _():
            heads(lambda h: (jnp.exp2(s_ref[h] + mask_ref[...] + near_bias(h)), 1.0))

    @pl.when(jnp.logical_not(bounded))
    def _():
        set_mask(0.0)

        @pl.when(far)
        def _():
            for h in range(N_HEADS_B):
                s = logits(h)
                s_ref[h] = s
                cm_ref[h] = jnp.max(s, axis=0, keepdims=True) + far_bias(h)

        @pl.when(jnp.logical_not(far))
        def _():
            for h in range(N_HEADS_B):
                s = logits(h) + near_bias(h)
                s_ref[h] = s
                cm_ref[h] = jnp.max(s, axis=0, keepdims=True)

        for h in range(N_HEADS_B):
            m_old = m_ref[h]
            m_new = jnp.maximum(m_old, cm_ref[h])
            alpha = jnp.exp2(m_old - m_new)
            m_ref[h] = m_new
            shift = jnp.where(far, m_new - far_bias(h), m_new)
            p = jnp.exp2(s_ref[h] - shift)
            l_ref[h] = alpha * l_ref[h] + jnp.sum(p, axis=0, keepdims=True)
            pv = jnp.dot(values(h), p.astype(BF16), preferred_element_type=F32)
            acc_ref[h] = alpha * acc_ref[h] + pv

    last = (qb * ATT_TQ + ATT_TQ - 1) // DSA_TK

    @pl.when(kc == last)
    def _():
        for h in range(N_HEADS_B):
            o_ref[0, :, h * HEAD_DIM:(h + 1) * HEAD_DIM] = (acc_ref[h] / l_ref[h]).T.astype(BF16)


def _dsa_scores(qi_t, ki, wi_t):
    bsz, _, s = qi_t.shape
    qs, ks = _dsa_schedule(s, DSA_TQ, SC_TK)
    npairs = len(qs)
    return pl.pallas_call(
        _dsa_score_kernel,
        grid_spec=pltpu.PrefetchScalarGridSpec(
            num_scalar_prefetch=2,
            grid=(bsz, npairs),
            in_specs=[
                pl.BlockSpec((1, N_IDX_HEADS * IDX_DIM, DSA_TQ), lambda b, t, qt, kt: (b, 0, qt[t])),
                pl.BlockSpec((1, SC_TK, IDX_DIM), lambda b, t, qt, kt: (b, kt[t], 0)),
                pl.BlockSpec((1, N_IDX_HEADS, DSA_TQ), lambda b, t, qt, kt: (b, 0, qt[t])),
            ],
            out_specs=[
                pl.BlockSpec((None, None, None, SC_TK, DSA_TQ),
                             lambda b, t, qt, kt: (b, kt[t], qt[t], 0, 0)),
                pl.BlockSpec((1, 2, DSA_TQ), lambda b, t, qt, kt: (b, 0, qt[t])),
            ],
            scratch_shapes=[pltpu.VMEM((WORD, s // WORD, DSA_TQ), I32),
                            pltpu.VMEM((s // WORD, DSA_TQ), I32),
                            pltpu.VMEM((SC_TK, DSA_TQ), I32)],
        ),
        out_shape=[jax.ShapeDtypeStruct((bsz, s // SC_TK, s // DSA_TQ, SC_TK, DSA_TQ), F32),
                   jax.ShapeDtypeStruct((bsz, 2, s), F32)],
        compiler_params=_cparams(("arbitrary", "arbitrary")),
        name="dsa_scores",
    )(qs, ks, qi_t, ki, wi_t)


def _dsa(main, vt, qi_t, ki, wi_t, bias, logit_bound):
    bsz, s, _ = main.shape
    w = WIDTH_B
    scores, thr = _dsa_scores(qi_t, ki, wi_t)
    ntile = bias.shape[1]
    qs, ks = _dsa_schedule(s, ATT_TQ, DSA_TK)
    out = pl.pallas_call(
        _dsa_attn_kernel,
        grid_spec=pltpu.PrefetchScalarGridSpec(
            num_scalar_prefetch=2,
            grid=(bsz, len(qs)),
            in_specs=[
                pl.BlockSpec((None, None, ATT_QT, DSA_TK, DSA_TQ),
                             lambda b, t, qt, kt: (b, kt[t] // SC_PER_ATT, qt[t], kt[t] % SC_PER_ATT, 0)),
                pl.BlockSpec((1, 2, ATT_TQ), lambda b, t, qt, kt: (b, 0, qt[t])),
                pl.BlockSpec((1, ATT_TQ, w), lambda b, t, qt, kt: (b, qt[t], PROJ_B0)),
                pl.BlockSpec((1, DSA_TK, w), lambda b, t, qt, kt: (b, kt[t], PROJ_B0 + 1)),
                pl.BlockSpec((None, None, w, DSA_TK), lambda b, t, qt, kt: (b, kt[t], 0, 0)),
                pl.BlockSpec((N_HEADS_B, ntile, DSA_TQ, DSA_TQ), lambda b, t, qt, kt: (0, 0, 0, 0),
                             pipeline_mode=pl.Buffered(1)),
                pl.BlockSpec(memory_space=pltpu.SMEM),
            ],
            out_specs=pl.BlockSpec((1, ATT_TQ, w), lambda b, t, qt, kt: (b, qt[t], 0)),
            scratch_shapes=[pltpu.VMEM((N_HEADS_B, HEAD_DIM, ATT_TQ), F32),
                            pltpu.VMEM((N_HEADS_B, 1, ATT_TQ), F32),
                            pltpu.VMEM((N_HEADS_B, 1, ATT_TQ), F32),
                            pltpu.VMEM((N_HEADS_B, DSA_TK, ATT_TQ), F32),
                            pltpu.VMEM((DSA_TK, ATT_TQ), F32),
                            pltpu.VMEM((N_HEADS_B, 1, ATT_TQ), F32)],
        ),
        out_shape=jax.ShapeDtypeStruct((bsz, s, w), BF16),
        compiler_params=_cparams(("arbitrary", "arbitrary")),
        name="dsa_attn",
    )(qs, ks, scores, thr, main, main, vt, bias, logit_bound)
    return out.reshape(bsz * s, w)


def _out_kernel(x_ref, ma_ref, mb_ref, wa_ref, wb_ref, o_ref):
    o_ref[...] = (x_ref[...]
                  + jnp.dot(ma_ref[...], wa_ref[...], preferred_element_type=F32)
                  + jnp.dot(mb_ref[...], wb_ref[...], preferred_element_type=F32))


def _out_proj(x2, mix_a, mix_b, w_out, tm=512):
    m, d = x2.shape
    row = lambda width: pl.BlockSpec((tm, width), lambda i: (i, 0))
    return pl.pallas_call(
        _out_kernel,
        grid=(m // tm,),
        in_specs=[row(d), row(WIDTH_A), row(WIDTH_B),
                  pl.BlockSpec((WIDTH_A, d), lambda i: (0, 0)),
                  pl.BlockSpec((WIDTH_B, d), lambda i: (1, 0))],
        out_specs=row(d),
        out_shape=jax.ShapeDtypeStruct((m, d), F32),
        compiler_params=_cparams(("parallel",)),
        name="out_proj",
    )(x2, mix_a, mix_b, w_out, w_out)


def kernel(x, rel_bias, norm_ffn1, w_ffn1_in, w_ffn1_out, norm_mix, w_in, q_norm_a, k_norm_a,
           q_norm_b, k_norm_b, w_out, norm_ffn2, w_ffn2_in, w_ffn2_out):
    bsz, s, d = x.shape
    m = bsz * s
    depth = norm_ffn1.shape[0]
    assert s % DIL_SPAN == 0 and s % SC_TK == 0, "sequence length must be a multiple of 2048"
    assert w_in.shape[-1] == PROJ_NBLK * PROJ_BLK + IDX_DIM + N_IDX_HEADS
    assert rel_bias.shape == (N_BUCKETS, N_HEADS_A + N_HEADS_B)
    scale = HEAD_DIM ** -0.5
    n_main = PROJ_NBLK * PROJ_BLK
    dil_bias, dsa_bias = _bias_tiles(rel_bias)
    x2 = x.reshape(m, d)
    ones = jnp.ones((PROJ_BLK,), F32)
    tile8 = lambda g: jnp.tile(g, PROJ_BLK // HEAD_DIM)
    for l in range(depth):
        x2 = _ffn(x2, norm_ffn1[l], w_ffn1_in[l].astype(BF16), w_ffn1_out[l].astype(BF16))

        w_main = w_in[l, :, :n_main].astype(BF16)
        w_idx = jnp.pad(w_in[l, :, n_main:], ((0, 0), (0, IDX_PAD - IDX_DIM - N_IDX_HEADS))).astype(BF16)
        head_gain = jnp.stack([tile8(q_norm_a[l] * scale), tile8(k_norm_a[l]), ones,
                               tile8(q_norm_b[l] * (scale * LOG2E)), tile8(k_norm_b[l]), ones, ones])
        main, vt, qi_t, ki, wi_t = _proj(x2, s, norm_mix[l], w_main, w_idx,
                                         head_gain.reshape(PROJ_NBLK, 1, PROJ_BLK))
        main = main.reshape(bsz, s, PROJ_NMAIN * PROJ_BLK)
        vt = vt.reshape(bsz, s // DSA_TK, WIDTH_B, DSA_TK)
        ki = ki.reshape(bsz, s, IDX_DIM)

        mix_a = _dilated(main, dil_bias)
        logit_bound = (1.02 * HEAD_DIM * jnp.max(jnp.abs(q_norm_b[l] * (scale * LOG2E)))
                       * jnp.max(jnp.abs(k_norm_b[l]))
                       + LOG2E * jnp.max(jnp.abs(rel_bias[:, N_HEADS_A:]))).reshape(1)
        mix_b = _dsa(main, vt, qi_t, ki, wi_t, dsa_bias, logit_bound)

        x2 = _out_proj(x2, mix_a, mix_b, w_out[l].astype(BF16))
        x2 = _ffn(x2, norm_ffn2[l], w_ffn2_in[l].astype(BF16), w_ffn2_out[l].astype(BF16))
    return x2.reshape(bsz, s, d)
```

```python
import functools
import math

import jax
import jax.numpy as jnp
import numpy as np
from jax import lax
from jax.experimental import pallas as pl
from jax.experimental.pallas import tpu as pltpu

F32 = jnp.float32
BF16 = jnp.bfloat16
I32 = jnp.int32

HEAD_DIM = 128
N_HEADS_A = 8
N_HEADS_B = 8
WIDTH_A = N_HEADS_A * HEAD_DIM
WIDTH_B = N_HEADS_B * HEAD_DIM
DILATIONS = (1, 4, 16)
DIL_BLK = 128
N_IDX_HEADS = 16
IDX_DIM = 64
TOPK_MAX = 256
N_BUCKETS = 32
REL_MAX_DIST = 2048
EPS = 1e-6
IDX_PAD = 128
IDX_SCALE = (IDX_DIM ** -0.5) * (N_IDX_HEADS ** -0.5)
LOG2E = math.log2(math.e)

VMEM_LIMIT = 56 * 1024 * 1024

DSA_TQ = 256
DSA_TK = 512
DSA_HALVES = DSA_TK // DSA_TQ
ATT_TQ = 512
ATT_QT = ATT_TQ // DSA_TQ
WORD = 32
SC_TK = 1024
SC_PER_ATT = SC_TK // DSA_TK
CHUNK_WORDS = SC_TK // WORD
SCORE_ROWS = 64
NEG_BIG = -1e30
DSA_SAFE_BOUND = 40.0
DSA_LOOKAHEAD = 2
INT_MIN = -2 ** 31


def _near_tiles():
    exact = N_BUCKETS // 2
    j = 1
    while True:
        d = j * DSA_TQ - (DSA_TQ - 1)
        b = exact + int(math.log(d / exact) / math.log(REL_MAX_DIST / exact) * (N_BUCKETS - exact) - 0.02)
        if d >= exact and b >= N_BUCKETS - 1:
            return j
        j += 1


DSA_NEAR = _near_tiles()


def _cparams(sem):
    return pltpu.CompilerParams(dimension_semantics=sem, vmem_limit_bytes=VMEM_LIMIT)


def _rel_bucket(dist):
    exact = N_BUCKETS // 2
    df = jnp.maximum(dist, 1).astype(F32)
    large = exact + (jnp.log(df / exact) / math.log(REL_MAX_DIST / exact)
                     * (N_BUCKETS - exact)).astype(I32)
    large = jnp.minimum(large, N_BUCKETS - 1)
    return jnp.where(dist < exact, dist, large)


def _lookup(tab_ref, bucket, col):
    val = jnp.zeros(bucket.shape, F32)
    for b in range(N_BUCKETS):
        val = jnp.where(bucket == b, tab_ref[b, col], val)
    return val


def _dil_bias_kernel(tab_ref, o_ref):
    dil = jnp.left_shift(1, 2 * pl.program_id(0))
    i = lax.broadcasted_iota(I32, (DIL_BLK, 2 * DIL_BLK), 0)
    j = lax.broadcasted_iota(I32, (DIL_BLK, 2 * DIL_BLK), 1)
    bucket = _rel_bucket(jnp.clip(DIL_BLK + i - j, 0, DIL_BLK) * dil)
    for h in range(N_HEADS_A):
        o_ref[0, h] = _lookup(tab_ref, bucket, h)


def _dsa_bias_kernel(tab_ref, o_ref):
    jj = pl.program_id(0)
    s = lax.broadcasted_iota(I32, (DSA_TQ, DSA_TQ), 0)
    t = lax.broadcasted_iota(I32, (DSA_TQ, DSA_TQ), 1)
    bucket = _rel_bucket(jnp.maximum((jj - 1) * DSA_TQ + t - s, 0))
    for h in range(N_HEADS_B):
        o_ref[h, 0] = _lookup(tab_ref, bucket, N_HEADS_A + h) * LOG2E


def _bias_tiles(rel_bias):
    smem = pl.BlockSpec(memory_space=pltpu.SMEM)
    dil = pl.pallas_call(
        _dil_bias_kernel,
        grid=(len(DILATIONS),),
        in_specs=[smem],
        out_specs=pl.BlockSpec((1, N_HEADS_A, DIL_BLK, 2 * DIL_BLK), lambda g: (g, 0, 0, 0)),
        out_shape=jax.ShapeDtypeStruct((len(DILATIONS), N_HEADS_A, DIL_BLK, 2 * DIL_BLK), F32),
        name="dil_bias",
    )(rel_bias)
    ntile = DSA_NEAR + 2
    dsa = pl.pallas_call(
        _dsa_bias_kernel,
        grid=(ntile,),
        in_specs=[smem],
        out_specs=pl.BlockSpec((N_HEADS_B, 1, DSA_TQ, DSA_TQ), lambda g: (0, g, 0, 0)),
        out_shape=jax.ShapeDtypeStruct((N_HEADS_B, ntile, DSA_TQ, DSA_TQ), F32),
        name="dsa_bias",
    )(rel_bias)
    return dil, dsa


def _ffn_kernel(x_ref, g_ref, wg_ref, wu_ref, wo_ref, o_ref, h_ref, acc_ref):
    f = pl.program_id(1)

    @pl.when(f == 0)
    def _():
        x = x_ref[...]
        ms = jnp.mean(x * x, axis=-1, keepdims=True)
        h_ref[...] = (x * lax.rsqrt(ms + EPS) * g_ref[...]).astype(BF16)
        acc_ref[...] = jnp.zeros_like(acc_ref)

    h = h_ref[...]
    gate = jnp.dot(h, wg_ref[...], preferred_element_type=F32)
    up = jnp.dot(h, wu_ref[...], preferred_element_type=F32)
    act = (gate * jax.nn.sigmoid(gate) * up).astype(BF16)
    acc_ref[...] += jnp.dot(act, wo_ref[...], preferred_element_type=F32)

    @pl.when(f == pl.num_programs(1) - 1)
    def _():
        o_ref[...] = x_ref[...] + 0.5 * acc_ref[...]


def _ffn(x2, gain, w_in, w_out, tm=1024, tf=256):
    m, d = x2.shape
    d_ff = w_out.shape[0]
    nf = d_ff // tf
    return pl.pallas_call(
        _ffn_kernel,
        grid=(m // tm, nf),
        in_specs=[
            pl.BlockSpec((tm, d), lambda i, f: (i, 0)),
            pl.BlockSpec((1, d), lambda i, f: (0, 0)),
            pl.BlockSpec((d, tf), lambda i, f: (0, f)),
            pl.BlockSpec((d, tf), lambda i, f: (0, f + nf)),
            pl.BlockSpec((tf, d), lambda i, f: (f, 0)),
        ],
        out_specs=pl.BlockSpec((tm, d), lambda i, f: (i, 0)),
        out_shape=jax.ShapeDtypeStruct((m, d), F32),
        scratch_shapes=[pltpu.VMEM((tm, d), BF16), pltpu.VMEM((tm, d), F32)],
        compiler_params=_cparams(("parallel", "arbitrary")),
        name="ffn",
    )(x2, gain.reshape(1, d), w_in, w_in, w_out)


PROJ_BLK = 1024
PROJ_NBLK = 7
PROJ_B0 = 3
PROJ_NMAIN = 5
PROJ_PARTS = 4
_NORM_BLOCKS = (0, 1, 3, 4)


def _proj_kernel(x_ref, g_ref, w_ref, widx_ref, hg_ref, o_ref, vt_ref, qit_ref, ki_ref, wit_ref, h_ref):
    j = pl.program_id(1)

    @pl.when(j == 0)
    def _():
        x = x_ref[...]
        ms = jnp.mean(x * x, axis=-1, keepdims=True)
        h_ref[...] = (x * lax.rsqrt(ms + EPS) * g_ref[...]).astype(BF16)

    rows = h_ref.shape[0] // PROJ_PARTS
    parts = [slice(part * rows, (part + 1) * rows) for part in range(PROJ_PARTS)]
    project = lambda rs: jnp.dot(h_ref[rs, :], w_ref[...], preferred_element_type=F32)

    @pl.when(j < PROJ_NMAIN)
    def _():
        is_norm = functools.reduce(jnp.logical_or, [j == b for b in _NORM_BLOCKS])
        for rs in parts:
            y = project(rs)
            for hd in range(PROJ_BLK // HEAD_DIM):
                sl = slice(hd * HEAD_DIM, (hd + 1) * HEAD_DIM)
                yh = y[:, sl]
                ms = jnp.mean(yh * yh, axis=-1, keepdims=True)
                inv = jnp.where(is_norm, lax.rsqrt(ms + EPS), 1.0)
                o_ref[rs, sl] = (yh * inv * hg_ref[0, :, sl]).astype(BF16)

    @pl.when(j == PROJ_NMAIN)
    def _():
        for rs in parts:
            chunk, off = divmod(rs.start, DSA_TK)
            vt_ref[chunk, :, off:off + rows] = project(rs).astype(BF16).T

    @pl.when(j == PROJ_NMAIN + 1)
    def _():
        for rs in parts:
            qit_ref[0, :, rs] = project(rs).astype(BF16).T
        idx = jnp.dot(h_ref[...], widx_ref[...], preferred_element_type=F32)
        ki_ref[...] = idx[:, :IDX_DIM].astype(BF16)
        wit_ref[0] = idx.T[IDX_DIM:IDX_DIM + N_IDX_HEADS, :]


def _proj(x2, s, gain, w_main, w_idx, head_gain, tm=1024):
    m, d = x2.shape
    per_seq = s // tm
    return pl.pallas_call(
        _proj_kernel,
        grid=(m // tm, PROJ_NBLK),
        in_specs=[
            pl.BlockSpec((tm, d), lambda i, j: (i, 0)),
            pl.BlockSpec((1, d), lambda i, j: (0, 0)),
            pl.BlockSpec((d, PROJ_BLK), lambda i, j: (0, j)),
            pl.BlockSpec((d, IDX_PAD), lambda i, j: (0, 0)),
            pl.BlockSpec((1, 1, PROJ_BLK), lambda i, j: (j, 0, 0)),
        ],
        out_specs=[
            pl.BlockSpec((tm, PROJ_BLK), lambda i, j: (i, jnp.minimum(j, PROJ_NMAIN - 1))),
            pl.BlockSpec((tm // DSA_TK, PROJ_BLK, DSA_TK), lambda i, j: (i, 0, 0)),
            pl.BlockSpec((1, PROJ_BLK, tm), lambda i, j: (i // per_seq, 0, i % per_seq)),
            pl.BlockSpec((tm, IDX_DIM), lambda i, j: (i, 0)),
            pl.BlockSpec((1, N_IDX_HEADS, tm), lambda i, j: (i // per_seq, 0, i % per_seq)),
        ],
        out_shape=[
            jax.ShapeDtypeStruct((m, PROJ_NMAIN * PROJ_BLK), BF16),
            jax.ShapeDtypeStruct((m // DSA_TK, PROJ_BLK, DSA_TK), BF16),
            jax.ShapeDtypeStruct((m // s, PROJ_BLK, s), BF16),
            jax.ShapeDtypeStruct((m, IDX_DIM), BF16),
            jax.ShapeDtypeStruct((m // s, N_IDX_HEADS, s), F32),
        ],
        scratch_shapes=[pltpu.VMEM((tm, d), BF16)],
        compiler_params=_cparams(("parallel", "arbitrary")),
        name="proj",
    )(x2, gain.reshape(1, d), w_main, w_idx, head_gain)


_NT = (((1,), (1,)), ((), ()))


DIL_SPAN = DIL_BLK * max(DILATIONS)
DIL_G = 2
DIL_UNITS = DIL_SPAN // DIL_BLK
DIL_UNROLL = DIL_UNITS


def _dilated_kernel(q_ref, kp_ref, kc_ref, vp_ref, vc_ref, b_ref, o_ref, qf, kf, vf, ob, lse_ref):
    n = pl.program_id(1)
    h0 = pl.program_id(2) * DIL_G
    for g in range(DIL_G):
        sl = slice(g * HEAD_DIM, (g + 1) * HEAD_DIM)
        qf[g] = q_ref[0, :, sl].astype(F32)
        kf[g, :DIL_SPAN] = kp_ref[0, :, sl].astype(F32)
        kf[g, DIL_SPAN:] = kc_ref[0, :, sl].astype(F32)
        vf[g, :DIL_SPAN] = vp_ref[0, :, sl].astype(F32)
        vf[g, DIL_SPAN:] = vc_ref[0, :, sl].astype(F32)
    i = lax.broadcasted_iota(I32, (DIL_BLK, 2 * DIL_BLK), 0)
    j = lax.broadcasted_iota(I32, (DIL_BLK, 2 * DIL_BLK), 1)
    band = jnp.logical_and(j >= i, j <= i + DIL_BLK)
    first = j >= DIL_BLK

    for br, dil in enumerate(DILATIONS):
        span = DIL_BLK * dil

        def unit(idx, carry, br=br, dil=dil, span=span):
            blk = idx // dil
            base = blk * span + idx % dil
            valid = jnp.logical_and(band, jnp.logical_or(first, n * (DIL_SPAN // span) + blk > 0))
            qrows = pl.ds(base, DIL_BLK, stride=dil)
            krows = pl.ds(DIL_SPAN + base - span, 2 * DIL_BLK, stride=dil)
            for g in range(DIL_G):
                q = qf[g, qrows, :].astype(BF16)
                k = kf[g, krows, :].astype(BF16)
                v = vf[g, krows, :].astype(BF16)
                s = lax.dot_general(q, k, _NT, preferred_element_type=F32) + b_ref[br, h0 + g]
                s = jnp.where(valid, s, -jnp.inf)
                mx = jnp.max(s, axis=-1, keepdims=True)
                p = jnp.exp(s - mx)
                den = jnp.sum(p, axis=-1, keepdims=True)
                ob[br, g, qrows, :] = jnp.dot(p.astype(BF16), v, preferred_element_type=F32) / den
                lse_ref[br, g, qrows, :] = jnp.broadcast_to(mx + jnp.log(den), (DIL_BLK, HEAD_DIM))
            return carry

        lax.fori_loop(0, DIL_UNITS, unit, 0, unroll=DIL_UNROLL)

    for g in range(DIL_G):
        l1, l2, l3 = lse_ref[0, g], lse_ref[1, g], lse_ref[2, g]
        mx = jnp.maximum(jnp.maximum(l1, l2), l3)
        e1, e2, e3 = jnp.exp(l1 - mx), jnp.exp(l2 - mx), jnp.exp(l3 - mx)
        mix = (e1 * ob[0, g] + e2 * ob[1, g] + e3 * ob[2, g]) / (e1 + e2 + e3)
        o_ref[0, :, g * HEAD_DIM:(g + 1) * HEAD_DIM] = mix.astype(BF16)


def _dilated(main, bias):
    bsz, s, _ = main.shape
    gw = DIL_G * HEAD_DIM
    ng = WIDTH_A // gw
    cur = lambda c: pl.BlockSpec((1, DIL_SPAN, gw), lambda b, n, g: (b, n, c * ng + g))
    prev = lambda c: pl.BlockSpec((1, DIL_SPAN, gw), lambda b, n, g: (b, jnp.maximum(n - 1, 0), c * ng + g))
    out = pl.pallas_call(
        _dilated_kernel,
        grid=(bsz, s // DIL_SPAN, ng),
        in_specs=[cur(0), prev(1), cur(1), prev(2), cur(2),
                  pl.BlockSpec(bias.shape, lambda b, n, g: (0, 0, 0, 0), pipeline_mode=pl.Buffered(1))],
        out_specs=pl.BlockSpec((1, DIL_SPAN, gw), lambda b, n, g: (b, n, g)),
        out_shape=jax.ShapeDtypeStruct((bsz, s, WIDTH_A), BF16),
        scratch_shapes=[pltpu.VMEM((DIL_G, DIL_SPAN, HEAD_DIM), F32),
                        pltpu.VMEM((DIL_G, 2 * DIL_SPAN, HEAD_DIM), F32),
                        pltpu.VMEM((DIL_G, 2 * DIL_SPAN, HEAD_DIM), F32),
                        pltpu.VMEM((len(DILATIONS), DIL_G, DIL_SPAN, HEAD_DIM), F32),
                        pltpu.VMEM((len(DILATIONS), DIL_G, DIL_SPAN, HEAD_DIM), F32)],
        compiler_params=_cparams(("parallel", "parallel", "parallel")),
        name="dilated",
    )(main, main, main, main, main, bias)
    return out.reshape(bsz * s, WIDTH_A)


def _dsa_schedule(s, tq, tk):
    qs, ks = [], []
    for i in range(s // tq):
        last = (i * tq + tq - 1) // tk
        for k in range(last + 1):
            qs.append(i)
            ks.append(k)
    return np.asarray(qs, np.int32), np.asarray(ks, np.int32)


def _sortable(bits):
    return bits ^ (lax.shift_right_arithmetic(bits, 31) & 0x7FFFFFFF)


def _bit_transpose(words):
    a = list(words)
    mask, j = 0x0000FFFF, 16
    while j:
        k = 0
        while k < WORD:
            t = (a[k] ^ lax.shift_right_logical(a[k + j], j)) & mask
            a[k] = a[k] ^ t
            a[k + j] = a[k + j] ^ lax.shift_left(t, j)
            k = (k + j + 1) & ~j
        j >>= 1
        mask = (mask ^ (mask << j)) & 0xFFFFFFFF
    return a


def _dsa_score_kernel(qtab, ktab, qit_ref, ki_ref, w_ref, sc_ref, thr_ref,
                      planes_ref, alive_ref, stage_ref):
    step = pl.program_id(1)
    qb = qtab[step]
    kc = ktab[step]
    w = w_ref[0] * IDX_SCALE
    srow = lax.broadcasted_iota(I32, (SCORE_ROWS, DSA_TQ), 0)
    tcol = lax.broadcasted_iota(I32, (SCORE_ROWS, DSA_TQ), 1)
    row0 = pl.multiple_of(kc * CHUNK_WORDS, CHUNK_WORDS)
    group = WORD * 8
    for g in range(SC_TK // group):
        for r0 in range(g * group, (g + 1) * group, SCORE_ROWS):
            kidx = ki_ref[0, r0:r0 + SCORE_ROWS, :]
            acc = jnp.zeros((SCORE_ROWS, DSA_TQ), F32)
            for h in range(N_IDX_HEADS):
                d = jnp.dot(kidx, qit_ref[0, h * IDX_DIM:(h + 1) * IDX_DIM, :],
                            preferred_element_type=F32)
                acc = acc + jnp.maximum(d, 0.0) * w[h:h + 1, :]
            causal = srow - tcol <= qb * DSA_TQ - kc * SC_TK - r0
            acc = jnp.where(causal, acc, -jnp.inf)
            sc_ref[r0:r0 + SCORE_ROWS, :] = acc
            stage_ref[r0:r0 + SCORE_ROWS, :] = _sortable(lax.bitcast_convert_type(acc, I32)) ^ INT_MIN
        for lanes in (slice(c, c + 128) for c in range(0, DSA_TQ, 128)):
            planes = _bit_transpose([stage_ref[g * group + 8 * n:g * group + 8 * (n + 1), lanes]
                                     for n in range(WORD)])
            for n in range(WORD):
                planes_ref[n, pl.ds(pl.multiple_of(row0 + 8 * g, 8), 8), lanes] = planes[n]

    last = (qb * DSA_TQ + DSA_TQ - 1) // SC_TK

    @pl.when(kc == last)
    def _():
        nwords = alive_ref.shape[0]
        rb = min(128, nwords)
        nrows = (last + 1) * CHUNK_WORDS
        nblk = (nrows + rb - 1) // rb
        tpos = qb * DSA_TQ + lax.broadcasted_iota(I32, (1, DSA_TQ), 1)
        want0 = jnp.minimum(tpos + 1, TOPK_MAX)
        rows = lambda blk: pl.ds(pl.multiple_of(blk * rb, rb), rb)

        def hits(alive, n, blk):
            hit = lax.population_count(alive & planes_ref[n, rows(blk), :])
            return hit.reshape(rb // 8, 8, DSA_TQ).sum(axis=0)

        def init(blk, cnt):
            ridx = blk * rb + lax.broadcasted_iota(I32, (rb, DSA_TQ), 0)
            alive = jnp.where(ridx < nrows, -1, 0)
            alive_ref[rows(blk), :] = alive
            return cnt + hits(alive, 0, blk)
        zero = jnp.zeros((8, DSA_TQ), I32)
        cnt0 = lax.fori_loop(0, nblk, init, zero)

        def bit_step(n, carry):
            r, want, cnt8 = carry
            cnt = cnt8.sum(axis=0, keepdims=True)
            take = cnt >= want
            r = r | jnp.where(take, lax.shift_left(jnp.int32(1), 31 - n), 0)
            want = jnp.where(take, want, want - cnt)
            flip = jnp.where(take, 0, -1)
            nxt = jnp.minimum(n + 1, WORD - 1)

            def sweep(blk, acc):
                alive = alive_ref[rows(blk), :] & (planes_ref[n, rows(blk), :] ^ flip)
                alive_ref[rows(blk), :] = alive
                return acc + hits(alive, nxt, blk)
            return r, want, lax.fori_loop(0, nblk, sweep, zero)

        r, want, _ = lax.fori_loop(0, WORD, bit_step, (jnp.zeros((1, DSA_TQ), I32), want0, cnt0))
        thr_ref[0, 0:1, :] = lax.bitcast_convert_type(_sortable(r ^ INT_MIN), F32)
        thr_ref[0, 1:2, :] = jnp.full((1, DSA_TQ), nwords * WORD, F32)

        def live(blk, acc):
            hit = lax.population_count(alive_ref[rows(blk), :])
            return acc + hit.reshape(rb // 8, 8, DSA_TQ).sum(axis=0)
        tied = lax.fori_loop(0, nblk, live, zero).sum(axis=0, keepdims=True)

        @pl.when(jnp.max(tied - want) > 0)
        def _():
            blk_bits = max(1, (nwords // 8 - 1).bit_length())
            lane_patterns = (-0x10000, -0xFF0100, -0xF0F0F10, -0x33333334, -0x55555556)
            nsec = blk_bits + len(lane_patterns) + 3

            def sec_plane(k, blk):
                ridx = blk * rb + lax.broadcasted_iota(I32, (rb, 1), 0)
                if k < blk_bits:
                    bit = 3 + blk_bits - 1 - k
                elif k < blk_bits + len(lane_patterns):
                    return jnp.full((rb, 1), lane_patterns[k - blk_bits], I32)
                else:
                    bit = nsec - 1 - k
                return jnp.where((lax.shift_right_logical(ridx, bit) & 1) == 0, -1, 0)

            def sec_count(k):
                def body(blk, acc):
                    hit = lax.population_count(alive_ref[rows(blk), :] & sec_plane(k, blk))
                    return acc + hit.reshape(rb // 8, 8, DSA_TQ).sum(axis=0)
                return lax.fori_loop(0, nblk, body, zero).sum(axis=0, keepdims=True)

            left = want
            code = jnp.zeros((1, DSA_TQ), I32)
            for k in range(nsec):
                cnt = sec_count(k)
                take = cnt >= left
                code = code | jnp.where(take, 1 << (nsec - 1 - k), 0)
                left = jnp.where(take, left, left - cnt)
                flip = jnp.where(take, 0, -1)

                def update(blk, c, k=k, flip=flip):
                    alive_ref[rows(blk), :] = alive_ref[rows(blk), :] & (sec_plane(k, blk) ^ flip)
                    return c
                lax.fori_loop(0, nblk, update, 0)
            inv = ~code
            cutoff = ((lax.shift_right_logical(inv, 8) & ((1 << blk_bits) - 1)) * (8 * WORD)
                      + 8 * ((code >> 3) & (WORD - 1) ^ (WORD - 1)) + (inv & 7))
            thr_ref[0, 1:2, :] = cutoff.astype(F32)


def _dsa_attn_kernel(qtab, ktab, sc_ref, thr_ref, q_ref, k_ref, vt_ref, b_ref, bound_ref, o_ref,
                     acc_ref, m_ref, l_ref, s_ref, mask_ref, cm_ref):
    step = pl.program_id(1)
    qb = qtab[step]
    kc = ktab[step]

    @pl.when(kc == 0)
    def _():
        acc_ref[...] = jnp.zeros_like(acc_ref)
        m_ref[...] = jnp.full_like(m_ref, NEG_BIG)
        l_ref[...] = jnp.zeros_like(l_ref)

    bound = bound_ref[0]
    bounded = bound <= DSA_SAFE_BOUND

    def set_mask(kept):
        key = (kc * DSA_TK + lax.broadcasted_iota(I32, (DSA_TK, DSA_TQ), 0)).astype(F32)
        for b in range(ATT_QT):
            cols = slice(b * DSA_TQ, (b + 1) * DSA_TQ)
            thr = thr_ref[0, 0:1, cols]
            tied = jnp.logical_and(sc_ref[b] == thr, key <= thr_ref[0, 1:2, cols])
            mask_ref[:, cols] = jnp.where(jnp.logical_or(sc_ref[b] > thr, tied), kept, -jnp.inf)

    tiles = [[jnp.clip((qb * ATT_QT + b) - (kc * DSA_HALVES + a), -1, DSA_NEAR) + 1
              for b in range(ATT_QT)] for a in range(DSA_HALVES)]
    far = qb * ATT_QT - (kc * DSA_HALVES + DSA_HALVES - 1) >= DSA_NEAR
    far_bias = lambda h: b_ref[h, DSA_NEAR + 1, 0:1, 0:1]

    def logits(h):
        sl = slice(h * HEAD_DIM, (h + 1) * HEAD_DIM)
        s = lax.dot_general(k_ref[0, :, sl], q_ref[0, :, sl], _NT, preferred_element_type=F32)
        return s + mask_ref[...]

    def near_bias(h):
        return jnp.concatenate(
            [jnp.concatenate([b_ref[h, tiles[a][b]] for b in range(ATT_QT)], axis=1)
             for a in range(DSA_HALVES)], axis=0)

    def values(h):
        return vt_ref[h * HEAD_DIM:(h + 1) * HEAD_DIM, :]

    @pl.when(bounded)
    def _():
        def raw_logits(h):
            sl = slice(h * HEAD_DIM, (h + 1) * HEAD_DIM)
            s_ref[h] = lax.dot_general(k_ref[0, :, sl], q_ref[0, :, sl], _NT,
                                       preferred_element_type=F32)

        def heads(weights):
            for h in range(DSA_LOOKAHEAD):
                raw_logits(h)
            set_mask(-bound)
            for h in range(N_HEADS_B):
                if h + DSA_LOOKAHEAD < N_HEADS_B:
                    raw_logits(h + DSA_LOOKAHEAD)
                p, gain = weights(h)
                l_ref[h] += gain * jnp.sum(p, axis=0, keepdims=True)
                acc_ref[h] += gain * jnp.dot(values(h), p.astype(BF16), preferred_element_type=F32)

        @pl.when(far)
        def _():
            heads(lambda h: (jnp.exp2(s_ref[h] + mask_ref[...]), jnp.exp2(far_bias(h))))

        @pl.when(jnp.logical_not(far))
        def _():
            heads(lambda h: (jnp.exp2(s_ref[h] + mask_ref[...] + near_bias(h)), 1.0))

    @pl.when(jnp.logical_not(bounded))
    def _():
        set_mask(0.0)

        @pl.when(far)
        def _():
            for h in range(N_HEADS_B):
                s = logits(h)
                s_ref[h] = s
                cm_ref[h] = jnp.max(s, axis=0, keepdims=True) + far_bias(h)

        @pl.when(jnp.logical_not(far))
        def _():
            for h in range(N_HEADS_B):
                s = logits(h) + near_bias(h)
                s_ref[h] = s
                cm_ref[h] = jnp.max(s, axis=0, keepdims=True)

        for h in range(N_HEADS_B):
            m_old = m_ref[h]
            m_new = jnp.maximum(m_old, cm_ref[h])
            alpha = jnp.exp2(m_old - m_new)
            m_ref[h] = m_new
            shift = jnp.where(far, m_new - far_bias(h), m_new)
            p = jnp.exp2(s_ref[h] - shift)
            l_ref[h] = alpha * l_ref[h] + jnp.sum(p, axis=0, keepdims=True)
            pv = jnp.dot(values(h), p.astype(BF16), preferred_element_type=F32)
            acc_ref[h] = alpha * acc_ref[h] + pv

    last = (qb * ATT_TQ + ATT_TQ - 1) // DSA_TK

    @pl.when(kc == last)
    def _():
        for h in range(N_HEADS_B):
            o_ref[0, :, h * HEAD_DIM:(h + 1) * HEAD_DIM] = (acc_ref[h] / l_ref[h]).T.astype(BF16)


def _dsa_scores(qi_t, ki, wi_t):
    bsz, _, s = qi_t.shape
    qs, ks = _dsa_schedule(s, DSA_TQ, SC_TK)
    npairs = len(qs)
    return pl.pallas_call(
        _dsa_score_kernel,
        grid_spec=pltpu.PrefetchScalarGridSpec(
            num_scalar_prefetch=2,
            grid=(bsz, npairs),
            in_specs=[
                pl.BlockSpec((1, N_IDX_HEADS * IDX_DIM, DSA_TQ), lambda b, t, qt, kt: (b, 0, qt[t])),
                pl.BlockSpec((1, SC_TK, IDX_DIM), lambda b, t, qt, kt: (b, kt[t], 0)),
                pl.BlockSpec((1, N_IDX_HEADS, DSA_TQ), lambda b, t, qt, kt: (b, 0, qt[t])),
            ],
            out_specs=[
                pl.BlockSpec((None, None, None, SC_TK, DSA_TQ),
                             lambda b, t, qt, kt: (b, kt[t], qt[t], 0, 0)),
                pl.BlockSpec((1, 2, DSA_TQ), lambda b, t, qt, kt: (b, 0, qt[t])),
            ],
            scratch_shapes=[pltpu.VMEM((WORD, s // WORD, DSA_TQ), I32),
                            pltpu.VMEM((s // WORD, DSA_TQ), I32),
                            pltpu.VMEM((SC_TK, DSA_TQ), I32)],
        ),
        out_shape=[jax.ShapeDtypeStruct((bsz, s // SC_TK, s // DSA_TQ, SC_TK, DSA_TQ), F32),
                   jax.ShapeDtypeStruct((bsz, 2, s), F32)],
        compiler_params=_cparams(("arbitrary", "arbitrary")),
        name="dsa_scores",
    )(qs, ks, qi_t, ki, wi_t)


def _dsa(main, vt, qi_t, ki, wi_t, bias, logit_bound):
    bsz, s, _ = main.shape
    w = WIDTH_B
    scores, thr = _dsa_scores(qi_t, ki, wi_t)
    ntile = bias.shape[1]
    qs, ks = _dsa_schedule(s, ATT_TQ, DSA_TK)
    out = pl.pallas_call(
        _dsa_attn_kernel,
        grid_spec=pltpu.PrefetchScalarGridSpec(
            num_scalar_prefetch=2,
            grid=(bsz, len(qs)),
            in_specs=[
                pl.BlockSpec((None, None, ATT_QT, DSA_TK, DSA_TQ),
                             lambda b, t, qt, kt: (b, kt[t] // SC_PER_ATT, qt[t], kt[t] % SC_PER_ATT, 0)),
                pl.BlockSpec((1, 2, ATT_TQ), lambda b, t, qt, kt: (b, 0, qt[t])),
                pl.BlockSpec((1, ATT_TQ, w), lambda b, t, qt, kt: (b, qt[t], PROJ_B0)),
                pl.BlockSpec((1, DSA_TK, w), lambda b, t, qt, kt: (b, kt[t], PROJ_B0 + 1)),
                pl.BlockSpec((None, None, w, DSA_TK), lambda b, t, qt, kt: (b, kt[t], 0, 0)),
                pl.BlockSpec((N_HEADS_B, ntile, DSA_TQ, DSA_TQ), lambda b, t, qt, kt: (0, 0, 0, 0),
                             pipeline_mode=pl.Buffered(1)),
                pl.BlockSpec(memory_space=pltpu.SMEM),
            ],
            out_specs=pl.BlockSpec((1, ATT_TQ, w), lambda b, t, qt, kt: (b, qt[t], 0)),
            scratch_shapes=[pltpu.VMEM((N_HEADS_B, HEAD_DIM, ATT_TQ), F32),
                            pltpu.VMEM((N_HEADS_B, 1, ATT_TQ), F32),
                            pltpu.VMEM((N_HEADS_B, 1, ATT_TQ), F32),
                            pltpu.VMEM((N_HEADS_B, DSA_TK, ATT_TQ), F32),
                            pltpu.VMEM((DSA_TK, ATT_TQ), F32),
                            pltpu.VMEM((N_HEADS_B, 1, ATT_TQ), F32)],
        ),
        out_shape=jax.ShapeDtypeStruct((bsz, s, w), BF16),
        compiler_params=_cparams(("arbitrary", "arbitrary")),
        name="dsa_attn",
    )(qs, ks, scores, thr, main, main, vt, bias, logit_bound)
    return out.reshape(bsz * s, w)


def _out_kernel(x_ref, ma_ref, mb_ref, wa_ref, wb_ref, o_ref):
    o_ref[...] = (x_ref[...]
                  + jnp.dot(ma_ref[...], wa_ref[...], preferred_element_type=F32)
                  + jnp.dot(mb_ref[...], wb_ref[...], preferred_element_type=F32))


def _out_proj(x2, mix_a, mix_b, w_out, tm=512):
    m, d = x2.shape
    row = lambda width: pl.BlockSpec((tm, width), lambda i: (i, 0))
    return pl.pallas_call(
        _out_kernel,
        grid=(m // tm,),
        in_specs=[row(d), row(WIDTH_A), row(WIDTH_B),
                  pl.BlockSpec((WIDTH_A, d), lambda i: (0, 0)),
                  pl.BlockSpec((WIDTH_B, d), lambda i: (1, 0))],
        out_specs=row(d),
        out_shape=jax.ShapeDtypeStruct((m, d), F32),
        compiler_params=_cparams(("parallel",)),
        name="out_proj",
    )(x2, mix_a, mix_b, w_out, w_out)


def kernel(x, rel_bias, norm_ffn1, w_ffn1_in, w_ffn1_out, norm_mix, w_in, q_norm_a, k_norm_a,
           q_norm_b, k_norm_b, w_out, norm_ffn2, w_ffn2_in, w_ffn2_out):
    bsz, s, d = x.shape
    m = bsz * s
    depth = norm_ffn1.shape[0]
    assert s % DIL_SPAN == 0 and s % SC_TK == 0, "sequence length must be a multiple of 2048"
    assert w_in.shape[-1] == PROJ_NBLK * PROJ_BLK + IDX_DIM + N_IDX_HEADS
    assert rel_bias.shape == (N_BUCKETS, N_HEADS_A + N_HEADS_B)
    scale = HEAD_DIM ** -0.5
    n_main = PROJ_NBLK * PROJ_BLK
    dil_bias, dsa_bias = _bias_tiles(rel_bias)
    x2 = x.reshape(m, d)
    ones = jnp.ones((PROJ_BLK,), F32)
    tile8 = lambda g: jnp.tile(g, PROJ_BLK // HEAD_DIM)
    for l in range(depth):
        x2 = _ffn(x2, norm_ffn1[l], w_ffn1_in[l].astype(BF16), w_ffn1_out[l].astype(BF16))

        w_main = w_in[l, :, :n_main].astype(BF16)
        w_idx = jnp.pad(w_in[l, :, n_main:], ((0, 0), (0, IDX_PAD - IDX_DIM - N_IDX_HEADS))).astype(BF16)
        head_gain = jnp.stack([tile8(q_norm_a[l] * scale), tile8(k_norm_a[l]), ones,
                               tile8(q_norm_b[l] * (scale * LOG2E)), tile8(k_norm_b[l]), ones, ones])
        main, vt, qi_t, ki, wi_t = _proj(x2, s, norm_mix[l], w_main, w_idx,
                                         head_gain.reshape(PROJ_NBLK, 1, PROJ_BLK))
        main = main.reshape(bsz, s, PROJ_NMAIN * PROJ_BLK)
        vt = vt.reshape(bsz, s // DSA_TK, WIDTH_B, DSA_TK)
        ki = ki.reshape(bsz, s, IDX_DIM)

        mix_a = _dilated(main, dil_bias)
        logit_bound = (1.02 * HEAD_DIM * jnp.max(jnp.abs(q_norm_b[l] * (scale * LOG2E)))
                       * jnp.max(jnp.abs(k_norm_b[l]))
                       + LOG2E * jnp.max(jnp.abs(rel_bias[:, N_HEADS_A:]))).reshape(1)
        mix_b = _dsa(main, vt, qi_t, ki, wi_t, dsa_bias, logit_bound)

        x2 = _out_proj(x2, mix_a, mix_b, w_out[l].astype(BF16))
        x2 = _ffn(x2, norm_ffn2[l], w_ffn2_in[l].astype(BF16), w_ffn2_out[l].astype(BF16))
    return x2.reshape(bsz, s, d)
```

```python
import functools
import math

import jax
import jax.numpy as jnp
import numpy as np
from jax import lax
from jax.experimental import pallas as pl
from jax.experimental.pallas import tpu as pltpu

F32 = jnp.float32
BF16 = jnp.bfloat16
I32 = jnp.int32

HEAD_DIM = 128
N_HEADS_A = 8
N_HEADS_B = 8
WIDTH_A = N_HEADS_A * HEAD_DIM
WIDTH_B = N_HEADS_B * HEAD_DIM
DILATIONS = (1, 4, 16)
DIL_BLK = 128
N_IDX_HEADS = 16
IDX_DIM = 64
TOPK_MAX = 256
N_BUCKETS = 32
REL_MAX_DIST = 2048
EPS = 1e-6
IDX_PAD = 128
IDX_SCALE = (IDX_DIM ** -0.5) * (N_IDX_HEADS ** -0.5)
LOG2E = math.log2(math.e)

VMEM_LIMIT = 56 * 1024 * 1024

DSA_TQ = 256
DSA_TK = 512
DSA_HALVES = DSA_TK // DSA_TQ
ATT_TQ = 512
ATT_QT = ATT_TQ // DSA_TQ
WORD = 32
SC_TK = 1024
SC_PER_ATT = SC_TK // DSA_TK
CHUNK_WORDS = SC_TK // WORD
SCORE_ROWS = 64
NEG_BIG = -1e30
DSA_SAFE_BOUND = 40.0
DSA_LOOKAHEAD = 2
INT_MIN = -2 ** 31


def _near_tiles():
    exact = N_BUCKETS // 2
    j = 1
    while True:
        d = j * DSA_TQ - (DSA_TQ - 1)
        b = exact + int(math.log(d / exact) / math.log(REL_MAX_DIST / exact) * (N_BUCKETS - exact) - 0.02)
        if d >= exact and b >= N_BUCKETS - 1:
            return j
        j += 1


DSA_NEAR = _near_tiles()


def _cparams(sem):
    return pltpu.CompilerParams(dimension_semantics=sem, vmem_limit_bytes=VMEM_LIMIT)


def _rel_bucket(dist):
    exact = N_BUCKETS // 2
    df = jnp.maximum(dist, 1).astype(F32)
    large = exact + (jnp.log(df / exact) / math.log(REL_MAX_DIST / exact)
                     * (N_BUCKETS - exact)).astype(I32)
    large = jnp.minimum(large, N_BUCKETS - 1)
    return jnp.where(dist < exact, dist, large)


def _lookup(tab_ref, bucket, col):
    val = jnp.zeros(bucket.shape, F32)
    for b in range(N_BUCKETS):
        val = jnp.where(bucket == b, tab_ref[b, col], val)
    return val


def _dil_bias_kernel(tab_ref, o_ref):
    dil = jnp.left_shift(1, 2 * pl.program_id(0))
    i = lax.broadcasted_iota(I32, (DIL_BLK, 2 * DIL_BLK), 0)
    j = lax.broadcasted_iota(I32, (DIL_BLK, 2 * DIL_BLK), 1)
    bucket = _rel_bucket(jnp.clip(DIL_BLK + i - j, 0, DIL_BLK) * dil)
    for h in range(N_HEADS_A):
        o_ref[0, h] = _lookup(tab_ref, bucket, h)


def _dsa_bias_kernel(tab_ref, o_ref):
    jj = pl.program_id(0)
    s = lax.broadcasted_iota(I32, (DSA_TQ, DSA_TQ), 0)
    t = lax.broadcasted_iota(I32, (DSA_TQ, DSA_TQ), 1)
    bucket = _rel_bucket(jnp.maximum((jj - 1) * DSA_TQ + t - s, 0))
    for h in range(N_HEADS_B):
        o_ref[h, 0] = _lookup(tab_ref, bucket, N_HEADS_A + h) * LOG2E


def _bias_tiles(rel_bias):
    smem = pl.BlockSpec(memory_space=pltpu.SMEM)
    dil = pl.pallas_call(
        _dil_bias_kernel,
        grid=(len(DILATIONS),),
        in_specs=[smem],
        out_specs=pl.BlockSpec((1, N_HEADS_A, DIL_BLK, 2 * DIL_BLK), lambda g: (g, 0, 0, 0)),
        out_shape=jax.ShapeDtypeStruct((len(DILATIONS), N_HEADS_A, DIL_BLK, 2 * DIL_BLK), F32),
        name="dil_bias",
    )(rel_bias)
    ntile = DSA_NEAR + 2
    dsa = pl.pallas_call(
        _dsa_bias_kernel,
        grid=(ntile,),
        in_specs=[smem],
        out_specs=pl.BlockSpec((N_HEADS_B, 1, DSA_TQ, DSA_TQ), lambda g: (0, g, 0, 0)),
        out_shape=jax.ShapeDtypeStruct((N_HEADS_B, ntile, DSA_TQ, DSA_TQ), F32),
        name="dsa_bias",
    )(rel_bias)
    return dil, dsa


def _ffn_kernel(x_ref, g_ref, wg_ref, wu_ref, wo_ref, o_ref, h_ref, acc_ref):
    f = pl.program_id(1)

    @pl.when(f == 0)
    def _():
        x = x_ref[...]
        ms = jnp.mean(x * x, axis=-1, keepdims=True)
        h_ref[...] = (x * lax.rsqrt(ms + EPS) * g_ref[...]).astype(BF16)
        acc_ref[...] = jnp.zeros_like(acc_ref)

    h = h_ref[...]
    gate = jnp.dot(h, wg_ref[...], preferred_element_type=F32)
    up = jnp.dot(h, wu_ref[...], preferred_element_type=F32)
    act = (gate * jax.nn.sigmoid(gate) * up).astype(BF16)
    acc_ref[...] += jnp.dot(act, wo_ref[...], preferred_element_type=F32)

    @pl.when(f == pl.num_programs(1) - 1)
    def _():
        o_ref[...] = x_ref[...] + 0.5 * acc_ref[...]


def _ffn(x2, gain, w_in, w_out, tm=512, tf=512):
    m, d = x2.shape
    d_ff = w_out.shape[0]
    nf = d_ff // tf
    return pl.pallas_call(
        _ffn_kernel,
        grid=(m // tm, nf),
        in_specs=[
            pl.BlockSpec((tm, d), lambda i, f: (i, 0)),
            pl.BlockSpec((1, d), lambda i, f: (0, 0)),
            pl.BlockSpec((d, tf), lambda i, f: (0, f)),
            pl.BlockSpec((d, tf), lambda i, f: (0, f + nf)),
            pl.BlockSpec((tf, d), lambda i, f: (f, 0)),
        ],
        out_specs=pl.BlockSpec((tm, d), lambda i, f: (i, 0)),
        out_shape=jax.ShapeDtypeStruct((m, d), F32),
        scratch_shapes=[pltpu.VMEM((tm, d), BF16), pltpu.VMEM((tm, d), F32)],
        compiler_params=_cparams(("parallel", "arbitrary")),
        name="ffn",
    )(x2, gain.reshape(1, d), w_in, w_in, w_out)


PROJ_BLK = 1024
PROJ_NBLK = 7
PROJ_B0 = 3
PROJ_NMAIN = 5
PROJ_PARTS = 4
_NORM_BLOCKS = (0, 1, 3, 4)


def _proj_kernel(x_ref, g_ref, w_ref, widx_ref, hg_ref, o_ref, vt_ref, qit_ref, ki_ref, wit_ref, h_ref):
    j = pl.program_id(1)

    @pl.when(j == 0)
    def _():
        x = x_ref[...]
        ms = jnp.mean(x * x, axis=-1, keepdims=True)
        h_ref[...] = (x * lax.rsqrt(ms + EPS) * g_ref[...]).astype(BF16)

    rows = h_ref.shape[0] // PROJ_PARTS
    parts = [slice(part * rows, (part + 1) * rows) for part in range(PROJ_PARTS)]
    project = lambda rs: jnp.dot(h_ref[rs, :], w_ref[...], preferred_element_type=F32)

    @pl.when(j < PROJ_NMAIN)
    def _():
        is_norm = functools.reduce(jnp.logical_or, [j == b for b in _NORM_BLOCKS])
        for rs in parts:
            y = project(rs)
            for hd in range(PROJ_BLK // HEAD_DIM):
                sl = slice(hd * HEAD_DIM, (hd + 1) * HEAD_DIM)
                yh = y[:, sl]
                ms = jnp.mean(yh * yh, axis=-1, keepdims=True)
                inv = jnp.where(is_norm, lax.rsqrt(ms + EPS), 1.0)
                o_ref[rs, sl] = (yh * inv * hg_ref[0, :, sl]).astype(BF16)

    @pl.when(j == PROJ_NMAIN)
    def _():
        for rs in parts:
            chunk, off = divmod(rs.start, DSA_TK)
            vt_ref[chunk, :, off:off + rows] = project(rs).astype(BF16).T

    @pl.when(j == PROJ_NMAIN + 1)
    def _():
        for rs in parts:
            qit_ref[0, :, rs] = project(rs).astype(BF16).T
        idx = jnp.dot(h_ref[...], widx_ref[...], preferred_element_type=F32)
        ki_ref[...] = idx[:, :IDX_DIM].astype(BF16)
        wit_ref[0] = idx.T[IDX_DIM:IDX_DIM + N_IDX_HEADS, :]


def _proj(x2, s, gain, w_main, w_idx, head_gain, tm=1024):
    m, d = x2.shape
    per_seq = s // tm
    return pl.pallas_call(
        _proj_kernel,
        grid=(m // tm, PROJ_NBLK),
        in_specs=[
            pl.BlockSpec((tm, d), lambda i, j: (i, 0)),
            pl.BlockSpec((1, d), lambda i, j: (0, 0)),
            pl.BlockSpec((d, PROJ_BLK), lambda i, j: (0, j)),
            pl.BlockSpec((d, IDX_PAD), lambda i, j: (0, 0)),
            pl.BlockSpec((1, 1, PROJ_BLK), lambda i, j: (j, 0, 0)),
        ],
        out_specs=[
            pl.BlockSpec((tm, PROJ_BLK), lambda i, j: (i, jnp.minimum(j, PROJ_NMAIN - 1))),
            pl.BlockSpec((tm // DSA_TK, PROJ_BLK, DSA_TK), lambda i, j: (i, 0, 0)),
            pl.BlockSpec((1, PROJ_BLK, tm), lambda i, j: (i // per_seq, 0, i % per_seq)),
            pl.BlockSpec((tm, IDX_DIM), lambda i, j: (i, 0)),
            pl.BlockSpec((1, N_IDX_HEADS, tm), lambda i, j: (i // per_seq, 0, i % per_seq)),
        ],
        out_shape=[
            jax.ShapeDtypeStruct((m, PROJ_NMAIN * PROJ_BLK), BF16),
            jax.ShapeDtypeStruct((m // DSA_TK, PROJ_BLK, DSA_TK), BF16),
            jax.ShapeDtypeStruct((m // s, PROJ_BLK, s), BF16),
            jax.ShapeDtypeStruct((m, IDX_DIM), BF16),
            jax.ShapeDtypeStruct((m // s, N_IDX_HEADS, s), F32),
        ],
        scratch_shapes=[pltpu.VMEM((tm, d), BF16)],
        compiler_params=_cparams(("parallel", "arbitrary")),
        name="proj",
    )(x2, gain.reshape(1, d), w_main, w_idx, head_gain)


_NT = (((1,), (1,)), ((), ()))


DIL_SPAN = DIL_BLK * max(DILATIONS)
DIL_G = 2
DIL_UNITS = DIL_SPAN // DIL_BLK
DIL_SPLIT = 4
DIL_UNROLL = DIL_UNITS


def _dilated_kernel(q_ref, kp_ref, kc_ref, vp_ref, vc_ref, b_ref, o_ref,
                    qf, kf, vf, ob, lse_ref, q4, k4, v4):
    n = pl.program_id(1)
    h0 = pl.program_id(2) * DIL_G
    for g in range(DIL_G):
        sl = slice(g * HEAD_DIM, (g + 1) * HEAD_DIM)
        qf[g] = q_ref[0, :, sl].astype(F32)
        kf[g, :DIL_SPAN] = kp_ref[0, :, sl].astype(F32)
        kf[g, DIL_SPAN:] = kc_ref[0, :, sl].astype(F32)
        vf[g, :DIL_SPAN] = vp_ref[0, :, sl].astype(F32)
        vf[g, DIL_SPAN:] = vc_ref[0, :, sl].astype(F32)
        for c in range(DIL_SPLIT):
            q4[g, c] = qf[g, pl.ds(c, DIL_SPAN // DIL_SPLIT, stride=DIL_SPLIT), :]
            k4[g, c] = kf[g, pl.ds(c, 2 * DIL_SPAN // DIL_SPLIT, stride=DIL_SPLIT), :]
            v4[g, c] = vf[g, pl.ds(c, 2 * DIL_SPAN // DIL_SPLIT, stride=DIL_SPLIT), :]
    i = lax.broadcasted_iota(I32, (DIL_BLK, 2 * DIL_BLK), 0)
    j = lax.broadcasted_iota(I32, (DIL_BLK, 2 * DIL_BLK), 1)
    band = jnp.logical_and(j >= i, j <= i + DIL_BLK)
    first = j >= DIL_BLK

    for br, dil in enumerate(DILATIONS):
        span = DIL_BLK * dil

        def unit(idx, carry, br=br, dil=dil, span=span):
            blk = idx // dil
            base = blk * span + idx % dil
            valid = jnp.logical_and(band, jnp.logical_or(first, n * (DIL_SPAN // span) + blk > 0))
            qrows = pl.ds(base, DIL_BLK, stride=dil)
            krows = pl.ds(DIL_SPAN + base - span, 2 * DIL_BLK, stride=dil)
            for g in range(DIL_G):
                if dil == DIL_SPLIT * DIL_SPLIT and span == DIL_SPAN:
                    c, b = idx % DIL_SPLIT, idx // DIL_SPLIT
                    q = q4[g, c, pl.ds(b, DIL_BLK, stride=DIL_SPLIT), :].astype(BF16)
                    k = k4[g, c, pl.ds(b, 2 * DIL_BLK, stride=DIL_SPLIT), :].astype(BF16)
                    v = v4[g, c, pl.ds(b, 2 * DIL_BLK, stride=DIL_SPLIT), :].astype(BF16)
                else:
                    q = qf[g, qrows, :].astype(BF16)
                    k = kf[g, krows, :].astype(BF16)
                    v = vf[g, krows, :].astype(BF16)
                s = lax.dot_general(q, k, _NT, preferred_element_type=F32) + b_ref[br, h0 + g]
                s = jnp.where(valid, s, -jnp.inf)
                mx = jnp.max(s, axis=-1, keepdims=True)
                p = jnp.exp(s - mx)
                den = jnp.sum(p, axis=-1, keepdims=True)
                ob[br, g, qrows, :] = jnp.dot(p.astype(BF16), v, preferred_element_type=F32) / den
                lse_ref[br, g, qrows, :] = jnp.broadcast_to(mx + jnp.log(den), (DIL_BLK, HEAD_DIM))
            return carry

        lax.fori_loop(0, DIL_UNITS, unit, 0, unroll=DIL_UNROLL)

    for g in range(DIL_G):
        l1, l2, l3 = lse_ref[0, g], lse_ref[1, g], lse_ref[2, g]
        mx = jnp.maximum(jnp.maximum(l1, l2), l3)
        e1, e2, e3 = jnp.exp(l1 - mx), jnp.exp(l2 - mx), jnp.exp(l3 - mx)
        mix = (e1 * ob[0, g] + e2 * ob[1, g] + e3 * ob[2, g]) / (e1 + e2 + e3)
        o_ref[0, :, g * HEAD_DIM:(g + 1) * HEAD_DIM] = mix.astype(BF16)


def _dilated(main, bias):
    bsz, s, _ = main.shape
    gw = DIL_G * HEAD_DIM
    ng = WIDTH_A // gw
    cur = lambda c: pl.BlockSpec((1, DIL_SPAN, gw), lambda b, n, g: (b, n, c * ng + g))
    prev = lambda c: pl.BlockSpec((1, DIL_SPAN, gw), lambda b, n, g: (b, jnp.maximum(n - 1, 0), c * ng + g))
    out = pl.pallas_call(
        _dilated_kernel,
        grid=(bsz, s // DIL_SPAN, ng),
        in_specs=[cur(0), prev(1), cur(1), prev(2), cur(2),
                  pl.BlockSpec(bias.shape, lambda b, n, g: (0, 0, 0, 0), pipeline_mode=pl.Buffered(1))],
        out_specs=pl.BlockSpec((1, DIL_SPAN, gw), lambda b, n, g: (b, n, g)),
        out_shape=jax.ShapeDtypeStruct((bsz, s, WIDTH_A), BF16),
        scratch_shapes=[pltpu.VMEM((DIL_G, DIL_SPAN, HEAD_DIM), F32),
                        pltpu.VMEM((DIL_G, 2 * DIL_SPAN, HEAD_DIM), F32),
                        pltpu.VMEM((DIL_G, 2 * DIL_SPAN, HEAD_DIM), F32),
                        pltpu.VMEM((len(DILATIONS), DIL_G, DIL_SPAN, HEAD_DIM), F32),
                        pltpu.VMEM((len(DILATIONS), DIL_G, DIL_SPAN, HEAD_DIM), F32),
                        pltpu.VMEM((DIL_G, DIL_SPLIT, DIL_SPAN // DIL_SPLIT, HEAD_DIM), F32),
                        pltpu.VMEM((DIL_G, DIL_SPLIT, 2 * DIL_SPAN // DIL_SPLIT, HEAD_DIM), F32),
                        pltpu.VMEM((DIL_G, DIL_SPLIT, 2 * DIL_SPAN // DIL_SPLIT, HEAD_DIM), F32)],
        compiler_params=_cparams(("parallel", "parallel", "parallel")),
        name="dilated",
    )(main, main, main, main, main, bias)
    return out.reshape(bsz * s, WIDTH_A)


def _dsa_schedule(s, tq, tk):
    qs, ks = [], []
    for i in range(s // tq):
        last = (i * tq + tq - 1) // tk
        for k in range(last + 1):
            qs.append(i)
            ks.append(k)
    return np.asarray(qs, np.int32), np.asarray(ks, np.int32)


def _sortable(bits):
    return bits ^ (lax.shift_right_arithmetic(bits, 31) & 0x7FFFFFFF)


def _bit_transpose(words):
    a = list(words)
    mask, j = 0x0000FFFF, 16
    while j:
        k = 0
        while k < WORD:
            t = (a[k] ^ lax.shift_right_logical(a[k + j], j)) & mask
            a[k] = a[k] ^ t
            a[k + j] = a[k + j] ^ lax.shift_left(t, j)
            k = (k + j + 1) & ~j
        j >>= 1
        mask = (mask ^ (mask << j)) & 0xFFFFFFFF
    return a


def _dsa_score_kernel(qtab, ktab, qit_ref, ki_ref, w_ref, sc_ref, thr_ref,
                      planes_ref, alive_ref, stage_ref):
    step = pl.program_id(1)
    qb = qtab[step]
    kc = ktab[step]
    w = w_ref[0] * IDX_SCALE
    srow = lax.broadcasted_iota(I32, (SCORE_ROWS, DSA_TQ), 0)
    tcol = lax.broadcasted_iota(I32, (SCORE_ROWS, DSA_TQ), 1)
    row0 = pl.multiple_of(kc * CHUNK_WORDS, CHUNK_WORDS)
    group = WORD * 8
    for g in range(SC_TK // group):
        for r0 in range(g * group, (g + 1) * group, SCORE_ROWS):
            kidx = ki_ref[0, r0:r0 + SCORE_ROWS, :]
            acc = jnp.zeros((SCORE_ROWS, DSA_TQ), F32)
            for h in range(N_IDX_HEADS):
                d = jnp.dot(kidx, qit_ref[0, h * IDX_DIM:(h + 1) * IDX_DIM, :],
                            preferred_element_type=F32)
                acc = acc + jnp.maximum(d, 0.0) * w[h:h + 1, :]
            causal = srow - tcol <= qb * DSA_TQ - kc * SC_TK - r0
            acc = jnp.where(causal, acc, -jnp.inf)
            sc_ref[r0:r0 + SCORE_ROWS, :] = acc
            stage_ref[r0:r0 + SCORE_ROWS, :] = _sortable(lax.bitcast_convert_type(acc, I32)) ^ INT_MIN
        for lanes in (slice(c, c + 128) for c in range(0, DSA_TQ, 128)):
            planes = _bit_transpose([stage_ref[g * group + 8 * n:g * group + 8 * (n + 1), lanes]
                                     for n in range(WORD)])
            for n in range(WORD):
                planes_ref[n, pl.ds(pl.multiple_of(row0 + 8 * g, 8), 8), lanes] = planes[n]

    last = (qb * DSA_TQ + DSA_TQ - 1) // SC_TK

    @pl.when(kc == last)
    def _():
        nwords = alive_ref.shape[0]
        rb = min(128, nwords)
        nrows = (last + 1) * CHUNK_WORDS
        nblk = (nrows + rb - 1) // rb
        tpos = qb * DSA_TQ + lax.broadcasted_iota(I32, (1, DSA_TQ), 1)
        want0 = jnp.minimum(tpos + 1, TOPK_MAX)
        rows = lambda blk: pl.ds(pl.multiple_of(blk * rb, rb), rb)

        def hits(alive, n, blk):
            hit = lax.population_count(alive & planes_ref[n, rows(blk), :])
            return hit.reshape(rb // 8, 8, DSA_TQ).sum(axis=0)

        def init(blk, cnt):
            ridx = blk * rb + lax.broadcasted_iota(I32, (rb, DSA_TQ), 0)
            alive = jnp.where(ridx < nrows, -1, 0)
            alive_ref[rows(blk), :] = alive
            return cnt + hits(alive, 0, blk)
        zero = jnp.zeros((8, DSA_TQ), I32)
        cnt0 = lax.fori_loop(0, nblk, init, zero)

        def bit_step(n, carry):
            r, want, cnt8 = carry
            cnt = cnt8.sum(axis=0, keepdims=True)
            take = cnt >= want
            r = r | jnp.where(take, lax.shift_left(jnp.int32(1), 31 - n), 0)
            want = jnp.where(take, want, want - cnt)
            flip = jnp.where(take, 0, -1)
            nxt = jnp.minimum(n + 1, WORD - 1)

            def sweep(blk, acc):
                alive = alive_ref[rows(blk), :] & (planes_ref[n, rows(blk), :] ^ flip)
                alive_ref[rows(blk), :] = alive
                return acc + hits(alive, nxt, blk)
            return r, want, lax.fori_loop(0, nblk, sweep, zero)

        r, want, _ = lax.fori_loop(0, WORD, bit_step, (jnp.zeros((1, DSA_TQ), I32), want0, cnt0))
        thr_ref[0, 0:1, :] = lax.bitcast_convert_type(_sortable(r ^ INT_MIN), F32)
        thr_ref[0, 1:2, :] = jnp.full((1, DSA_TQ), nwords * WORD, F32)

        def live(blk, acc):
            hit = lax.population_count(alive_ref[rows(blk), :])
            return acc + hit.reshape(rb // 8, 8, DSA_TQ).sum(axis=0)
        tied = lax.fori_loop(0, nblk, live, zero).sum(axis=0, keepdims=True)

        @pl.when(jnp.max(tied - want) > 0)
        def _():
            blk_bits = max(1, (nwords // 8 - 1).bit_length())
            lane_patterns = (-0x10000, -0xFF0100, -0xF0F0F10, -0x33333334, -0x55555556)
            nsec = blk_bits + len(lane_patterns) + 3

            def sec_plane(k, blk):
                ridx = blk * rb + lax.broadcasted_iota(I32, (rb, 1), 0)
                if k < blk_bits:
                    bit = 3 + blk_bits - 1 - k
                elif k < blk_bits + len(lane_patterns):
                    return jnp.full((rb, 1), lane_patterns[k - blk_bits], I32)
                else:
                    bit = nsec - 1 - k
                return jnp.where((lax.shift_right_logical(ridx, bit) & 1) == 0, -1, 0)

            def sec_count(k):
                def body(blk, acc):
                    hit = lax.population_count(alive_ref[rows(blk), :] & sec_plane(k, blk))
                    return acc + hit.reshape(rb // 8, 8, DSA_TQ).sum(axis=0)
                return lax.fori_loop(0, nblk, body, zero).sum(axis=0, keepdims=True)

            left = want
            code = jnp.zeros((1, DSA_TQ), I32)
            for k in range(nsec):
                cnt = sec_count(k)
                take = cnt >= left
                code = code | jnp.where(take, 1 << (nsec - 1 - k), 0)
                left = jnp.where(take, left, left - cnt)
                flip = jnp.where(take, 0, -1)

                def update(blk, c, k=k, flip=flip):
                    alive_ref[rows(blk), :] = alive_ref[rows(blk), :] & (sec_plane(k, blk) ^ flip)
                    return c
                lax.fori_loop(0, nblk, update, 0)
            inv = ~code
            cutoff = ((lax.shift_right_logical(inv, 8) & ((1 << blk_bits) - 1)) * (8 * WORD)
                      + 8 * ((code >> 3) & (WORD - 1) ^ (WORD - 1)) + (inv & 7))
            thr_ref[0, 1:2, :] = cutoff.astype(F32)


def _dsa_attn_kernel(qtab, ktab, sc_ref, thr_ref, q_ref, k_ref, vt_ref, b_ref, bound_ref, o_ref,
                     acc_ref, m_ref, l_ref, s_ref, mask_ref, cm_ref):
    step = pl.program_id(1)
    qb = qtab[step]
    kc = ktab[step]

    @pl.when(kc == 0)
    def _():
        acc_ref[...] = jnp.zeros_like(acc_ref)
        m_ref[...] = jnp.full_like(m_ref, NEG_BIG)
        l_ref[...] = jnp.zeros_like(l_ref)

    bound = bound_ref[0]
    bounded = bound <= DSA_SAFE_BOUND

    def set_mask(kept):
        key = (kc * DSA_TK + lax.broadcasted_iota(I32, (DSA_TK, DSA_TQ), 0)).astype(F32)
        for b in range(ATT_QT):
            cols = slice(b * DSA_TQ, (b + 1) * DSA_TQ)
            thr = thr_ref[0, 0:1, cols]
            tied = jnp.logical_and(sc_ref[b] == thr, key <= thr_ref[0, 1:2, cols])
            mask_ref[:, cols] = jnp.where(jnp.logical_or(sc_ref[b] > thr, tied), kept, -jnp.inf)

    tiles = [[jnp.clip((qb * ATT_QT + b) - (kc * DSA_HALVES + a), -1, DSA_NEAR) + 1
              for b in range(ATT_QT)] for a in range(DSA_HALVES)]
    far = qb * ATT_QT - (kc * DSA_HALVES + DSA_HALVES - 1) >= DSA_NEAR
    far_bias = lambda h: b_ref[h, DSA_NEAR + 1, 0:1, 0:1]

    def logits(h):
        sl = slice(h * HEAD_DIM, (h + 1) * HEAD_DIM)
        s = lax.dot_general(k_ref[0, :, sl], q_ref[0, :, sl], _NT, preferred_element_type=F32)
        return s + mask_ref[...]

    def near_bias(h):
        return jnp.concatenate(
            [jnp.concatenate([b_ref[h, tiles[a][b]] for b in range(ATT_QT)], axis=1)
             for a in range(DSA_HALVES)], axis=0)

    def values(h):
        return vt_ref[h * HEAD_DIM:(h + 1) * HEAD_DIM, :]

    @pl.when(bounded)
    def _():
        def raw_logits(h):
            sl = slice(h * HEAD_DIM, (h + 1) * HEAD_DIM)
            s_ref[h] = lax.dot_general(k_ref[0, :, sl], q_ref[0, :, sl], _NT,
                                       preferred_element_type=F32)

        def heads(weights):
            for h in range(DSA_LOOKAHEAD):
                raw_logits(h)
            set_mask(-bound)
            for h in range(N_HEADS_B):
                if h + DSA_LOOKAHEAD < N_HEADS_B:
                    raw_logits(h + DSA_LOOKAHEAD)
                p, gain = weights(h)
                l_ref[h] += gain * jnp.sum(p, axis=0, keepdims=True)
                acc_ref[h] += gain * jnp.dot(values(h), p.astype(BF16), preferred_element_type=F32)

        @pl.when(far)
        def _():
            heads(lambda h: (jnp.exp2(s_ref[h] + mask_ref[...]), jnp.exp2(far_bias(h))))

        @pl.when(jnp.logical_not(far))
        def _():
            heads(lambda h: (jnp.exp2(s_ref[h] + mask_ref[...] + near_bias(h)), 1.0))

    @pl.when(jnp.logical_not(bounded))
    def _():
        set_mask(0.0)

        @pl.when(far)
        def _():
            for h in range(N_HEADS_B):
                s = logits(h)
                s_ref[h] = s
                cm_ref[h] = jnp.max(s, axis=0, keepdims=True) + far_bias(h)

        @pl.when(jnp.logical_not(far))
        def _():
            for h in range(N_HEADS_B):
                s = logits(h) + near_bias(h)
                s_ref[h] = s
                cm_ref[h] = jnp.max(s, axis=0, keepdims=True)

        for h in range(N_HEADS_B):
            m_old = m_ref[h]
            m_new = jnp.maximum(m_old, cm_ref[h])
            alpha = jnp.exp2(m_old - m_new)
            m_ref[h] = m_new
            shift = jnp.where(far, m_new - far_bias(h), m_new)
            p = jnp.exp2(s_ref[h] - shift)
            l_ref[h] = alpha * l_ref[h] + jnp.sum(p, axis=0, keepdims=True)
            pv = jnp.dot(values(h), p.astype(BF16), preferred_element_type=F32)
            acc_ref[h] = alpha * acc_ref[h] + pv

    last = (qb * ATT_TQ + ATT_TQ - 1) // DSA_TK

    @pl.when(kc == last)
    def _():
        for h in range(N_HEADS_B):
            o_ref[0, :, h * HEAD_DIM:(h + 1) * HEAD_DIM] = (acc_ref[h] / l_ref[h]).T.astype(BF16)


def _dsa_scores(qi_t, ki, wi_t):
    bsz, _, s = qi_t.shape
    qs, ks = _dsa_schedule(s, DSA_TQ, SC_TK)
    npairs = len(qs)
    return pl.pallas_call(
        _dsa_score_kernel,
        grid_spec=pltpu.PrefetchScalarGridSpec(
            num_scalar_prefetch=2,
            grid=(bsz, npairs),
            in_specs=[
                pl.BlockSpec((1, N_IDX_HEADS * IDX_DIM, DSA_TQ), lambda b, t, qt, kt: (b, 0, qt[t])),
                pl.BlockSpec((1, SC_TK, IDX_DIM), lambda b, t, qt, kt: (b, kt[t], 0)),
                pl.BlockSpec((1, N_IDX_HEADS, DSA_TQ), lambda b, t, qt, kt: (b, 0, qt[t])),
            ],
            out_specs=[
                pl.BlockSpec((None, None, None, SC_TK, DSA_TQ),
                             lambda b, t, qt, kt: (b, kt[t], qt[t], 0, 0)),
                pl.BlockSpec((1, 2, DSA_TQ), lambda b, t, qt, kt: (b, 0, qt[t])),
            ],
            scratch_shapes=[pltpu.VMEM((WORD, s // WORD, DSA_TQ), I32),
                            pltpu.VMEM((s // WORD, DSA_TQ), I32),
                            pltpu.VMEM((SC_TK, DSA_TQ), I32)],
        ),
        out_shape=[jax.ShapeDtypeStruct((bsz, s // SC_TK, s // DSA_TQ, SC_TK, DSA_TQ), F32),
                   jax.ShapeDtypeStruct((bsz, 2, s), F32)],
        compiler_params=_cparams(("arbitrary", "arbitrary")),
        name="dsa_scores",
    )(qs, ks, qi_t, ki, wi_t)


def _dsa(main, vt, qi_t, ki, wi_t, bias, logit_bound):
    bsz, s, _ = main.shape
    w = WIDTH_B
    scores, thr = _dsa_scores(qi_t, ki, wi_t)
    ntile = bias.shape[1]
    qs, ks = _dsa_schedule(s, ATT_TQ, DSA_TK)
    out = pl.pallas_call(
        _dsa_attn_kernel,
        grid_spec=pltpu.PrefetchScalarGridSpec(
            num_scalar_prefetch=2,
            grid=(bsz, len(qs)),
            in_specs=[
                pl.BlockSpec((None, None, ATT_QT, DSA_TK, DSA_TQ),
                             lambda b, t, qt, kt: (b, kt[t] // SC_PER_ATT, qt[t], kt[t] % SC_PER_ATT, 0)),
                pl.BlockSpec((1, 2, ATT_TQ), lambda b, t, qt, kt: (b, 0, qt[t])),
                pl.BlockSpec((1, ATT_TQ, w), lambda b, t, qt, kt: (b, qt[t], PROJ_B0)),
                pl.BlockSpec((1, DSA_TK, w), lambda b, t, qt, kt: (b, kt[t], PROJ_B0 + 1)),
                pl.BlockSpec((None, None, w, DSA_TK), lambda b, t, qt, kt: (b, kt[t], 0, 0)),
                pl.BlockSpec((N_HEADS_B, ntile, DSA_TQ, DSA_TQ), lambda b, t, qt, kt: (0, 0, 0, 0),
                             pipeline_mode=pl.Buffered(1)),
                pl.BlockSpec(memory_space=pltpu.SMEM),
            ],
            out_specs=pl.BlockSpec((1, ATT_TQ, w), lambda b, t, qt, kt: (b, qt[t], 0)),
            scratch_shapes=[pltpu.VMEM((N_HEADS_B, HEAD_DIM, ATT_TQ), F32),
                            pltpu.VMEM((N_HEADS_B, 1, ATT_TQ), F32),
                            pltpu.VMEM((N_HEADS_B, 1, ATT_TQ), F32),
                            pltpu.VMEM((N_HEADS_B, DSA_TK, ATT_TQ), F32),
                            pltpu.VMEM((DSA_TK, ATT_TQ), F32),
                            pltpu.VMEM((N_HEADS_B, 1, ATT_TQ), F32)],
        ),
        out_shape=jax.ShapeDtypeStruct((bsz, s, w), BF16),
        compiler_params=_cparams(("arbitrary", "arbitrary")),
        name="dsa_attn",
    )(qs, ks, scores, thr, main, main, vt, bias, logit_bound)
    return out.reshape(bsz * s, w)


def _out_kernel(x_ref, ma_ref, mb_ref, wa_ref, wb_ref, o_ref):
    o_ref[...] = (x_ref[...]
                  + jnp.dot(ma_ref[...], wa_ref[...], preferred_element_type=F32)
                  + jnp.dot(mb_ref[...], wb_ref[...], preferred_element_type=F32))


def _out_proj(x2, mix_a, mix_b, w_out, tm=512):
    m, d = x2.shape
    row = lambda width: pl.BlockSpec((tm, width), lambda i: (i, 0))
    return pl.pallas_call(
        _out_kernel,
        grid=(m // tm,),
        in_specs=[row(d), row(WIDTH_A), row(WIDTH_B),
                  pl.BlockSpec((WIDTH_A, d), lambda i: (0, 0)),
                  pl.BlockSpec((WIDTH_B, d), lambda i: (1, 0))],
        out_specs=row(d),
        out_shape=jax.ShapeDtypeStruct((m, d), F32),
        compiler_params=_cparams(("parallel",)),
        name="out_proj",
    )(x2, mix_a, mix_b, w_out, w_out)


def kernel(x, rel_bias, norm_ffn1, w_ffn1_in, w_ffn1_out, norm_mix, w_in, q_norm_a, k_norm_a,
           q_norm_b, k_norm_b, w_out, norm_ffn2, w_ffn2_in, w_ffn2_out):
    bsz, s, d = x.shape
    m = bsz * s
    depth = norm_ffn1.shape[0]
    assert s % DIL_SPAN == 0 and s % SC_TK == 0, "sequence length must be a multiple of 2048"
    assert w_in.shape[-1] == PROJ_NBLK * PROJ_BLK + IDX_DIM + N_IDX_HEADS
    assert rel_bias.shape == (N_BUCKETS, N_HEADS_A + N_HEADS_B)
    scale = HEAD_DIM ** -0.5
    n_main = PROJ_NBLK * PROJ_BLK
    dil_bias, dsa_bias = _bias_tiles(rel_bias)
    x2 = x.reshape(m, d)
    ones = jnp.ones((PROJ_BLK,), F32)
    tile8 = lambda g: jnp.tile(g, PROJ_BLK // HEAD_DIM)
    for l in range(depth):
        x2 = _ffn(x2, norm_ffn1[l], w_ffn1_in[l].astype(BF16), w_ffn1_out[l].astype(BF16))

        w_main = w_in[l, :, :n_main].astype(BF16)
        w_idx = jnp.pad(w_in[l, :, n_main:], ((0, 0), (0, IDX_PAD - IDX_DIM - N_IDX_HEADS))).astype(BF16)
        head_gain = jnp.stack([tile8(q_norm_a[l] * scale), tile8(k_norm_a[l]), ones,
                               tile8(q_norm_b[l] * (scale * LOG2E)), tile8(k_norm_b[l]), ones, ones])
        main, vt, qi_t, ki, wi_t = _proj(x2, s, norm_mix[l], w_main, w_idx,
                                         head_gain.reshape(PROJ_NBLK, 1, PROJ_BLK))
        main = main.reshape(bsz, s, PROJ_NMAIN * PROJ_BLK)
        vt = vt.reshape(bsz, s // DSA_TK, WIDTH_B, DSA_TK)
        ki = ki.reshape(bsz, s, IDX_DIM)

        mix_a = _dilated(main, dil_bias)
        logit_bound = (1.02 * HEAD_DIM * jnp.max(jnp.abs(q_norm_b[l] * (scale * LOG2E)))
                       * jnp.max(jnp.abs(k_norm_b[l]))
                       + LOG2E * jnp.max(jnp.abs(rel_bias[:, N_HEADS_A:]))).reshape(1)
        mix_b = _dsa(main, vt, qi_t, ki, wi_t, dsa_bias, logit_bound)

        x2 = _out_proj(x2, mix_a, mix_b, w_out[l].astype(BF16))
        x2 = _ffn(x2, norm_ffn2[l], w_ffn2_in[l].astype(BF16), w_ffn2_out[l].astype(BF16))
    return x2.reshape(bsz, s, d)
```
